```python
import math
import jax, jax.numpy as jnp
from jax import lax
import numpy as np

D_MODEL = 4096
BATCH = 2
SEQ = 4096
DEPTH = 1

MIX_POOL = D_MODEL // 2
N_POOL_GROUPS = 4
POOL_WINDOWS = (2, 4, 8, 16)
POOL_GW = MIX_POOL // N_POOL_GROUPS
HEAD_DIM = 128
N_Q_HEADS = (D_MODEL - MIX_POOL) // HEAD_DIM
N_KV_HEADS = 4
Q_PER_KV = N_Q_HEADS // N_KV_HEADS
N_IDX_HEADS = 8
IDX_DIM = 64
TOPK_MAX = 256
Q_BLOCK = 128
ROPE_THETA = 10000.0
N_MEM = 256
N_MEM_HEADS = 4
MEM_HEAD_DIM = D_MODEL // N_MEM_HEADS
N_GROUPS = 8
EXPERTS_PER_GROUP = 8
N_EXPERTS = N_GROUPS * EXPERTS_PER_GROUP
TOP_K_IN_GROUP = 2
EXPERT_FF = 512
MOE_BLOCK = 128
LN_EPS = 1e-5
ALPHA = (2.0 * DEPTH) ** 0.25
BETA = (8.0 * DEPTH) ** -0.25
IN_SIZES = (MIX_POOL, N_Q_HEADS * HEAD_DIM, N_KV_HEADS * HEAD_DIM, N_KV_HEADS * HEAD_DIM,
            N_IDX_HEADS * IDX_DIM, IDX_DIM, N_IDX_HEADS)

kernel_name = "hybrid_pool_dsa_hiermoe_deepnorm"


def layer_norm(x, g, b):
    xf = x.astype(jnp.float32)
    mu = jnp.mean(xf, axis=-1, keepdims=True)
    var = jnp.mean(jnp.square(xf - mu), axis=-1, keepdims=True)
    y = (xf - mu) * lax.rsqrt(var + LN_EPS) * g.astype(jnp.float32) + b.astype(jnp.float32)
    return y.astype(x.dtype)


def rope_tables(positions, dim):
    inv = 1.0 / (ROPE_THETA ** (jnp.arange(0, dim, 2, dtype=jnp.float32) / dim))
    ang = positions.astype(jnp.float32)[..., None] * inv
    return jnp.cos(ang), jnp.sin(ang)


def apply_rope(x, cos, sin):
    xf = x.astype(jnp.float32)
    half = x.shape[-1] // 2
    x1, x2 = xf[..., :half], xf[..., half:]
    c, s = cos[:, :, None, :], sin[:, :, None, :]
    return jnp.concatenate([x1 * c - x2 * s, x2 * c + x1 * s], axis=-1).astype(x.dtype)


def pool_mixer(v, pool_w, pool_scale):
    B, S, _ = v.shape
    vg = v.reshape(B, S, N_POOL_GROUPS, POOL_GW)
    csum = jnp.cumsum(vg.astype(jnp.float32), axis=1)
    csum = jnp.pad(csum, ((0, 0), (1, 0), (0, 0), (0, 0)))
    t = jnp.arange(S)
    outs = []
    for g, w in enumerate(POOL_WINDOWS):
        start = jnp.maximum(t + 1 - w, 0)
        win_sum = csum[:, 1:, g] - csum[:, start, g]
        cnt = (t + 1 - start).astype(jnp.float32)[None, :, None]
        outs.append(win_sum / cnt - vg[:, :, g].astype(jnp.float32))
    pooled = jnp.stack(outs, axis=2).astype(v.dtype)
    mixed = jnp.einsum('bsgc,gcd->bsgd', pooled, pool_w)
    return mixed.reshape(B, S, MIX_POOL) * pool_scale


def dsa_attention(q, k, v, qi, ki, wi):
    B, S = q.shape[0], q.shape[1]
    n_sel = min(TOPK_MAX, S // 4)
    nb = S // Q_BLOCK
    key_pos = jnp.arange(S)

    def to_blocks(a):
        return a.reshape((B, nb, Q_BLOCK) + a.shape[2:]).swapaxes(0, 1)

    def block_fn(args):
        blk, qb, qib, wib = args
        tpos = blk * Q_BLOCK + jnp.arange(Q_BLOCK)
        rel = jax.nn.relu(jnp.einsum('bqhd,bsd->bqhs', qib, ki).astype(jnp.float32))
        score = jnp.einsum('bqhs,bqh->bqs', rel, wib.astype(jnp.float32))
        causal = key_pos[None, :] <= tpos[:, None]
        score = jnp.where(causal[None], score, -jnp.inf)
        _, idx = lax.top_k(score, n_sel)
        valid = idx <= tpos[None, :, None]
        kg = jax.vmap(lambda kb, ib: kb[ib])(k, idx)
        vg = jax.vmap(lambda vb, ib: vb[ib])(v, idx)
        qg = qb.reshape(B, Q_BLOCK, N_KV_HEADS, Q_PER_KV, HEAD_DIM)
        logits = jnp.einsum('bqgrd,bqkgd->bqgrk', qg, kg).astype(jnp.float32) * (HEAD_DIM ** -0.5)
        logits = jnp.where(valid[:, :, None, None, :], logits, -jnp.inf)
        p = jax.nn.softmax(logits, axis=-1).astype(v.dtype)
        o = jnp.einsum('bqgrk,bqkgd->bqgrd', p, vg)
        return o.reshape(B, Q_BLOCK, N_Q_HEADS * HEAD_DIM)

    out = lax.map(block_fn, (jnp.arange(nb), to_blocks(q), to_blocks(qi), to_blocks(wi)))
    return out.swapaxes(0, 1).reshape(B, S, N_Q_HEADS * HEAD_DIM)


def mem_cross_attention(x, mem, w_mq, w_mk, w_mv, w_mo):
    B, S, _ = x.shape
    M = mem.shape[1]
    q = (x @ w_mq).reshape(B, S, N_MEM_HEADS, MEM_HEAD_DIM)
    k = (mem @ w_mk).reshape(B, M, N_MEM_HEADS, MEM_HEAD_DIM)
    v = (mem @ w_mv).reshape(B, M, N_MEM_HEADS, MEM_HEAD_DIM)
    logits = jnp.einsum('bshd,bmhd->bhsm', q, k).astype(jnp.float32) * (MEM_HEAD_DIM ** -0.5)
    p = jax.nn.softmax(logits, axis=-1).astype(v.dtype)
    o = jnp.einsum('bhsm,bmhd->bshd', p, v).reshape(B, S, D_MODEL)
    return o @ w_mo


def hier_moe(x, w_gr, b_gr, w_er, b_er, w_gate, w_up, w_down):
    B, S, D = x.shape
    N = B * S
    xt = x.reshape(N, D)
    g_logits = (xt @ w_gr).astype(jnp.float32) + b_gr.astype(jnp.float32)
    g_prob = jax.nn.softmax(g_logits, axis=-1)
    g_idx = jnp.argmax(g_logits, axis=-1)
    g_gate = jnp.take_along_axis(g_prob, g_idx[:, None], axis=-1)
    e_logits = ((xt @ w_er).astype(jnp.float32) + b_er.astype(jnp.float32)).reshape(N, N_GROUPS, EXPERTS_PER_GROUP)
    e_logits = jnp.take_along_axis(e_logits, g_idx[:, None, None], axis=1)[:, 0]
    top_vals, top_loc = lax.top_k(e_logits, TOP_K_IN_GROUP)
    gate = jax.nn.softmax(top_vals, axis=-1) * g_gate
    expert_id = g_idx[:, None] * EXPERTS_PER_GROUP + top_loc

    A = N * TOP_K_IN_GROUP
    eid = expert_id.reshape(A)
    tok = jnp.arange(A, dtype=jnp.int32) // TOP_K_IN_GROUP
    wgt = gate.reshape(A)
    order = jnp.argsort(eid)
    e_sorted = eid[order]
    counts = jnp.bincount(eid, length=N_EXPERTS)
    padded = ((counts + MOE_BLOCK - 1) // MOE_BLOCK) * MOE_BLOCK
    start = jnp.cumsum(counts) - counts
    pend = jnp.cumsum(padded)
    pstart = pend - padded
    dest = pstart[e_sorted] + (jnp.arange(A) - start[e_sorted])
    P = A + N_EXPERTS * MOE_BLOCK
    row_tok = jnp.full((P,), N, dtype=jnp.int32).at[dest].set(tok[order])
    row_w = jnp.zeros((P,), jnp.float32).at[dest].set(wgt[order])
    nblk = P // MOE_BLOCK
    blk_e = jnp.minimum(jnp.searchsorted(pend, jnp.arange(nblk) * MOE_BLOCK, side='right'), N_EXPERTS - 1)
    xpad = jnp.concatenate([xt, jnp.zeros((1, D), xt.dtype)], axis=0)

    def expert_block(args):
        rows, e = args
        xb = xpad[rows]
        h = jax.nn.silu(xb @ w_gate[e]) * (xb @ w_up[e])
        return h @ w_down[e]

    yb = lax.map(expert_block, (row_tok.reshape(nblk, MOE_BLOCK), blk_e))
    contrib = yb.reshape(P, D).astype(jnp.float32) * row_w[:, None]
    y = jax.ops.segment_sum(contrib, row_tok, num_segments=N + 1)[:N]
    return y.astype(x.dtype).reshape(B, S, D)


def setup_inputs(seed: int = 0) -> dict:
    key = jax.random.key(seed)
    ks = jax.random.split(key, 23)
    f32 = jnp.float32
    L = DEPTH
    d_in = sum(IN_SIZES)

    def nrm(k, shape, scale):
        return jax.random.normal(k, shape, f32) * scale

    return {
        'x': nrm(ks[0], (BATCH, SEQ, D_MODEL), 1.0),
        'mem': nrm(ks[1], (BATCH, N_MEM, D_MODEL), 1.0),
        'positions': jnp.broadcast_to(jnp.arange(SEQ, dtype=jnp.int32), (BATCH, SEQ)),
        'w_in': nrm(ks[2], (L, D_MODEL, d_in), D_MODEL ** -0.5),
        'pool_w': nrm(ks[3], (L, N_POOL_GROUPS, POOL_GW, POOL_GW), POOL_GW ** -0.5),
        'pool_scale': 1.0 + nrm(ks[4], (L, MIX_POOL), 0.02),
        'w_o': nrm(ks[5], (L, D_MODEL, D_MODEL), BETA * D_MODEL ** -0.5),
        'ln1_g': 1.0 + nrm(ks[6], (L, D_MODEL), 0.02),
        'ln1_b': nrm(ks[7], (L, D_MODEL), 0.02),
        'w_mq': nrm(ks[8], (L, D_MODEL, D_MODEL), D_MODEL ** -0.5),
        'w_mk': nrm(ks[9], (L, D_MODEL, D_MODEL), D_MODEL ** -0.5),
        'w_mv': nrm(ks[10], (L, D_MODEL, D_MODEL), D_MODEL ** -0.5),
        'w_mo': nrm(ks[11], (L, D_MODEL, D_MODEL), BETA * D_MODEL ** -0.5),
        'ln2_g': 1.0 + nrm(ks[12], (L, D_MODEL), 0.02),
        'ln2_b': nrm(ks[13], (L, D_MODEL), 0.02),
        'w_group_router': nrm(ks[14], (L, D_MODEL, N_GROUPS), D_MODEL ** -0.5),
        'b_group_router': nrm(ks[15], (L, N_GROUPS), 0.01),
        'w_expert_router': nrm(ks[16], (L, D_MODEL, N_EXPERTS), D_MODEL ** -0.5),
        'b_expert_router': nrm(ks[17], (L, N_EXPERTS), 0.01),
        'w_gate': nrm(ks[18], (L, N_EXPERTS, D_MODEL, EXPERT_FF), D_MODEL ** -0.5),
        'w_up': nrm(ks[19], (L, N_EXPERTS, D_MODEL, EXPERT_FF), D_MODEL ** -0.5),
        'w_down': nrm(ks[20], (L, N_EXPERTS, EXPERT_FF, D_MODEL), BETA * EXPERT_FF ** -0.5),
        'ln3_g': 1.0 + nrm(ks[21], (L, D_MODEL), 0.02),
        'ln3_b': nrm(ks[22], (L, D_MODEL), 0.02),
    }


def reference(x, mem, positions, w_in, pool_w, pool_scale, w_o, ln1_g, ln1_b,
              w_mq, w_mk, w_mv, w_mo, ln2_g, ln2_b,
              w_group_router, b_group_router, w_expert_router, b_expert_router,
              w_gate, w_up, w_down, ln3_g, ln3_b):
    B, S, _ = x.shape
    cos_h, sin_h = rope_tables(positions, HEAD_DIM)
    cos_i, sin_i = rope_tables(positions, IDX_DIM)
    split_pts = []
    acc = 0
    for sz in IN_SIZES[:-1]:
        acc += sz
        split_pts.append(acc)
    idx_w_scale = (N_IDX_HEADS ** -0.5) * (IDX_DIM ** -0.5)

    for l in range(DEPTH):
        h = x @ w_in[l]
        v_pool, q, k, v, qi, ki, wi = jnp.split(h, split_pts, axis=-1)
        q = apply_rope(q.reshape(B, S, N_Q_HEADS, HEAD_DIM), cos_h, sin_h)
        k = apply_rope(k.reshape(B, S, N_KV_HEADS, HEAD_DIM), cos_h, sin_h)
        v = v.reshape(B, S, N_KV_HEADS, HEAD_DIM)
        qi = apply_rope(qi.reshape(B, S, N_IDX_HEADS, IDX_DIM), cos_i, sin_i)
        ki = apply_rope(ki.reshape(B, S, 1, IDX_DIM), cos_i, sin_i)[:, :, 0]
        wi = wi * idx_w_scale
        a_pool = pool_mixer(v_pool, pool_w[l], pool_scale[l])
        a_attn = dsa_attention(q, k, v, qi, ki, wi)
        mix = jnp.concatenate([a_pool, a_attn], axis=-1) @ w_o[l]
        x = layer_norm(ALPHA * x + mix, ln1_g[l], ln1_b[l])
        c = mem_cross_attention(x, mem, w_mq[l], w_mk[l], w_mv[l], w_mo[l])
        x = layer_norm(ALPHA * x + c, ln2_g[l], ln2_b[l])
        f = hier_moe(x, w_group_router[l], b_group_router[l], w_expert_router[l], b_expert_router[l],
                     w_gate[l], w_up[l], w_down[l])
        x = layer_norm(ALPHA * x + f, ln3_g[l], ln3_b[l])
    return x
```

```python
import functools

import jax
import jax.numpy as jnp
from jax import lax
from jax.experimental import pallas as pl
from jax.experimental.pallas import tpu as pltpu

F32 = jnp.float32
BF16 = jnp.bfloat16
I32 = jnp.int32

MIX_POOL = 2048
N_POOL_GROUPS = 4
POOL_WINDOWS = (2, 4, 8, 16)
POOL_GW = MIX_POOL // N_POOL_GROUPS
HEAD_DIM = 128
N_Q_HEADS = 16
N_KV_HEADS = 4
Q_PER_KV = N_Q_HEADS // N_KV_HEADS
N_IDX_HEADS = 8
IDX_DIM = 64
TOPK_MAX = 256
ROPE_THETA = 10000.0
N_MEM_HEADS = 4
N_GROUPS = 8
EXPERTS_PER_GROUP = 8
N_EXPERTS = N_GROUPS * EXPERTS_PER_GROUP
LN_EPS = 1e-5
DEPTH = 1
ALPHA = (2.0 * DEPTH) ** 0.25

LANES = 128
QB = 128
LC = 512
MOE_BLK = 256
POOL_HALO = 16
NEG = -1e30
INT_MIN = -2147483648
VMEM_LIMIT = 56 * 1024 * 1024


def _cparams(n_axes, vmem=VMEM_LIMIT):
    return pltpu.CompilerParams(dimension_semantics=("arbitrary",) * n_axes, vmem_limit_bytes=vmem)


def _dot(a, b):
    return jnp.dot(a, b, preferred_element_type=F32)


def _dot_nt(a, b):
    return lax.dot_general(a, b, (((1,), (1,)), ((), ())), preferred_element_type=F32)


def _mm_kernel(*refs, n_lhs, n_extra, epilogue):
    lhs = refs[:n_lhs]
    ws = refs[n_lhs:2 * n_lhs]
    extras = refs[2 * n_lhs:2 * n_lhs + n_extra]
    outs = refs[2 * n_lhs + n_extra:]
    acc = _dot(lhs[0][...], ws[0][...])
    for l in range(1, n_lhs):
        acc = acc + _dot(lhs[l][...], ws[l][...])
    epilogue(acc, extras, outs)


def _matmul(lhs_list, w_list, *, name, bm, bn, n, w_col_off=0, extras=(), out_dtype=F32, epilogue=None):
    m = lhs_list[0].shape[0]
    grid = (n // bn, m // bm)
    in_specs = [pl.BlockSpec((bm, a.shape[1]), lambda j, i: (i, 0)) for a in lhs_list]
    in_specs += [pl.BlockSpec((w.shape[0], bn), lambda j, i: (0, j + w_col_off)) for w in w_list]
    in_specs += [pl.BlockSpec(blk, (lambda j, i, f=f: f(i, j))) for (_, blk, f) in extras]
    if epilogue is None:
        def epilogue(acc, ex, outs):
            outs[0][...] = acc.astype(outs[0].dtype)
    kern = functools.partial(_mm_kernel, n_lhs=len(lhs_list), n_extra=len(extras), epilogue=epilogue)
    return pl.pallas_call(
        kern, grid=grid, in_specs=in_specs,
        out_specs=pl.BlockSpec((bm, bn), lambda j, i: (i, j)),
        out_shape=jax.ShapeDtypeStruct((m, n), out_dtype),
        compiler_params=_cparams(2), name=name,
    )(*lhs_list, *w_list, *[e[0] for e in extras])


def _residual_epilogue(acc, ex, outs):
    outs[0][...] = acc + ALPHA * ex[0][...]


def _rope128(a, cosf, sinf):
    return a * cosf + pltpu.roll(a, 64, 1) * sinf


def _rope64(a, cos64, sin_lo, sin_hi):
    return a * cos64 + pltpu.roll(a, 96, 1) * sin_lo + pltpu.roll(a, 32, 1) * sin_hi


def _inproj_kernel(x_ref, w_ref, cosf_ref, sinf_ref, cos64_ref, sinlo_ref, sinhi_ref,
                   h_ref, k2_ref, kw_ref, acc_ref, *, q_scale, wi_scale):
    j = pl.program_id(1)
    acc_ref[...] = _dot(x_ref[...], w_ref[...])
    bn = acc_ref.shape[1]

    @pl.when(j < 5)
    def _():
        scale = jnp.where(j < 4, q_scale, 1.0).astype(F32)
        cosf = cosf_ref[...] * scale
        sinf = sinf_ref[...] * scale
        for c in range(bn // LANES):
            a = acc_ref[:, c * LANES:(c + 1) * LANES]
            h_ref[:, c * LANES:(c + 1) * LANES] = _rope128(a, cosf, sinf).astype(h_ref.dtype)

    @pl.when(j == 5)
    def _():
        h_ref[...] = acc_ref[...].astype(h_ref.dtype)

    @pl.when(j == 6)
    def _():
        for c in range(bn // LANES):
            a = acc_ref[:, c * LANES:(c + 1) * LANES]
            r = _rope64(a, cos64_ref[...], sinlo_ref[...], sinhi_ref[...])
            h_ref[:, c * LANES:(c + 1) * LANES] = r.astype(h_ref.dtype)

    @pl.when(j == 7)
    def _():
        a = acc_ref[:, 0:LANES]
        r = _rope64(a, cos64_ref[...], sinlo_ref[...], sinhi_ref[...])
        lane = lax.broadcasted_iota(I32, a.shape, 1)
        kw_ref[...] = jnp.where(lane < IDX_DIM, r, a * wi_scale)
        k2_ref[...] = jnp.where(lane < IDX_DIM, r, pltpu.roll(r, 64, 1)).astype(k2_ref.dtype)


def _inproj_attn(x_bf, w_bf, tables, *, col_off_blocks, bm, bn):
    m = x_bf.shape[0]
    n_tiles = 8
    q_scale = HEAD_DIM ** -0.5
    wi_scale = (N_IDX_HEADS ** -0.5) * (IDX_DIM ** -0.5)
    tab_spec = pl.BlockSpec((bm, LANES), lambda i, j: (i, 0))
    kern = functools.partial(_inproj_kernel, q_scale=q_scale, wi_scale=wi_scale)
    return pl.pallas_call(
        kern, grid=(m // bm, n_tiles),
        in_specs=[pl.BlockSpec((bm, x_bf.shape[1]), lambda i, j: (i, 0)),
                  pl.BlockSpec((w_bf.shape[0], bn), lambda i, j: (0, j + col_off_blocks))] + [tab_spec] * 5,
        out_specs=[pl.BlockSpec((bm, bn), lambda i, j: (i, jnp.minimum(j, n_tiles - 2))),
                   pl.BlockSpec((bm, LANES), lambda i, j: (i, 0)),
                   pl.BlockSpec((bm, LANES), lambda i, j: (i, 0))],
        out_shape=[jax.ShapeDtypeStruct((m, (n_tiles - 1) * bn), BF16),
                   jax.ShapeDtypeStruct((m, LANES), BF16),
                   jax.ShapeDtypeStruct((m, LANES), F32)],
        scratch_shapes=[pltpu.VMEM((bm, bn), F32)],
        compiler_params=_cparams(2), name="inproj_attn",
    )(x_bf, w_bf, *tables)


def _pool_kernel(v_ref, pw_ref, ps_ref, o_ref, ext_ref):
    s = pl.program_id(1)
    ts = v_ref.shape[0]

    @pl.when(s == 0)
    def _():
        ext_ref[0:POOL_HALO, :] = jnp.zeros((POOL_HALO, ext_ref.shape[1]), F32)

    ext_ref[POOL_HALO:, :] = v_ref[...]
    t = s * ts + lax.broadcasted_iota(I32, (ts, 1), 0)
    for g, w in enumerate(POOL_WINDOWS):
        cols = slice(g * POOL_GW, (g + 1) * POOL_GW)
        e = ext_ref[:, cols]
        step = 1
        while step < w:
            e = e + pltpu.roll(e, step, 0)
            step *= 2
        win = e[POOL_HALO:, :]
        cnt = jnp.minimum(t + 1, w).astype(F32)
        pooled = win / cnt - v_ref[:, cols]
        mixed = _dot(pooled.astype(BF16), pw_ref[g])
        o_ref[:, cols] = (mixed * ps_ref[:, cols]).astype(o_ref.dtype)
    ext_ref[0:POOL_HALO, :] = v_ref[ts - POOL_HALO:, :]


def _pool_mixer(v_pool, pool_w_bf, pool_scale, *, batch, seq, ts):
    v3 = v_pool.reshape(batch, seq, MIX_POOL)
    out = pl.pallas_call(
        _pool_kernel, grid=(batch, seq // ts),
        in_specs=[pl.BlockSpec((None, ts, MIX_POOL), lambda b, s: (b, s, 0)),
                  pl.BlockSpec((N_POOL_GROUPS, POOL_GW, POOL_GW), lambda b, s: (0, 0, 0)),
                  pl.BlockSpec((1, MIX_POOL), lambda b, s: (0, 0))],
        out_specs=pl.BlockSpec((None, ts, MIX_POOL), lambda b, s: (b, s, 0)),
        out_shape=jax.ShapeDtypeStruct((batch, seq, MIX_POOL), BF16),
        scratch_shapes=[pltpu.VMEM((POOL_HALO + ts, MIX_POOL), F32)],
        compiler_params=_cparams(2), name="pool_mixer",
    )(v3, pool_w_bf, pool_scale.reshape(1, MIX_POOL))
    return out.reshape(batch * seq, MIX_POOL)


def _dsa_kernel(q_ref, k_ref, v_ref, k2_ref, qi_ref, kw_ref, o_ref,
                key_sc, bias_sc, q_sc, m_sc, l_sc, acc_sc, *, n_sel, seq):
    blk = pl.program_id(1)
    nchunk = blk // (LC // QB) + 1
    row_t = blk * QB + lax.broadcasted_iota(I32, (QB, 1), 0)
    lane_q = lax.broadcasted_iota(I32, (QB, LANES), 1)
    col0 = lax.broadcasted_iota(I32, (1, LC), 1)

    wi = [kw_ref[:, IDX_DIM + h:IDX_DIM + h + 1] for h in range(N_IDX_HEADS)]

    def score_chunk(c, carry):
        off = pl.multiple_of(c * LC, LC)
        k2c = k2_ref[pl.ds(off, LC), :]
        sc = jnp.zeros((QB, LC), F32)
        for jj in range(N_IDX_HEADS // 2):
            xq = qi_ref[:, jj * LANES:(jj + 1) * LANES]
            zero = jnp.zeros_like(xq)
            r_lo = _dot_nt(jnp.where(lane_q < IDX_DIM, xq, zero), k2c)
            r_hi = _dot_nt(jnp.where(lane_q >= IDX_DIM, xq, zero), k2c)
            sc = sc + jnp.maximum(r_lo, 0.0) * wi[2 * jj] + jnp.maximum(r_hi, 0.0) * wi[2 * jj + 1]
        sc = sc + 0.0
        bits = pltpu.bitcast(sc, I32)
        key = bits ^ ((bits >> 31) & 0x7FFFFFFF)
        key = jnp.where(c * LC + col0 <= row_t, key, INT_MIN)
        key_sc[c] = key
        return carry

    lax.fori_loop(0, nchunk, score_chunk, 0)

    def count(pred_fn):
        def body(c, acc):
            m = pred_fn(key_sc[c], c)
            for u in range(LC // LANES):
                acc = acc + m[:, u * LANES:(u + 1) * LANES]
            return acc
        acc = lax.fori_loop(0, nchunk, body, jnp.zeros((QB, LANES), F32))
        return jnp.sum(acc, axis=1, keepdims=True)

    def bit_body(i, t_u):
        cand_u = t_u | jnp.left_shift(jnp.int32(1), 31 - i)
        cand = cand_u ^ INT_MIN
        cnt = count(lambda kk, c: jnp.where(kk >= cand, 1.0, 0.0))
        return jnp.where(cnt >= n_sel, cand_u, t_u)

    t_u = lax.fori_loop(0, 32, bit_body, jnp.zeros((QB, 1), I32))
    thr = t_u ^ INT_MIN
    has_thr = thr != INT_MIN
    n_gt = count(lambda kk, c: jnp.where(kk > thr, 1.0, 0.0))
    n_eq = count(lambda kk, c: jnp.where(kk == thr, 1.0, 0.0))
    need = n_sel - n_gt

    def tie_search():
        def tbody(i, p):
            cand = p | jnp.left_shift(jnp.int32(1), (seq.bit_length() - 2) - i)
            cnt = count(lambda kk, c: jnp.where(kk == thr, jnp.where(c * LC + col0 < cand, 1.0, 0.0), 0.0))
            return jnp.where(cnt < need, cand, p)
        return lax.fori_loop(0, seq.bit_length() - 1, tbody, jnp.zeros((QB, 1), I32))

    ambiguous = jnp.max(jnp.where(has_thr & (n_eq > need), 1.0, 0.0)) > 0.0
    tie_hi = lax.cond(ambiguous, tie_search, lambda: jnp.full((QB, 1), seq, I32))
    tie_hi = jnp.where(has_thr, tie_hi, -1)

    def bias_chunk(c, carry):
        kk = key_sc[c]
        tie_ok = jnp.where(c * LC + col0 <= tie_hi, 0.0, NEG)
        bias_sc[c] = jnp.where(kk > thr, 0.0, jnp.where(kk == thr, tie_ok, NEG))
        return carry

    lax.fori_loop(0, nchunk, bias_chunk, 0)

    for g in range(N_KV_HEADS):
        for r in range(Q_PER_KV):
            h = g * Q_PER_KV + r
            q_sc[g, r * QB:(r + 1) * QB, :] = q_ref[:, h * HEAD_DIM:(h + 1) * HEAD_DIM]
    m_sc[...] = jnp.full(m_sc.shape, NEG, F32)
    l_sc[...] = jnp.zeros(l_sc.shape, F32)
    acc_sc[...] = jnp.zeros(acc_sc.shape, F32)

    def attn_chunk(c, carry):
        off = pl.multiple_of(c * LC, LC)
        bias = bias_sc[c]
        bias4 = jnp.concatenate([bias] * Q_PER_KV, axis=0)
        for g in range(N_KV_HEADS):
            kc = k_ref[pl.ds(off, LC), g * HEAD_DIM:(g + 1) * HEAD_DIM]
            vc = v_ref[pl.ds(off, LC), g * HEAD_DIM:(g + 1) * HEAD_DIM]
            s = _dot_nt(q_sc[g], kc) + bias4
            m_old = m_sc[g]
            m_new = jnp.maximum(m_old, jnp.max(s, axis=1, keepdims=True))
            alpha = jnp.exp(m_old - m_new)
            p = jnp.exp(s - m_new)
            l_sc[g] = alpha * l_sc[g] + jnp.sum(p, axis=1, keepdims=True)
            acc_sc[g] = alpha * acc_sc[g] + _dot(p.astype(BF16), vc)
            m_sc[g] = m_new
        return carry

    lax.fori_loop(0, nchunk, attn_chunk, 0)

    for g in range(N_KV_HEADS):
        o = acc_sc[g] / l_sc[g]
        for r in range(Q_PER_KV):
            h = g * Q_PER_KV + r
            o_ref[:, h * HEAD_DIM:(h + 1) * HEAD_DIM] = o[r * QB:(r + 1) * QB, :].astype(o_ref.dtype)


def _dsa_attention(h2, k2, kw, *, batch, seq, bn):
    nb = seq // QB
    n_sel = min(TOPK_MAX, seq // 4)
    nch = seq // LC
    q_cols = N_Q_HEADS * HEAD_DIM
    kv_cols = N_KV_HEADS * HEAD_DIM
    assert bn == kv_cols and q_cols == 4 * bn and seq % LC == 0 and LC >= n_sel
    kern = functools.partial(_dsa_kernel, n_sel=float(n_sel), seq=seq)
    rows = Q_PER_KV * QB
    return pl.pallas_call(
        kern, grid=(batch, nb),
        in_specs=[pl.BlockSpec((QB, q_cols), lambda b, i: (b * nb + i, 0)),
                  pl.BlockSpec((seq, kv_cols), lambda b, i: (b, 4)),
                  pl.BlockSpec((seq, kv_cols), lambda b, i: (b, 5)),
                  pl.BlockSpec((seq, LANES), lambda b, i: (b, 0)),
                  pl.BlockSpec((QB, bn), lambda b, i: (b * nb + i, 6)),
                  pl.BlockSpec((QB, LANES), lambda b, i: (b * nb + i, 0))],
        out_specs=pl.BlockSpec((QB, q_cols), lambda b, i: (b * nb + i, 0)),
        out_shape=jax.ShapeDtypeStruct((batch * seq, q_cols), BF16),
        scratch_shapes=[pltpu.VMEM((nch, QB, LC), I32),
                        pltpu.VMEM((nch, QB, LC), F32),
                        pltpu.VMEM((N_KV_HEADS, rows, HEAD_DIM), BF16),
                        pltpu.VMEM((N_KV_HEADS, rows, 1), F32),
                        pltpu.VMEM((N_KV_HEADS, rows, 1), F32),
                        pltpu.VMEM((N_KV_HEADS, rows, HEAD_DIM), F32)],
        compiler_params=_cparams(2), name="dsa_attention",
    )(h2, h2, h2, k2, h2, kw)


def _ln_rows(x, g, b):
    mu = jnp.mean(x, axis=-1, keepdims=True)
    xc = x - mu
    var = jnp.mean(xc * xc, axis=-1, keepdims=True)
    return xc * lax.rsqrt(var + LN_EPS) * g + b


def _ln_kernel(x_ref, g_ref, b_ref, o_ref, obf_ref):
    y = _ln_rows(x_ref[...], g_ref[...], b_ref[...])
    o_ref[...] = y
    obf_ref[...] = y.astype(obf_ref.dtype)


def _layer_norm(x, g, b, *, bm):
    m, d = x.shape
    row = pl.BlockSpec((bm, d), lambda i: (i, 0))
    vec = pl.BlockSpec((1, d), lambda i: (0, 0))
    return pl.pallas_call(
        _ln_kernel, grid=(m // bm,), in_specs=[row, vec, vec], out_specs=[row, row],
        out_shape=[jax.ShapeDtypeStruct((m, d), F32), jax.ShapeDtypeStruct((m, d), BF16)],
        compiler_params=_cparams(1), name="layer_norm",
    )(x, g.reshape(1, d), b.reshape(1, d))


def _xattn_kernel(q_ref, k_ref, v_ref, o_ref):
    dh = q_ref.shape[1] // N_MEM_HEADS
    for h in range(N_MEM_HEADS):
        cols = slice(h * dh, (h + 1) * dh)
        s = _dot_nt(q_ref[:, cols], k_ref[:, cols])
        p = jnp.exp(s - jnp.max(s, axis=1, keepdims=True))
        o = _dot(p.astype(BF16), v_ref[:, cols]) / jnp.sum(p, axis=1, keepdims=True)
        o_ref[:, cols] = o.astype(o_ref.dtype)


def _cross_attention(q, k, v, *, batch, seq, n_mem, bm):
    d = q.shape[1]
    nb = seq // bm
    return pl.pallas_call(
        _xattn_kernel, grid=(batch, nb),
        in_specs=[pl.BlockSpec((bm, d), lambda b, i: (b * nb + i, 0)),
                  pl.BlockSpec((n_mem, d), lambda b, i: (b, 0)),
                  pl.BlockSpec((n_mem, d), lambda b, i: (b, 0))],
        out_specs=pl.BlockSpec((bm, d), lambda b, i: (b * nb + i, 0)),
        out_shape=jax.ShapeDtypeStruct((batch * seq, d), BF16),
        compiler_params=_cparams(2), name="cross_attention",
    )(q, k, v)


def _router_kernel(x_ref, w_ref, b_ref, id_ref, gate_ref):
    logits = jnp.dot(x_ref[...], w_ref[...], preferred_element_type=F32,
                     precision=lax.Precision.HIGHEST) + b_ref[...]
    lane = lax.broadcasted_iota(I32, logits.shape, 1)
    lane_f = lane.astype(F32)
    big = float(LANES)
    is_g = lane < N_GROUPS
    gl = jnp.where(is_g, logits, -jnp.inf)
    g_max = jnp.max(gl, axis=1, keepdims=True)
    g_idx = jnp.min(jnp.where(gl == g_max, lane_f, big), axis=1, keepdims=True)
    g_gate = 1.0 / jnp.sum(jnp.where(is_g, jnp.exp(gl - g_max), 0.0), axis=1, keepdims=True)
    e_lo = N_GROUPS + g_idx * EXPERTS_PER_GROUP
    in_grp = (lane_f >= e_lo) & (lane_f < e_lo + EXPERTS_PER_GROUP)
    el = jnp.where(in_grp, logits, -jnp.inf)
    v1 = jnp.max(el, axis=1, keepdims=True)
    i1 = jnp.min(jnp.where(el == v1, lane_f, big), axis=1, keepdims=True)
    el2 = jnp.where(lane_f == i1, -jnp.inf, el)
    v2 = jnp.max(el2, axis=1, keepdims=True)
    i2 = jnp.min(jnp.where(el2 == v2, lane_f, big), axis=1, keepdims=True)
    z = jnp.exp(v2 - v1)
    w1 = g_gate / (1.0 + z)
    w2 = g_gate * z / (1.0 + z)
    ids = jnp.where(lane == 0, i1 - N_GROUPS, jnp.where(lane == 1, i2 - N_GROUPS, 0.0))
    id_ref[...] = ids.astype(I32)
    gate_ref[...] = jnp.where(lane == 0, w1, jnp.where(lane == 1, w2, 0.0))


def _router(x, w_r, b_r, *, bm):
    m, d = x.shape
    out = pl.BlockSpec((bm, LANES), lambda i: (i, 0))
    return pl.pallas_call(
        _router_kernel, grid=(m // bm,),
        in_specs=[pl.BlockSpec((bm, d), lambda i: (i, 0)),
                  pl.BlockSpec((d, LANES), lambda i: (0, 0)),
                  pl.BlockSpec((1, LANES), lambda i: (0, 0))],
        out_specs=[out, out],
        out_shape=[jax.ShapeDtypeStruct((m, LANES), I32), jax.ShapeDtypeStruct((m, LANES), F32)],
        compiler_params=_cparams(1), name="router",
    )(x, w_r, b_r)


def _rank_kernel(id_ref, rank_ref, cnt_ref, carry_ref):
    i = pl.program_id(0)
    bt = id_ref.shape[0]

    @pl.when(i == 0)
    def _():
        carry_ref[...] = jnp.zeros(carry_ref.shape, F32)

    ids = id_ref[...]
    lane = lax.broadcasted_iota(I32, (bt, LANES), 1)
    e0 = ids[:, 0:1]
    e1 = ids[:, 1:2] + N_EXPERTS
    hit0 = lane == e0
    hit1 = lane == e1
    onehot = jnp.where(hit0, 1.0, jnp.where(hit1, 1.0, 0.0))
    r_io = lax.broadcasted_iota(I32, (bt, bt), 0)
    c_io = lax.broadcasted_iota(I32, (bt, bt), 1)
    tri = jnp.where(c_io < r_io, 1.0, 0.0).astype(BF16)
    prefix = _dot(tri, onehot.astype(BF16)) + carry_ref[0:1, :]
    rank0 = jnp.sum(jnp.where(hit0, prefix, 0.0), axis=1, keepdims=True)
    rank1 = jnp.sum(jnp.where(hit1, prefix, 0.0), axis=1, keepdims=True)
    rank_ref[...] = jnp.where(lane == 0, rank0, jnp.where(lane == 1, rank1, 0.0))
    total = carry_ref[0:1, :] + jnp.sum(onehot, axis=0, keepdims=True)
    carry_ref[...] = jnp.broadcast_to(total, carry_ref.shape)
    cnt_ref[...] = jnp.broadcast_to(total, cnt_ref.shape)


def _slot_ranks(ids, *, bt):
    m = ids.shape[0]
    return pl.pallas_call(
        _rank_kernel, grid=(m // bt,),
        in_specs=[pl.BlockSpec((bt, LANES), lambda i: (i, 0))],
        out_specs=[pl.BlockSpec((bt, LANES), lambda i: (i, 0)),
                   pl.BlockSpec((8, LANES), lambda i: (0, 0))],
        out_shape=[jax.ShapeDtypeStruct((m, LANES), F32), jax.ShapeDtypeStruct((8, LANES), F32)],
        scratch_shapes=[pltpu.VMEM((8, LANES), F32)],
        compiler_params=_cparams(1), name="slot_ranks",
    )(ids)


def _row_copy(src_ref, src_row, dst_ref, dst_row, sem):
    return pltpu.make_async_copy(src_ref.at[pl.ds(src_row, 1)], dst_ref.at[pl.ds(dst_row, 1)], sem)


def _dispatch_kernel(pos0_ref, pos1_ref, x_ref, xs_in_ref, xs_ref, sem):
    del xs_in_ref
    i = pl.program_id(0)
    bt = x_ref.shape[0]

    def issue(r, carry):
        t = i * bt + r
        _row_copy(x_ref, r, xs_ref, pos0_ref[t], sem.at[0]).start()
        _row_copy(x_ref, r, xs_ref, pos1_ref[t], sem.at[1]).start()
        return carry

    lax.fori_loop(0, bt, issue, 0)

    def drain(r, carry):
        _row_copy(x_ref, 0, xs_ref, 0, sem.at[0]).wait()
        _row_copy(x_ref, 0, xs_ref, 0, sem.at[1]).wait()
        return carry

    lax.fori_loop(0, bt, drain, 0)


def _dispatch(x, pos0, pos1, n_slots, *, bt):
    m, d = x.shape
    xs0 = jnp.zeros((n_slots, d), x.dtype)
    return pl.pallas_call(
        _dispatch_kernel,
        grid_spec=pltpu.PrefetchScalarGridSpec(
            num_scalar_prefetch=2, grid=(m // bt,),
            in_specs=[pl.BlockSpec((bt, d), lambda i, p0, p1: (i, 0)),
                      pl.BlockSpec(memory_space=pl.ANY)],
            out_specs=pl.BlockSpec(memory_space=pl.ANY),
            scratch_shapes=[pltpu.SemaphoreType.DMA((2,))]),
        out_shape=jax.ShapeDtypeStruct((n_slots, d), x.dtype),
        input_output_aliases={3: 0},
        compiler_params=_cparams(1), name="dispatch",
    )(pos0, pos1, x, xs0)


def _ffn_kernel(blk_e_ref, nused_ref, x_ref, wg_ref, wu_ref, wd_ref, y_ref):
    del blk_e_ref

    @pl.when(pl.program_id(0) < nused_ref[0])
    def _():
        x = x_ref[...].astype(BF16)
        gate = _dot(x, wg_ref[...])
        up = _dot(x, wu_ref[...])
        h = gate * (1.0 / (1.0 + jnp.exp(-gate))) * up
        y_ref[...] = _dot(h.astype(BF16), wd_ref[...])


def _expert_ffn(xs, blk_e, nused, wg, wu, wd):
    n_slots, d = xs.shape
    ff = wg.shape[2]
    nblk = n_slots // MOE_BLK

    def row_map(i, be, nu):
        return (jnp.minimum(i, nu[0] - 1), 0)

    return pl.pallas_call(
        _ffn_kernel,
        grid_spec=pltpu.PrefetchScalarGridSpec(
            num_scalar_prefetch=2, grid=(nblk,),
            in_specs=[pl.BlockSpec((MOE_BLK, d), row_map),
                      pl.BlockSpec((None, d, ff), lambda i, be, nu: (be[i], 0, 0)),
                      pl.BlockSpec((None, d, ff), lambda i, be, nu: (be[i], 0, 0)),
                      pl.BlockSpec((None, ff, d), lambda i, be, nu: (be[i], 0, 0))],
            out_specs=pl.BlockSpec((MOE_BLK, d), row_map)),
        out_shape=jax.ShapeDtypeStruct((n_slots, d), F32),
        input_output_aliases={2: 0},
        compiler_params=_cparams(1), name="expert_ffn",
    )(blk_e, nused, xs, wg, wu, wd)


def _combine_kernel(pos0_ref, pos1_ref, x_ref, gate_ref, g_ref, b_ref, ys_ref, o_ref, y0_ref, y1_ref, sem):
    i = pl.program_id(0)
    bt = x_ref.shape[0]

    def issue(r, carry):
        t = i * bt + r
        _row_copy(ys_ref, pos0_ref[t], y0_ref, r, sem.at[0]).start()
        _row_copy(ys_ref, pos1_ref[t], y1_ref, r, sem.at[1]).start()
        return carry

    lax.fori_loop(0, bt, issue, 0)

    def drain(r, carry):
        _row_copy(ys_ref, 0, y0_ref, 0, sem.at[0]).wait()
        _row_copy(ys_ref, 0, y1_ref, 0, sem.at[1]).wait()
        return carry

    lax.fori_loop(0, bt, drain, 0)
    gates = gate_ref[...]
    f = y0_ref[...] * gates[:, 0:1] + y1_ref[...] * gates[:, 1:2]
    o_ref[...] = _ln_rows(ALPHA * x_ref[...] + f, g_ref[...], b_ref[...])


def _combine(x, gates, ys, pos0, pos1, g, b, *, bt):
    m, d = x.shape
    row = pl.BlockSpec((bt, d), lambda i, p0, p1: (i, 0))
    vec = pl.BlockSpec((1, d), lambda i, p0, p1: (0, 0))
    return pl.pallas_call(
        _combine_kernel,
        grid_spec=pltpu.PrefetchScalarGridSpec(
            num_scalar_prefetch=2, grid=(m // bt,),
            in_specs=[row, pl.BlockSpec((bt, LANES), lambda i, p0, p1: (i, 0)), vec, vec,
                      pl.BlockSpec(memory_space=pl.ANY)],
            out_specs=row,
            scratch_shapes=[pltpu.VMEM((bt, d), F32), pltpu.VMEM((bt, d), F32),
                            pltpu.SemaphoreType.DMA((2,))]),
        out_shape=jax.ShapeDtypeStruct((m, d), F32),
        compiler_params=_cparams(1), name="combine_ln",
    )(pos0, pos1, x, gates, g.reshape(1, d), b.reshape(1, d), ys)


def _rope_tables(positions, n_rows):
    pos = positions.astype(F32).reshape(n_rows, 1)

    def cs(dim):
        inv = 1.0 / (ROPE_THETA ** (jnp.arange(0, dim, 2, dtype=F32) / dim))
        ang = pos * inv[None, :]
        return jnp.cos(ang), jnp.sin(ang)

    c_h, s_h = cs(HEAD_DIM)
    c_i, s_i = cs(IDX_DIM)
    z_i = jnp.zeros_like(s_i)
    cosf = jnp.concatenate([c_h, c_h], axis=1)
    sinf = jnp.concatenate([-s_h, s_h], axis=1)
    cos64 = jnp.concatenate([c_i, c_i, c_i, c_i], axis=1)
    sin_lo = jnp.concatenate([-s_i, z_i, -s_i, z_i], axis=1)
    sin_hi = jnp.concatenate([z_i, s_i, z_i, s_i], axis=1)
    return cosf, sinf, cos64, sin_lo, sin_hi


def kernel(x, mem, positions, w_in, pool_w, pool_scale, w_o, ln1_g, ln1_b, w_mq, w_mk, w_mv, w_mo, ln2_g, ln2_b,
           w_group_router, b_group_router, w_expert_router, b_expert_router, w_gate, w_up, w_down, ln3_g, ln3_b):
    batch, seq, d = x.shape
    n_mem = mem.shape[1]
    n = batch * seq
    bn_attn = N_KV_HEADS * HEAD_DIM
    tables = _rope_tables(positions, n)
    xf = x.reshape(n, d)
    n_slots = 2 * n + N_EXPERTS * MOE_BLK
    nblk = n_slots // MOE_BLK

    for l in range(w_in.shape[0]):
        x_bf = xf.astype(BF16)
        d_in = w_in.shape[2]
        d_in_pad = MIX_POOL + 8 * bn_attn
        w_in_bf = jnp.pad(w_in[l].astype(BF16), ((0, 0), (0, d_in_pad - d_in)))

        v_pool = _matmul([x_bf], [w_in_bf], name="inproj_pool", bm=1024, bn=1024, n=MIX_POOL)
        h2, k2, kw = _inproj_attn(x_bf, w_in_bf, tables, col_off_blocks=MIX_POOL // bn_attn, bm=1024, bn=bn_attn)
        a_pool = _pool_mixer(v_pool, pool_w[l].astype(BF16), pool_scale[l], batch=batch, seq=seq, ts=512)
        a_attn = _dsa_attention(h2, k2, kw, batch=batch, seq=seq, bn=bn_attn)
        w_o_bf = w_o[l].astype(BF16)
        res = (xf, (512, 1024), lambda i, j: (i, j))
        pre = _matmul([a_pool, a_attn], [w_o_bf[:MIX_POOL], w_o_bf[MIX_POOL:]], name="outproj", bm=512, bn=1024, n=d,
                      extras=(res,), epilogue=_residual_epilogue)
        x1, x1_bf = _layer_norm(pre, ln1_g[l], ln1_b[l], bm=256)

        mq_scale = (d // N_MEM_HEADS) ** -0.5

        def q_epilogue(acc, ex, outs):
            outs[0][...] = (acc * mq_scale).astype(outs[0].dtype)

        qm = _matmul([x1_bf], [w_mq[l].astype(BF16)], name="mem_q", bm=1024, bn=1024, n=d, out_dtype=BF16,
                     epilogue=q_epilogue)
        mem_bf = mem.reshape(batch * n_mem, d).astype(BF16)
        km = _matmul([mem_bf], [w_mk[l].astype(BF16)], name="mem_k", bm=batch * n_mem, bn=1024, n=d, out_dtype=BF16)
        vm = _matmul([mem_bf], [w_mv[l].astype(BF16)], name="mem_v", bm=batch * n_mem, bn=1024, n=d, out_dtype=BF16)
        om = _cross_attention(qm, km, vm, batch=batch, seq=seq, n_mem=n_mem, bm=512)
        res = (x1, (512, 1024), lambda i, j: (i, j))
        pre = _matmul([om], [w_mo[l].astype(BF16)], name="mem_o", bm=512, bn=1024, n=d, extras=(res,),
                      epilogue=_residual_epilogue)
        x2, _ = _layer_norm(pre, ln2_g[l], ln2_b[l], bm=256)

        w_r = jnp.pad(jnp.concatenate([w_group_router[l], w_expert_router[l]], axis=1),
                      ((0, 0), (0, LANES - N_GROUPS - N_EXPERTS)))
        b_r = jnp.pad(jnp.concatenate([b_group_router[l], b_expert_router[l]]),
                      (0, LANES - N_GROUPS - N_EXPERTS)).reshape(1, LANES)
        ids, gates = _router(x2, w_r, b_r, bm=512)
        ranks, totals = _slot_ranks(ids, bt=512)
        e0, e1 = ids[:, 0], ids[:, 1]
        c0 = totals[0, :N_EXPERTS].astype(I32)
        c1 = totals[0, N_EXPERTS:].astype(I32)
        padded = ((c0 + c1 + MOE_BLK - 1) // MOE_BLK) * MOE_BLK
        pend = jnp.cumsum(padded)
        pstart = pend - padded
        pos0 = pstart[e0] + ranks[:, 0].astype(I32)
        pos1 = pstart[e1] + c0[e1] + ranks[:, 1].astype(I32)
        nused = pend[-1] // MOE_BLK
        blk_i = jnp.minimum(jnp.arange(nblk, dtype=I32), nused - 1)
        blk_e = jnp.minimum(jnp.searchsorted(pend, blk_i * MOE_BLK, side='right'), N_EXPERTS - 1).astype(I32)

        xs = _dispatch(x2, pos0, pos1, n_slots, bt=256)
        ys = _expert_ffn(xs, blk_e, nused.reshape(1).astype(I32),
                         w_gate[l].astype(BF16), w_up[l].astype(BF16), w_down[l].astype(BF16))
        xf = _combine(x2, gates, ys, pos0, pos1, ln3_g[l], ln3_b[l], bt=256)
    return xf.reshape(batch, seq, d)
```

```python
import functools

import jax
import jax.numpy as jnp
from jax import lax
from jax.experimental import pallas as pl
from jax.experimental.pallas import tpu as pltpu

F32 = jnp.float32
BF16 = jnp.bfloat16
I32 = jnp.int32

MIX_POOL = 2048
N_POOL_GROUPS = 4
POOL_WINDOWS = (2, 4, 8, 16)
POOL_GW = MIX_POOL // N_POOL_GROUPS
HEAD_DIM = 128
N_Q_HEADS = 16
N_KV_HEADS = 4
Q_PER_KV = N_Q_HEADS // N_KV_HEADS
N_IDX_HEADS = 8
IDX_DIM = 64
TOPK_MAX = 256
ROPE_THETA = 10000.0
N_MEM_HEADS = 4
N_GROUPS = 8
EXPERTS_PER_GROUP = 8
N_EXPERTS = N_GROUPS * EXPERTS_PER_GROUP
LN_EPS = 1e-5
DEPTH = 1
ALPHA = (2.0 * DEPTH) ** 0.25

LANES = 128
QB = 128
LC = 512
MOE_BLK = 256
POOL_HALO = 16
NEG = -1e30
N_HALVINGS = 8
LOG2E = 1.4426950408889634
VMEM_LIMIT = 56 * 1024 * 1024


def _cparams(n_axes, vmem=VMEM_LIMIT):
    return pltpu.CompilerParams(dimension_semantics=("arbitrary",) * n_axes, vmem_limit_bytes=vmem)


def _dot(a, b):
    return jnp.dot(a, b, preferred_element_type=F32)


def _dot_nt(a, b):
    return lax.dot_general(a, b, (((1,), (1,)), ((), ())), preferred_element_type=F32)


def _dot_tn(a, b):
    return lax.dot_general(a, b, (((0,), (0,)), ((), ())), preferred_element_type=F32)


FOLD_ROWS = 64


def _fold_rows(x, reduce_fn):
    r, c = x.shape
    return reduce_fn(x.reshape(r // FOLD_ROWS, FOLD_ROWS, c), axis=0)


def _mm_kernel(*refs, n_lhs, n_extra, epilogue):
    lhs = refs[:n_lhs]
    ws = refs[n_lhs:2 * n_lhs]
    extras = refs[2 * n_lhs:2 * n_lhs + n_extra]
    outs = refs[2 * n_lhs + n_extra:]
    acc = _dot(lhs[0][...], ws[0][...])
    for l in range(1, n_lhs):
        acc = acc + _dot(lhs[l][...], ws[l][...])
    epilogue(acc, extras, outs)


def _matmul(lhs_list, w_list, *, name, bm, bn, n, w_col_off=0, extras=(), out_dtype=F32, epilogue=None):
    m = lhs_list[0].shape[0]
    grid = (n // bn, m // bm)
    in_specs = [pl.BlockSpec((bm, a.shape[1]), lambda j, i: (i, 0)) for a in lhs_list]
    in_specs += [pl.BlockSpec((a.shape[1], bn), (lambda j, i, l=l: (l, j + w_col_off)))
                 for l, a in enumerate(lhs_list)]
    in_specs += [pl.BlockSpec(blk, (lambda j, i, f=f: f(i, j))) for (_, blk, f) in extras]
    if epilogue is None:
        def epilogue(acc, ex, outs):
            outs[0][...] = acc.astype(outs[0].dtype)
    kern = functools.partial(_mm_kernel, n_lhs=len(lhs_list), n_extra=len(extras), epilogue=epilogue)
    return pl.pallas_call(
        kern, grid=grid, in_specs=in_specs,
        out_specs=pl.BlockSpec((bm, bn), lambda j, i: (i, j)),
        out_shape=jax.ShapeDtypeStruct((m, n), out_dtype),
        compiler_params=_cparams(2), name=name,
    )(*lhs_list, *w_list, *[e[0] for e in extras])


def _residual_epilogue(acc, ex, outs):
    outs[0][...] = acc + ALPHA * ex[0][...]


def _rope128(a, cosf, sinf):
    return a * cosf + pltpu.roll(a, 64, 1) * sinf


def _rope64(a, cos64, sin_lo, sin_hi):
    return a * cos64 + pltpu.roll(a, 96, 1) * sin_lo + pltpu.roll(a, 32, 1) * sin_hi


def _inproj_kernel(x_ref, w_ref, cosf_ref, sinf_ref, cos64_ref, sinlo_ref, sinhi_ref,
                   h_ref, k2_ref, kw_ref, acc_ref, *, q_scale, wi_scale):
    j = pl.program_id(1)
    acc_ref[...] = _dot(x_ref[...], w_ref[...])
    bn = acc_ref.shape[1]

    @pl.when(j < 5)
    def _():
        scale = jnp.where(j < 4, q_scale, 1.0).astype(F32)
        cosf = cosf_ref[...] * scale
        sinf = sinf_ref[...] * scale
        for c in range(bn // LANES):
            a = acc_ref[:, c * LANES:(c + 1) * LANES]
            h_ref[:, c * LANES:(c + 1) * LANES] = _rope128(a, cosf, sinf).astype(h_ref.dtype)

    @pl.when(j == 5)
    def _():
        h_ref[...] = acc_ref[...].astype(h_ref.dtype)

    @pl.when(j == 6)
    def _():
        for c in range(bn // LANES):
            a = acc_ref[:, c * LANES:(c + 1) * LANES]
            r = _rope64(a, cos64_ref[...], sinlo_ref[...], sinhi_ref[...])
            h_ref[:, c * LANES:(c + 1) * LANES] = r.astype(h_ref.dtype)

    @pl.when(j == 7)
    def _():
        a = acc_ref[:, 0:LANES]
        r = _rope64(a, cos64_ref[...], sinlo_ref[...], sinhi_ref[...])
        lane = lax.broadcasted_iota(I32, a.shape, 1)
        kw_ref[...] = jnp.where(lane < IDX_DIM, r, a * wi_scale)
        k2_ref[...] = jnp.where(lane < IDX_DIM, r, pltpu.roll(r, 64, 1)).astype(k2_ref.dtype)


def _inproj_attn(x_bf, w_bf, tables, *, col_off_blocks, bm, bn):
    m = x_bf.shape[0]
    n_tiles = 8
    q_scale = HEAD_DIM ** -0.5 * LOG2E
    wi_scale = (N_IDX_HEADS ** -0.5) * (IDX_DIM ** -0.5)
    tab_spec = pl.BlockSpec((bm, LANES), lambda i, j: (i, 0))
    kern = functools.partial(_inproj_kernel, q_scale=q_scale, wi_scale=wi_scale)
    return pl.pallas_call(
        kern, grid=(m // bm, n_tiles),
        in_specs=[pl.BlockSpec((bm, x_bf.shape[1]), lambda i, j: (i, 0)),
                  pl.BlockSpec((w_bf.shape[0], bn), lambda i, j: (0, j + col_off_blocks))] + [tab_spec] * 5,
        out_specs=[pl.BlockSpec((bm, bn), lambda i, j: (i, jnp.minimum(j, n_tiles - 2))),
                   pl.BlockSpec((bm, LANES), lambda i, j: (i, 0)),
                   pl.BlockSpec((bm, LANES), lambda i, j: (i, 0))],
        out_shape=[jax.ShapeDtypeStruct((m, (n_tiles - 1) * bn), BF16),
                   jax.ShapeDtypeStruct((m, LANES), BF16),
                   jax.ShapeDtypeStruct((m, LANES), F32)],
        scratch_shapes=[pltpu.VMEM((bm, bn), F32)],
        compiler_params=_cparams(2), name="inproj_attn",
    )(x_bf, w_bf, *tables)


def _pool_kernel(v_ref, pw_ref, ps_ref, o_ref, ext_ref):
    s = pl.program_id(1)
    ts = v_ref.shape[0]

    @pl.when(s == 0)
    def _():
        ext_ref[0:POOL_HALO, :] = jnp.zeros((POOL_HALO, ext_ref.shape[1]), F32)

    ext_ref[POOL_HALO:, :] = v_ref[...]
    t = s * ts + lax.broadcasted_iota(I32, (ts, 1), 0)
    for g, w in enumerate(POOL_WINDOWS):
        cols = slice(g * POOL_GW, (g + 1) * POOL_GW)
        e = ext_ref[:, cols]
        step = 1
        while step < w:
            e = e + pltpu.roll(e, step, 0)
            step *= 2
        win = e[POOL_HALO:, :]
        cnt = jnp.minimum(t + 1, w).astype(F32)
        pooled = win / cnt - v_ref[:, cols]
        mixed = _dot(pooled.astype(BF16), pw_ref[g])
        o_ref[:, cols] = (mixed * ps_ref[:, cols]).astype(o_ref.dtype)
    ext_ref[0:POOL_HALO, :] = v_ref[ts - POOL_HALO:, :]


def _pool_mixer(v_pool, pool_w_bf, pool_scale, *, batch, seq, ts):
    v3 = v_pool.reshape(batch, seq, MIX_POOL)
    out = pl.pallas_call(
        _pool_kernel, grid=(batch, seq // ts),
        in_specs=[pl.BlockSpec((None, ts, MIX_POOL), lambda b, s: (b, s, 0)),
                  pl.BlockSpec((N_POOL_GROUPS, POOL_GW, POOL_GW), lambda b, s: (0, 0, 0)),
                  pl.BlockSpec((1, MIX_POOL), lambda b, s: (0, 0))],
        out_specs=pl.BlockSpec((None, ts, MIX_POOL), lambda b, s: (b, s, 0)),
        out_shape=jax.ShapeDtypeStruct((batch, seq, MIX_POOL), BF16),
        scratch_shapes=[pltpu.VMEM((POOL_HALO + ts, MIX_POOL), F32)],
        compiler_params=_cparams(2), name="pool_mixer",
    )(v3, pool_w_bf, pool_scale.reshape(1, MIX_POOL))
    return out.reshape(batch * seq, MIX_POOL)


def _dsa_kernel(q_ref, k_ref, v_ref, k2_ref, qi_ref, kw_ref, o_ref,
                key_sc, bias_sc, xi_sc, qt_sc, m_sc, l_sc, acc_sc, s_sc, *, n_sel, seq):
    blk = pl.program_id(1)
    nchunk = blk // (LC // QB) + 1
    q_pos = blk * QB + lax.broadcasted_iota(I32, (1, QB), 1)
    k_iota = lax.broadcasted_iota(I32, (LC, 1), 0)
    sub = lax.broadcasted_iota(I32, (LANES, QB), 0)
    eye = jnp.where(sub == lax.broadcasted_iota(I32, (LANES, QB), 1), 1.0, 0.0).astype(BF16)

    def transposed(x):
        return _dot_nt(eye, x)

    for jj in range(N_IDX_HEADS // 2):
        xt = transposed(qi_ref[:, jj * LANES:(jj + 1) * LANES])
        xi_sc[jj, :, 0:QB] = jnp.where(sub < IDX_DIM, xt, 0.0).astype(BF16)
        xi_sc[jj, :, QB:2 * QB] = jnp.where(sub >= IDX_DIM, xt, 0.0).astype(BF16)
    for g in range(N_KV_HEADS):
        for r in range(Q_PER_KV):
            h = g * Q_PER_KV + r
            qt_sc[g, 0:HEAD_DIM, r * QB:(r + 1) * QB] = transposed(
                q_ref[:, h * HEAD_DIM:(h + 1) * HEAD_DIM]).astype(BF16)
            qt_sc[g, HEAD_DIM:, r * QB:(r + 1) * QB] = eye
    kw_t = kw_ref[...].T
    wi = [kw_t[IDX_DIM + h:IDX_DIM + h + 1, :] for h in range(N_IDX_HEADS)]

    inf = jnp.inf
    part = (FOLD_ROWS, QB)

    def score_chunk(c, carry):
        s_min, s_max = carry
        off = pl.multiple_of(c * LC, LC)
        k2c = k2_ref[pl.ds(off, LC), :]
        sc = jnp.zeros((LC, QB), F32)
        for jj in range(N_IDX_HEADS // 2):
            rel = jnp.maximum(_dot(k2c, xi_sc[jj]), 0.0)
            sc = sc + rel[:, 0:QB] * wi[2 * jj] + rel[:, QB:2 * QB] * wi[2 * jj + 1]
        causal = off + k_iota <= q_pos
        key_sc[pl.ds(off, LC), :] = jnp.where(causal, sc, -inf)
        s_min = jnp.minimum(s_min, _fold_rows(jnp.where(causal, sc, inf), jnp.min))
        s_max = jnp.maximum(s_max, _fold_rows(jnp.where(causal, sc, -inf), jnp.max))
        return s_min, s_max

    s_min, s_max = lax.fori_loop(0, nchunk, score_chunk, (jnp.full(part, inf, F32), jnp.full(part, -inf, F32)))

    def count(pred_fn):
        def body(c, acc):
            off = pl.multiple_of(c * LC, LC)
            m = pred_fn(key_sc[pl.ds(off, LC), :], off)
            return acc + _fold_rows(m, jnp.sum)
        acc = lax.fori_loop(0, nchunk, body, jnp.zeros(part, F32))
        return jnp.sum(acc, axis=0, keepdims=True)

    has_thr = q_pos + 1 >= int(n_sel)
    lo0 = jnp.where(has_thr, jnp.min(s_min, axis=0, keepdims=True), -inf)
    hi0 = jnp.where(has_thr, jnp.max(s_max, axis=0, keepdims=True), -inf)

    def midpoint(lo, hi):
        mid = 0.5 * (lo + hi)
        return jnp.where(mid <= lo, hi, mid)

    def halve(i, st):
        lo, hi = st
        mid = midpoint(lo, hi)
        enough = count(lambda sc, off: jnp.where(sc >= mid, 1.0, 0.0)) >= n_sel
        return jnp.where(enough, mid, lo), jnp.where(enough, hi, mid)

    lo0, hi0 = lax.fori_loop(0, N_HALVINGS, halve, (lo0, hi0))

    def open_rows(lo, hi):
        return jnp.max(jnp.where(lo < hi, 1.0, 0.0)) > 0.0

    def bisect_cond(st):
        it, lo, hi = st
        return open_rows(lo, hi) & (it < seq)

    def bisect_body(st):
        it, lo, hi = st
        mid = midpoint(lo, hi)

        def body(c, acc):
            cnt, up, dn = acc
            off = pl.multiple_of(c * LC, LC)
            sc = key_sc[pl.ds(off, LC), :]
            ge = sc >= mid
            cnt = cnt + _fold_rows(jnp.where(ge, 1.0, 0.0), jnp.sum)
            up = jnp.minimum(up, _fold_rows(jnp.where(ge, sc, inf), jnp.min))
            dn = jnp.maximum(dn, _fold_rows(jnp.where(ge, -inf, sc), jnp.max))
            return cnt, up, dn

        cnt, up, dn = lax.fori_loop(
            0, nchunk, body, (jnp.zeros(part, F32), jnp.full(part, inf, F32), jnp.full(part, -inf, F32)))
        enough = jnp.sum(cnt, axis=0, keepdims=True) >= n_sel
        is_open = lo < hi
        new_lo = jnp.where(is_open & enough, jnp.min(up, axis=0, keepdims=True), lo)
        new_hi = jnp.where(is_open & jnp.logical_not(enough), jnp.max(dn, axis=0, keepdims=True), hi)
        return it + 1, new_lo, new_hi

    _, thr, _ = lax.while_loop(bisect_cond, bisect_body, (jnp.int32(0), lo0, hi0))
    n_gt = count(lambda kk, off: jnp.where(kk > thr, 1.0, 0.0))
    n_eq = count(lambda kk, off: jnp.where(kk == thr, 1.0, 0.0))
    need = n_sel - n_gt

    def tie_search():
        def tbody(i, p):
            cand = p | jnp.left_shift(jnp.int32(1), (seq.bit_length() - 2) - i)
            cnt = count(lambda kk, off: jnp.where(kk == thr, jnp.where(off + k_iota < cand, 1.0, 0.0), 0.0))
            return jnp.where(cnt < need, cand, p)
        return lax.fori_loop(0, seq.bit_length() - 1, tbody, jnp.zeros((1, QB), I32))

    ambiguous = jnp.max(jnp.where(has_thr & (n_eq > need), 1.0, 0.0)) > 0.0
    tie_hi = lax.cond(ambiguous, tie_search, lambda: jnp.full((1, QB), seq, I32))
    tie_hi = jnp.where(has_thr, tie_hi, -1)

    def bias_chunk(c, carry):
        off = pl.multiple_of(c * LC, LC)
        kk = key_sc[pl.ds(off, LC), :]
        tie_ok = jnp.where(off + k_iota <= tie_hi, 0.0, NEG)
        bias = jnp.where(kk > thr, 0.0, jnp.where(kk == thr, tie_ok, NEG))
        bias_sc[pl.ds(off, LC), :] = bias.astype(BF16)
        return carry

    lax.fori_loop(0, nchunk, bias_chunk, 0)

    m_sc[...] = jnp.full(m_sc.shape, NEG, F32)
    l_sc[...] = jnp.zeros(l_sc.shape, F32)
    acc_sc[...] = jnp.zeros(acc_sc.shape, F32)

    def attn_chunk(c, carry):
        off = pl.multiple_of(c * LC, LC)
        bias = bias_sc[pl.ds(off, LC), :]

        def logits(g):
            kc = k_ref[pl.ds(off, LC), g * HEAD_DIM:(g + 1) * HEAD_DIM]
            return _dot(jnp.concatenate([kc, bias], axis=1), qt_sc[g])

        m_news = []
        for g in range(N_KV_HEADS):
            s = logits(g)
            s_sc[g] = s
            m_news.append(jnp.maximum(m_sc[g], jnp.max(_fold_rows(s, jnp.max), axis=0, keepdims=True)))
        for g in range(N_KV_HEADS):
            vc = v_ref[pl.ds(off, LC), g * HEAD_DIM:(g + 1) * HEAD_DIM]
            m_old = m_sc[g]
            m_new = m_news[g]
            alpha = jnp.exp2(m_old - m_new)
            p = jnp.exp2(s_sc[g] - m_new)
            l_sc[g] = alpha * l_sc[g] + jnp.sum(_fold_rows(p, jnp.sum), axis=0, keepdims=True)
            acc_sc[g] = alpha * acc_sc[g] + _dot_tn(vc, p.astype(BF16))
            m_sc[g] = m_new
        return carry

    lax.fori_loop(0, nchunk, attn_chunk, 0)

    for g in range(N_KV_HEADS):
        o_t = acc_sc[g] / l_sc[g]
        for r in range(Q_PER_KV):
            h = g * Q_PER_KV + r
            o_ref[:, h * HEAD_DIM:(h + 1) * HEAD_DIM] = o_t[:, r * QB:(r + 1) * QB].T.astype(o_ref.dtype)


def _dsa_attention(h2, k2, kw, *, batch, seq, bn):
    nb = seq // QB
    n_sel = min(TOPK_MAX, seq // 4)
    q_cols = N_Q_HEADS * HEAD_DIM
    kv_cols = N_KV_HEADS * HEAD_DIM
    assert bn == kv_cols and q_cols == 4 * bn and seq % LC == 0 and LC >= n_sel
    kern = functools.partial(_dsa_kernel, n_sel=float(n_sel), seq=seq)
    rows = Q_PER_KV * QB
    return pl.pallas_call(
        kern, grid=(batch, nb),
        in_specs=[pl.BlockSpec((QB, q_cols), lambda b, i: (b * nb + i, 0)),
                  pl.BlockSpec((seq, kv_cols), lambda b, i: (b, 4)),
                  pl.BlockSpec((seq, kv_cols), lambda b, i: (b, 5)),
                  pl.BlockSpec((seq, LANES), lambda b, i: (b, 0)),
                  pl.BlockSpec((QB, bn), lambda b, i: (b * nb + i, 6)),
                  pl.BlockSpec((QB, LANES), lambda b, i: (b * nb + i, 0))],
        out_specs=pl.BlockSpec((QB, q_cols), lambda b, i: (b * nb + i, 0)),
        out_shape=jax.ShapeDtypeStruct((batch * seq, q_cols), BF16),
        scratch_shapes=[pltpu.VMEM((seq, QB), F32),
                        pltpu.VMEM((seq, QB), BF16),
                        pltpu.VMEM((N_IDX_HEADS // 2, LANES, 2 * QB), BF16),
                        pltpu.VMEM((N_KV_HEADS, HEAD_DIM + QB, rows), BF16),
                        pltpu.VMEM((N_KV_HEADS, 1, rows), F32),
                        pltpu.VMEM((N_KV_HEADS, 1, rows), F32),
                        pltpu.VMEM((N_KV_HEADS, HEAD_DIM, rows), F32),
                        pltpu.VMEM((N_KV_HEADS, LC, rows), F32)],
        compiler_params=_cparams(2), name="dsa_attention",
    )(h2, h2, h2, k2, h2, kw)


def _ln_rows(x, g, b):
    mu = jnp.mean(x, axis=-1, keepdims=True)
    xc = x - mu
    var = jnp.mean(xc * xc, axis=-1, keepdims=True)
    return xc * lax.rsqrt(var + LN_EPS) * g + b


def _ln_kernel(x_ref, g_ref, b_ref, o_ref, obf_ref):
    y = _ln_rows(x_ref[...], g_ref[...], b_ref[...])
    o_ref[...] = y
    obf_ref[...] = y.astype(obf_ref.dtype)


def _layer_norm(x, g, b, *, bm):
    m, d = x.shape
    row = pl.BlockSpec((bm, d), lambda i: (i, 0))
    vec = pl.BlockSpec((1, d), lambda i: (0, 0))
    return pl.pallas_call(
        _ln_kernel, grid=(m // bm,), in_specs=[row, vec, vec], out_specs=[row, row],
        out_shape=[jax.ShapeDtypeStruct((m, d), F32), jax.ShapeDtypeStruct((m, d), BF16)],
        compiler_params=_cparams(1), name="layer_norm",
    )(x, g.reshape(1, d), b.reshape(1, d))


def _xattn_kernel(q_ref, k_ref, v_ref, o_ref):
    dh = q_ref.shape[1] // N_MEM_HEADS
    for h in range(N_MEM_HEADS):
        cols = slice(h * dh, (h + 1) * dh)
        s = _dot_nt(q_ref[:, cols], k_ref[:, cols])
        p = jnp.exp(s - jnp.max(s, axis=1, keepdims=True))
        o = _dot(p.astype(BF16), v_ref[:, cols]) / jnp.sum(p, axis=1, keepdims=True)
        o_ref[:, cols] = o.astype(o_ref.dtype)


def _cross_attention(q, k, v, *, batch, seq, n_mem, bm):
    d = q.shape[1]
    nb = seq // bm
    return pl.pallas_call(
        _xattn_kernel, grid=(batch, nb),
        in_specs=[pl.BlockSpec((bm, d), lambda b, i: (b * nb + i, 0)),
                  pl.BlockSpec((n_mem, d), lambda b, i: (b, 0)),
                  pl.BlockSpec((n_mem, d), lambda b, i: (b, 0))],
        out_specs=pl.BlockSpec((bm, d), lambda b, i: (b * nb + i, 0)),
        out_shape=jax.ShapeDtypeStruct((batch * seq, d), BF16),
        compiler_params=_cparams(2), name="cross_attention",
    )(q, k, v)


def _router_kernel(x_ref, w_ref, b_ref, id_ref, gate_ref):
    logits = jnp.dot(x_ref[...], w_ref[...], preferred_element_type=F32,
                     precision=lax.Precision.HIGHEST) + b_ref[...]
    lane = lax.broadcasted_iota(I32, logits.shape, 1)
    lane_f = lane.astype(F32)
    big = float(LANES)
    is_g = lane < N_GROUPS
    gl = jnp.where(is_g, logits, -jnp.inf)
    g_max = jnp.max(gl, axis=1, keepdims=True)
    g_idx = jnp.min(jnp.where(gl == g_max, lane_f, big), axis=1, keepdims=True)
    g_gate = 1.0 / jnp.sum(jnp.where(is_g, jnp.exp(gl - g_max), 0.0), axis=1, keepdims=True)
    e_lo = N_GROUPS + g_idx * EXPERTS_PER_GROUP
    in_grp = (lane_f >= e_lo) & (lane_f < e_lo + EXPERTS_PER_GROUP)
    el = jnp.where(in_grp, logits, -jnp.inf)
    v1 = jnp.max(el, axis=1, keepdims=True)
    i1 = jnp.min(jnp.where(el == v1, lane_f, big), axis=1, keepdims=True)
    el2 = jnp.where(lane_f == i1, -jnp.inf, el)
    v2 = jnp.max(el2, axis=1, keepdims=True)
    i2 = jnp.min(jnp.where(el2 == v2, lane_f, big), axis=1, keepdims=True)
    z = jnp.exp(v2 - v1)
    w1 = g_gate / (1.0 + z)
    w2 = g_gate * z / (1.0 + z)
    ids = jnp.where(lane == 0, i1 - N_GROUPS, jnp.where(lane == 1, i2 - N_GROUPS, 0.0))
    id_ref[...] = ids.astype(I32)
    gate_ref[...] = jnp.where(lane == 0, w1, jnp.where(lane == 1, w2, 0.0))


def _router(x, w_r, b_r, *, bm):
    m, d = x.shape
    out = pl.BlockSpec((bm, LANES), lambda i: (i, 0))
    return pl.pallas_call(
        _router_kernel, grid=(m // bm,),
        in_specs=[pl.BlockSpec((bm, d), lambda i: (i, 0)),
                  pl.BlockSpec((d, LANES), lambda i: (0, 0)),
                  pl.BlockSpec((1, LANES), lambda i: (0, 0))],
        out_specs=[out, out],
        out_shape=[jax.ShapeDtypeStruct((m, LANES), I32), jax.ShapeDtypeStruct((m, LANES), F32)],
        compiler_params=_cparams(1), name="router",
    )(x, w_r, b_r)


def _rank_kernel(id_ref, rank_ref, cnt_ref, carry_ref):
    i = pl.program_id(0)
    bt = id_ref.shape[0]

    @pl.when(i == 0)
    def _():
        carry_ref[...] = jnp.zeros(carry_ref.shape, F32)

    ids = id_ref[...]
    lane = lax.broadcasted_iota(I32, (bt, LANES), 1)
    e0 = ids[:, 0:1]
    e1 = ids[:, 1:2] + N_EXPERTS
    hit0 = lane == e0
    hit1 = lane == e1
    onehot = jnp.where(hit0, 1.0, jnp.where(hit1, 1.0, 0.0))
    r_io = lax.broadcasted_iota(I32, (bt, bt), 0)
    c_io = lax.broadcasted_iota(I32, (bt, bt), 1)
    tri = jnp.where(c_io < r_io, 1.0, 0.0).astype(BF16)
    prefix = _dot(tri, onehot.astype(BF16)) + carry_ref[0:1, :]
    rank0 = jnp.sum(jnp.where(hit0, prefix, 0.0), axis=1, keepdims=True)
    rank1 = jnp.sum(jnp.where(hit1, prefix, 0.0), axis=1, keepdims=True)
    rank_ref[...] = jnp.where(lane == 0, rank0, jnp.where(lane == 1, rank1, 0.0))
    total = carry_ref[0:1, :] + jnp.sum(onehot, axis=0, keepdims=True)
    carry_ref[...] = jnp.broadcast_to(total, carry_ref.shape)
    cnt_ref[...] = jnp.broadcast_to(total, cnt_ref.shape)


def _slot_ranks(ids, *, bt):
    m = ids.shape[0]
    return pl.pallas_call(
        _rank_kernel, grid=(m // bt,),
        in_specs=[pl.BlockSpec((bt, LANES), lambda i: (i, 0))],
        out_specs=[pl.BlockSpec((bt, LANES), lambda i: (i, 0)),
                   pl.BlockSpec((8, LANES), lambda i: (0, 0))],
        out_shape=[jax.ShapeDtypeStruct((m, LANES), F32), jax.ShapeDtypeStruct((8, LANES), F32)],
        scratch_shapes=[pltpu.VMEM((8, LANES), F32)],
        compiler_params=_cparams(1), name="slot_ranks",
    )(ids)


def _row_copy(src_ref, src_row, dst_ref, dst_row, sem):
    return pltpu.make_async_copy(src_ref.at[pl.ds(src_row, 1)], dst_ref.at[pl.ds(dst_row, 1)], sem)


def _dispatch_kernel(pos0_ref, pos1_ref, x_ref, xs_in_ref, xs_ref, sem):
    del xs_in_ref
    i = pl.program_id(0)
    bt = x_ref.shape[0]

    def issue(r, carry):
        t = i * bt + r
        _row_copy(x_ref, r, xs_ref, pos0_ref[t], sem.at[0]).start()
        _row_copy(x_ref, r, xs_ref, pos1_ref[t], sem.at[1]).start()
        return carry

    lax.fori_loop(0, bt, issue, 0)

    def drain(r, carry):
        _row_copy(x_ref, 0, xs_ref, 0, sem.at[0]).wait()
        _row_copy(x_ref, 0, xs_ref, 0, sem.at[1]).wait()
        return carry

    lax.fori_loop(0, bt, drain, 0)


def _dispatch(x, pos0, pos1, n_slots, *, bt):
    m, d = x.shape
    xs0 = jnp.zeros((n_slots, d), x.dtype)
    return pl.pallas_call(
        _dispatch_kernel,
        grid_spec=pltpu.PrefetchScalarGridSpec(
            num_scalar_prefetch=2, grid=(m // bt,),
            in_specs=[pl.BlockSpec((bt, d), lambda i, p0, p1: (i, 0)),
                      pl.BlockSpec(memory_space=pl.ANY)],
            out_specs=pl.BlockSpec(memory_space=pl.ANY),
            scratch_shapes=[pltpu.SemaphoreType.DMA((2,))]),
        out_shape=jax.ShapeDtypeStruct((n_slots, d), x.dtype),
        input_output_aliases={3: 0},
        compiler_params=_cparams(1), name="dispatch",
    )(pos0, pos1, x, xs0)


def _ffn_kernel(blk_e_ref, nused_ref, x_ref, wg_ref, wu_ref, wd_ref, y_ref):
    del blk_e_ref

    @pl.when(pl.program_id(0) < nused_ref[0])
    def _():
        x = x_ref[...].astype(BF16)
        gate = _dot(x, wg_ref[...])
        up = _dot(x, wu_ref[...])
        h = gate * (1.0 / (1.0 + jnp.exp(-gate))) * up
        y_ref[...] = _dot(h.astype(BF16), wd_ref[...])


def _expert_ffn(xs, blk_e, nused, wg, wu, wd):
    n_slots, d = xs.shape
    ff = wg.shape[2]
    nblk = n_slots // MOE_BLK

    def row_map(i, be, nu):
        return (jnp.minimum(i, nu[0] - 1), 0)

    return pl.pallas_call(
        _ffn_kernel,
        grid_spec=pltpu.PrefetchScalarGridSpec(
            num_scalar_prefetch=2, grid=(nblk,),
            in_specs=[pl.BlockSpec((MOE_BLK, d), row_map),
                      pl.BlockSpec((None, d, ff), lambda i, be, nu: (be[i], 0, 0)),
                      pl.BlockSpec((None, d, ff), lambda i, be, nu: (be[i], 0, 0)),
                      pl.BlockSpec((None, ff, d), lambda i, be, nu: (be[i], 0, 0))],
            out_specs=pl.BlockSpec((MOE_BLK, d), row_map)),
        out_shape=jax.ShapeDtypeStruct((n_slots, d), F32),
        input_output_aliases={2: 0},
        compiler_params=_cparams(1), name="expert_ffn",
    )(blk_e, nused, xs, wg, wu, wd)


def _combine_kernel(pos0_ref, pos1_ref, x_ref, gate_ref, g_ref, b_ref, ys_ref, o_ref, y0_ref, y1_ref, sem):
    i = pl.program_id(0)
    bt = x_ref.shape[0]

    def issue(r, carry):
        t = i * bt + r
        _row_copy(ys_ref, pos0_ref[t], y0_ref, r, sem.at[0]).start()
        _row_copy(ys_ref, pos1_ref[t], y1_ref, r, sem.at[1]).start()
        return carry

    lax.fori_loop(0, bt, issue, 0)

    def drain(r, carry):
        _row_copy(ys_ref, 0, y0_ref, 0, sem.at[0]).wait()
        _row_copy(ys_ref, 0, y1_ref, 0, sem.at[1]).wait()
        return carry

    lax.fori_loop(0, bt, drain, 0)
    gates = gate_ref[...]
    f = y0_ref[...] * gates[:, 0:1] + y1_ref[...] * gates[:, 1:2]
    o_ref[...] = _ln_rows(ALPHA * x_ref[...] + f, g_ref[...], b_ref[...])


def _combine(x, gates, ys, pos0, pos1, g, b, *, bt):
    m, d = x.shape
    row = pl.BlockSpec((bt, d), lambda i, p0, p1: (i, 0))
    vec = pl.BlockSpec((1, d), lambda i, p0, p1: (0, 0))
    return pl.pallas_call(
        _combine_kernel,
        grid_spec=pltpu.PrefetchScalarGridSpec(
            num_scalar_prefetch=2, grid=(m // bt,),
            in_specs=[row, pl.BlockSpec((bt, LANES), lambda i, p0, p1: (i, 0)), vec, vec,
                      pl.BlockSpec(memory_space=pl.ANY)],
            out_specs=row,
            scratch_shapes=[pltpu.VMEM((bt, d), F32), pltpu.VMEM((bt, d), F32),
                            pltpu.SemaphoreType.DMA((2,))]),
        out_shape=jax.ShapeDtypeStruct((m, d), F32),
        compiler_params=_cparams(1), name="combine_ln",
    )(pos0, pos1, x, gates, g.reshape(1, d), b.reshape(1, d), ys)


def _rope_tables(positions, n_rows):
    pos = positions.astype(F32).reshape(n_rows, 1)

    def cs(dim):
        inv = 1.0 / (ROPE_THETA ** (jnp.arange(0, dim, 2, dtype=F32) / dim))
        ang = pos * inv[None, :]
        return jnp.cos(ang), jnp.sin(ang)

    c_h, s_h = cs(HEAD_DIM)
    c_i, s_i = cs(IDX_DIM)
    z_i = jnp.zeros_like(s_i)
    cosf = jnp.concatenate([c_h, c_h], axis=1)
    sinf = jnp.concatenate([-s_h, s_h], axis=1)
    cos64 = jnp.concatenate([c_i, c_i, c_i, c_i], axis=1)
    sin_lo = jnp.concatenate([-s_i, z_i, -s_i, z_i], axis=1)
    sin_hi = jnp.concatenate([z_i, s_i, z_i, s_i], axis=1)
    return cosf, sinf, cos64, sin_lo, sin_hi


def kernel(x, mem, positions, w_in, pool_w, pool_scale, w_o, ln1_g, ln1_b, w_mq, w_mk, w_mv, w_mo, ln2_g, ln2_b,
           w_group_router, b_group_router, w_expert_router, b_expert_router, w_gate, w_up, w_down, ln3_g, ln3_b):
    batch, seq, d = x.shape
    n_mem = mem.shape[1]
    n = batch * seq
    bn_attn = N_KV_HEADS * HEAD_DIM
    tables = _rope_tables(positions, n)
    xf = x.reshape(n, d)
    n_slots = 2 * n + N_EXPERTS * MOE_BLK
    nblk = n_slots // MOE_BLK

    for l in range(w_in.shape[0]):
        x_bf = xf.astype(BF16)
        d_in = w_in.shape[2]
        d_in_pad = MIX_POOL + 8 * bn_attn
        w_in_bf = jnp.pad(w_in[l].astype(BF16), ((0, 0), (0, d_in_pad - d_in)))

        v_pool = _matmul([x_bf], [w_in_bf], name="inproj_pool", bm=1024, bn=1024, n=MIX_POOL)
        h2, k2, kw = _inproj_attn(x_bf, w_in_bf, tables, col_off_blocks=MIX_POOL // bn_attn, bm=1024, bn=bn_attn)
        a_pool = _pool_mixer(v_pool, pool_w[l].astype(BF16), pool_scale[l], batch=batch, seq=seq, ts=512)
        a_attn = _dsa_attention(h2, k2, kw, batch=batch, seq=seq, bn=bn_attn)
        w_o_bf = w_o[l].astype(BF16)
        res = (xf, (512, 1024), lambda i, j: (i, j))
        pre = _matmul([a_pool, a_attn], [w_o_bf, w_o_bf], name="outproj", bm=512, bn=1024, n=d,
                      extras=(res,), epilogue=_residual_epilogue)
        x1, x1_bf = _layer_norm(pre, ln1_g[l], ln1_b[l], bm=256)

        mq_scale = (d // N_MEM_HEADS) ** -0.5

        def q_epilogue(acc, ex, outs):
            outs[0][...] = (acc * mq_scale).astype(outs[0].dtype)

        qm = _matmul([x1_bf], [w_mq[l].astype(BF16)], name="mem_q", bm=1024, bn=1024, n=d, out_dtype=BF16,
                     epilogue=q_epilogue)
        mem_bf = mem.reshape(batch * n_mem, d).astype(BF16)
        km = _matmul([mem_bf], [w_mk[l].astype(BF16)], name="mem_k", bm=batch * n_mem, bn=1024, n=d, out_dtype=BF16)
        vm = _matmul([mem_bf], [w_mv[l].astype(BF16)], name="mem_v", bm=batch * n_mem, bn=1024, n=d, out_dtype=BF16)
        om = _cross_attention(qm, km, vm, batch=batch, seq=seq, n_mem=n_mem, bm=512)
        res = (x1, (512, 1024), lambda i, j: (i, j))
        pre = _matmul([om], [w_mo[l].astype(BF16)], name="mem_o", bm=512, bn=1024, n=d, extras=(res,),
                      epilogue=_residual_epilogue)
        x2, _ = _layer_norm(pre, ln2_g[l], ln2_b[l], bm=256)

        w_r = jnp.pad(jnp.concatenate([w_group_router[l], w_expert_router[l]], axis=1),
                      ((0, 0), (0, LANES - N_GROUPS - N_EXPERTS)))
        b_r = jnp.pad(jnp.concatenate([b_group_router[l], b_expert_router[l]]),
                      (0, LANES - N_GROUPS - N_EXPERTS)).reshape(1, LANES)
        ids, gates = _router(x2, w_r, b_r, bm=512)
        ranks, totals = _slot_ranks(ids, bt=512)
        e0, e1 = ids[:, 0], ids[:, 1]
        c0 = totals[0, :N_EXPERTS].astype(I32)
        c1 = totals[0, N_EXPERTS:].astype(I32)
        padded = ((c0 + c1 + MOE_BLK - 1) // MOE_BLK) * MOE_BLK
        pend = jnp.cumsum(padded)
        pstart = pend - padded
        pos0 = pstart[e0] + ranks[:, 0].astype(I32)
        pos1 = pstart[e1] + c0[e1] + ranks[:, 1].astype(I32)
        nused = pend[-1] // MOE_BLK
        blk_i = jnp.minimum(jnp.arange(nblk, dtype=I32), nused - 1)
        blk_e = jnp.sum((pend[None, :] <= (blk_i * MOE_BLK)[:, None]).astype(I32), axis=1)
        blk_e = jnp.minimum(blk_e, N_EXPERTS - 1)

        xs = _dispatch(x2, pos0, pos1, n_slots, bt=256)
        ys = _expert_ffn(xs, blk_e, nused.reshape(1).astype(I32),
                         w_gate[l].astype(BF16), w_up[l].astype(BF16), w_down[l].astype(BF16))
        xf = _combine(x2, gates, ys, pos0, pos1, ln3_g[l], ln3_b[l], bt=256)
    return xf.reshape(batch, seq, d)
```

```python
import functools

import jax
import jax.numpy as jnp
from jax import lax
from jax.experimental import pallas as pl
from jax.experimental.pallas import tpu as pltpu

F32 = jnp.float32
BF16 = jnp.bfloat16
I32 = jnp.int32

MIX_POOL = 2048
N_POOL_GROUPS = 4
POOL_WINDOWS = (2, 4, 8, 16)
POOL_GW = MIX_POOL // N_POOL_GROUPS
HEAD_DIM = 128
N_Q_HEADS = 16
N_KV_HEADS = 4
Q_PER_KV = N_Q_HEADS // N_KV_HEADS
N_IDX_HEADS = 8
IDX_DIM = 64
TOPK_MAX = 256
ROPE_THETA = 10000.0
N_MEM_HEADS = 4
N_GROUPS = 8
EXPERTS_PER_GROUP = 8
N_EXPERTS = N_GROUPS * EXPERTS_PER_GROUP
LN_EPS = 1e-5
DEPTH = 1
ALPHA = (2.0 * DEPTH) ** 0.25

LANES = 128
QB = 128
LC = 512
MOE_BLK = 256
W_GROUPS = 8
POOL_HALO = 16
NEG = -1e30
N_HALVINGS = 8
LOG2E = 1.4426950408889634
VMEM_LIMIT = 56 * 1024 * 1024


def _cparams(n_axes, vmem=VMEM_LIMIT):
    return pltpu.CompilerParams(dimension_semantics=("arbitrary",) * n_axes, vmem_limit_bytes=vmem)


def _dot(a, b):
    return jnp.dot(a, b, preferred_element_type=F32)


def _dot_nt(a, b):
    return lax.dot_general(a, b, (((1,), (1,)), ((), ())), preferred_element_type=F32)


def _dot_tn(a, b):
    return lax.dot_general(a, b, (((0,), (0,)), ((), ())), preferred_element_type=F32)


FOLD_ROWS = 64


def _fold_rows(x, reduce_fn):
    r, c = x.shape
    return reduce_fn(x.reshape(r // FOLD_ROWS, FOLD_ROWS, c), axis=0)


def _mm_kernel(*refs, n_lhs, n_extra, epilogue):
    lhs = refs[:n_lhs]
    ws = refs[n_lhs:2 * n_lhs]
    extras = refs[2 * n_lhs:2 * n_lhs + n_extra]
    outs = refs[2 * n_lhs + n_extra:]
    acc = _dot(lhs[0][...], ws[0][...])
    for l in range(1, n_lhs):
        acc = acc + _dot(lhs[l][...], ws[l][...])
    epilogue(acc, extras, outs)


def _matmul(lhs_list, w_list, *, name, bm, bn, n, w_col_off=0, extras=(), out_dtype=F32, epilogue=None):
    m = lhs_list[0].shape[0]
    grid = (n // bn, m // bm)
    in_specs = [pl.BlockSpec((bm, a.shape[1]), lambda j, i: (i, 0)) for a in lhs_list]
    in_specs += [pl.BlockSpec((a.shape[1], bn), (lambda j, i, l=l: (l, j + w_col_off)))
                 for l, a in enumerate(lhs_list)]
    in_specs += [pl.BlockSpec(blk, (lambda j, i, f=f: f(i, j))) for (_, blk, f) in extras]
    if epilogue is None:
        def epilogue(acc, ex, outs):
            outs[0][...] = acc.astype(outs[0].dtype)
    kern = functools.partial(_mm_kernel, n_lhs=len(lhs_list), n_extra=len(extras), epilogue=epilogue)
    return pl.pallas_call(
        kern, grid=grid, in_specs=in_specs,
        out_specs=pl.BlockSpec((bm, bn), lambda j, i: (i, j)),
        out_shape=jax.ShapeDtypeStruct((m, n), out_dtype),
        compiler_params=_cparams(2), name=name,
    )(*lhs_list, *w_list, *[e[0] for e in extras])


def _residual_epilogue(acc, ex, outs):
    outs[0][...] = acc + ALPHA * ex[0][...]


def _rope128(a, cosf, sinf):
    return a * cosf + pltpu.roll(a, 64, 1) * sinf


def _rope64(a, cos64, sin_lo, sin_hi):
    return a * cos64 + pltpu.roll(a, 96, 1) * sin_lo + pltpu.roll(a, 32, 1) * sin_hi


def _inproj_kernel(x_ref, w_ref, cosf_ref, sinf_ref, cos64_ref, sinlo_ref, sinhi_ref,
                   h_ref, k2_ref, kw_ref, acc_ref, *, q_scale, wi_scale):
    j = pl.program_id(1)
    acc_ref[...] = _dot(x_ref[...], w_ref[...])
    bn = acc_ref.shape[1]

    @pl.when(j < 5)
    def _():
        scale = jnp.where(j < 4, q_scale, 1.0).astype(F32)
        cosf = cosf_ref[...] * scale
        sinf = sinf_ref[...] * scale
        for c in range(bn // LANES):
            a = acc_ref[:, c * LANES:(c + 1) * LANES]
            h_ref[:, c * LANES:(c + 1) * LANES] = _rope128(a, cosf, sinf).astype(h_ref.dtype)

    @pl.when(j == 5)
    def _():
        h_ref[...] = acc_ref[...].astype(h_ref.dtype)

    @pl.when(j == 6)
    def _():
        for c in range(bn // LANES):
            a = acc_ref[:, c * LANES:(c + 1) * LANES]
            r = _rope64(a, cos64_ref[...], sinlo_ref[...], sinhi_ref[...])
            h_ref[:, c * LANES:(c + 1) * LANES] = r.astype(h_ref.dtype)

    @pl.when(j == 7)
    def _():
        a = acc_ref[:, 0:LANES]
        r = _rope64(a, cos64_ref[...], sinlo_ref[...], sinhi_ref[...])
        lane = lax.broadcasted_iota(I32, a.shape, 1)
        kw_ref[...] = jnp.where(lane < IDX_DIM, r, a * wi_scale)
        k2_ref[...] = jnp.where(lane < IDX_DIM, r, pltpu.roll(r, 64, 1)).astype(k2_ref.dtype)


def _inproj_attn(x_bf, w_bf, tables, *, col_off_blocks, bm, bn):
    m = x_bf.shape[0]
    n_tiles = 8
    q_scale = HEAD_DIM ** -0.5 * LOG2E
    wi_scale = (N_IDX_HEADS ** -0.5) * (IDX_DIM ** -0.5)
    tab_spec = pl.BlockSpec((bm, LANES), lambda i, j: (i, 0))
    kern = functools.partial(_inproj_kernel, q_scale=q_scale, wi_scale=wi_scale)
    return pl.pallas_call(
        kern, grid=(m // bm, n_tiles),
        in_specs=[pl.BlockSpec((bm, x_bf.shape[1]), lambda i, j: (i, 0)),
                  pl.BlockSpec((w_bf.shape[0], bn), lambda i, j: (0, j + col_off_blocks))] + [tab_spec] * 5,
        out_specs=[pl.BlockSpec((bm, bn), lambda i, j: (i, jnp.minimum(j, n_tiles - 2))),
                   pl.BlockSpec((bm, LANES), lambda i, j: (i, 0)),
                   pl.BlockSpec((bm, LANES), lambda i, j: (i, 0))],
        out_shape=[jax.ShapeDtypeStruct((m, (n_tiles - 1) * bn), BF16),
                   jax.ShapeDtypeStruct((m, LANES), BF16),
                   jax.ShapeDtypeStruct((m, LANES), F32)],
        scratch_shapes=[pltpu.VMEM((bm, bn), F32)],
        compiler_params=_cparams(2), name="inproj_attn",
    )(x_bf, w_bf, *tables)


def _pool_kernel(v_ref, pw_ref, ps_ref, o_ref, ext_ref):
    s = pl.program_id(1)
    ts = v_ref.shape[0]

    @pl.when(s == 0)
    def _():
        ext_ref[0:POOL_HALO, :] = jnp.zeros((POOL_HALO, ext_ref.shape[1]), F32)

    ext_ref[POOL_HALO:, :] = v_ref[...]
    t = s * ts + lax.broadcasted_iota(I32, (ts, 1), 0)
    for g, w in enumerate(POOL_WINDOWS):
        cols = slice(g * POOL_GW, (g + 1) * POOL_GW)
        e = ext_ref[:, cols]
        step = 1
        while step < w:
            e = e + pltpu.roll(e, step, 0)
            step *= 2
        win = e[POOL_HALO:, :]
        cnt = jnp.minimum(t + 1, w).astype(F32)
        pooled = win / cnt - v_ref[:, cols]
        mixed = _dot(pooled.astype(BF16), pw_ref[g])
        o_ref[:, cols] = (mixed * ps_ref[:, cols]).astype(o_ref.dtype)
    ext_ref[0:POOL_HALO, :] = v_ref[ts - POOL_HALO:, :]


def _pool_mixer(v_pool, pool_w_bf, pool_scale, *, batch, seq, ts):
    v3 = v_pool.reshape(batch, seq, MIX_POOL)
    out = pl.pallas_call(
        _pool_kernel, grid=(batch, seq // ts),
        in_specs=[pl.BlockSpec((None, ts, MIX_POOL), lambda b, s: (b, s, 0)),
                  pl.BlockSpec((N_POOL_GROUPS, POOL_GW, POOL_GW), lambda b, s: (0, 0, 0)),
                  pl.BlockSpec((1, MIX_POOL), lambda b, s: (0, 0))],
        out_specs=pl.BlockSpec((None, ts, MIX_POOL), lambda b, s: (b, s, 0)),
        out_shape=jax.ShapeDtypeStruct((batch, seq, MIX_POOL), BF16),
        scratch_shapes=[pltpu.VMEM((POOL_HALO + ts, MIX_POOL), F32)],
        compiler_params=_cparams(2), name="pool_mixer",
    )(v3, pool_w_bf, pool_scale.reshape(1, MIX_POOL))
    return out.reshape(batch * seq, MIX_POOL)


def _dsa_kernel(q_ref, k_ref, v_ref, k2_ref, qi_ref, kw_ref, o_ref,
                key_sc, bias_sc, xi_sc, qt_sc, m_sc, l_sc, acc_sc, s_sc, *, n_sel, seq):
    blk = pl.program_id(1)
    nchunk = blk // (LC // QB) + 1
    q_pos = blk * QB + lax.broadcasted_iota(I32, (1, QB), 1)
    k_iota = lax.broadcasted_iota(I32, (LC, 1), 0)
    sub = lax.broadcasted_iota(I32, (LANES, QB), 0)
    eye = jnp.where(sub == lax.broadcasted_iota(I32, (LANES, QB), 1), 1.0, 0.0).astype(BF16)

    def transposed(x):
        return _dot_nt(eye, x)

    for jj in range(N_IDX_HEADS // 2):
        xt = transposed(qi_ref[:, jj * LANES:(jj + 1) * LANES])
        xi_sc[jj, :, 0:QB] = jnp.where(sub < IDX_DIM, xt, 0.0).astype(BF16)
        xi_sc[jj, :, QB:2 * QB] = jnp.where(sub >= IDX_DIM, xt, 0.0).astype(BF16)
    for g in range(N_KV_HEADS):
        for r in range(Q_PER_KV):
            h = g * Q_PER_KV + r
            qt_sc[g, 0:HEAD_DIM, r * QB:(r + 1) * QB] = transposed(
                q_ref[:, h * HEAD_DIM:(h + 1) * HEAD_DIM]).astype(BF16)
            qt_sc[g, HEAD_DIM:, r * QB:(r + 1) * QB] = eye
    kw_t = kw_ref[...].T
    wi = [kw_t[IDX_DIM + h:IDX_DIM + h + 1, :] for h in range(N_IDX_HEADS)]

    inf = jnp.inf
    part = (FOLD_ROWS, QB)

    def score_chunk(c, carry):
        s_min, s_max = carry
        off = pl.multiple_of(c * LC, LC)
        k2c = k2_ref[pl.ds(off, LC), :]
        sc = jnp.zeros((LC, QB), F32)
        for jj in range(N_IDX_HEADS // 2):
            rel = jnp.maximum(_dot(k2c, xi_sc[jj]), 0.0)
            sc = sc + rel[:, 0:QB] * wi[2 * jj] + rel[:, QB:2 * QB] * wi[2 * jj + 1]
        causal = off + k_iota <= q_pos
        key_sc[pl.ds(off, LC), :] = jnp.where(causal, sc, -inf)
        s_min = jnp.minimum(s_min, _fold_rows(jnp.where(causal, sc, inf), jnp.min))
        s_max = jnp.maximum(s_max, _fold_rows(jnp.where(causal, sc, -inf), jnp.max))
        return s_min, s_max

    s_min, s_max = lax.fori_loop(0, nchunk, score_chunk, (jnp.full(part, inf, F32), jnp.full(part, -inf, F32)))

    def count(pred_fn):
        def body(c, acc):
            off = pl.multiple_of(c * LC, LC)
            m = pred_fn(key_sc[pl.ds(off, LC), :], off)
            return acc + _fold_rows(m, jnp.sum)
        acc = lax.fori_loop(0, nchunk, body, jnp.zeros(part, F32))
        return jnp.sum(acc, axis=0, keepdims=True)

    has_thr = q_pos + 1 >= int(n_sel)
    lo0 = jnp.where(has_thr, jnp.min(s_min, axis=0, keepdims=True), -inf)
    hi0 = jnp.where(has_thr, jnp.max(s_max, axis=0, keepdims=True), -inf)

    def midpoint(lo, hi):
        mid = 0.5 * (lo + hi)
        return jnp.where(mid <= lo, hi, mid)

    def halve(i, st):
        lo, hi = st
        mid = midpoint(lo, hi)
        enough = count(lambda sc, off: jnp.where(sc >= mid, 1.0, 0.0)) >= n_sel
        return jnp.where(enough, mid, lo), jnp.where(enough, hi, mid)

    lo0, hi0 = lax.fori_loop(0, N_HALVINGS, halve, (lo0, hi0))

    def open_rows(lo, hi):
        return jnp.max(jnp.where(lo < hi, 1.0, 0.0)) > 0.0

    def bisect_cond(st):
        it, lo, hi = st
        return open_rows(lo, hi) & (it < seq)

    def bisect_body(st):
        it, lo, hi = st
        mid = midpoint(lo, hi)

        def body(c, acc):
            cnt, up, dn = acc
            off = pl.multiple_of(c * LC, LC)
            sc = key_sc[pl.ds(off, LC), :]
            ge = sc >= mid
            cnt = cnt + _fold_rows(jnp.where(ge, 1.0, 0.0), jnp.sum)
            up = jnp.minimum(up, _fold_rows(jnp.where(ge, sc, inf), jnp.min))
            dn = jnp.maximum(dn, _fold_rows(jnp.where(ge, -inf, sc), jnp.max))
            return cnt, up, dn

        cnt, up, dn = lax.fori_loop(
            0, nchunk, body, (jnp.zeros(part, F32), jnp.full(part, inf, F32), jnp.full(part, -inf, F32)))
        enough = jnp.sum(cnt, axis=0, keepdims=True) >= n_sel
        is_open = lo < hi
        new_lo = jnp.where(is_open & enough, jnp.min(up, axis=0, keepdims=True), lo)
        new_hi = jnp.where(is_open & jnp.logical_not(enough), jnp.max(dn, axis=0, keepdims=True), hi)
        return it + 1, new_lo, new_hi

    _, thr, _ = lax.while_loop(bisect_cond, bisect_body, (jnp.int32(0), lo0, hi0))
    n_gt = count(lambda kk, off: jnp.where(kk > thr, 1.0, 0.0))
    n_eq = count(lambda kk, off: jnp.where(kk == thr, 1.0, 0.0))
    need = n_sel - n_gt

    def tie_search():
        def tbody(i, p):
            cand = p | jnp.left_shift(jnp.int32(1), (seq.bit_length() - 2) - i)
            cnt = count(lambda kk, off: jnp.where(kk == thr, jnp.where(off + k_iota < cand, 1.0, 0.0), 0.0))
            return jnp.where(cnt < need, cand, p)
        return lax.fori_loop(0, seq.bit_length() - 1, tbody, jnp.zeros((1, QB), I32))

    ambiguous = jnp.max(jnp.where(has_thr & (n_eq > need), 1.0, 0.0)) > 0.0
    tie_hi = lax.cond(ambiguous, tie_search, lambda: jnp.full((1, QB), seq, I32))
    tie_hi = jnp.where(has_thr, tie_hi, -1)

    def bias_chunk(c, carry):
        off = pl.multiple_of(c * LC, LC)
        kk = key_sc[pl.ds(off, LC), :]
        tie_ok = jnp.where(off + k_iota <= tie_hi, 0.0, NEG)
        bias = jnp.where(kk > thr, 0.0, jnp.where(kk == thr, tie_ok, NEG))
        bias_sc[pl.ds(off, LC), :] = bias.astype(BF16)
        return carry

    lax.fori_loop(0, nchunk, bias_chunk, 0)

    m_sc[...] = jnp.full(m_sc.shape, NEG, F32)
    l_sc[...] = jnp.zeros(l_sc.shape, F32)
    acc_sc[...] = jnp.zeros(acc_sc.shape, F32)

    def attn_chunk(c, carry):
        off = pl.multiple_of(c * LC, LC)
        bias = bias_sc[pl.ds(off, LC), :]

        def logits(g):
            kc = k_ref[pl.ds(off, LC), g * HEAD_DIM:(g + 1) * HEAD_DIM]
            return _dot(jnp.concatenate([kc, bias], axis=1), qt_sc[g])

        m_news = []
        for g in range(N_KV_HEADS):
            s = logits(g)
            s_sc[g] = s
            m_news.append(jnp.maximum(m_sc[g], jnp.max(_fold_rows(s, jnp.max), axis=0, keepdims=True)))
        for g in range(N_KV_HEADS):
            vc = v_ref[pl.ds(off, LC), g * HEAD_DIM:(g + 1) * HEAD_DIM]
            m_old = m_sc[g]
            m_new = m_news[g]
            alpha = jnp.exp2(m_old - m_new)
            p = jnp.exp2(s_sc[g] - m_new)
            l_sc[g] = alpha * l_sc[g] + jnp.sum(_fold_rows(p, jnp.sum), axis=0, keepdims=True)
            acc_sc[g] = alpha * acc_sc[g] + _dot_tn(vc, p.astype(BF16))
            m_sc[g] = m_new
        return carry

    lax.fori_loop(0, nchunk, attn_chunk, 0)

    for g in range(N_KV_HEADS):
        o_t = acc_sc[g] / l_sc[g]
        for r in range(Q_PER_KV):
            h = g * Q_PER_KV + r
            o_ref[:, h * HEAD_DIM:(h + 1) * HEAD_DIM] = o_t[:, r * QB:(r + 1) * QB].T.astype(o_ref.dtype)


def _dsa_attention(h2, k2, kw, *, batch, seq, bn):
    nb = seq // QB
    n_sel = min(TOPK_MAX, seq // 4)
    q_cols = N_Q_HEADS * HEAD_DIM
    kv_cols = N_KV_HEADS * HEAD_DIM
    assert bn == kv_cols and q_cols == 4 * bn and seq % LC == 0 and LC >= n_sel
    kern = functools.partial(_dsa_kernel, n_sel=float(n_sel), seq=seq)
    rows = Q_PER_KV * QB
    return pl.pallas_call(
        kern, grid=(batch, nb),
        in_specs=[pl.BlockSpec((QB, q_cols), lambda b, i: (b * nb + i, 0)),
                  pl.BlockSpec((seq, kv_cols), lambda b, i: (b, 4)),
                  pl.BlockSpec((seq, kv_cols), lambda b, i: (b, 5)),
                  pl.BlockSpec((seq, LANES), lambda b, i: (b, 0)),
                  pl.BlockSpec((QB, bn), lambda b, i: (b * nb + i, 6)),
                  pl.BlockSpec((QB, LANES), lambda b, i: (b * nb + i, 0))],
        out_specs=pl.BlockSpec((QB, q_cols), lambda b, i: (b * nb + i, 0)),
        out_shape=jax.ShapeDtypeStruct((batch * seq, q_cols), BF16),
        scratch_shapes=[pltpu.VMEM((seq, QB), F32),
                        pltpu.VMEM((seq, QB), BF16),
                        pltpu.VMEM((N_IDX_HEADS // 2, LANES, 2 * QB), BF16),
                        pltpu.VMEM((N_KV_HEADS, HEAD_DIM + QB, rows), BF16),
                        pltpu.VMEM((N_KV_HEADS, 1, rows), F32),
                        pltpu.VMEM((N_KV_HEADS, 1, rows), F32),
                        pltpu.VMEM((N_KV_HEADS, HEAD_DIM, rows), F32),
                        pltpu.VMEM((N_KV_HEADS, LC, rows), F32)],
        compiler_params=_cparams(2), name="dsa_attention",
    )(h2, h2, h2, k2, h2, kw)


def _ln_rows(x, g, b):
    mu = jnp.mean(x, axis=-1, keepdims=True)
    xc = x - mu
    var = jnp.mean(xc * xc, axis=-1, keepdims=True)
    return xc * lax.rsqrt(var + LN_EPS) * g + b


def _ln_kernel(x_ref, g_ref, b_ref, o_ref, obf_ref):
    y = _ln_rows(x_ref[...], g_ref[...], b_ref[...])
    o_ref[...] = y
    obf_ref[...] = y.astype(obf_ref.dtype)


def _layer_norm(x, g, b, *, bm):
    m, d = x.shape
    row = pl.BlockSpec((bm, d), lambda i: (i, 0))
    vec = pl.BlockSpec((1, d), lambda i: (0, 0))
    return pl.pallas_call(
        _ln_kernel, grid=(m // bm,), in_specs=[row, vec, vec], out_specs=[row, row],
        out_shape=[jax.ShapeDtypeStruct((m, d), F32), jax.ShapeDtypeStruct((m, d), BF16)],
        compiler_params=_cparams(1), name="layer_norm",
    )(x, g.reshape(1, d), b.reshape(1, d))


def _xattn_kernel(q_ref, k_ref, v_ref, o_ref):
    dh = q_ref.shape[1] // N_MEM_HEADS
    for h in range(N_MEM_HEADS):
        cols = slice(h * dh, (h + 1) * dh)
        s = _dot_nt(q_ref[:, cols], k_ref[:, cols])
        p = jnp.exp(s - jnp.max(s, axis=1, keepdims=True))
        o = _dot(p.astype(BF16), v_ref[:, cols]) / jnp.sum(p, axis=1, keepdims=True)
        o_ref[:, cols] = o.astype(o_ref.dtype)


def _cross_attention(q, k, v, *, batch, seq, n_mem, bm):
    d = q.shape[1]
    nb = seq // bm
    return pl.pallas_call(
        _xattn_kernel, grid=(batch, nb),
        in_specs=[pl.BlockSpec((bm, d), lambda b, i: (b * nb + i, 0)),
                  pl.BlockSpec((n_mem, d), lambda b, i: (b, 0)),
                  pl.BlockSpec((n_mem, d), lambda b, i: (b, 0))],
        out_specs=pl.BlockSpec((bm, d), lambda b, i: (b * nb + i, 0)),
        out_shape=jax.ShapeDtypeStruct((batch * seq, d), BF16),
        compiler_params=_cparams(2), name="cross_attention",
    )(q, k, v)


def _router_kernel(x_ref, w_ref, b_ref, id_ref, gate_ref):
    logits = jnp.dot(x_ref[...], w_ref[...], preferred_element_type=F32,
                     precision=lax.Precision.HIGHEST) + b_ref[...]
    lane = lax.broadcasted_iota(I32, logits.shape, 1)
    lane_f = lane.astype(F32)
    big = float(LANES)
    is_g = lane < N_GROUPS
    gl = jnp.where(is_g, logits, -jnp.inf)
    g_max = jnp.max(gl, axis=1, keepdims=True)
    g_idx = jnp.min(jnp.where(gl == g_max, lane_f, big), axis=1, keepdims=True)
    g_gate = 1.0 / jnp.sum(jnp.where(is_g, jnp.exp(gl - g_max), 0.0), axis=1, keepdims=True)
    e_lo = N_GROUPS + g_idx * EXPERTS_PER_GROUP
    in_grp = (lane_f >= e_lo) & (lane_f < e_lo + EXPERTS_PER_GROUP)
    el = jnp.where(in_grp, logits, -jnp.inf)
    v1 = jnp.max(el, axis=1, keepdims=True)
    i1 = jnp.min(jnp.where(el == v1, lane_f, big), axis=1, keepdims=True)
    el2 = jnp.where(lane_f == i1, -jnp.inf, el)
    v2 = jnp.max(el2, axis=1, keepdims=True)
    i2 = jnp.min(jnp.where(el2 == v2, lane_f, big), axis=1, keepdims=True)
    z = jnp.exp(v2 - v1)
    w1 = g_gate / (1.0 + z)
    w2 = g_gate * z / (1.0 + z)
    ids = jnp.where(lane == 0, i1 - N_GROUPS, jnp.where(lane == 1, i2 - N_GROUPS, 0.0))
    id_ref[...] = ids.astype(I32)
    gate_ref[...] = jnp.where(lane == 0, w1, jnp.where(lane == 1, w2, 0.0))


def _router(x, w_r, b_r, *, bm):
    m, d = x.shape
    out = pl.BlockSpec((bm, LANES), lambda i: (i, 0))
    return pl.pallas_call(
        _router_kernel, grid=(m // bm,),
        in_specs=[pl.BlockSpec((bm, d), lambda i: (i, 0)),
                  pl.BlockSpec((d, LANES), lambda i: (0, 0)),
                  pl.BlockSpec((1, LANES), lambda i: (0, 0))],
        out_specs=[out, out],
        out_shape=[jax.ShapeDtypeStruct((m, LANES), I32), jax.ShapeDtypeStruct((m, LANES), F32)],
        compiler_params=_cparams(1), name="router",
    )(x, w_r, b_r)


def _rank_kernel(id_ref, rank_ref, cnt_ref, carry_ref):
    i = pl.program_id(0)
    bt = id_ref.shape[0]

    @pl.when(i == 0)
    def _():
        carry_ref[...] = jnp.zeros(carry_ref.shape, F32)

    ids = id_ref[...]
    lane = lax.broadcasted_iota(I32, (bt, LANES), 1)
    e0 = ids[:, 0:1]
    e1 = ids[:, 1:2] + N_EXPERTS
    hit0 = lane == e0
    hit1 = lane == e1
    onehot = jnp.where(hit0, 1.0, jnp.where(hit1, 1.0, 0.0))
    r_io = lax.broadcasted_iota(I32, (bt, bt), 0)
    c_io = lax.broadcasted_iota(I32, (bt, bt), 1)
    tri = jnp.where(c_io < r_io, 1.0, 0.0).astype(BF16)
    prefix = _dot(tri, onehot.astype(BF16)) + carry_ref[0:1, :]
    rank0 = jnp.sum(jnp.where(hit0, prefix, 0.0), axis=1, keepdims=True)
    rank1 = jnp.sum(jnp.where(hit1, prefix, 0.0), axis=1, keepdims=True)
    rank_ref[...] = jnp.where(lane == 0, rank0, jnp.where(lane == 1, rank1, 0.0))
    total = carry_ref[0:1, :] + jnp.sum(onehot, axis=0, keepdims=True)
    carry_ref[...] = jnp.broadcast_to(total, carry_ref.shape)
    cnt_ref[...] = jnp.broadcast_to(total, cnt_ref.shape)


def _slot_ranks(ids, *, bt):
    m = ids.shape[0]
    return pl.pallas_call(
        _rank_kernel, grid=(m // bt,),
        in_specs=[pl.BlockSpec((bt, LANES), lambda i: (i, 0))],
        out_specs=[pl.BlockSpec((bt, LANES), lambda i: (i, 0)),
                   pl.BlockSpec((8, LANES), lambda i: (0, 0))],
        out_shape=[jax.ShapeDtypeStruct((m, LANES), F32), jax.ShapeDtypeStruct((8, LANES), F32)],
        scratch_shapes=[pltpu.VMEM((8, LANES), F32)],
        compiler_params=_cparams(1), name="slot_ranks",
    )(ids)


def _pos_kernel(id_ref, rank_ref, tab_ref, pos_ref):
    ids = id_ref[...]
    ranks = rank_ref[...]
    lane = lax.broadcasted_iota(I32, ids.shape, 1)
    tab = tab_ref[...]
    base0 = jnp.sum(jnp.where(lane == ids[:, 0:1], tab, 0.0), axis=1, keepdims=True)
    base1 = jnp.sum(jnp.where(lane == ids[:, 1:2] + N_EXPERTS, tab, 0.0), axis=1, keepdims=True)
    pos = jnp.where(lane == 0, base0 + ranks[:, 0:1], jnp.where(lane == 1, base1 + ranks[:, 1:2], 0.0))
    pos_ref[...] = pos.astype(I32)


def _slot_positions(ids, ranks, table, *, bt):
    m = ids.shape[0]
    blk = pl.BlockSpec((bt, LANES), lambda i: (i, 0))
    return pl.pallas_call(
        _pos_kernel, grid=(m // bt,),
        in_specs=[blk, blk, pl.BlockSpec((1, LANES), lambda i: (0, 0))],
        out_specs=blk,
        out_shape=jax.ShapeDtypeStruct((m, LANES), I32),
        compiler_params=_cparams(1), name="slot_positions",
    )(ids, ranks, table)


def _row_copy(src_ref, src_row, dst_ref, dst_row, sem):
    return pltpu.make_async_copy(src_ref.at[pl.ds(src_row, 1)], dst_ref.at[pl.ds(dst_row, 1)], sem)


def _dispatch_kernel(pos0_ref, pos1_ref, x_ref, xs_in_ref, xs_ref, sem):
    del xs_in_ref
    i = pl.program_id(0)
    bt = x_ref.shape[0]

    def issue(r, carry):
        t = i * bt + r
        _row_copy(x_ref, r, xs_ref, pos0_ref[t], sem.at[0]).start()
        _row_copy(x_ref, r, xs_ref, pos1_ref[t], sem.at[1]).start()
        return carry

    lax.fori_loop(0, bt, issue, 0)

    def drain(r, carry):
        _row_copy(x_ref, 0, xs_ref, 0, sem.at[0]).wait()
        _row_copy(x_ref, 0, xs_ref, 0, sem.at[1]).wait()
        return carry

    lax.fori_loop(0, bt, drain, 0)


def _dispatch(x, pos0, pos1, n_slots, *, bt):
    m, d = x.shape
    xs0 = jnp.zeros((n_slots, d), x.dtype)
    return pl.pallas_call(
        _dispatch_kernel,
        grid_spec=pltpu.PrefetchScalarGridSpec(
            num_scalar_prefetch=2, grid=(m // bt,),
            in_specs=[pl.BlockSpec((bt, d), lambda i, p0, p1: (i, 0)),
                      pl.BlockSpec(memory_space=pl.ANY)],
            out_specs=pl.BlockSpec(memory_space=pl.ANY),
            scratch_shapes=[pltpu.SemaphoreType.DMA((2,))]),
        out_shape=jax.ShapeDtypeStruct((n_slots, d), x.dtype),
        input_output_aliases={3: 0},
        compiler_params=_cparams(1), name="dispatch",
    )(pos0, pos1, x, xs0)


def _ffn_kernel(blk_e_ref, nxt_e_ref, set_ref, nused_ref, x_ref, wg_hbm, wu_hbm, wd_hbm, y_ref,
                wg_bf, wu_bf, wd_bf, st_g, st_u, st_d, sem):
    i = pl.program_id(0)
    d, ff = wg_bf.shape[1], wg_bf.shape[2]
    rg, rd = d // W_GROUPS, ff // W_GROUPS

    def group_copies(e, g):
        slot = g % 2
        return (pltpu.make_async_copy(wg_hbm.at[e, pl.ds(g * rg, rg), :], st_g.at[slot], sem.at[0, slot]),
                pltpu.make_async_copy(wu_hbm.at[e, pl.ds(g * rg, rg), :], st_u.at[slot], sem.at[1, slot]),
                pltpu.make_async_copy(wd_hbm.at[e, pl.ds(g * rd, rd), :], st_d.at[slot], sem.at[2, slot]))

    def start(e, g):
        for c in group_copies(e, g):
            c.start()

    def finish(e, g, s):
        for c in group_copies(e, g):
            c.wait()
        slot = g % 2
        wg_bf[s, g * rg:(g + 1) * rg, :] = st_g[slot].astype(BF16)
        wu_bf[s, g * rg:(g + 1) * rg, :] = st_u[slot].astype(BF16)
        wd_bf[s, g * rd:(g + 1) * rd, :] = st_d[slot].astype(BF16)

    def load_expert(e, s, between=()):
        start(e, 0)
        start(e, 1)
        for g in range(W_GROUPS):
            if g < len(between):
                between[g]()
            finish(e, g, s)
            if g + 2 < W_GROUPS:
                start(e, g + 2)

    e = blk_e_ref[i]
    s = set_ref[i]
    nxt = nxt_e_ref[i]
    first_of_expert = jnp.logical_or(i == 0, blk_e_ref[jnp.maximum(i - 1, 0)] != e)
    active = i < nused_ref[0]
    prefetch = jnp.logical_and(jnp.logical_and(active, first_of_expert), nxt >= 0)

    @pl.when(i == 0)
    def _():
        load_expert(e, s)

    def ffn_steps():
        vals = {}

        def gate_step():
            vals["x"] = x_ref[...].astype(BF16)
            vals["gate"] = _dot(vals["x"], wg_bf[s])

        def up_step():
            gate = vals["gate"]
            vals["h"] = (gate * (1.0 / (1.0 + jnp.exp(-gate))) * _dot(vals["x"], wu_bf[s])).astype(BF16)

        def down_step():
            y_ref[...] = _dot(vals["h"], wd_bf[s])

        return gate_step, up_step, down_step

    @pl.when(prefetch)
    def _():
        load_expert(nxt, 1 - s, between=ffn_steps())

    @pl.when(jnp.logical_and(active, jnp.logical_not(prefetch)))
    def _():
        for step in ffn_steps():
            step()


def _expert_ffn(xs, blk_e, nxt_e, set_idx, nused, wg, wu, wd):
    n_slots, d = xs.shape
    ff = wg.shape[2]
    nblk = n_slots // MOE_BLK
    assert d % W_GROUPS == 0 and ff % W_GROUPS == 0 and W_GROUPS >= 3

    def row_map(i, be, nx, si, nu):
        return (jnp.minimum(i, nu[0] - 1), 0)

    hbm = pl.BlockSpec(memory_space=pl.ANY)
    return pl.pallas_call(
        _ffn_kernel,
        grid_spec=pltpu.PrefetchScalarGridSpec(
            num_scalar_prefetch=4, grid=(nblk,),
            in_specs=[pl.BlockSpec((MOE_BLK, d), row_map), hbm, hbm, hbm],
            out_specs=pl.BlockSpec((MOE_BLK, d), row_map),
            scratch_shapes=[pltpu.VMEM((2, d, ff), BF16), pltpu.VMEM((2, d, ff), BF16), pltpu.VMEM((2, ff, d), BF16),
                            pltpu.VMEM((2, d // W_GROUPS, ff), F32), pltpu.VMEM((2, d // W_GROUPS, ff), F32),
                            pltpu.VMEM((2, ff // W_GROUPS, d), F32),
                            pltpu.SemaphoreType.DMA((3, 2))]),
        out_shape=jax.ShapeDtypeStruct((n_slots, d), F32),
        input_output_aliases={4: 0},
        compiler_params=_cparams(1, vmem=58 * 1024 * 1024), name="expert_ffn",
    )(blk_e, nxt_e, set_idx, nused, xs, wg, wu, wd)


def _combine_kernel(pos0_ref, pos1_ref, x_ref, gate_ref, g_ref, b_ref, ys_ref, o_ref, y0_ref, y1_ref, sem):
    i = pl.program_id(0)
    bt = x_ref.shape[0]

    def issue(r, carry):
        t = i * bt + r
        _row_copy(ys_ref, pos0_ref[t], y0_ref, r, sem.at[0]).start()
        _row_copy(ys_ref, pos1_ref[t], y1_ref, r, sem.at[1]).start()
        return carry

    lax.fori_loop(0, bt, issue, 0)

    def drain(r, carry):
        _row_copy(ys_ref, 0, y0_ref, 0, sem.at[0]).wait()
        _row_copy(ys_ref, 0, y1_ref, 0, sem.at[1]).wait()
        return carry

    lax.fori_loop(0, bt, drain, 0)
    gates = gate_ref[...]
    f = y0_ref[...] * gates[:, 0:1] + y1_ref[...] * gates[:, 1:2]
    o_ref[...] = _ln_rows(ALPHA * x_ref[...] + f, g_ref[...], b_ref[...])


def _combine(x, gates, ys, pos0, pos1, g, b, *, bt):
    m, d = x.shape
    row = pl.BlockSpec((bt, d), lambda i, p0, p1: (i, 0))
    vec = pl.BlockSpec((1, d), lambda i, p0, p1: (0, 0))
    return pl.pallas_call(
        _combine_kernel,
        grid_spec=pltpu.PrefetchScalarGridSpec(
            num_scalar_prefetch=2, grid=(m // bt,),
            in_specs=[row, pl.BlockSpec((bt, LANES), lambda i, p0, p1: (i, 0)), vec, vec,
                      pl.BlockSpec(memory_space=pl.ANY)],
            out_specs=row,
            scratch_shapes=[pltpu.VMEM((bt, d), F32), pltpu.VMEM((bt, d), F32),
                            pltpu.SemaphoreType.DMA((2,))]),
        out_shape=jax.ShapeDtypeStruct((m, d), F32),
        compiler_params=_cparams(1), name="combine_ln",
    )(pos0, pos1, x, gates, g.reshape(1, d), b.reshape(1, d), ys)


def _rope_tables(positions, n_rows):
    pos = positions.astype(F32).reshape(n_rows, 1)

    def cs(dim):
        inv = 1.0 / (ROPE_THETA ** (jnp.arange(0, dim, 2, dtype=F32) / dim))
        ang = pos * inv[None, :]
        return jnp.cos(ang), jnp.sin(ang)

    c_h, s_h = cs(HEAD_DIM)
    c_i, s_i = cs(IDX_DIM)
    z_i = jnp.zeros_like(s_i)
    cosf = jnp.concatenate([c_h, c_h], axis=1)
    sinf = jnp.concatenate([-s_h, s_h], axis=1)
    cos64 = jnp.concatenate([c_i, c_i, c_i, c_i], axis=1)
    sin_lo = jnp.concatenate([-s_i, z_i, -s_i, z_i], axis=1)
    sin_hi = jnp.concatenate([z_i, s_i, z_i, s_i], axis=1)
    return cosf, sinf, cos64, sin_lo, sin_hi


def kernel(x, mem, positions, w_in, pool_w, pool_scale, w_o, ln1_g, ln1_b, w_mq, w_mk, w_mv, w_mo, ln2_g, ln2_b,
           w_group_router, b_group_router, w_expert_router, b_expert_router, w_gate, w_up, w_down, ln3_g, ln3_b):
    batch, seq, d = x.shape
    n_mem = mem.shape[1]
    n = batch * seq
    bn_attn = N_KV_HEADS * HEAD_DIM
    tables = _rope_tables(positions, n)
    xf = x.reshape(n, d)
    n_slots = 2 * n + N_EXPERTS * MOE_BLK
    nblk = n_slots // MOE_BLK

    for l in range(w_in.shape[0]):
        x_bf = xf.astype(BF16)
        d_in = w_in.shape[2]
        d_in_pad = MIX_POOL + 8 * bn_attn
        w_in_bf = jnp.pad(w_in[l].astype(BF16), ((0, 0), (0, d_in_pad - d_in)))

        v_pool = _matmul([x_bf], [w_in_bf], name="inproj_pool", bm=1024, bn=1024, n=MIX_POOL)
        h2, k2, kw = _inproj_attn(x_bf, w_in_bf, tables, col_off_blocks=MIX_POOL // bn_attn, bm=1024, bn=bn_attn)
        a_pool = _pool_mixer(v_pool, pool_w[l].astype(BF16), pool_scale[l], batch=batch, seq=seq, ts=512)
        a_attn = _dsa_attention(h2, k2, kw, batch=batch, seq=seq, bn=bn_attn)
        w_o_bf = w_o[l].astype(BF16)
        res = (xf, (512, 1024), lambda i, j: (i, j))
        pre = _matmul([a_pool, a_attn], [w_o_bf, w_o_bf], name="outproj", bm=512, bn=1024, n=d,
                      extras=(res,), epilogue=_residual_epilogue)
        x1, x1_bf = _layer_norm(pre, ln1_g[l], ln1_b[l], bm=256)

        mq_scale = (d // N_MEM_HEADS) ** -0.5

        def q_epilogue(acc, ex, outs):
            outs[0][...] = (acc * mq_scale).astype(outs[0].dtype)

        qm = _matmul([x1_bf], [w_mq[l].astype(BF16)], name="mem_q", bm=1024, bn=1024, n=d, out_dtype=BF16,
                     epilogue=q_epilogue)
        mem_bf = mem.reshape(batch * n_mem, d).astype(BF16)
        km = _matmul([mem_bf], [w_mk[l].astype(BF16)], name="mem_k", bm=batch * n_mem, bn=1024, n=d, out_dtype=BF16)
        vm = _matmul([mem_bf], [w_mv[l].astype(BF16)], name="mem_v", bm=batch * n_mem, bn=1024, n=d, out_dtype=BF16)
        om = _cross_attention(qm, km, vm, batch=batch, seq=seq, n_mem=n_mem, bm=512)
        res = (x1, (512, 1024), lambda i, j: (i, j))
        pre = _matmul([om], [w_mo[l].astype(BF16)], name="mem_o", bm=512, bn=1024, n=d, extras=(res,),
                      epilogue=_residual_epilogue)
        x2, _ = _layer_norm(pre, ln2_g[l], ln2_b[l], bm=256)

        w_r = jnp.pad(jnp.concatenate([w_group_router[l], w_expert_router[l]], axis=1),
                      ((0, 0), (0, LANES - N_GROUPS - N_EXPERTS)))
        b_r = jnp.pad(jnp.concatenate([b_group_router[l], b_expert_router[l]]),
                      (0, LANES - N_GROUPS - N_EXPERTS)).reshape(1, LANES)
        ids, gates = _router(x2, w_r, b_r, bm=512)
        ranks, totals = _slot_ranks(ids, bt=512)
        c0 = totals[0, :N_EXPERTS].astype(I32)
        c1 = totals[0, N_EXPERTS:].astype(I32)
        padded = ((c0 + c1 + MOE_BLK - 1) // MOE_BLK) * MOE_BLK
        pend = jnp.cumsum(padded)
        pstart = pend - padded
        table = jnp.concatenate([pstart, pstart + c0]).astype(F32).reshape(1, LANES)
        pos = _slot_positions(ids, ranks, table, bt=512)
        pos0, pos1 = pos[:, 0], pos[:, 1]
        nused = pend[-1] // MOE_BLK
        blk_i = jnp.minimum(jnp.arange(nblk, dtype=I32), nused - 1)
        blk_e = jnp.sum((pend[None, :] <= (blk_i * MOE_BLK)[:, None]).astype(I32), axis=1)
        blk_e = jnp.minimum(blk_e, N_EXPERTS - 1)
        later = blk_e[None, :] > blk_e[:, None]
        nxt_e = jnp.min(jnp.where(later, blk_e[None, :], N_EXPERTS), axis=1)
        nxt_e = jnp.where(nxt_e == N_EXPERTS, -1, nxt_e).astype(I32)
        new_e = jnp.concatenate([jnp.ones((1,), I32), (blk_e[1:] != blk_e[:-1]).astype(I32)])
        set_idx = (jnp.cumsum(new_e) - 1) % 2

        xs = _dispatch(x2, pos0, pos1, n_slots, bt=256)
        ys = _expert_ffn(xs, blk_e, nxt_e, set_idx.astype(I32), nused.reshape(1).astype(I32),
                         w_gate[l], w_up[l], w_down[l])
        xf = _combine(x2, gates, ys, pos0, pos1, ln3_g[l], ln3_b[l], bt=256)
    return xf.reshape(batch, seq, d)
```

```python
import functools

import jax
import jax.numpy as jnp
from jax import lax
from jax.experimental import pallas as pl
from jax.experimental.pallas import tpu as pltpu

F32 = jnp.float32
BF16 = jnp.bfloat16
I32 = jnp.int32
U32 = jnp.uint32

MIX_POOL = 2048
N_POOL_GROUPS = 4
POOL_WINDOWS = (2, 4, 8, 16)
POOL_GW = MIX_POOL // N_POOL_GROUPS
HEAD_DIM = 128
N_Q_HEADS = 16
N_KV_HEADS = 4
Q_PER_KV = N_Q_HEADS // N_KV_HEADS
N_IDX_HEADS = 8
IDX_DIM = 64
TOPK_MAX = 256
ROPE_THETA = 10000.0
N_MEM_HEADS = 4
N_GROUPS = 8
EXPERTS_PER_GROUP = 8
N_EXPERTS = N_GROUPS * EXPERTS_PER_GROUP
LN_EPS = 1e-5
DEPTH = 1
ALPHA = (2.0 * DEPTH) ** 0.25

LANES = 128
QB = 128
LC = 512
MOE_BLK = 256
W_GROUPS = 8
W_STAGE_SLOTS = 4
POOL_HALO = 16
NEG = -1e30
N_HALVINGS = 8
LOG2E = 1.4426950408889634
VMEM_LIMIT = 56 * 1024 * 1024


def _cparams(n_axes, vmem=VMEM_LIMIT):
    return pltpu.CompilerParams(dimension_semantics=("arbitrary",) * n_axes, vmem_limit_bytes=vmem)


def _dot(a, b):
    return jnp.dot(a, b, preferred_element_type=F32)


def _dot_nt(a, b):
    return lax.dot_general(a, b, (((1,), (1,)), ((), ())), preferred_element_type=F32)


def _dot_tn(a, b):
    return lax.dot_general(a, b, (((0,), (0,)), ((), ())), preferred_element_type=F32)


FOLD_ROWS = 64


def _fold_rows(x, reduce_fn):
    r, c = x.shape
    return reduce_fn(x.reshape(r // FOLD_ROWS, FOLD_ROWS, c), axis=0)


def _mm_kernel(*refs, n_lhs, n_extra, epilogue):
    lhs = refs[:n_lhs]
    ws = refs[n_lhs:2 * n_lhs]
    extras = refs[2 * n_lhs:2 * n_lhs + n_extra]
    outs = refs[2 * n_lhs + n_extra:]
    acc = _dot(lhs[0][...], ws[0][...])
    for l in range(1, n_lhs):
        acc = acc + _dot(lhs[l][...], ws[l][...])
    epilogue(acc, extras, outs)


def _matmul(lhs_list, w_list, *, name, bm, bn, n, w_col_off=0, extras=(), out_dtype=F32, epilogue=None):
    m = lhs_list[0].shape[0]
    grid = (n // bn, m // bm)
    in_specs = [pl.BlockSpec((bm, a.shape[1]), lambda j, i: (i, 0)) for a in lhs_list]
    in_specs += [pl.BlockSpec((a.shape[1], bn), (lambda j, i, l=l: (l, j + w_col_off)))
                 for l, a in enumerate(lhs_list)]
    in_specs += [pl.BlockSpec(blk, (lambda j, i, f=f: f(i, j))) for (_, blk, f) in extras]
    if epilogue is None:
        def epilogue(acc, ex, outs):
            outs[0][...] = acc.astype(outs[0].dtype)
    kern = functools.partial(_mm_kernel, n_lhs=len(lhs_list), n_extra=len(extras), epilogue=epilogue)
    return pl.pallas_call(
        kern, grid=grid, in_specs=in_specs,
        out_specs=pl.BlockSpec((bm, bn), lambda j, i: (i, j)),
        out_shape=jax.ShapeDtypeStruct((m, n), out_dtype),
        compiler_params=_cparams(2), name=name,
    )(*lhs_list, *w_list, *[e[0] for e in extras])


def _residual_epilogue(acc, ex, outs):
    outs[0][...] = acc + ALPHA * ex[0][...]


def _rope128(a, cosf, sinf):
    return a * cosf + pltpu.roll(a, 64, 1) * sinf


def _rope64(a, cos64, sin_lo, sin_hi):
    return a * cos64 + pltpu.roll(a, 96, 1) * sin_lo + pltpu.roll(a, 32, 1) * sin_hi


def _inproj_kernel(x_ref, w_ref, cosf_ref, sinf_ref, cos64_ref, sinlo_ref, sinhi_ref,
                   h_ref, k2_ref, kw_ref, acc_ref, *, q_scale, wi_scale):
    j = pl.program_id(1)
    acc_ref[...] = _dot(x_ref[...], w_ref[...])
    bn = acc_ref.shape[1]

    @pl.when(j < 5)
    def _():
        scale = jnp.where(j < 4, q_scale, 1.0).astype(F32)
        cosf = cosf_ref[...] * scale
        sinf = sinf_ref[...] * scale
        for c in range(bn // LANES):
            a = acc_ref[:, c * LANES:(c + 1) * LANES]
            h_ref[:, c * LANES:(c + 1) * LANES] = _rope128(a, cosf, sinf).astype(h_ref.dtype)

    @pl.when(j == 5)
    def _():
        h_ref[...] = acc_ref[...].astype(h_ref.dtype)

    @pl.when(j == 6)
    def _():
        for c in range(bn // LANES):
            a = acc_ref[:, c * LANES:(c + 1) * LANES]
            r = _rope64(a, cos64_ref[...], sinlo_ref[...], sinhi_ref[...])
            h_ref[:, c * LANES:(c + 1) * LANES] = r.astype(h_ref.dtype)

    @pl.when(j == 7)
    def _():
        a = acc_ref[:, 0:LANES]
        r = _rope64(a, cos64_ref[...], sinlo_ref[...], sinhi_ref[...])
        lane = lax.broadcasted_iota(I32, a.shape, 1)
        kw_ref[...] = jnp.where(lane < IDX_DIM, r, a * wi_scale)
        k2_ref[...] = jnp.where(lane < IDX_DIM, r, pltpu.roll(r, 64, 1)).astype(k2_ref.dtype)


def _inproj_attn(x_bf, w_bf, tables, *, col_off_blocks, bm, bn):
    m = x_bf.shape[0]
    n_tiles = 8
    q_scale = HEAD_DIM ** -0.5 * LOG2E
    wi_scale = (N_IDX_HEADS ** -0.5) * (IDX_DIM ** -0.5)
    tab_spec = pl.BlockSpec((bm, LANES), lambda i, j: (i, 0))
    kern = functools.partial(_inproj_kernel, q_scale=q_scale, wi_scale=wi_scale)
    return pl.pallas_call(
        kern, grid=(m // bm, n_tiles),
        in_specs=[pl.BlockSpec((bm, x_bf.shape[1]), lambda i, j: (i, 0)),
                  pl.BlockSpec((w_bf.shape[0], bn), lambda i, j: (0, j + col_off_blocks))] + [tab_spec] * 5,
        out_specs=[pl.BlockSpec((bm, bn), lambda i, j: (i, jnp.minimum(j, n_tiles - 2))),
                   pl.BlockSpec((bm, LANES), lambda i, j: (i, 0)),
                   pl.BlockSpec((bm, LANES), lambda i, j: (i, 0))],
        out_shape=[jax.ShapeDtypeStruct((m, (n_tiles - 1) * bn), BF16),
                   jax.ShapeDtypeStruct((m, LANES), BF16),
                   jax.ShapeDtypeStruct((m, LANES), F32)],
        scratch_shapes=[pltpu.VMEM((bm, bn), F32)],
        compiler_params=_cparams(2), name="inproj_attn",
    )(x_bf, w_bf, *tables)


def _pool_kernel(v_ref, pw_ref, ps_ref, o_ref, ext_ref):
    s = pl.program_id(1)
    ts = v_ref.shape[0]

    @pl.when(s == 0)
    def _():
        ext_ref[0:POOL_HALO, :] = jnp.zeros((POOL_HALO, ext_ref.shape[1]), F32)

    ext_ref[POOL_HALO:, :] = v_ref[...]
    t = s * ts + lax.broadcasted_iota(I32, (ts, 1), 0)
    for g, w in enumerate(POOL_WINDOWS):
        cols = slice(g * POOL_GW, (g + 1) * POOL_GW)
        e = ext_ref[:, cols]
        step = 1
        while step < w:
            e = e + pltpu.roll(e, step, 0)
            step *= 2
        win = e[POOL_HALO:, :]
        cnt = jnp.minimum(t + 1, w).astype(F32)
        pooled = win / cnt - v_ref[:, cols]
        mixed = _dot(pooled.astype(BF16), pw_ref[g])
        o_ref[:, cols] = (mixed * ps_ref[:, cols]).astype(o_ref.dtype)
    ext_ref[0:POOL_HALO, :] = v_ref[ts - POOL_HALO:, :]


def _pool_mixer(v_pool, pool_w_bf, pool_scale, *, batch, seq, ts):
    v3 = v_pool.reshape(batch, seq, MIX_POOL)
    out = pl.pallas_call(
        _pool_kernel, grid=(batch, seq // ts),
        in_specs=[pl.BlockSpec((None, ts, MIX_POOL), lambda b, s: (b, s, 0)),
                  pl.BlockSpec((N_POOL_GROUPS, POOL_GW, POOL_GW), lambda b, s: (0, 0, 0)),
                  pl.BlockSpec((1, MIX_POOL), lambda b, s: (0, 0))],
        out_specs=pl.BlockSpec((None, ts, MIX_POOL), lambda b, s: (b, s, 0)),
        out_shape=jax.ShapeDtypeStruct((batch, seq, MIX_POOL), BF16),
        scratch_shapes=[pltpu.VMEM((POOL_HALO + ts, MIX_POOL), F32)],
        compiler_params=_cparams(2), name="pool_mixer",
    )(v3, pool_w_bf, pool_scale.reshape(1, MIX_POOL))
    return out.reshape(batch * seq, MIX_POOL)


def _dsa_kernel(q_ref, k_ref, v_ref, k2_ref, qi_ref, kw_ref, o_ref,
                key_sc, bias_sc, xi_sc, qt_sc, m_sc, l_sc, acc_sc, s_sc, *, n_sel, seq):
    blk = pl.program_id(1)
    nchunk = blk // (LC // QB) + 1
    q_pos = blk * QB + lax.broadcasted_iota(I32, (1, QB), 1)
    k_iota = lax.broadcasted_iota(I32, (LC, 1), 0)
    sub = lax.broadcasted_iota(I32, (LANES, QB), 0)
    eye = jnp.where(sub == lax.broadcasted_iota(I32, (LANES, QB), 1), 1.0, 0.0).astype(BF16)

    def transposed(x):
        return _dot_nt(eye, x)

    for jj in range(N_IDX_HEADS // 2):
        xt = transposed(qi_ref[:, jj * LANES:(jj + 1) * LANES])
        xi_sc[jj, :, 0:QB] = jnp.where(sub < IDX_DIM, xt, 0.0).astype(BF16)
        xi_sc[jj, :, QB:2 * QB] = jnp.where(sub >= IDX_DIM, xt, 0.0).astype(BF16)
    for g in range(N_KV_HEADS):
        for r in range(Q_PER_KV):
            h = g * Q_PER_KV + r
            qt_sc[g, 0:HEAD_DIM, r * QB:(r + 1) * QB] = transposed(
                q_ref[:, h * HEAD_DIM:(h + 1) * HEAD_DIM]).astype(BF16)
            qt_sc[g, HEAD_DIM:, r * QB:(r + 1) * QB] = eye
    kw_t = kw_ref[...].T
    wi = [kw_t[IDX_DIM + h:IDX_DIM + h + 1, :] for h in range(N_IDX_HEADS)]

    inf = jnp.inf
    part = (FOLD_ROWS, QB)

    def score_chunk(c, carry):
        s_min, s_max = carry
        off = pl.multiple_of(c * LC, LC)
        k2c = k2_ref[pl.ds(off, LC), :]
        sc = jnp.zeros((LC, QB), F32)
        for jj in range(N_IDX_HEADS // 2):
            rel = jnp.maximum(_dot(k2c, xi_sc[jj]), 0.0)
            sc = sc + rel[:, 0:QB] * wi[2 * jj] + rel[:, QB:2 * QB] * wi[2 * jj + 1]
        causal = off + k_iota <= q_pos
        key_sc[pl.ds(off, LC), :] = jnp.where(causal, sc, -inf)
        s_min = jnp.minimum(s_min, _fold_rows(jnp.where(causal, sc, inf), jnp.min))
        s_max = jnp.maximum(s_max, _fold_rows(jnp.where(causal, sc, -inf), jnp.max))
        return s_min, s_max

    s_min, s_max = lax.fori_loop(0, nchunk, score_chunk, (jnp.full(part, inf, F32), jnp.full(part, -inf, F32)))

    def count(pred_fn):
        def body(c, acc):
            off = pl.multiple_of(c * LC, LC)
            m = pred_fn(key_sc[pl.ds(off, LC), :], off)
            return acc + _fold_rows(m, jnp.sum)
        acc = lax.fori_loop(0, nchunk, body, jnp.zeros(part, F32))
        return jnp.sum(acc, axis=0, keepdims=True)

    has_thr = q_pos + 1 >= int(n_sel)
    lo0 = jnp.where(has_thr, jnp.min(s_min, axis=0, keepdims=True), -inf)
    hi0 = jnp.where(has_thr, jnp.max(s_max, axis=0, keepdims=True), -inf)

    def midpoint(lo, hi):
        mid = 0.5 * (lo + hi)
        return jnp.where(mid <= lo, hi, mid)

    def halve(i, st):
        lo, hi = st
        mid = midpoint(lo, hi)
        enough = count(lambda sc, off: jnp.where(sc >= mid, 1.0, 0.0)) >= n_sel
        return jnp.where(enough, mid, lo), jnp.where(enough, hi, mid)

    lo0, hi0 = lax.fori_loop(0, N_HALVINGS, halve, (lo0, hi0))

    def open_rows(lo, hi):
        return jnp.max(jnp.where(lo < hi, 1.0, 0.0)) > 0.0

    def bisect_cond(st):
        it, lo, hi = st
        return open_rows(lo, hi) & (it < seq)

    def bisect_body(st):
        it, lo, hi = st
        mid = midpoint(lo, hi)

        def body(c, acc):
            cnt, up, dn = acc
            off = pl.multiple_of(c * LC, LC)
            sc = key_sc[pl.ds(off, LC), :]
            ge = sc >= mid
            cnt = cnt + _fold_rows(jnp.where(ge, 1.0, 0.0), jnp.sum)
            up = jnp.minimum(up, _fold_rows(jnp.where(ge, sc, inf), jnp.min))
            dn = jnp.maximum(dn, _fold_rows(jnp.where(ge, -inf, sc), jnp.max))
            return cnt, up, dn

        cnt, up, dn = lax.fori_loop(
            0, nchunk, body, (jnp.zeros(part, F32), jnp.full(part, inf, F32), jnp.full(part, -inf, F32)))
        enough = jnp.sum(cnt, axis=0, keepdims=True) >= n_sel
        is_open = lo < hi
        new_lo = jnp.where(is_open & enough, jnp.min(up, axis=0, keepdims=True), lo)
        new_hi = jnp.where(is_open & jnp.logical_not(enough), jnp.max(dn, axis=0, keepdims=True), hi)
        return it + 1, new_lo, new_hi

    _, thr, _ = lax.while_loop(bisect_cond, bisect_body, (jnp.int32(0), lo0, hi0))
    n_gt = count(lambda kk, off: jnp.where(kk > thr, 1.0, 0.0))
    n_eq = count(lambda kk, off: jnp.where(kk == thr, 1.0, 0.0))
    need = n_sel - n_gt

    def tie_search():
        def tbody(i, p):
            cand = p | jnp.left_shift(jnp.int32(1), (seq.bit_length() - 2) - i)
            cnt = count(lambda kk, off: jnp.where(kk == thr, jnp.where(off + k_iota < cand, 1.0, 0.0), 0.0))
            return jnp.where(cnt < need, cand, p)
        return lax.fori_loop(0, seq.bit_length() - 1, tbody, jnp.zeros((1, QB), I32))

    ambiguous = jnp.max(jnp.where(has_thr & (n_eq > need), 1.0, 0.0)) > 0.0
    tie_hi = lax.cond(ambiguous, tie_search, lambda: jnp.full((1, QB), seq, I32))
    tie_hi = jnp.where(has_thr, tie_hi, -1)

    def bias_chunk(c, carry):
        off = pl.multiple_of(c * LC, LC)
        kk = key_sc[pl.ds(off, LC), :]
        tie_ok = jnp.where(off + k_iota <= tie_hi, 0.0, NEG)
        bias = jnp.where(kk > thr, 0.0, jnp.where(kk == thr, tie_ok, NEG))
        bias_sc[pl.ds(off, LC), :] = bias.astype(BF16)
        return carry

    lax.fori_loop(0, nchunk, bias_chunk, 0)

    m_sc[...] = jnp.full(m_sc.shape, NEG, F32)
    l_sc[...] = jnp.zeros(l_sc.shape, F32)
    acc_sc[...] = jnp.zeros(acc_sc.shape, F32)

    def attn_chunk(c, carry):
        off = pl.multiple_of(c * LC, LC)
        bias = bias_sc[pl.ds(off, LC), :]

        def logits(g):
            kc = k_ref[pl.ds(off, LC), g * HEAD_DIM:(g + 1) * HEAD_DIM]
            return _dot(jnp.concatenate([kc, bias], axis=1), qt_sc[g])

        m_news = []
        for g in range(N_KV_HEADS):
            s = logits(g)
            s_sc[g] = s
            m_news.append(jnp.maximum(m_sc[g], jnp.max(_fold_rows(s, jnp.max), axis=0, keepdims=True)))
        for g in range(N_KV_HEADS):
            vc = v_ref[pl.ds(off, LC), g * HEAD_DIM:(g + 1) * HEAD_DIM]
            m_old = m_sc[g]
            m_new = m_news[g]
            alpha = jnp.exp2(m_old - m_new)
            p = jnp.exp2(s_sc[g] - m_new)
            l_sc[g] = alpha * l_sc[g] + jnp.sum(_fold_rows(p, jnp.sum), axis=0, keepdims=True)
            acc_sc[g] = alpha * acc_sc[g] + _dot_tn(vc, p.astype(BF16))
            m_sc[g] = m_new
        return carry

    lax.fori_loop(0, nchunk, attn_chunk, 0)

    for g in range(N_KV_HEADS):
        o_t = acc_sc[g] / l_sc[g]
        for r in range(Q_PER_KV):
            h = g * Q_PER_KV + r
            o_ref[:, h * HEAD_DIM:(h + 1) * HEAD_DIM] = o_t[:, r * QB:(r + 1) * QB].T.astype(o_ref.dtype)


def _dsa_attention(h2, k2, kw, *, batch, seq, bn):
    nb = seq // QB
    n_sel = min(TOPK_MAX, seq // 4)
    q_cols = N_Q_HEADS * HEAD_DIM
    kv_cols = N_KV_HEADS * HEAD_DIM
    assert bn == kv_cols and q_cols == 4 * bn and seq % LC == 0 and LC >= n_sel
    kern = functools.partial(_dsa_kernel, n_sel=float(n_sel), seq=seq)
    rows = Q_PER_KV * QB
    return pl.pallas_call(
        kern, grid=(batch, nb),
        in_specs=[pl.BlockSpec((QB, q_cols), lambda b, i: (b * nb + i, 0)),
                  pl.BlockSpec((seq, kv_cols), lambda b, i: (b, 4)),
                  pl.BlockSpec((seq, kv_cols), lambda b, i: (b, 5)),
                  pl.BlockSpec((seq, LANES), lambda b, i: (b, 0)),
                  pl.BlockSpec((QB, bn), lambda b, i: (b * nb + i, 6)),
                  pl.BlockSpec((QB, LANES), lambda b, i: (b * nb + i, 0))],
        out_specs=pl.BlockSpec((QB, q_cols), lambda b, i: (b * nb + i, 0)),
        out_shape=jax.ShapeDtypeStruct((batch * seq, q_cols), BF16),
        scratch_shapes=[pltpu.VMEM((seq, QB), F32),
                        pltpu.VMEM((seq, QB), BF16),
                        pltpu.VMEM((N_IDX_HEADS // 2, LANES, 2 * QB), BF16),
                        pltpu.VMEM((N_KV_HEADS, HEAD_DIM + QB, rows), BF16),
                        pltpu.VMEM((N_KV_HEADS, 1, rows), F32),
                        pltpu.VMEM((N_KV_HEADS, 1, rows), F32),
                        pltpu.VMEM((N_KV_HEADS, HEAD_DIM, rows), F32),
                        pltpu.VMEM((N_KV_HEADS, LC, rows), F32)],
        compiler_params=_cparams(2), name="dsa_attention",
    )(h2, h2, h2, k2, h2, kw)


def _ln_rows(x, g, b):
    mu = jnp.mean(x, axis=-1, keepdims=True)
    xc = x - mu
    var = jnp.mean(xc * xc, axis=-1, keepdims=True)
    return xc * lax.rsqrt(var + LN_EPS) * g + b


def _pack_halves(y):
    half = y.shape[1] // 2
    return pltpu.pack_elementwise([y[:, :half], y[:, half:]], packed_dtype=BF16)


def _unpack_halves(p):
    lo = pltpu.unpack_elementwise(p, index=0, packed_dtype=BF16, unpacked_dtype=F32)
    hi = pltpu.unpack_elementwise(p, index=1, packed_dtype=BF16, unpacked_dtype=F32)
    return jnp.concatenate([lo, hi], axis=1)


def _ln_kernel(x_ref, g_ref, b_ref, o_ref, o2_ref, *, packed):
    y = _ln_rows(x_ref[...], g_ref[...], b_ref[...])
    o_ref[...] = y
    o2_ref[...] = _pack_halves(y) if packed else y.astype(o2_ref.dtype)


def _layer_norm(x, g, b, *, bm, packed=False):
    m, d = x.shape
    row = pl.BlockSpec((bm, d), lambda i: (i, 0))
    vec = pl.BlockSpec((1, d), lambda i: (0, 0))
    second = ((m, d // 2), U32) if packed else ((m, d), BF16)
    return pl.pallas_call(
        functools.partial(_ln_kernel, packed=packed), grid=(m // bm,), in_specs=[row, vec, vec],
        out_specs=[row, pl.BlockSpec((bm, second[0][1]), lambda i: (i, 0))],
        out_shape=[jax.ShapeDtypeStruct((m, d), F32), jax.ShapeDtypeStruct(*second)],
        compiler_params=_cparams(1), name="layer_norm",
    )(x, g.reshape(1, d), b.reshape(1, d))


def _xattn_kernel(q_ref, k_ref, v_ref, o_ref):
    dh = q_ref.shape[1] // N_MEM_HEADS
    for h in range(N_MEM_HEADS):
        cols = slice(h * dh, (h + 1) * dh)
        s = _dot_nt(q_ref[:, cols], k_ref[:, cols])
        p = jnp.exp(s - jnp.max(s, axis=1, keepdims=True))
        o = _dot(p.astype(BF16), v_ref[:, cols]) / jnp.sum(p, axis=1, keepdims=True)
        o_ref[:, cols] = o.astype(o_ref.dtype)


def _cross_attention(q, k, v, *, batch, seq, n_mem, bm):
    d = q.shape[1]
    nb = seq // bm
    return pl.pallas_call(
        _xattn_kernel, grid=(batch, nb),
        in_specs=[pl.BlockSpec((bm, d), lambda b, i: (b * nb + i, 0)),
                  pl.BlockSpec((n_mem, d), lambda b, i: (b, 0)),
                  pl.BlockSpec((n_mem, d), lambda b, i: (b, 0))],
        out_specs=pl.BlockSpec((bm, d), lambda b, i: (b * nb + i, 0)),
        out_shape=jax.ShapeDtypeStruct((batch * seq, d), BF16),
        compiler_params=_cparams(2), name="cross_attention",
    )(q, k, v)


def _router_kernel(x_ref, w_ref, b_ref, id_ref, gate_ref):
    logits = jnp.dot(x_ref[...], w_ref[...], preferred_element_type=F32,
                     precision=lax.Precision.HIGHEST) + b_ref[...]
    lane = lax.broadcasted_iota(I32, logits.shape, 1)
    lane_f = lane.astype(F32)
    big = float(LANES)
    is_g = lane < N_GROUPS
    gl = jnp.where(is_g, logits, -jnp.inf)
    g_max = jnp.max(gl, axis=1, keepdims=True)
    g_idx = jnp.min(jnp.where(gl == g_max, lane_f, big), axis=1, keepdims=True)
    g_gate = 1.0 / jnp.sum(jnp.where(is_g, jnp.exp(gl - g_max), 0.0), axis=1, keepdims=True)
    e_lo = N_GROUPS + g_idx * EXPERTS_PER_GROUP
    in_grp = (lane_f >= e_lo) & (lane_f < e_lo + EXPERTS_PER_GROUP)
    el = jnp.where(in_grp, logits, -jnp.inf)
    v1 = jnp.max(el, axis=1, keepdims=True)
    i1 = jnp.min(jnp.where(el == v1, lane_f, big), axis=1, keepdims=True)
    el2 = jnp.where(lane_f == i1, -jnp.inf, el)
    v2 = jnp.max(el2, axis=1, keepdims=True)
    i2 = jnp.min(jnp.where(el2 == v2, lane_f, big), axis=1, keepdims=True)
    z = jnp.exp(v2 - v1)
    w1 = g_gate / (1.0 + z)
    w2 = g_gate * z / (1.0 + z)
    ids = jnp.where(lane == 0, i1 - N_GROUPS, jnp.where(lane == 1, i2 - N_GROUPS, 0.0))
    id_ref[...] = ids.astype(I32)
    gate_ref[...] = jnp.where(lane == 0, w1, jnp.where(lane == 1, w2, 0.0))


def _router(x, w_r, b_r, *, bm):
    m, d = x.shape
    out = pl.BlockSpec((bm, LANES), lambda i: (i, 0))
    return pl.pallas_call(
        _router_kernel, grid=(m // bm,),
        in_specs=[pl.BlockSpec((bm, d), lambda i: (i, 0)),
                  pl.BlockSpec((d, LANES), lambda i: (0, 0)),
                  pl.BlockSpec((1, LANES), lambda i: (0, 0))],
        out_specs=[out, out],
        out_shape=[jax.ShapeDtypeStruct((m, LANES), I32), jax.ShapeDtypeStruct((m, LANES), F32)],
        compiler_params=_cparams(1), name="router",
    )(x, w_r, b_r)


def _rank_kernel(id_ref, rank_ref, cnt_ref, carry_ref):
    i = pl.program_id(0)
    bt = id_ref.shape[0]

    @pl.when(i == 0)
    def _():
        carry_ref[...] = jnp.zeros(carry_ref.shape, F32)

    ids = id_ref[...]
    lane = lax.broadcasted_iota(I32, (bt, LANES), 1)
    e0 = ids[:, 0:1]
    e1 = ids[:, 1:2] + N_EXPERTS
    hit0 = lane == e0
    hit1 = lane == e1
    onehot = jnp.where(hit0, 1.0, jnp.where(hit1, 1.0, 0.0))
    r_io = lax.broadcasted_iota(I32, (bt, bt), 0)
    c_io = lax.broadcasted_iota(I32, (bt, bt), 1)
    tri = jnp.where(c_io < r_io, 1.0, 0.0).astype(BF16)
    prefix = _dot(tri, onehot.astype(BF16)) + carry_ref[0:1, :]
    rank0 = jnp.sum(jnp.where(hit0, prefix, 0.0), axis=1, keepdims=True)
    rank1 = jnp.sum(jnp.where(hit1, prefix, 0.0), axis=1, keepdims=True)
    rank_ref[...] = jnp.where(lane == 0, rank0, jnp.where(lane == 1, rank1, 0.0))
    total = carry_ref[0:1, :] + jnp.sum(onehot, axis=0, keepdims=True)
    carry_ref[...] = jnp.broadcast_to(total, carry_ref.shape)
    cnt_ref[...] = jnp.broadcast_to(total, cnt_ref.shape)


def _slot_ranks(ids, *, bt):
    m = ids.shape[0]
    return pl.pallas_call(
        _rank_kernel, grid=(m // bt,),
        in_specs=[pl.BlockSpec((bt, LANES), lambda i: (i, 0))],
        out_specs=[pl.BlockSpec((bt, LANES), lambda i: (i, 0)),
                   pl.BlockSpec((8, LANES), lambda i: (0, 0))],
        out_shape=[jax.ShapeDtypeStruct((m, LANES), F32), jax.ShapeDtypeStruct((8, LANES), F32)],
        scratch_shapes=[pltpu.VMEM((8, LANES), F32)],
        compiler_params=_cparams(1), name="slot_ranks",
    )(ids)


def _pos_kernel(id_ref, rank_ref, tab_ref, pos_ref):
    ids = id_ref[...]
    ranks = rank_ref[...]
    lane = lax.broadcasted_iota(I32, ids.shape, 1)
    tab = tab_ref[...]
    base0 = jnp.sum(jnp.where(lane == ids[:, 0:1], tab, 0.0), axis=1, keepdims=True)
    base1 = jnp.sum(jnp.where(lane == ids[:, 1:2] + N_EXPERTS, tab, 0.0), axis=1, keepdims=True)
    pos = jnp.where(lane == 0, base0 + ranks[:, 0:1], jnp.where(lane == 1, base1 + ranks[:, 1:2], 0.0))
    pos_ref[...] = pos.astype(I32)


def _slot_positions(ids, ranks, table, *, bt):
    m = ids.shape[0]
    blk = pl.BlockSpec((bt, LANES), lambda i: (i, 0))
    return pl.pallas_call(
        _pos_kernel, grid=(m // bt,),
        in_specs=[blk, blk, pl.BlockSpec((1, LANES), lambda i: (0, 0))],
        out_specs=blk,
        out_shape=jax.ShapeDtypeStruct((m, LANES), I32),
        compiler_params=_cparams(1), name="slot_positions",
    )(ids, ranks, table)


def _row_copy(src_ref, src_row, dst_ref, dst_row, sem):
    return pltpu.make_async_copy(src_ref.at[pl.ds(src_row, 1)], dst_ref.at[pl.ds(dst_row, 1)], sem)


def _dispatch_kernel(pos0_ref, pos1_ref, x_ref, xs_in_ref, xs_ref, sem):
    del xs_in_ref
    i = pl.program_id(0)
    bt = x_ref.shape[0]

    def issue(r, carry):
        t = i * bt + r
        _row_copy(x_ref, r, xs_ref, pos0_ref[t], sem.at[0]).start()
        _row_copy(x_ref, r, xs_ref, pos1_ref[t], sem.at[1]).start()
        return carry

    lax.fori_loop(0, bt, issue, 0)

    def drain(r, carry):
        _row_copy(x_ref, 0, xs_ref, 0, sem.at[0]).wait()
        _row_copy(x_ref, 0, xs_ref, 0, sem.at[1]).wait()
        return carry

    lax.fori_loop(0, bt, drain, 0)


def _dispatch(x, pos0, pos1, n_slots, *, bt):
    m, d = x.shape
    xs0 = jnp.zeros((n_slots, d), x.dtype)
    return pl.pallas_call(
        _dispatch_kernel,
        grid_spec=pltpu.PrefetchScalarGridSpec(
            num_scalar_prefetch=2, grid=(m // bt,),
            in_specs=[pl.BlockSpec((bt, d), lambda i, p0, p1: (i, 0)),
                      pl.BlockSpec(memory_space=pl.ANY)],
            out_specs=pl.BlockSpec(memory_space=pl.ANY),
            scratch_shapes=[pltpu.SemaphoreType.DMA((2,))]),
        out_shape=jax.ShapeDtypeStruct((n_slots, d), x.dtype),
        input_output_aliases={3: 0},
        compiler_params=_cparams(1), name="dispatch",
    )(pos0, pos1, x, xs0)


def _ffn_kernel(blk_e_ref, nxt_e_ref, set_ref, nused_ref, x_ref, wg_hbm, wu_hbm, wd_hbm, y_ref,
                wg_bf, wu_bf, wd_bf, st_g, st_u, st_d, sem):
    i = pl.program_id(0)
    d, ff = wg_bf.shape[1], wg_bf.shape[2]
    rg, rd = d // W_GROUPS, ff // W_GROUPS

    n_slot = st_g.shape[0]

    def group_copies(e, g):
        slot = g % n_slot
        return (pltpu.make_async_copy(wg_hbm.at[e, pl.ds(g * rg, rg), :], st_g.at[slot], sem.at[0, slot]),
                pltpu.make_async_copy(wu_hbm.at[e, pl.ds(g * rg, rg), :], st_u.at[slot], sem.at[1, slot]),
                pltpu.make_async_copy(wd_hbm.at[e, pl.ds(g * rd, rd), :], st_d.at[slot], sem.at[2, slot]))

    def start(e, g):
        for c in group_copies(e, g):
            c.start()

    def finish(e, g, s):
        for c in group_copies(e, g):
            c.wait()
        slot = g % n_slot
        wg_bf[s, g * rg:(g + 1) * rg, :] = st_g[slot].astype(BF16)
        wu_bf[s, g * rg:(g + 1) * rg, :] = st_u[slot].astype(BF16)
        wd_bf[s, g * rd:(g + 1) * rd, :] = st_d[slot].astype(BF16)

    def load_expert(e, s, between=()):
        for g in range(n_slot):
            start(e, g)
        for g in range(W_GROUPS):
            if g < len(between):
                between[g]()
            finish(e, g, s)
            if g + n_slot < W_GROUPS:
                start(e, g + n_slot)

    e = blk_e_ref[i]
    s = set_ref[i]
    nxt = nxt_e_ref[i]
    first_of_expert = jnp.logical_or(i == 0, blk_e_ref[jnp.maximum(i - 1, 0)] != e)
    active = i < nused_ref[0]
    prefetch = jnp.logical_and(jnp.logical_and(active, first_of_expert), nxt >= 0)

    @pl.when(i == 0)
    def _():
        load_expert(e, s)

    def ffn_steps():
        vals = {}

        def gate_step():
            vals["x"] = _unpack_halves(x_ref[...]).astype(BF16)
            vals["gate"] = _dot(vals["x"], wg_bf[s])

        def up_step():
            gate = vals["gate"]
            vals["h"] = (gate * (1.0 / (1.0 + jnp.exp(-gate))) * _dot(vals["x"], wu_bf[s])).astype(BF16)

        def down_step():
            y_ref[...] = _pack_halves(_dot(vals["h"], wd_bf[s]))

        return gate_step, up_step, down_step

    @pl.when(prefetch)
    def _():
        load_expert(nxt, 1 - s, between=ffn_steps())

    @pl.when(jnp.logical_and(active, jnp.logical_not(prefetch)))
    def _():
        for step in ffn_steps():
            step()


def _expert_ffn(xs, blk_e, nxt_e, set_idx, nused, wg, wu, wd):
    n_slots, dp = xs.shape
    d, ff = wg.shape[1], wg.shape[2]
    nblk = n_slots // MOE_BLK
    assert dp * 2 == d and d % W_GROUPS == 0 and ff % W_GROUPS == 0 and W_GROUPS >= W_STAGE_SLOTS > 3

    def row_map(i, be, nx, si, nu):
        return (jnp.minimum(i, nu[0] - 1), 0)

    hbm = pl.BlockSpec(memory_space=pl.ANY)
    return pl.pallas_call(
        _ffn_kernel,
        grid_spec=pltpu.PrefetchScalarGridSpec(
            num_scalar_prefetch=4, grid=(nblk,),
            in_specs=[pl.BlockSpec((MOE_BLK, dp), row_map), hbm, hbm, hbm],
            out_specs=pl.BlockSpec((MOE_BLK, dp), row_map),
            scratch_shapes=[pltpu.VMEM((2, d, ff), BF16), pltpu.VMEM((2, d, ff), BF16), pltpu.VMEM((2, ff, d), BF16),
                            pltpu.VMEM((W_STAGE_SLOTS, d // W_GROUPS, ff), F32),
                            pltpu.VMEM((W_STAGE_SLOTS, d // W_GROUPS, ff), F32),
                            pltpu.VMEM((W_STAGE_SLOTS, ff // W_GROUPS, d), F32),
                            pltpu.SemaphoreType.DMA((3, W_STAGE_SLOTS))]),
        out_shape=jax.ShapeDtypeStruct((n_slots, dp), xs.dtype),
        input_output_aliases={4: 0},
        compiler_params=_cparams(1), name="expert_ffn",
    )(blk_e, nxt_e, set_idx, nused, xs, wg, wu, wd)


def _combine_kernel(pos0_ref, pos1_ref, x_ref, gate_ref, g_ref, b_ref, ys_ref, o_ref, y0_ref, y1_ref, sem):
    i = pl.program_id(0)
    bt = x_ref.shape[0]

    def issue(r, carry):
        t = i * bt + r
        _row_copy(ys_ref, pos0_ref[t], y0_ref, r, sem.at[0]).start()
        _row_copy(ys_ref, pos1_ref[t], y1_ref, r, sem.at[1]).start()
        return carry

    lax.fori_loop(0, bt, issue, 0)

    def drain(r, carry):
        _row_copy(ys_ref, 0, y0_ref, 0, sem.at[0]).wait()
        _row_copy(ys_ref, 0, y1_ref, 0, sem.at[1]).wait()
        return carry

    lax.fori_loop(0, bt, drain, 0)
    gates = gate_ref[...]
    f = _unpack_halves(y0_ref[...]) * gates[:, 0:1] + _unpack_halves(y1_ref[...]) * gates[:, 1:2]
    o_ref[...] = _ln_rows(ALPHA * x_ref[...] + f, g_ref[...], b_ref[...])


def _combine(x, gates, ys, pos0, pos1, g, b, *, bt):
    m, d = x.shape
    dp = ys.shape[1]
    row = pl.BlockSpec((bt, d), lambda i, p0, p1: (i, 0))
    vec = pl.BlockSpec((1, d), lambda i, p0, p1: (0, 0))
    return pl.pallas_call(
        _combine_kernel,
        grid_spec=pltpu.PrefetchScalarGridSpec(
            num_scalar_prefetch=2, grid=(m // bt,),
            in_specs=[row, pl.BlockSpec((bt, LANES), lambda i, p0, p1: (i, 0)), vec, vec,
                      pl.BlockSpec(memory_space=pl.ANY)],
            out_specs=row,
            scratch_shapes=[pltpu.VMEM((bt, dp), ys.dtype), pltpu.VMEM((bt, dp), ys.dtype),
                            pltpu.SemaphoreType.DMA((2,))]),
        out_shape=jax.ShapeDtypeStruct((m, d), F32),
        compiler_params=_cparams(1), name="combine_ln",
    )(pos0, pos1, x, gates, g.reshape(1, d), b.reshape(1, d), ys)


def _rope_tables(positions, n_rows):
    pos = positions.astype(F32).reshape(n_rows, 1)

    def cs(dim):
        inv = 1.0 / (ROPE_THETA ** (jnp.arange(0, dim, 2, dtype=F32) / dim))
        ang = pos * inv[None, :]
        return jnp.cos(ang), jnp.sin(ang)

    c_h, s_h = cs(HEAD_DIM)
    c_i, s_i = cs(IDX_DIM)
    z_i = jnp.zeros_like(s_i)
    cosf = jnp.concatenate([c_h, c_h], axis=1)
    sinf = jnp.concatenate([-s_h, s_h], axis=1)
    cos64 = jnp.concatenate([c_i, c_i, c_i, c_i], axis=1)
    sin_lo = jnp.concatenate([-s_i, z_i, -s_i, z_i], axis=1)
    sin_hi = jnp.concatenate([z_i, s_i, z_i, s_i], axis=1)
    return cosf, sinf, cos64, sin_lo, sin_hi


def kernel(x, mem, positions, w_in, pool_w, pool_scale, w_o, ln1_g, ln1_b, w_mq, w_mk, w_mv, w_mo, ln2_g, ln2_b,
           w_group_router, b_group_router, w_expert_router, b_expert_router, w_gate, w_up, w_down, ln3_g, ln3_b):
    batch, seq, d = x.shape
    n_mem = mem.shape[1]
    n = batch * seq
    bn_attn = N_KV_HEADS * HEAD_DIM
    tables = _rope_tables(positions, n)
    xf = x.reshape(n, d)
    n_slots = 2 * n + N_EXPERTS * MOE_BLK
    nblk = n_slots // MOE_BLK

    for l in range(w_in.shape[0]):
        x_bf = xf.astype(BF16)
        d_in = w_in.shape[2]
        d_in_pad = MIX_POOL + 8 * bn_attn
        w_in_bf = jnp.pad(w_in[l].astype(BF16), ((0, 0), (0, d_in_pad - d_in)))

        v_pool = _matmul([x_bf], [w_in_bf], name="inproj_pool", bm=1024, bn=1024, n=MIX_POOL)
        h2, k2, kw = _inproj_attn(x_bf, w_in_bf, tables, col_off_blocks=MIX_POOL // bn_attn, bm=1024, bn=bn_attn)
        a_pool = _pool_mixer(v_pool, pool_w[l].astype(BF16), pool_scale[l], batch=batch, seq=seq, ts=512)
        a_attn = _dsa_attention(h2, k2, kw, batch=batch, seq=seq, bn=bn_attn)
        w_o_bf = w_o[l].astype(BF16)
        res = (xf, (512, 1024), lambda i, j: (i, j))
        pre = _matmul([a_pool, a_attn], [w_o_bf, w_o_bf], name="outproj", bm=512, bn=1024, n=d,
                      extras=(res,), epilogue=_residual_epilogue)
        x1, x1_bf = _layer_norm(pre, ln1_g[l], ln1_b[l], bm=256)

        mq_scale = (d // N_MEM_HEADS) ** -0.5

        def q_epilogue(acc, ex, outs):
            outs[0][...] = (acc * mq_scale).astype(outs[0].dtype)

        qm = _matmul([x1_bf], [w_mq[l].astype(BF16)], name="mem_q", bm=1024, bn=1024, n=d, out_dtype=BF16,
                     epilogue=q_epilogue)
        mem_bf = mem.reshape(batch * n_mem, d).astype(BF16)
        km = _matmul([mem_bf], [w_mk[l].astype(BF16)], name="mem_k", bm=batch * n_mem, bn=1024, n=d, out_dtype=BF16)
        vm = _matmul([mem_bf], [w_mv[l].astype(BF16)], name="mem_v", bm=batch * n_mem, bn=1024, n=d, out_dtype=BF16)
        om = _cross_attention(qm, km, vm, batch=batch, seq=seq, n_mem=n_mem, bm=512)
        res = (x1, (512, 1024), lambda i, j: (i, j))
        pre = _matmul([om], [w_mo[l].astype(BF16)], name="mem_o", bm=512, bn=1024, n=d, extras=(res,),
                      epilogue=_residual_epilogue)
        x2, x2_rows = _layer_norm(pre, ln2_g[l], ln2_b[l], bm=256, packed=True)

        w_r = jnp.pad(jnp.concatenate([w_group_router[l], w_expert_router[l]], axis=1),
                      ((0, 0), (0, LANES - N_GROUPS - N_EXPERTS)))
        b_r = jnp.pad(jnp.concatenate([b_group_router[l], b_expert_router[l]]),
                      (0, LANES - N_GROUPS - N_EXPERTS)).reshape(1, LANES)
        ids, gates = _router(x2, w_r, b_r, bm=512)
        ranks, totals = _slot_ranks(ids, bt=512)
        c0 = totals[0, :N_EXPERTS].astype(I32)
        c1 = totals[0, N_EXPERTS:].astype(I32)
        padded = ((c0 + c1 + MOE_BLK - 1) // MOE_BLK) * MOE_BLK
        pend = jnp.cumsum(padded)
        pstart = pend - padded
        table = jnp.concatenate([pstart, pstart + c0]).astype(F32).reshape(1, LANES)
        pos = _slot_positions(ids, ranks, table, bt=512)
        pos0, pos1 = pos[:, 0], pos[:, 1]
        nused = pend[-1] // MOE_BLK
        blk_i = jnp.minimum(jnp.arange(nblk, dtype=I32), nused - 1)
        blk_e = jnp.sum((pend[None, :] <= (blk_i * MOE_BLK)[:, None]).astype(I32), axis=1)
        blk_e = jnp.minimum(blk_e, N_EXPERTS - 1)
        later = blk_e[None, :] > blk_e[:, None]
        nxt_e = jnp.min(jnp.where(later, blk_e[None, :], N_EXPERTS), axis=1)
        nxt_e = jnp.where(nxt_e == N_EXPERTS, -1, nxt_e).astype(I32)
        new_e = jnp.concatenate([jnp.ones((1,), I32), (blk_e[1:] != blk_e[:-1]).astype(I32)])
        set_idx = (jnp.cumsum(new_e) - 1) % 2

        xs = _dispatch(x2_rows, pos0, pos1, n_slots, bt=256)
        ys = _expert_ffn(xs, blk_e, nxt_e, set_idx.astype(I32), nused.reshape(1).astype(I32),
                         w_gate[l], w_up[l], w_down[l])
        xf = _combine(x2, gates, ys, pos0, pos1, ln3_g[l], ln3_b[l], bt=256)
    return xf.reshape(batch, seq, d)
```

```python
import functools

import jax
import jax.numpy as jnp
from jax import lax
from jax.experimental import pallas as pl
from jax.experimental.pallas import tpu as pltpu

F32 = jnp.float32
BF16 = jnp.bfloat16
I32 = jnp.int32
U32 = jnp.uint32

MIX_POOL = 2048
N_POOL_GROUPS = 4
POOL_WINDOWS = (2, 4, 8, 16)
POOL_GW = MIX_POOL // N_POOL_GROUPS
HEAD_DIM = 128
N_Q_HEADS = 16
N_KV_HEADS = 4
Q_PER_KV = N_Q_HEADS // N_KV_HEADS
N_IDX_HEADS = 8
IDX_DIM = 64
TOPK_MAX = 256
ROPE_THETA = 10000.0
N_MEM_HEADS = 4
N_GROUPS = 8
EXPERTS_PER_GROUP = 8
N_EXPERTS = N_GROUPS * EXPERTS_PER_GROUP
LN_EPS = 1e-5
DEPTH = 1
ALPHA = (2.0 * DEPTH) ** 0.25

LANES = 128
QB = 128
LC = 512
DENSE_BN = 512
MOE_BLK = 256
W_GROUPS = 8
W_STAGE_SLOTS = 4
POOL_HALO = 16
NEG = -1e30
N_HALVINGS = 8
LOG2E = 1.4426950408889634
VMEM_LIMIT = 56 * 1024 * 1024


def _cparams(n_axes, vmem=VMEM_LIMIT):
    return pltpu.CompilerParams(dimension_semantics=("arbitrary",) * n_axes, vmem_limit_bytes=vmem)


def _dot(a, b):
    return jnp.dot(a, b, preferred_element_type=F32)


def _dot_nt(a, b):
    return lax.dot_general(a, b, (((1,), (1,)), ((), ())), preferred_element_type=F32)


def _dot_tn(a, b):
    return lax.dot_general(a, b, (((0,), (0,)), ((), ())), preferred_element_type=F32)


FOLD_ROWS = 64


def _fold_rows(x, reduce_fn):
    r, c = x.shape
    return reduce_fn(x.reshape(r // FOLD_ROWS, FOLD_ROWS, c), axis=0)


def _mm_kernel(*refs, n_lhs, n_extra, epilogue):
    lhs = refs[:n_lhs]
    ws = refs[n_lhs:2 * n_lhs]
    extras = refs[2 * n_lhs:2 * n_lhs + n_extra]
    outs = refs[2 * n_lhs + n_extra:-n_lhs]
    w_bf = refs[-n_lhs:]

    @pl.when(pl.program_id(1) == 0)
    def _():
        for l in range(n_lhs):
            w_bf[l][...] = ws[l][...].astype(BF16)

    acc = _dot(lhs[0][...], w_bf[0][...])
    for l in range(1, n_lhs):
        acc = acc + _dot(lhs[l][...], w_bf[l][...])
    epilogue(acc, extras, outs)


def _matmul(lhs_list, w_list, *, name, bm, bn, n, w_col_off=0, extras=(), out_dtype=F32, epilogue=None):
    m = lhs_list[0].shape[0]
    grid = (n // bn, m // bm)
    in_specs = [pl.BlockSpec((bm, a.shape[1]), lambda j, i: (i, 0)) for a in lhs_list]
    in_specs += [pl.BlockSpec((a.shape[1], bn), (lambda j, i, l=l: (l, j + w_col_off)))
                 for l, a in enumerate(lhs_list)]
    in_specs += [pl.BlockSpec(blk, (lambda j, i, f=f: f(i, j))) for (_, blk, f) in extras]
    if epilogue is None:
        def epilogue(acc, ex, outs):
            outs[0][...] = acc.astype(outs[0].dtype)
    kern = functools.partial(_mm_kernel, n_lhs=len(lhs_list), n_extra=len(extras), epilogue=epilogue)
    return pl.pallas_call(
        kern, grid=grid, in_specs=in_specs,
        out_specs=pl.BlockSpec((bm, bn), lambda j, i: (i, j)),
        out_shape=jax.ShapeDtypeStruct((m, n), out_dtype),
        scratch_shapes=[pltpu.VMEM((a.shape[1], bn), BF16) for a in lhs_list],
        compiler_params=_cparams(2), name=name,
    )(*lhs_list, *w_list, *[e[0] for e in extras])


def _residual_epilogue(acc, ex, outs):
    outs[0][...] = acc + ALPHA * ex[0][...]


def _rope128(a, cosf, sinf):
    return a * cosf + pltpu.roll(a, 64, 1) * sinf


def _rope64(a, cos64, sin_lo, sin_hi):
    return a * cos64 + pltpu.roll(a, 96, 1) * sin_lo + pltpu.roll(a, 32, 1) * sin_hi


def _inproj_kernel(x_ref, w_ref, cosf_ref, sinf_ref, cos64_ref, sinlo_ref, sinhi_ref,
                   h_ref, acc_ref, w_bf, *, q_scale):
    j = pl.program_id(0)

    @pl.when(pl.program_id(1) == 0)
    def _():
        w_bf[...] = w_ref[...].astype(BF16)

    acc_ref[...] = _dot(x_ref[...], w_bf[...])
    bn = acc_ref.shape[1]

    @pl.when(j < 5)
    def _():
        scale = jnp.where(j < 4, q_scale, 1.0).astype(F32)
        cosf = cosf_ref[...] * scale
        sinf = sinf_ref[...] * scale
        for c in range(bn // LANES):
            a = acc_ref[:, c * LANES:(c + 1) * LANES]
            h_ref[:, c * LANES:(c + 1) * LANES] = _rope128(a, cosf, sinf).astype(h_ref.dtype)

    @pl.when(j == 5)
    def _():
        h_ref[...] = acc_ref[...].astype(h_ref.dtype)

    @pl.when(j == 6)
    def _():
        for c in range(bn // LANES):
            a = acc_ref[:, c * LANES:(c + 1) * LANES]
            r = _rope64(a, cos64_ref[...], sinlo_ref[...], sinhi_ref[...])
            h_ref[:, c * LANES:(c + 1) * LANES] = r.astype(h_ref.dtype)


def _inproj_attn(x_bf, w, tables, *, col_off_blocks, bm, bn):
    m, d = x_bf.shape
    n_tiles = 7
    q_scale = HEAD_DIM ** -0.5 * LOG2E
    tab_spec = pl.BlockSpec((bm, LANES), lambda j, i: (i, 0))
    return pl.pallas_call(
        functools.partial(_inproj_kernel, q_scale=q_scale), grid=(n_tiles, m // bm),
        in_specs=[pl.BlockSpec((bm, d), lambda j, i: (i, 0)),
                  pl.BlockSpec((d, bn), lambda j, i: (0, j + col_off_blocks))] + [tab_spec] * 5,
        out_specs=pl.BlockSpec((bm, bn), lambda j, i: (i, j)),
        out_shape=jax.ShapeDtypeStruct((m, n_tiles * bn), BF16),
        scratch_shapes=[pltpu.VMEM((bm, bn), F32), pltpu.VMEM((d, bn), BF16)],
        compiler_params=_cparams(2), name="inproj_attn",
    )(x_bf, w, *tables)


def _inproj_idx_kernel(x_ref, w_ref, cos64_ref, sinlo_ref, sinhi_ref, k2_ref, kw_ref, *, wi_scale):
    a = _dot(x_ref[...], w_ref[...].astype(BF16))
    r = _rope64(a, cos64_ref[...], sinlo_ref[...], sinhi_ref[...])
    lane = lax.broadcasted_iota(I32, a.shape, 1)
    kw_ref[...] = jnp.where(lane < IDX_DIM, r, a * wi_scale)
    k2_ref[...] = jnp.where(lane < IDX_DIM, r, pltpu.roll(r, 64, 1)).astype(k2_ref.dtype)


def _inproj_idx(x_bf, w_kw, tables, *, bm):
    m, d = x_bf.shape
    wi_scale = (N_IDX_HEADS ** -0.5) * (IDX_DIM ** -0.5)
    tab_spec = pl.BlockSpec((bm, LANES), lambda i: (i, 0))
    return pl.pallas_call(
        functools.partial(_inproj_idx_kernel, wi_scale=wi_scale), grid=(m // bm,),
        in_specs=[pl.BlockSpec((bm, d), lambda i: (i, 0)), pl.BlockSpec((d, LANES), lambda i: (0, 0))]
        + [tab_spec] * 3,
        out_specs=[tab_spec, tab_spec],
        out_shape=[jax.ShapeDtypeStruct((m, LANES), BF16), jax.ShapeDtypeStruct((m, LANES), F32)],
        compiler_params=_cparams(1), name="inproj_idx",
    )(x_bf, w_kw, *tables[2:])


def _pool_kernel(v_ref, pw_ref, ps_ref, o_ref, ext_ref):
    s = pl.program_id(1)
    ts = v_ref.shape[0]

    @pl.when(s == 0)
    def _():
        ext_ref[0:POOL_HALO, :] = jnp.zeros((POOL_HALO, ext_ref.shape[1]), F32)

    ext_ref[POOL_HALO:, :] = v_ref[...]
    t = s * ts + lax.broadcasted_iota(I32, (ts, 1), 0)
    for g, w in enumerate(POOL_WINDOWS):
        cols = slice(g * POOL_GW, (g + 1) * POOL_GW)
        e = ext_ref[:, cols]
        step = 1
        while step < w:
            e = e + pltpu.roll(e, step, 0)
            step *= 2
        win = e[POOL_HALO:, :]
        cnt = jnp.minimum(t + 1, w).astype(F32)
        pooled = win / cnt - v_ref[:, cols]
        mixed = _dot(pooled.astype(BF16), pw_ref[g])
        o_ref[:, cols] = (mixed * ps_ref[:, cols]).astype(o_ref.dtype)
    ext_ref[0:POOL_HALO, :] = v_ref[ts - POOL_HALO:, :]


def _pool_mixer(v_pool, pool_w_bf, pool_scale, *, batch, seq, ts):
    v3 = v_pool.reshape(batch, seq, MIX_POOL)
    out = pl.pallas_call(
        _pool_kernel, grid=(batch, seq // ts),
        in_specs=[pl.BlockSpec((None, ts, MIX_POOL), lambda b, s: (b, s, 0)),
                  pl.BlockSpec((N_POOL_GROUPS, POOL_GW, POOL_GW), lambda b, s: (0, 0, 0)),
                  pl.BlockSpec((1, MIX_POOL), lambda b, s: (0, 0))],
        out_specs=pl.BlockSpec((None, ts, MIX_POOL), lambda b, s: (b, s, 0)),
        out_shape=jax.ShapeDtypeStruct((batch, seq, MIX_POOL), BF16),
        scratch_shapes=[pltpu.VMEM((POOL_HALO + ts, MIX_POOL), F32)],
        compiler_params=_cparams(2), name="pool_mixer",
    )(v3, pool_w_bf, pool_scale.reshape(1, MIX_POOL))
    return out.reshape(batch * seq, MIX_POOL)


def _dsa_kernel(q_ref, k_ref, v_ref, k2_ref, qi_ref, kw_ref, o_ref,
                key_sc, bias_sc, xi_sc, qt_sc, m_sc, l_sc, acc_sc, s_sc, *, n_sel, seq):
    blk = pl.program_id(1)
    nchunk = blk // (LC // QB) + 1
    q_pos = blk * QB + lax.broadcasted_iota(I32, (1, QB), 1)
    k_iota = lax.broadcasted_iota(I32, (LC, 1), 0)
    sub = lax.broadcasted_iota(I32, (LANES, QB), 0)
    eye = jnp.where(sub == lax.broadcasted_iota(I32, (LANES, QB), 1), 1.0, 0.0).astype(BF16)

    def transposed(x):
        return _dot_nt(eye, x)

    for jj in range(N_IDX_HEADS // 2):
        xt = transposed(qi_ref[:, jj * LANES:(jj + 1) * LANES])
        xi_sc[jj, :, 0:QB] = jnp.where(sub < IDX_DIM, xt, 0.0).astype(BF16)
        xi_sc[jj, :, QB:2 * QB] = jnp.where(sub >= IDX_DIM, xt, 0.0).astype(BF16)
    for g in range(N_KV_HEADS):
        for r in range(Q_PER_KV):
            h = g * Q_PER_KV + r
            qt_sc[g, 0:HEAD_DIM, r * QB:(r + 1) * QB] = transposed(
                q_ref[:, h * HEAD_DIM:(h + 1) * HEAD_DIM]).astype(BF16)
            qt_sc[g, HEAD_DIM:, r * QB:(r + 1) * QB] = eye
    kw_t = kw_ref[...].T
    wi = [kw_t[IDX_DIM + h:IDX_DIM + h + 1, :] for h in range(N_IDX_HEADS)]

    inf = jnp.inf
    part = (FOLD_ROWS, QB)

    def score_chunk(c, carry):
        s_min, s_max = carry
        off = pl.multiple_of(c * LC, LC)
        k2c = k2_ref[pl.ds(off, LC), :]
        sc = jnp.zeros((LC, QB), F32)
        for jj in range(N_IDX_HEADS // 2):
            rel = jnp.maximum(_dot(k2c, xi_sc[jj]), 0.0)
            sc = sc + rel[:, 0:QB] * wi[2 * jj] + rel[:, QB:2 * QB] * wi[2 * jj + 1]
        causal = off + k_iota <= q_pos
        key_sc[pl.ds(off, LC), :] = jnp.where(causal, sc, -inf)
        s_min = jnp.minimum(s_min, _fold_rows(jnp.where(causal, sc, inf), jnp.min))
        s_max = jnp.maximum(s_max, _fold_rows(jnp.where(causal, sc, -inf), jnp.max))
        return s_min, s_max

    s_min, s_max = lax.fori_loop(0, nchunk, score_chunk, (jnp.full(part, inf, F32), jnp.full(part, -inf, F32)))

    def count(pred_fn):
        def body(c, acc):
            off = pl.multiple_of(c * LC, LC)
            m = pred_fn(key_sc[pl.ds(off, LC), :], off)
            return acc + _fold_rows(m, jnp.sum)
        acc = lax.fori_loop(0, nchunk, body, jnp.zeros(part, F32))
        return jnp.sum(acc, axis=0, keepdims=True)

    has_thr = q_pos + 1 >= int(n_sel)
    lo0 = jnp.where(has_thr, jnp.min(s_min, axis=0, keepdims=True), -inf)
    hi0 = jnp.where(has_thr, jnp.max(s_max, axis=0, keepdims=True), -inf)

    def midpoint(lo, hi):
        mid = 0.5 * (lo + hi)
        return jnp.where(mid <= lo, hi, mid)

    def halve(i, st):
        lo, hi = st
        mid = midpoint(lo, hi)
        enough = count(lambda sc, off: jnp.where(sc >= mid, 1.0, 0.0)) >= n_sel
        return jnp.where(enough, mid, lo), jnp.where(enough, hi, mid)

    lo0, hi0 = lax.fori_loop(0, N_HALVINGS, halve, (lo0, hi0))

    def open_rows(lo, hi):
        return jnp.max(jnp.where(lo < hi, 1.0, 0.0)) > 0.0

    def bisect_cond(st):
        it, lo, hi = st
        return open_rows(lo, hi) & (it < seq)

    def bisect_body(st):
        it, lo, hi = st
        mid = midpoint(lo, hi)

        def body(c, acc):
            cnt, up, dn = acc
            off = pl.multiple_of(c * LC, LC)
            sc = key_sc[pl.ds(off, LC), :]
            ge = sc >= mid
            cnt = cnt + _fold_rows(jnp.where(ge, 1.0, 0.0), jnp.sum)
            up = jnp.minimum(up, _fold_rows(jnp.where(ge, sc, inf), jnp.min))
            dn = jnp.maximum(dn, _fold_rows(jnp.where(ge, -inf, sc), jnp.max))
            return cnt, up, dn

        cnt, up, dn = lax.fori_loop(
            0, nchunk, body, (jnp.zeros(part, F32), jnp.full(part, inf, F32), jnp.full(part, -inf, F32)))
        enough = jnp.sum(cnt, axis=0, keepdims=True) >= n_sel
        is_open = lo < hi
        new_lo = jnp.where(is_open & enough, jnp.min(up, axis=0, keepdims=True), lo)
        new_hi = jnp.where(is_open & jnp.logical_not(enough), jnp.max(dn, axis=0, keepdims=True), hi)
        return it + 1, new_lo, new_hi

    _, thr, _ = lax.while_loop(bisect_cond, bisect_body, (jnp.int32(0), lo0, hi0))
    n_gt = count(lambda kk, off: jnp.where(kk > thr, 1.0, 0.0))
    n_eq = count(lambda kk, off: jnp.where(kk == thr, 1.0, 0.0))
    need = n_sel - n_gt

    def tie_search():
        def tbody(i, p):
            cand = p | jnp.left_shift(jnp.int32(1), (seq.bit_length() - 2) - i)
            cnt = count(lambda kk, off: jnp.where(kk == thr, jnp.where(off + k_iota < cand, 1.0, 0.0), 0.0))
            return jnp.where(cnt < need, cand, p)
        return lax.fori_loop(0, seq.bit_length() - 1, tbody, jnp.zeros((1, QB), I32))

    ambiguous = jnp.max(jnp.where(has_thr & (n_eq > need), 1.0, 0.0)) > 0.0
    tie_hi = lax.cond(ambiguous, tie_search, lambda: jnp.full((1, QB), seq, I32))
    tie_hi = jnp.where(has_thr, tie_hi, -1)

    def bias_chunk(c, carry):
        off = pl.multiple_of(c * LC, LC)
        kk = key_sc[pl.ds(off, LC), :]
        tie_ok = jnp.where(off + k_iota <= tie_hi, 0.0, NEG)
        bias = jnp.where(kk > thr, 0.0, jnp.where(kk == thr, tie_ok, NEG))
        bias_sc[pl.ds(off, LC), :] = bias.astype(BF16)
        return carry

    lax.fori_loop(0, nchunk, bias_chunk, 0)

    m_sc[...] = jnp.full(m_sc.shape, NEG, F32)
    l_sc[...] = jnp.zeros(l_sc.shape, F32)
    acc_sc[...] = jnp.zeros(acc_sc.shape, F32)

    def attn_chunk(c, carry):
        off = pl.multiple_of(c * LC, LC)
        bias = bias_sc[pl.ds(off, LC), :]

        def logits(g):
            kc = k_ref[pl.ds(off, LC), g * HEAD_DIM:(g + 1) * HEAD_DIM]
            return _dot(jnp.concatenate([kc, bias], axis=1), qt_sc[g])

        m_news = []
        for g in range(N_KV_HEADS):
            s = logits(g)
            s_sc[g] = s
            m_news.append(jnp.maximum(m_sc[g], jnp.max(_fold_rows(s, jnp.max), axis=0, keepdims=True)))
        for g in range(N_KV_HEADS):
            vc = v_ref[pl.ds(off, LC), g * HEAD_DIM:(g + 1) * HEAD_DIM]
            m_old = m_sc[g]
            m_new = m_news[g]
            alpha = jnp.exp2(m_old - m_new)
            p = jnp.exp2(s_sc[g] - m_new)
            l_sc[g] = alpha * l_sc[g] + jnp.sum(_fold_rows(p, jnp.sum), axis=0, keepdims=True)
            acc_sc[g] = alpha * acc_sc[g] + _dot_tn(vc, p.astype(BF16))
            m_sc[g] = m_new
        return carry

    lax.fori_loop(0, nchunk, attn_chunk, 0)

    for g in range(N_KV_HEADS):
        o_t = acc_sc[g] / l_sc[g]
        for r in range(Q_PER_KV):
            h = g * Q_PER_KV + r
            o_ref[:, h * HEAD_DIM:(h + 1) * HEAD_DIM] = o_t[:, r * QB:(r + 1) * QB].T.astype(o_ref.dtype)


def _dsa_attention(h2, k2, kw, *, batch, seq, bn):
    nb = seq // QB
    n_sel = min(TOPK_MAX, seq // 4)
    q_cols = N_Q_HEADS * HEAD_DIM
    kv_cols = N_KV_HEADS * HEAD_DIM
    assert bn == kv_cols and q_cols == 4 * bn and seq % LC == 0 and LC >= n_sel
    kern = functools.partial(_dsa_kernel, n_sel=float(n_sel), seq=seq)
    rows = Q_PER_KV * QB
    return pl.pallas_call(
        kern, grid=(batch, nb),
        in_specs=[pl.BlockSpec((QB, q_cols), lambda b, i: (b * nb + i, 0)),
                  pl.BlockSpec((seq, kv_cols), lambda b, i: (b, 4)),
                  pl.BlockSpec((seq, kv_cols), lambda b, i: (b, 5)),
                  pl.BlockSpec((seq, LANES), lambda b, i: (b, 0)),
                  pl.BlockSpec((QB, bn), lambda b, i: (b * nb + i, 6)),
                  pl.BlockSpec((QB, LANES), lambda b, i: (b * nb + i, 0))],
        out_specs=pl.BlockSpec((QB, q_cols), lambda b, i: (b * nb + i, 0)),
        out_shape=jax.ShapeDtypeStruct((batch * seq, q_cols), BF16),
        scratch_shapes=[pltpu.VMEM((seq, QB), F32),
                        pltpu.VMEM((seq, QB), BF16),
                        pltpu.VMEM((N_IDX_HEADS // 2, LANES, 2 * QB), BF16),
                        pltpu.VMEM((N_KV_HEADS, HEAD_DIM + QB, rows), BF16),
                        pltpu.VMEM((N_KV_HEADS, 1, rows), F32),
                        pltpu.VMEM((N_KV_HEADS, 1, rows), F32),
                        pltpu.VMEM((N_KV_HEADS, HEAD_DIM, rows), F32),
                        pltpu.VMEM((N_KV_HEADS, LC, rows), F32)],
        compiler_params=_cparams(2), name="dsa_attention",
    )(h2, h2, h2, k2, h2, kw)


def _ln_rows(x, g, b):
    mu = jnp.mean(x, axis=-1, keepdims=True)
    xc = x - mu
    var = jnp.mean(xc * xc, axis=-1, keepdims=True)
    return xc * lax.rsqrt(var + LN_EPS) * g + b


def _pack_halves(y):
    half = y.shape[1] // 2
    return pltpu.pack_elementwise([y[:, :half], y[:, half:]], packed_dtype=BF16)


def _unpack_halves(p):
    lo = pltpu.unpack_elementwise(p, index=0, packed_dtype=BF16, unpacked_dtype=F32)
    hi = pltpu.unpack_elementwise(p, index=1, packed_dtype=BF16, unpacked_dtype=F32)
    return jnp.concatenate([lo, hi], axis=1)


def _ln_kernel(x_ref, g_ref, b_ref, o_ref, o2_ref, *, packed):
    y = _ln_rows(x_ref[...], g_ref[...], b_ref[...])
    o_ref[...] = y
    o2_ref[...] = _pack_halves(y) if packed else y.astype(o2_ref.dtype)


def _layer_norm(x, g, b, *, bm, packed=False):
    m, d = x.shape
    row = pl.BlockSpec((bm, d), lambda i: (i, 0))
    vec = pl.BlockSpec((1, d), lambda i: (0, 0))
    second = ((m, d // 2), U32) if packed else ((m, d), BF16)
    return pl.pallas_call(
        functools.partial(_ln_kernel, packed=packed), grid=(m // bm,), in_specs=[row, vec, vec],
        out_specs=[row, pl.BlockSpec((bm, second[0][1]), lambda i: (i, 0))],
        out_shape=[jax.ShapeDtypeStruct((m, d), F32), jax.ShapeDtypeStruct(*second)],
        compiler_params=_cparams(1), name="layer_norm",
    )(x, g.reshape(1, d), b.reshape(1, d))


def _xattn_kernel(q_ref, k_ref, v_ref, o_ref):
    dh = q_ref.shape[1] // N_MEM_HEADS
    for h in range(N_MEM_HEADS):
        cols = slice(h * dh, (h + 1) * dh)
        s = _dot_nt(q_ref[:, cols], k_ref[:, cols])
        p = jnp.exp(s - jnp.max(s, axis=1, keepdims=True))
        o = _dot(p.astype(BF16), v_ref[:, cols]) / jnp.sum(p, axis=1, keepdims=True)
        o_ref[:, cols] = o.astype(o_ref.dtype)


def _cross_attention(q, k, v, *, batch, seq, n_mem, bm):
    d = q.shape[1]
    nb = seq // bm
    return pl.pallas_call(
        _xattn_kernel, grid=(batch, nb),
        in_specs=[pl.BlockSpec((bm, d), lambda b, i: (b * nb + i, 0)),
                  pl.BlockSpec((n_mem, d), lambda b, i: (b, 0)),
                  pl.BlockSpec((n_mem, d), lambda b, i: (b, 0))],
        out_specs=pl.BlockSpec((bm, d), lambda b, i: (b * nb + i, 0)),
        out_shape=jax.ShapeDtypeStruct((batch * seq, d), BF16),
        compiler_params=_cparams(2), name="cross_attention",
    )(q, k, v)


def _router_kernel(x_ref, w_ref, b_ref, id_ref, gate_ref):
    logits = jnp.dot(x_ref[...], w_ref[...], preferred_element_type=F32,
                     precision=lax.Precision.HIGHEST) + b_ref[...]
    lane = lax.broadcasted_iota(I32, logits.shape, 1)
    lane_f = lane.astype(F32)
    big = float(LANES)
    is_g = lane < N_GROUPS
    gl = jnp.where(is_g, logits, -jnp.inf)
    g_max = jnp.max(gl, axis=1, keepdims=True)
    g_idx = jnp.min(jnp.where(gl == g_max, lane_f, big), axis=1, keepdims=True)
    g_gate = 1.0 / jnp.sum(jnp.where(is_g, jnp.exp(gl - g_max), 0.0), axis=1, keepdims=True)
    e_lo = N_GROUPS + g_idx * EXPERTS_PER_GROUP
    in_grp = (lane_f >= e_lo) & (lane_f < e_lo + EXPERTS_PER_GROUP)
    el = jnp.where(in_grp, logits, -jnp.inf)
    v1 = jnp.max(el, axis=1, keepdims=True)
    i1 = jnp.min(jnp.where(el == v1, lane_f, big), axis=1, keepdims=True)
    el2 = jnp.where(lane_f == i1, -jnp.inf, el)
    v2 = jnp.max(el2, axis=1, keepdims=True)
    i2 = jnp.min(jnp.where(el2 == v2, lane_f, big), axis=1, keepdims=True)
    z = jnp.exp(v2 - v1)
    w1 = g_gate / (1.0 + z)
    w2 = g_gate * z / (1.0 + z)
    ids = jnp.where(lane == 0, i1 - N_GROUPS, jnp.where(lane == 1, i2 - N_GROUPS, 0.0))
    id_ref[...] = ids.astype(I32)
    gate_ref[...] = jnp.where(lane == 0, w1, jnp.where(lane == 1, w2, 0.0))


def _router(x, w_r, b_r, *, bm):
    m, d = x.shape
    out = pl.BlockSpec((bm, LANES), lambda i: (i, 0))
    return pl.pallas_call(
        _router_kernel, grid=(m // bm,),
        in_specs=[pl.BlockSpec((bm, d), lambda i: (i, 0)),
                  pl.BlockSpec((d, LANES), lambda i: (0, 0)),
                  pl.BlockSpec((1, LANES), lambda i: (0, 0))],
        out_specs=[out, out],
        out_shape=[jax.ShapeDtypeStruct((m, LANES), I32), jax.ShapeDtypeStruct((m, LANES), F32)],
        compiler_params=_cparams(1), name="router",
    )(x, w_r, b_r)


def _rank_kernel(id_ref, rank_ref, cnt_ref, carry_ref):
    i = pl.program_id(0)
    bt = id_ref.shape[0]

    @pl.when(i == 0)
    def _():
        carry_ref[...] = jnp.zeros(carry_ref.shape, F32)

    ids = id_ref[...]
    lane = lax.broadcasted_iota(I32, (bt, LANES), 1)
    e0 = ids[:, 0:1]
    e1 = ids[:, 1:2] + N_EXPERTS
    hit0 = lane == e0
    hit1 = lane == e1
    onehot = jnp.where(hit0, 1.0, jnp.where(hit1, 1.0, 0.0))
    r_io = lax.broadcasted_iota(I32, (bt, bt), 0)
    c_io = lax.broadcasted_iota(I32, (bt, bt), 1)
    tri = jnp.where(c_io < r_io, 1.0, 0.0).astype(BF16)
    prefix = _dot(tri, onehot.astype(BF16)) + carry_ref[0:1, :]
    rank0 = jnp.sum(jnp.where(hit0, prefix, 0.0), axis=1, keepdims=True)
    rank1 = jnp.sum(jnp.where(hit1, prefix, 0.0), axis=1, keepdims=True)
    rank_ref[...] = jnp.where(lane == 0, rank0, jnp.where(lane == 1, rank1, 0.0))
    total = carry_ref[0:1, :] + jnp.sum(onehot, axis=0, keepdims=True)
    carry_ref[...] = jnp.broadcast_to(total, carry_ref.shape)
    cnt_ref[...] = jnp.broadcast_to(total, cnt_ref.shape)


def _slot_ranks(ids, *, bt):
    m = ids.shape[0]
    return pl.pallas_call(
        _rank_kernel, grid=(m // bt,),
        in_specs=[pl.BlockSpec((bt, LANES), lambda i: (i, 0))],
        out_specs=[pl.BlockSpec((bt, LANES), lambda i: (i, 0)),
                   pl.BlockSpec((8, LANES), lambda i: (0, 0))],
        out_shape=[jax.ShapeDtypeStruct((m, LANES), F32), jax.ShapeDtypeStruct((8, LANES), F32)],
        scratch_shapes=[pltpu.VMEM((8, LANES), F32)],
        compiler_params=_cparams(1), name="slot_ranks",
    )(ids)


def _pos_kernel(id_ref, rank_ref, tab_ref, pos_ref):
    ids = id_ref[...]
    ranks = rank_ref[...]
    lane = lax.broadcasted_iota(I32, ids.shape, 1)
    tab = tab_ref[...]
    base0 = jnp.sum(jnp.where(lane == ids[:, 0:1], tab, 0.0), axis=1, keepdims=True)
    base1 = jnp.sum(jnp.where(lane == ids[:, 1:2] + N_EXPERTS, tab, 0.0), axis=1, keepdims=True)
    pos = jnp.where(lane == 0, base0 + ranks[:, 0:1], jnp.where(lane == 1, base1 + ranks[:, 1:2], 0.0))
    pos_ref[...] = pos.astype(I32)


def _slot_positions(ids, ranks, table, *, bt):
    m = ids.shape[0]
    blk = pl.BlockSpec((bt, LANES), lambda i: (i, 0))
    return pl.pallas_call(
        _pos_kernel, grid=(m // bt,),
        in_specs=[blk, blk, pl.BlockSpec((1, LANES), lambda i: (0, 0))],
        out_specs=blk,
        out_shape=jax.ShapeDtypeStruct((m, LANES), I32),
        compiler_params=_cparams(1), name="slot_positions",
    )(ids, ranks, table)


def _row_copy(src_ref, src_row, dst_ref, dst_row, sem):
    return pltpu.make_async_copy(src_ref.at[pl.ds(src_row, 1)], dst_ref.at[pl.ds(dst_row, 1)], sem)


def _dispatch_kernel(pos0_ref, pos1_ref, x_ref, xs_in_ref, xs_ref, sem):
    del xs_in_ref
    i = pl.program_id(0)
    bt = x_ref.shape[0]

    def issue(r, carry):
        t = i * bt + r
        _row_copy(x_ref, r, xs_ref, pos0_ref[t], sem.at[0]).start()
        _row_copy(x_ref, r, xs_ref, pos1_ref[t], sem.at[1]).start()
        return carry

    lax.fori_loop(0, bt, issue, 0)

    def drain(r, carry):
        _row_copy(x_ref, 0, xs_ref, 0, sem.at[0]).wait()
        _row_copy(x_ref, 0, xs_ref, 0, sem.at[1]).wait()
        return carry

    lax.fori_loop(0, bt, drain, 0)


def _dispatch(x, pos0, pos1, n_slots, *, bt):
    m, d = x.shape
    xs0 = jnp.zeros((n_slots, d), x.dtype)
    return pl.pallas_call(
        _dispatch_kernel,
        grid_spec=pltpu.PrefetchScalarGridSpec(
            num_scalar_prefetch=2, grid=(m // bt,),
            in_specs=[pl.BlockSpec((bt, d), lambda i, p0, p1: (i, 0)),
                      pl.BlockSpec(memory_space=pl.ANY)],
            out_specs=pl.BlockSpec(memory_space=pl.ANY),
            scratch_shapes=[pltpu.SemaphoreType.DMA((2,))]),
        out_shape=jax.ShapeDtypeStruct((n_slots, d), x.dtype),
        input_output_aliases={3: 0},
        compiler_params=_cparams(1), name="dispatch",
    )(pos0, pos1, x, xs0)


def _ffn_kernel(blk_e_ref, nxt_e_ref, set_ref, nused_ref, x_ref, wg_hbm, wu_hbm, wd_hbm, y_ref,
                wg_bf, wu_bf, wd_bf, st_g, st_u, st_d, sem):
    i = pl.program_id(0)
    d, ff = wg_bf.shape[1], wg_bf.shape[2]
    rg, rd = d // W_GROUPS, ff // W_GROUPS

    n_slot = st_g.shape[0]

    def group_copies(e, g):
        slot = g % n_slot
        return (pltpu.make_async_copy(wg_hbm.at[e, pl.ds(g * rg, rg), :], st_g.at[slot], sem.at[0, slot]),
                pltpu.make_async_copy(wu_hbm.at[e, pl.ds(g * rg, rg), :], st_u.at[slot], sem.at[1, slot]),
                pltpu.make_async_copy(wd_hbm.at[e, pl.ds(g * rd, rd), :], st_d.at[slot], sem.at[2, slot]))

    def start(e, g):
        for c in group_copies(e, g):
            c.start()

    def finish(e, g, s):
        for c in group_copies(e, g):
            c.wait()
        slot = g % n_slot
        wg_bf[s, g * rg:(g + 1) * rg, :] = st_g[slot].astype(BF16)
        wu_bf[s, g * rg:(g + 1) * rg, :] = st_u[slot].astype(BF16)
        wd_bf[s, g * rd:(g + 1) * rd, :] = st_d[slot].astype(BF16)

    def load_expert(e, s, between=()):
        for g in range(n_slot):
            start(e, g)
        for g in range(W_GROUPS):
            if g < len(between):
                between[g]()
            finish(e, g, s)
            if g + n_slot < W_GROUPS:
                start(e, g + n_slot)

    e = blk_e_ref[i]
    s = set_ref[i]
    nxt = nxt_e_ref[i]
    first_of_expert = jnp.logical_or(i == 0, blk_e_ref[jnp.maximum(i - 1, 0)] != e)
    active = i < nused_ref[0]
    prefetch = jnp.logical_and(jnp.logical_and(active, first_of_expert), nxt >= 0)

    @pl.when(i == 0)
    def _():
        load_expert(e, s)

    def ffn_steps():
        vals = {}

        def gate_step():
            vals["x"] = _unpack_halves(x_ref[...]).astype(BF16)
            vals["gate"] = _dot(vals["x"], wg_bf[s])

        def up_step():
            gate = vals["gate"]
            vals["h"] = (gate * (1.0 / (1.0 + jnp.exp(-gate))) * _dot(vals["x"], wu_bf[s])).astype(BF16)

        def down_step():
            y_ref[...] = _pack_halves(_dot(vals["h"], wd_bf[s]))

        return gate_step, up_step, down_step

    @pl.when(prefetch)
    def _():
        load_expert(nxt, 1 - s, between=ffn_steps())

    @pl.when(jnp.logical_and(active, jnp.logical_not(prefetch)))
    def _():
        for step in ffn_steps():
            step()


def _expert_ffn(xs, blk_e, nxt_e, set_idx, nused, wg, wu, wd):
    n_slots, dp = xs.shape
    d, ff = wg.shape[1], wg.shape[2]
    nblk = n_slots // MOE_BLK
    assert dp * 2 == d and d % W_GROUPS == 0 and ff % W_GROUPS == 0 and W_GROUPS >= W_STAGE_SLOTS > 3

    def row_map(i, be, nx, si, nu):
        return (jnp.minimum(i, nu[0] - 1), 0)

    hbm = pl.BlockSpec(memory_space=pl.ANY)
    return pl.pallas_call(
        _ffn_kernel,
        grid_spec=pltpu.PrefetchScalarGridSpec(
            num_scalar_prefetch=4, grid=(nblk,),
            in_specs=[pl.BlockSpec((MOE_BLK, dp), row_map), hbm, hbm, hbm],
            out_specs=pl.BlockSpec((MOE_BLK, dp), row_map),
            scratch_shapes=[pltpu.VMEM((2, d, ff), BF16), pltpu.VMEM((2, d, ff), BF16), pltpu.VMEM((2, ff, d), BF16),
                            pltpu.VMEM((W_STAGE_SLOTS, d // W_GROUPS, ff), F32),
                            pltpu.VMEM((W_STAGE_SLOTS, d // W_GROUPS, ff), F32),
                            pltpu.VMEM((W_STAGE_SLOTS, ff // W_GROUPS, d), F32),
                            pltpu.SemaphoreType.DMA((3, W_STAGE_SLOTS))]),
        out_shape=jax.ShapeDtypeStruct((n_slots, dp), xs.dtype),
        input_output_aliases={4: 0},
        compiler_params=_cparams(1), name="expert_ffn",
    )(blk_e, nxt_e, set_idx, nused, xs, wg, wu, wd)


def _combine_kernel(pos0_ref, pos1_ref, x_ref, gate_ref, g_ref, b_ref, ys_ref, o_ref, y0_ref, y1_ref, sem):
    i = pl.program_id(0)
    bt = x_ref.shape[0]

    def issue(r, carry):
        t = i * bt + r
        _row_copy(ys_ref, pos0_ref[t], y0_ref, r, sem.at[0]).start()
        _row_copy(ys_ref, pos1_ref[t], y1_ref, r, sem.at[1]).start()
        return carry

    lax.fori_loop(0, bt, issue, 0)

    def drain(r, carry):
        _row_copy(ys_ref, 0, y0_ref, 0, sem.at[0]).wait()
        _row_copy(ys_ref, 0, y1_ref, 0, sem.at[1]).wait()
        return carry

    lax.fori_loop(0, bt, drain, 0)
    gates = gate_ref[...]
    f = _unpack_halves(y0_ref[...]) * gates[:, 0:1] + _unpack_halves(y1_ref[...]) * gates[:, 1:2]
    o_ref[...] = _ln_rows(ALPHA * x_ref[...] + f, g_ref[...], b_ref[...])


def _combine(x, gates, ys, pos0, pos1, g, b, *, bt):
    m, d = x.shape
    dp = ys.shape[1]
    row = pl.BlockSpec((bt, d), lambda i, p0, p1: (i, 0))
    vec = pl.BlockSpec((1, d), lambda i, p0, p1: (0, 0))
    return pl.pallas_call(
        _combine_kernel,
        grid_spec=pltpu.PrefetchScalarGridSpec(
            num_scalar_prefetch=2, grid=(m // bt,),
            in_specs=[row, pl.BlockSpec((bt, LANES), lambda i, p0, p1: (i, 0)), vec, vec,
                      pl.BlockSpec(memory_space=pl.ANY)],
            out_specs=row,
            scratch_shapes=[pltpu.VMEM((bt, dp), ys.dtype), pltpu.VMEM((bt, dp), ys.dtype),
                            pltpu.SemaphoreType.DMA((2,))]),
        out_shape=jax.ShapeDtypeStruct((m, d), F32),
        compiler_params=_cparams(1), name="combine_ln",
    )(pos0, pos1, x, gates, g.reshape(1, d), b.reshape(1, d), ys)


def _rope_tables(positions, n_rows):
    pos = positions.astype(F32).reshape(n_rows, 1)

    def cs(dim):
        inv = 1.0 / (ROPE_THETA ** (jnp.arange(0, dim, 2, dtype=F32) / dim))
        ang = pos * inv[None, :]
        return jnp.cos(ang), jnp.sin(ang)

    c_h, s_h = cs(HEAD_DIM)
    c_i, s_i = cs(IDX_DIM)
    z_i = jnp.zeros_like(s_i)
    cosf = jnp.concatenate([c_h, c_h], axis=1)
    sinf = jnp.concatenate([-s_h, s_h], axis=1)
    cos64 = jnp.concatenate([c_i, c_i, c_i, c_i], axis=1)
    sin_lo = jnp.concatenate([-s_i, z_i, -s_i, z_i], axis=1)
    sin_hi = jnp.concatenate([z_i, s_i, z_i, s_i], axis=1)
    return cosf, sinf, cos64, sin_lo, sin_hi


def kernel(x, mem, positions, w_in, pool_w, pool_scale, w_o, ln1_g, ln1_b, w_mq, w_mk, w_mv, w_mo, ln2_g, ln2_b,
           w_group_router, b_group_router, w_expert_router, b_expert_router, w_gate, w_up, w_down, ln3_g, ln3_b):
    batch, seq, d = x.shape
    n_mem = mem.shape[1]
    n = batch * seq
    bn_attn = N_KV_HEADS * HEAD_DIM
    tables = _rope_tables(positions, n)
    xf = x.reshape(n, d)
    n_slots = 2 * n + N_EXPERTS * MOE_BLK
    nblk = n_slots // MOE_BLK

    for l in range(w_in.shape[0]):
        x_bf = xf.astype(BF16)
        kw_col = MIX_POOL + 7 * bn_attn
        w_kw = jnp.pad(w_in[l][:, kw_col:], ((0, 0), (0, LANES - (w_in.shape[2] - kw_col))))

        v_pool = _matmul([x_bf], [w_in[l]], name="inproj_pool", bm=1024, bn=DENSE_BN, n=MIX_POOL)
        h2 = _inproj_attn(x_bf, w_in[l], tables, col_off_blocks=MIX_POOL // bn_attn, bm=1024, bn=bn_attn)
        k2, kw = _inproj_idx(x_bf, w_kw, tables, bm=1024)
        a_pool = _pool_mixer(v_pool, pool_w[l].astype(BF16), pool_scale[l], batch=batch, seq=seq, ts=512)
        a_attn = _dsa_attention(h2, k2, kw, batch=batch, seq=seq, bn=bn_attn)
        res = (xf, (512, DENSE_BN), lambda i, j: (i, j))
        pre = _matmul([a_pool, a_attn], [w_o[l], w_o[l]], name="outproj", bm=512, bn=DENSE_BN, n=d,
                      extras=(res,), epilogue=_residual_epilogue)
        x1, x1_bf = _layer_norm(pre, ln1_g[l], ln1_b[l], bm=256)

        mq_scale = (d // N_MEM_HEADS) ** -0.5

        def q_epilogue(acc, ex, outs):
            outs[0][...] = (acc * mq_scale).astype(outs[0].dtype)

        qm = _matmul([x1_bf], [w_mq[l]], name="mem_q", bm=1024, bn=DENSE_BN, n=d, out_dtype=BF16,
                     epilogue=q_epilogue)
        mem_bf = mem.reshape(batch * n_mem, d).astype(BF16)
        km = _matmul([mem_bf], [w_mk[l]], name="mem_k", bm=batch * n_mem, bn=DENSE_BN, n=d, out_dtype=BF16)
        vm = _matmul([mem_bf], [w_mv[l]], name="mem_v", bm=batch * n_mem, bn=DENSE_BN, n=d, out_dtype=BF16)
        om = _cross_attention(qm, km, vm, batch=batch, seq=seq, n_mem=n_mem, bm=512)
        res = (x1, (512, DENSE_BN), lambda i, j: (i, j))
        pre = _matmul([om], [w_mo[l]], name="mem_o", bm=512, bn=DENSE_BN, n=d, extras=(res,),
                      epilogue=_residual_epilogue)
        x2, x2_rows = _layer_norm(pre, ln2_g[l], ln2_b[l], bm=256, packed=True)

        w_r = jnp.pad(jnp.concatenate([w_group_router[l], w_expert_router[l]], axis=1),
                      ((0, 0), (0, LANES - N_GROUPS - N_EXPERTS)))
        b_r = jnp.pad(jnp.concatenate([b_group_router[l], b_expert_router[l]]),
                      (0, LANES - N_GROUPS - N_EXPERTS)).reshape(1, LANES)
        ids, gates = _router(x2, w_r, b_r, bm=512)
        ranks, totals = _slot_ranks(ids, bt=512)
        c0 = totals[0, :N_EXPERTS].astype(I32)
        c1 = totals[0, N_EXPERTS:].astype(I32)
        padded = ((c0 + c1 + MOE_BLK - 1) // MOE_BLK) * MOE_BLK
        pend = jnp.cumsum(padded)
        pstart = pend - padded
        table = jnp.concatenate([pstart, pstart + c0]).astype(F32).reshape(1, LANES)
        pos = _slot_positions(ids, ranks, table, bt=512)
        pos0, pos1 = pos[:, 0], pos[:, 1]
        nused = pend[-1] // MOE_BLK
        blk_i = jnp.minimum(jnp.arange(nblk, dtype=I32), nused - 1)
        blk_e = jnp.sum((pend[None, :] <= (blk_i * MOE_BLK)[:, None]).astype(I32), axis=1)
        blk_e = jnp.minimum(blk_e, N_EXPERTS - 1)
        later = blk_e[None, :] > blk_e[:, None]
        nxt_e = jnp.min(jnp.where(later, blk_e[None, :], N_EXPERTS), axis=1)
        nxt_e = jnp.where(nxt_e == N_EXPERTS, -1, nxt_e).astype(I32)
        new_e = jnp.concatenate([jnp.ones((1,), I32), (blk_e[1:] != blk_e[:-1]).astype(I32)])
        set_idx = (jnp.cumsum(new_e) - 1) % 2

        xs = _dispatch(x2_rows, pos0, pos1, n_slots, bt=256)
        ys = _expert_ffn(xs, blk_e, nxt_e, set_idx.astype(I32), nused.reshape(1).astype(I32),
                         w_gate[l], w_up[l], w_down[l])
        xf = _combine(x2, gates, ys, pos0, pos1, ln3_g[l], ln3_b[l], bt=256)
    return xf.reshape(batch, seq, d)
```

```python
import functools

import jax
import jax.numpy as jnp
from jax import lax
from jax.experimental import pallas as pl
from jax.experimental.pallas import tpu as pltpu

F32 = jnp.float32
BF16 = jnp.bfloat16
I32 = jnp.int32
U32 = jnp.uint32

MIX_POOL = 2048
N_POOL_GROUPS = 4
POOL_WINDOWS = (2, 4, 8, 16)
POOL_GW = MIX_POOL // N_POOL_GROUPS
HEAD_DIM = 128
N_Q_HEADS = 16
N_KV_HEADS = 4
Q_PER_KV = N_Q_HEADS // N_KV_HEADS
N_IDX_HEADS = 8
IDX_DIM = 64
TOPK_MAX = 256
ROPE_THETA = 10000.0
N_MEM_HEADS = 4
N_GROUPS = 8
EXPERTS_PER_GROUP = 8
N_EXPERTS = N_GROUPS * EXPERTS_PER_GROUP
LN_EPS = 1e-5
DEPTH = 1
ALPHA = (2.0 * DEPTH) ** 0.25

LANES = 128
QB = 128
LC = 512
DENSE_BN = 512
MOE_BLK = 256
W_GROUPS = 8
W_STAGE_SLOTS = 4
POOL_HALO = 16
NEG = -1e30
N_HALVINGS = 8
LOG2E = 1.4426950408889634
VMEM_LIMIT = 56 * 1024 * 1024


def _cparams(n_axes, vmem=VMEM_LIMIT):
    return pltpu.CompilerParams(dimension_semantics=("arbitrary",) * n_axes, vmem_limit_bytes=vmem)


def _dot(a, b):
    return jnp.dot(a, b, preferred_element_type=F32)


def _dot_nt(a, b):
    return lax.dot_general(a, b, (((1,), (1,)), ((), ())), preferred_element_type=F32)


def _dot_tn(a, b):
    return lax.dot_general(a, b, (((0,), (0,)), ((), ())), preferred_element_type=F32)


FOLD_ROWS = 64


def _fold_rows(x, reduce_fn):
    r, c = x.shape
    return reduce_fn(x.reshape(r // FOLD_ROWS, FOLD_ROWS, c), axis=0)


def _mm_kernel(*refs, n_lhs, n_extra, epilogue):
    lhs = refs[:n_lhs]
    ws = refs[n_lhs:2 * n_lhs]
    extras = refs[2 * n_lhs:2 * n_lhs + n_extra]
    outs = refs[2 * n_lhs + n_extra:-n_lhs]
    w_bf = refs[-n_lhs:]

    @pl.when(pl.program_id(1) == 0)
    def _():
        for l in range(n_lhs):
            w_bf[l][...] = ws[l][...].astype(BF16)

    acc = _dot(lhs[0][...], w_bf[0][...])
    for l in range(1, n_lhs):
        acc = acc + _dot(lhs[l][...], w_bf[l][...])
    epilogue(acc, extras, outs)


def _matmul(lhs_list, w_list, *, layer, name, bm, bn, n, w_col_off=0, extras=(), out_dtype=F32, epilogue=None):
    m = lhs_list[0].shape[0]
    grid = (n // bn, m // bm)
    in_specs = [pl.BlockSpec((bm, a.shape[1]), lambda j, i: (i, 0)) for a in lhs_list]
    in_specs += [pl.BlockSpec((None, a.shape[1], bn), (lambda j, i, l=l: (layer, l, j + w_col_off)))
                 for l, a in enumerate(lhs_list)]
    in_specs += [pl.BlockSpec(blk, (lambda j, i, f=f: f(i, j))) for (_, blk, f) in extras]
    if epilogue is None:
        def epilogue(acc, ex, outs):
            outs[0][...] = acc.astype(outs[0].dtype)
    kern = functools.partial(_mm_kernel, n_lhs=len(lhs_list), n_extra=len(extras), epilogue=epilogue)
    return pl.pallas_call(
        kern, grid=grid, in_specs=in_specs,
        out_specs=pl.BlockSpec((bm, bn), lambda j, i: (i, j)),
        out_shape=jax.ShapeDtypeStruct((m, n), out_dtype),
        scratch_shapes=[pltpu.VMEM((a.shape[1], bn), BF16) for a in lhs_list],
        compiler_params=_cparams(2), name=name,
    )(*lhs_list, *w_list, *[e[0] for e in extras])


def _residual_epilogue(acc, ex, outs):
    outs[0][...] = acc + ALPHA * ex[0][...]


def _rope128(a, cosf, sinf):
    return a * cosf + pltpu.roll(a, 64, 1) * sinf


def _rope64(a, cos64, sin_lo, sin_hi):
    return a * cos64 + pltpu.roll(a, 96, 1) * sin_lo + pltpu.roll(a, 32, 1) * sin_hi


def _inproj_kernel(x_ref, w_ref, cosf_ref, sinf_ref, cos64_ref, sinlo_ref, sinhi_ref,
                   h_ref, acc_ref, w_bf, *, q_scale):
    j = pl.program_id(0)

    @pl.when(pl.program_id(1) == 0)
    def _():
        w_bf[...] = w_ref[...].astype(BF16)

    acc_ref[...] = _dot(x_ref[...], w_bf[...])
    bn = acc_ref.shape[1]

    @pl.when(j < 5)
    def _():
        scale = jnp.where(j < 4, q_scale, 1.0).astype(F32)
        cosf = cosf_ref[...] * scale
        sinf = sinf_ref[...] * scale
        for c in range(bn // LANES):
            a = acc_ref[:, c * LANES:(c + 1) * LANES]
            h_ref[:, c * LANES:(c + 1) * LANES] = _rope128(a, cosf, sinf).astype(h_ref.dtype)

    @pl.when(j == 5)
    def _():
        h_ref[...] = acc_ref[...].astype(h_ref.dtype)

    @pl.when(j == 6)
    def _():
        for c in range(bn // LANES):
            a = acc_ref[:, c * LANES:(c + 1) * LANES]
            r = _rope64(a, cos64_ref[...], sinlo_ref[...], sinhi_ref[...])
            h_ref[:, c * LANES:(c + 1) * LANES] = r.astype(h_ref.dtype)


def _inproj_attn(x_bf, w, tables, *, layer, col_off_blocks, bm, bn):
    m, d = x_bf.shape
    n_tiles = 7
    q_scale = HEAD_DIM ** -0.5 * LOG2E
    tab_spec = pl.BlockSpec((bm, LANES), lambda j, i: (i, 0))
    return pl.pallas_call(
        functools.partial(_inproj_kernel, q_scale=q_scale), grid=(n_tiles, m // bm),
        in_specs=[pl.BlockSpec((bm, d), lambda j, i: (i, 0)),
                  pl.BlockSpec((None, d, bn), lambda j, i: (layer, 0, j + col_off_blocks))] + [tab_spec] * 5,
        out_specs=pl.BlockSpec((bm, bn), lambda j, i: (i, j)),
        out_shape=jax.ShapeDtypeStruct((m, n_tiles * bn), BF16),
        scratch_shapes=[pltpu.VMEM((bm, bn), F32), pltpu.VMEM((d, bn), BF16)],
        compiler_params=_cparams(2), name="inproj_attn",
    )(x_bf, w, *tables)


def _inproj_idx_kernel(x_ref, w_ref, cos64_ref, sinlo_ref, sinhi_ref, k2_ref, kw_ref, *, wi_scale):
    a = _dot(x_ref[...], w_ref[...].astype(BF16))
    r = _rope64(a, cos64_ref[...], sinlo_ref[...], sinhi_ref[...])
    lane = lax.broadcasted_iota(I32, a.shape, 1)
    kw_ref[...] = jnp.where(lane < IDX_DIM, r, a * wi_scale)
    k2_ref[...] = jnp.where(lane < IDX_DIM, r, pltpu.roll(r, 64, 1)).astype(k2_ref.dtype)


def _inproj_idx(x_bf, w_kw, tables, *, bm):
    m, d = x_bf.shape
    wi_scale = (N_IDX_HEADS ** -0.5) * (IDX_DIM ** -0.5)
    tab_spec = pl.BlockSpec((bm, LANES), lambda i: (i, 0))
    return pl.pallas_call(
        functools.partial(_inproj_idx_kernel, wi_scale=wi_scale), grid=(m // bm,),
        in_specs=[pl.BlockSpec((bm, d), lambda i: (i, 0)), pl.BlockSpec((d, LANES), lambda i: (0, 0))]
        + [tab_spec] * 3,
        out_specs=[tab_spec, tab_spec],
        out_shape=[jax.ShapeDtypeStruct((m, LANES), BF16), jax.ShapeDtypeStruct((m, LANES), F32)],
        compiler_params=_cparams(1), name="inproj_idx",
    )(x_bf, w_kw, *tables[2:])


def _pool_kernel(v_ref, pw_ref, ps_ref, o_ref, ext_ref):
    s = pl.program_id(1)
    ts = v_ref.shape[0]

    @pl.when(s == 0)
    def _():
        ext_ref[0:POOL_HALO, :] = jnp.zeros((POOL_HALO, ext_ref.shape[1]), F32)

    ext_ref[POOL_HALO:, :] = v_ref[...]
    t = s * ts + lax.broadcasted_iota(I32, (ts, 1), 0)
    for g, w in enumerate(POOL_WINDOWS):
        cols = slice(g * POOL_GW, (g + 1) * POOL_GW)
        e = ext_ref[:, cols]
        step = 1
        while step < w:
            e = e + pltpu.roll(e, step, 0)
            step *= 2
        win = e[POOL_HALO:, :]
        cnt = jnp.minimum(t + 1, w).astype(F32)
        pooled = win / cnt - v_ref[:, cols]
        mixed = _dot(pooled.astype(BF16), pw_ref[g])
        o_ref[:, cols] = (mixed * ps_ref[:, cols]).astype(o_ref.dtype)
    ext_ref[0:POOL_HALO, :] = v_ref[ts - POOL_HALO:, :]


def _pool_mixer(v_pool, pool_w_bf, pool_scale, *, batch, seq, ts):
    v3 = v_pool.reshape(batch, seq, MIX_POOL)
    out = pl.pallas_call(
        _pool_kernel, grid=(batch, seq // ts),
        in_specs=[pl.BlockSpec((None, ts, MIX_POOL), lambda b, s: (b, s, 0)),
                  pl.BlockSpec((N_POOL_GROUPS, POOL_GW, POOL_GW), lambda b, s: (0, 0, 0)),
                  pl.BlockSpec((1, MIX_POOL), lambda b, s: (0, 0))],
        out_specs=pl.BlockSpec((None, ts, MIX_POOL), lambda b, s: (b, s, 0)),
        out_shape=jax.ShapeDtypeStruct((batch, seq, MIX_POOL), BF16),
        scratch_shapes=[pltpu.VMEM((POOL_HALO + ts, MIX_POOL), F32)],
        compiler_params=_cparams(2), name="pool_mixer",
    )(v3, pool_w_bf, pool_scale.reshape(1, MIX_POOL))
    return out.reshape(batch * seq, MIX_POOL)


def _dsa_kernel(q_ref, k_ref, v_ref, k2_ref, qi_ref, kw_ref, o_ref,
                key_sc, bias_sc, xi_sc, qt_sc, m_sc, l_sc, acc_sc, s_sc, *, n_sel, seq):
    blk = pl.program_id(1)
    nchunk = blk // (LC // QB) + 1
    q_pos = blk * QB + lax.broadcasted_iota(I32, (1, QB), 1)
    k_iota = lax.broadcasted_iota(I32, (LC, 1), 0)
    sub = lax.broadcasted_iota(I32, (LANES, QB), 0)
    eye = jnp.where(sub == lax.broadcasted_iota(I32, (LANES, QB), 1), 1.0, 0.0).astype(BF16)

    def transposed(x):
        return _dot_nt(eye, x)

    for jj in range(N_IDX_HEADS // 2):
        xt = transposed(qi_ref[:, jj * LANES:(jj + 1) * LANES])
        xi_sc[jj, :, 0:QB] = jnp.where(sub < IDX_DIM, xt, 0.0).astype(BF16)
        xi_sc[jj, :, QB:2 * QB] = jnp.where(sub >= IDX_DIM, xt, 0.0).astype(BF16)
    for g in range(N_KV_HEADS):
        for r in range(Q_PER_KV):
            h = g * Q_PER_KV + r
            qt_sc[g, 0:HEAD_DIM, r * QB:(r + 1) * QB] = transposed(
                q_ref[:, h * HEAD_DIM:(h + 1) * HEAD_DIM]).astype(BF16)
            qt_sc[g, HEAD_DIM:, r * QB:(r + 1) * QB] = eye
    kw_t = kw_ref[...].T
    wi = [kw_t[IDX_DIM + h:IDX_DIM + h + 1, :] for h in range(N_IDX_HEADS)]

    inf = jnp.inf
    part = (FOLD_ROWS, QB)

    def score_chunk(c, carry):
        s_min, s_max = carry
        off = pl.multiple_of(c * LC, LC)
        k2c = k2_ref[pl.ds(off, LC), :]
        sc = jnp.zeros((LC, QB), F32)
        for jj in range(N_IDX_HEADS // 2):
            rel = jnp.maximum(_dot(k2c, xi_sc[jj]), 0.0)
            sc = sc + rel[:, 0:QB] * wi[2 * jj] + rel[:, QB:2 * QB] * wi[2 * jj + 1]
        causal = off + k_iota <= q_pos
        key_sc[pl.ds(off, LC), :] = jnp.where(causal, sc, -inf)
        s_min = jnp.minimum(s_min, _fold_rows(jnp.where(causal, sc, inf), jnp.min))
        s_max = jnp.maximum(s_max, _fold_rows(jnp.where(causal, sc, -inf), jnp.max))
        return s_min, s_max

    s_min, s_max = lax.fori_loop(0, nchunk, score_chunk, (jnp.full(part, inf, F32), jnp.full(part, -inf, F32)))

    def count(pred_fn):
        def body(c, acc):
            off = pl.multiple_of(c * LC, LC)
            m = pred_fn(key_sc[pl.ds(off, LC), :], off)
            return acc + _fold_rows(m, jnp.sum)
        acc = lax.fori_loop(0, nchunk, body, jnp.zeros(part, F32))
        return jnp.sum(acc, axis=0, keepdims=True)

    has_thr = q_pos + 1 >= int(n_sel)
    lo0 = jnp.where(has_thr, jnp.min(s_min, axis=0, keepdims=True), -inf)
    hi0 = jnp.where(has_thr, jnp.max(s_max, axis=0, keepdims=True), -inf)

    def midpoint(lo, hi):
        mid = 0.5 * (lo + hi)
        return jnp.where(mid <= lo, hi, mid)

    def halve(i, st):
        lo, hi = st
        mid = midpoint(lo, hi)
        enough = count(lambda sc, off: jnp.where(sc >= mid, 1.0, 0.0)) >= n_sel
        return jnp.where(enough, mid, lo), jnp.where(enough, hi, mid)

    lo0, hi0 = lax.fori_loop(0, N_HALVINGS, halve, (lo0, hi0))

    def open_rows(lo, hi):
        return jnp.max(jnp.where(lo < hi, 1.0, 0.0)) > 0.0

    def bisect_cond(st):
        it, lo, hi = st
        return open_rows(lo, hi) & (it < seq)

    def bisect_body(st):
        it, lo, hi = st
        mid = midpoint(lo, hi)

        def body(c, acc):
            cnt, up, dn = acc
            off = pl.multiple_of(c * LC, LC)
            sc = key_sc[pl.ds(off, LC), :]
            ge = sc >= mid
            cnt = cnt + _fold_rows(jnp.where(ge, 1.0, 0.0), jnp.sum)
            up = jnp.minimum(up, _fold_rows(jnp.where(ge, sc, inf), jnp.min))
            dn = jnp.maximum(dn, _fold_rows(jnp.where(ge, -inf, sc), jnp.max))
            return cnt, up, dn

        cnt, up, dn = lax.fori_loop(
            0, nchunk, body, (jnp.zeros(part, F32), jnp.full(part, inf, F32), jnp.full(part, -inf, F32)))
        enough = jnp.sum(cnt, axis=0, keepdims=True) >= n_sel
        is_open = lo < hi
        new_lo = jnp.where(is_open & enough, jnp.min(up, axis=0, keepdims=True), lo)
        new_hi = jnp.where(is_open & jnp.logical_not(enough), jnp.max(dn, axis=0, keepdims=True), hi)
        return it + 1, new_lo, new_hi

    _, thr, _ = lax.while_loop(bisect_cond, bisect_body, (jnp.int32(0), lo0, hi0))
    n_gt = count(lambda kk, off: jnp.where(kk > thr, 1.0, 0.0))
    n_eq = count(lambda kk, off: jnp.where(kk == thr, 1.0, 0.0))
    need = n_sel - n_gt

    def tie_search():
        def tbody(i, p):
            cand = p | jnp.left_shift(jnp.int32(1), (seq.bit_length() - 2) - i)
            cnt = count(lambda kk, off: jnp.where(kk == thr, jnp.where(off + k_iota < cand, 1.0, 0.0), 0.0))
            return jnp.where(cnt < need, cand, p)
        return lax.fori_loop(0, seq.bit_length() - 1, tbody, jnp.zeros((1, QB), I32))

    ambiguous = jnp.max(jnp.where(has_thr & (n_eq > need), 1.0, 0.0)) > 0.0
    tie_hi = lax.cond(ambiguous, tie_search, lambda: jnp.full((1, QB), seq, I32))
    tie_hi = jnp.where(has_thr, tie_hi, -1)

    def bias_chunk(c, carry):
        off = pl.multiple_of(c * LC, LC)
        kk = key_sc[pl.ds(off, LC), :]
        tie_ok = jnp.where(off + k_iota <= tie_hi, 0.0, NEG)
        bias = jnp.where(kk > thr, 0.0, jnp.where(kk == thr, tie_ok, NEG))
        bias_sc[pl.ds(off, LC), :] = bias.astype(BF16)
        return carry

    lax.fori_loop(0, nchunk, bias_chunk, 0)

    m_sc[...] = jnp.full(m_sc.shape, NEG, F32)
    l_sc[...] = jnp.zeros(l_sc.shape, F32)
    acc_sc[...] = jnp.zeros(acc_sc.shape, F32)

    def attn_chunk(c, carry):
        off = pl.multiple_of(c * LC, LC)
        bias = bias_sc[pl.ds(off, LC), :]

        def logits(g):
            kc = k_ref[pl.ds(off, LC), g * HEAD_DIM:(g + 1) * HEAD_DIM]
            return _dot(jnp.concatenate([kc, bias], axis=1), qt_sc[g])

        m_news = []
        for g in range(N_KV_HEADS):
            s = logits(g)
            s_sc[g] = s
            m_news.append(jnp.maximum(m_sc[g], jnp.max(_fold_rows(s, jnp.max), axis=0, keepdims=True)))
        for g in range(N_KV_HEADS):
            vc = v_ref[pl.ds(off, LC), g * HEAD_DIM:(g + 1) * HEAD_DIM]
            m_old = m_sc[g]
            m_new = m_news[g]
            alpha = jnp.exp2(m_old - m_new)
            p = jnp.exp2(s_sc[g] - m_new)
            l_sc[g] = alpha * l_sc[g] + jnp.sum(_fold_rows(p, jnp.sum), axis=0, keepdims=True)
            acc_sc[g] = alpha * acc_sc[g] + _dot_tn(vc, p.astype(BF16))
            m_sc[g] = m_new
        return carry

    lax.fori_loop(0, nchunk, attn_chunk, 0)

    for g in range(N_KV_HEADS):
        o_t = acc_sc[g] / l_sc[g]
        for r in range(Q_PER_KV):
            h = g * Q_PER_KV + r
            o_ref[:, h * HEAD_DIM:(h + 1) * HEAD_DIM] = o_t[:, r * QB:(r + 1) * QB].T.astype(o_ref.dtype)


def _dsa_attention(h2, k2, kw, *, batch, seq, bn):
    nb = seq // QB
    n_sel = min(TOPK_MAX, seq // 4)
    q_cols = N_Q_HEADS * HEAD_DIM
    kv_cols = N_KV_HEADS * HEAD_DIM
    assert bn == kv_cols and q_cols == 4 * bn and seq % LC == 0 and LC >= n_sel
    kern = functools.partial(_dsa_kernel, n_sel=float(n_sel), seq=seq)
    rows = Q_PER_KV * QB
    return pl.pallas_call(
        kern, grid=(batch, nb),
        in_specs=[pl.BlockSpec((QB, q_cols), lambda b, i: (b * nb + i, 0)),
                  pl.BlockSpec((seq, kv_cols), lambda b, i: (b, 4)),
                  pl.BlockSpec((seq, kv_cols), lambda b, i: (b, 5)),
                  pl.BlockSpec((seq, LANES), lambda b, i: (b, 0)),
                  pl.BlockSpec((QB, bn), lambda b, i: (b * nb + i, 6)),
                  pl.BlockSpec((QB, LANES), lambda b, i: (b * nb + i, 0))],
        out_specs=pl.BlockSpec((QB, q_cols), lambda b, i: (b * nb + i, 0)),
        out_shape=jax.ShapeDtypeStruct((batch * seq, q_cols), BF16),
        scratch_shapes=[pltpu.VMEM((seq, QB), F32),
                        pltpu.VMEM((seq, QB), BF16),
                        pltpu.VMEM((N_IDX_HEADS // 2, LANES, 2 * QB), BF16),
                        pltpu.VMEM((N_KV_HEADS, HEAD_DIM + QB, rows), BF16),
                        pltpu.VMEM((N_KV_HEADS, 1, rows), F32),
                        pltpu.VMEM((N_KV_HEADS, 1, rows), F32),
                        pltpu.VMEM((N_KV_HEADS, HEAD_DIM, rows), F32),
                        pltpu.VMEM((N_KV_HEADS, LC, rows), F32)],
        compiler_params=_cparams(2), name="dsa_attention",
    )(h2, h2, h2, k2, h2, kw)


def _ln_rows(x, g, b):
    mu = jnp.mean(x, axis=-1, keepdims=True)
    xc = x - mu
    var = jnp.mean(xc * xc, axis=-1, keepdims=True)
    return xc * lax.rsqrt(var + LN_EPS) * g + b


def _pack_halves(y):
    half = y.shape[1] // 2
    return pltpu.pack_elementwise([y[:, :half], y[:, half:]], packed_dtype=BF16)


def _unpack_halves(p):
    lo = pltpu.unpack_elementwise(p, index=0, packed_dtype=BF16, unpacked_dtype=F32)
    hi = pltpu.unpack_elementwise(p, index=1, packed_dtype=BF16, unpacked_dtype=F32)
    return jnp.concatenate([lo, hi], axis=1)


def _ln_kernel(x_ref, g_ref, b_ref, o_ref, o2_ref, *, packed):
    y = _ln_rows(x_ref[...], g_ref[...], b_ref[...])
    o_ref[...] = y
    o2_ref[...] = _pack_halves(y) if packed else y.astype(o2_ref.dtype)


def _layer_norm(x, g, b, *, bm, packed=False):
    m, d = x.shape
    row = pl.BlockSpec((bm, d), lambda i: (i, 0))
    vec = pl.BlockSpec((1, d), lambda i: (0, 0))
    second = ((m, d // 2), U32) if packed else ((m, d), BF16)
    return pl.pallas_call(
        functools.partial(_ln_kernel, packed=packed), grid=(m // bm,), in_specs=[row, vec, vec],
        out_specs=[row, pl.BlockSpec((bm, second[0][1]), lambda i: (i, 0))],
        out_shape=[jax.ShapeDtypeStruct((m, d), F32), jax.ShapeDtypeStruct(*second)],
        compiler_params=_cparams(1), name="layer_norm",
    )(x, g.reshape(1, d), b.reshape(1, d))


def _xattn_kernel(q_ref, k_ref, v_ref, o_ref):
    dh = q_ref.shape[1] // N_MEM_HEADS
    for h in range(N_MEM_HEADS):
        cols = slice(h * dh, (h + 1) * dh)
        s = _dot_nt(q_ref[:, cols], k_ref[:, cols])
        p = jnp.exp(s - jnp.max(s, axis=1, keepdims=True))
        o = _dot(p.astype(BF16), v_ref[:, cols]) / jnp.sum(p, axis=1, keepdims=True)
        o_ref[:, cols] = o.astype(o_ref.dtype)


def _cross_attention(q, k, v, *, batch, seq, n_mem, bm):
    d = q.shape[1]
    nb = seq // bm
    return pl.pallas_call(
        _xattn_kernel, grid=(batch, nb),
        in_specs=[pl.BlockSpec((bm, d), lambda b, i: (b * nb + i, 0)),
                  pl.BlockSpec((n_mem, d), lambda b, i: (b, 0)),
                  pl.BlockSpec((n_mem, d), lambda b, i: (b, 0))],
        out_specs=pl.BlockSpec((bm, d), lambda b, i: (b * nb + i, 0)),
        out_shape=jax.ShapeDtypeStruct((batch * seq, d), BF16),
        compiler_params=_cparams(2), name="cross_attention",
    )(q, k, v)


def _router_kernel(x_ref, w_ref, b_ref, id_ref, gate_ref):
    logits = jnp.dot(x_ref[...], w_ref[...], preferred_element_type=F32,
                     precision=lax.Precision.HIGHEST) + b_ref[...]
    lane = lax.broadcasted_iota(I32, logits.shape, 1)
    lane_f = lane.astype(F32)
    big = float(LANES)
    is_g = lane < N_GROUPS
    gl = jnp.where(is_g, logits, -jnp.inf)
    g_max = jnp.max(gl, axis=1, keepdims=True)
    g_idx = jnp.min(jnp.where(gl == g_max, lane_f, big), axis=1, keepdims=True)
    g_gate = 1.0 / jnp.sum(jnp.where(is_g, jnp.exp(gl - g_max), 0.0), axis=1, keepdims=True)
    e_lo = N_GROUPS + g_idx * EXPERTS_PER_GROUP
    in_grp = (lane_f >= e_lo) & (lane_f < e_lo + EXPERTS_PER_GROUP)
    el = jnp.where(in_grp, logits, -jnp.inf)
    v1 = jnp.max(el, axis=1, keepdims=True)
    i1 = jnp.min(jnp.where(el == v1, lane_f, big), axis=1, keepdims=True)
    el2 = jnp.where(lane_f == i1, -jnp.inf, el)
    v2 = jnp.max(el2, axis=1, keepdims=True)
    i2 = jnp.min(jnp.where(el2 == v2, lane_f, big), axis=1, keepdims=True)
    z = jnp.exp(v2 - v1)
    w1 = g_gate / (1.0 + z)
    w2 = g_gate * z / (1.0 + z)
    ids = jnp.where(lane == 0, i1 - N_GROUPS, jnp.where(lane == 1, i2 - N_GROUPS, 0.0))
    id_ref[...] = ids.astype(I32)
    gate_ref[...] = jnp.where(lane == 0, w1, jnp.where(lane == 1, w2, 0.0))


def _router(x, w_r, b_r, *, bm):
    m, d = x.shape
    out = pl.BlockSpec((bm, LANES), lambda i: (i, 0))
    return pl.pallas_call(
        _router_kernel, grid=(m // bm,),
        in_specs=[pl.BlockSpec((bm, d), lambda i: (i, 0)),
                  pl.BlockSpec((d, LANES), lambda i: (0, 0)),
                  pl.BlockSpec((1, LANES), lambda i: (0, 0))],
        out_specs=[out, out],
        out_shape=[jax.ShapeDtypeStruct((m, LANES), I32), jax.ShapeDtypeStruct((m, LANES), F32)],
        compiler_params=_cparams(1), name="router",
    )(x, w_r, b_r)


def _rank_kernel(id_ref, rank_ref, cnt_ref, carry_ref):
    i = pl.program_id(0)
    bt = id_ref.shape[0]

    @pl.when(i == 0)
    def _():
        carry_ref[...] = jnp.zeros(carry_ref.shape, F32)

    ids = id_ref[...]
    lane = lax.broadcasted_iota(I32, (bt, LANES), 1)
    e0 = ids[:, 0:1]
    e1 = ids[:, 1:2] + N_EXPERTS
    hit0 = lane == e0
    hit1 = lane == e1
    onehot = jnp.where(hit0, 1.0, jnp.where(hit1, 1.0, 0.0))
    r_io = lax.broadcasted_iota(I32, (bt, bt), 0)
    c_io = lax.broadcasted_iota(I32, (bt, bt), 1)
    tri = jnp.where(c_io < r_io, 1.0, 0.0).astype(BF16)
    prefix = _dot(tri, onehot.astype(BF16)) + carry_ref[0:1, :]
    rank0 = jnp.sum(jnp.where(hit0, prefix, 0.0), axis=1, keepdims=True)
    rank1 = jnp.sum(jnp.where(hit1, prefix, 0.0), axis=1, keepdims=True)
    rank_ref[...] = jnp.where(lane == 0, rank0, jnp.where(lane == 1, rank1, 0.0))
    total = carry_ref[0:1, :] + jnp.sum(onehot, axis=0, keepdims=True)
    carry_ref[...] = jnp.broadcast_to(total, carry_ref.shape)
    cnt_ref[...] = jnp.broadcast_to(total, cnt_ref.shape)


def _slot_ranks(ids, *, bt):
    m = ids.shape[0]
    return pl.pallas_call(
        _rank_kernel, grid=(m // bt,),
        in_specs=[pl.BlockSpec((bt, LANES), lambda i: (i, 0))],
        out_specs=[pl.BlockSpec((bt, LANES), lambda i: (i, 0)),
                   pl.BlockSpec((8, LANES), lambda i: (0, 0))],
        out_shape=[jax.ShapeDtypeStruct((m, LANES), F32), jax.ShapeDtypeStruct((8, LANES), F32)],
        scratch_shapes=[pltpu.VMEM((8, LANES), F32)],
        compiler_params=_cparams(1), name="slot_ranks",
    )(ids)


def _pos_kernel(id_ref, rank_ref, tab_ref, pos_ref):
    ids = id_ref[...]
    ranks = rank_ref[...]
    lane = lax.broadcasted_iota(I32, ids.shape, 1)
    tab = tab_ref[...]
    base0 = jnp.sum(jnp.where(lane == ids[:, 0:1], tab, 0.0), axis=1, keepdims=True)
    base1 = jnp.sum(jnp.where(lane == ids[:, 1:2] + N_EXPERTS, tab, 0.0), axis=1, keepdims=True)
    pos = jnp.where(lane == 0, base0 + ranks[:, 0:1], jnp.where(lane == 1, base1 + ranks[:, 1:2], 0.0))
    pos_ref[...] = pos.astype(I32)


def _slot_positions(ids, ranks, table, *, bt):
    m = ids.shape[0]
    blk = pl.BlockSpec((bt, LANES), lambda i: (i, 0))
    return pl.pallas_call(
        _pos_kernel, grid=(m // bt,),
        in_specs=[blk, blk, pl.BlockSpec((1, LANES), lambda i: (0, 0))],
        out_specs=blk,
        out_shape=jax.ShapeDtypeStruct((m, LANES), I32),
        compiler_params=_cparams(1), name="slot_positions",
    )(ids, ranks, table)


def _row_copy(src_ref, src_row, dst_ref, dst_row, sem):
    return pltpu.make_async_copy(src_ref.at[pl.ds(src_row, 1)], dst_ref.at[pl.ds(dst_row, 1)], sem)


def _dispatch_kernel(pos0_ref, pos1_ref, x_ref, xs_in_ref, xs_ref, sem):
    del xs_in_ref
    i = pl.program_id(0)
    bt = x_ref.shape[0]

    def issue(r, carry):
        t = i * bt + r
        _row_copy(x_ref, r, xs_ref, pos0_ref[t], sem.at[0]).start()
        _row_copy(x_ref, r, xs_ref, pos1_ref[t], sem.at[1]).start()
        return carry

    lax.fori_loop(0, bt, issue, 0)
    for k in range(2):
        pltpu.make_async_copy(x_ref, xs_ref.at[pl.ds(0, bt)], sem.at[k]).wait()


def _dispatch(x, pos0, pos1, n_slots, *, bt):
    m, d = x.shape
    xs0 = jnp.zeros((n_slots, d), x.dtype)
    return pl.pallas_call(
        _dispatch_kernel,
        grid_spec=pltpu.PrefetchScalarGridSpec(
            num_scalar_prefetch=2, grid=(m // bt,),
            in_specs=[pl.BlockSpec((bt, d), lambda i, p0, p1: (i, 0)),
                      pl.BlockSpec(memory_space=pl.ANY)],
            out_specs=pl.BlockSpec(memory_space=pl.ANY),
            scratch_shapes=[pltpu.SemaphoreType.DMA((2,))]),
        out_shape=jax.ShapeDtypeStruct((n_slots, d), x.dtype),
        input_output_aliases={3: 0},
        compiler_params=_cparams(1), name="dispatch",
    )(pos0, pos1, x, xs0)


def _ffn_kernel(blk_e_ref, nxt_e_ref, set_ref, nused_ref, x_ref, wg_hbm, wu_hbm, wd_hbm, y_ref,
                wg_bf, wu_bf, wd_bf, st_g, st_u, st_d, sem):
    i = pl.program_id(0)
    d, ff = wg_bf.shape[1], wg_bf.shape[2]
    rg, rd = d // W_GROUPS, ff // W_GROUPS

    n_slot = st_g.shape[0]

    def group_copies(e, g):
        slot = g % n_slot
        return (pltpu.make_async_copy(wg_hbm.at[e, pl.ds(g * rg, rg), :], st_g.at[slot], sem.at[0, slot]),
                pltpu.make_async_copy(wu_hbm.at[e, pl.ds(g * rg, rg), :], st_u.at[slot], sem.at[1, slot]),
                pltpu.make_async_copy(wd_hbm.at[e, pl.ds(g * rd, rd), :], st_d.at[slot], sem.at[2, slot]))

    def start(e, g):
        for c in group_copies(e, g):
            c.start()

    def finish(e, g, s):
        for c in group_copies(e, g):
            c.wait()
        slot = g % n_slot
        wg_bf[s, g * rg:(g + 1) * rg, :] = st_g[slot].astype(BF16)
        wu_bf[s, g * rg:(g + 1) * rg, :] = st_u[slot].astype(BF16)
        wd_bf[s, g * rd:(g + 1) * rd, :] = st_d[slot].astype(BF16)

    def load_expert(e, s, between=()):
        for g in range(n_slot):
            start(e, g)
        for g in range(W_GROUPS):
            if g < len(between):
                between[g]()
            finish(e, g, s)
            if g + n_slot < W_GROUPS:
                start(e, g + n_slot)

    e = blk_e_ref[i]
    s = set_ref[i]
    nxt = nxt_e_ref[i]
    first_of_expert = jnp.logical_or(i == 0, blk_e_ref[jnp.maximum(i - 1, 0)] != e)
    active = i < nused_ref[0]
    prefetch = jnp.logical_and(jnp.logical_and(active, first_of_expert), nxt >= 0)

    @pl.when(i == 0)
    def _():
        load_expert(e, s)

    def ffn_steps():
        vals = {}

        def gate_step():
            vals["x"] = _unpack_halves(x_ref[...]).astype(BF16)
            vals["gate"] = _dot(vals["x"], wg_bf[s])

        def up_step():
            gate = vals["gate"]
            vals["h"] = (gate * (1.0 / (1.0 + jnp.exp(-gate))) * _dot(vals["x"], wu_bf[s])).astype(BF16)

        def down_step():
            y_ref[...] = _pack_halves(_dot(vals["h"], wd_bf[s]))

        return gate_step, up_step, down_step

    @pl.when(prefetch)
    def _():
        load_expert(nxt, 1 - s, between=ffn_steps())

    @pl.when(jnp.logical_and(active, jnp.logical_not(prefetch)))
    def _():
        for step in ffn_steps():
            step()


def _expert_ffn(xs, blk_e, nxt_e, set_idx, nused, wg, wu, wd):
    n_slots, dp = xs.shape
    d, ff = wg.shape[1], wg.shape[2]
    nblk = n_slots // MOE_BLK
    assert dp * 2 == d and d % W_GROUPS == 0 and ff % W_GROUPS == 0 and W_GROUPS >= W_STAGE_SLOTS > 3

    def row_map(i, be, nx, si, nu):
        return (jnp.minimum(i, nu[0] - 1), 0)

    hbm = pl.BlockSpec(memory_space=pl.ANY)
    return pl.pallas_call(
        _ffn_kernel,
        grid_spec=pltpu.PrefetchScalarGridSpec(
            num_scalar_prefetch=4, grid=(nblk,),
            in_specs=[pl.BlockSpec((MOE_BLK, dp), row_map), hbm, hbm, hbm],
            out_specs=pl.BlockSpec((MOE_BLK, dp), row_map),
            scratch_shapes=[pltpu.VMEM((2, d, ff), BF16), pltpu.VMEM((2, d, ff), BF16), pltpu.VMEM((2, ff, d), BF16),
                            pltpu.VMEM((W_STAGE_SLOTS, d // W_GROUPS, ff), F32),
                            pltpu.VMEM((W_STAGE_SLOTS, d // W_GROUPS, ff), F32),
                            pltpu.VMEM((W_STAGE_SLOTS, ff // W_GROUPS, d), F32),
                            pltpu.SemaphoreType.DMA((3, W_STAGE_SLOTS))]),
        out_shape=jax.ShapeDtypeStruct((n_slots, dp), xs.dtype),
        input_output_aliases={4: 0},
        compiler_params=_cparams(1), name="expert_ffn",
    )(blk_e, nxt_e, set_idx, nused, xs, wg, wu, wd)


def _combine_kernel(pos0_ref, pos1_ref, x_ref, gate_ref, g_ref, b_ref, ys_ref, o_ref, y0_ref, y1_ref, sem):
    i = pl.program_id(0)
    bt = x_ref.shape[0]

    def issue(r, carry):
        t = i * bt + r
        _row_copy(ys_ref, pos0_ref[t], y0_ref, r, sem.at[0]).start()
        _row_copy(ys_ref, pos1_ref[t], y1_ref, r, sem.at[1]).start()
        return carry

    lax.fori_loop(0, bt, issue, 0)
    pltpu.make_async_copy(ys_ref.at[pl.ds(0, bt)], y0_ref, sem.at[0]).wait()
    pltpu.make_async_copy(ys_ref.at[pl.ds(0, bt)], y1_ref, sem.at[1]).wait()
    gates = gate_ref[...]
    f = _unpack_halves(y0_ref[...]) * gates[:, 0:1] + _unpack_halves(y1_ref[...]) * gates[:, 1:2]
    o_ref[...] = _ln_rows(ALPHA * x_ref[...] + f, g_ref[...], b_ref[...])


def _combine(x, gates, ys, pos0, pos1, g, b, *, bt):
    m, d = x.shape
    dp = ys.shape[1]
    row = pl.BlockSpec((bt, d), lambda i, p0, p1: (i, 0))
    vec = pl.BlockSpec((1, d), lambda i, p0, p1: (0, 0))
    return pl.pallas_call(
        _combine_kernel,
        grid_spec=pltpu.PrefetchScalarGridSpec(
            num_scalar_prefetch=2, grid=(m // bt,),
            in_specs=[row, pl.BlockSpec((bt, LANES), lambda i, p0, p1: (i, 0)), vec, vec,
                      pl.BlockSpec(memory_space=pl.ANY)],
            out_specs=row,
            scratch_shapes=[pltpu.VMEM((bt, dp), ys.dtype), pltpu.VMEM((bt, dp), ys.dtype),
                            pltpu.SemaphoreType.DMA((2,))]),
        out_shape=jax.ShapeDtypeStruct((m, d), F32),
        compiler_params=_cparams(1), name="combine_ln",
    )(pos0, pos1, x, gates, g.reshape(1, d), b.reshape(1, d), ys)


def _rope_tables(positions, n_rows):
    pos = positions.astype(F32).reshape(n_rows, 1)

    def cs(dim):
        inv = 1.0 / (ROPE_THETA ** (jnp.arange(0, dim, 2, dtype=F32) / dim))
        ang = pos * inv[None, :]
        return jnp.cos(ang), jnp.sin(ang)

    c_h, s_h = cs(HEAD_DIM)
    c_i, s_i = cs(IDX_DIM)
    z_i = jnp.zeros_like(s_i)
    cosf = jnp.concatenate([c_h, c_h], axis=1)
    sinf = jnp.concatenate([-s_h, s_h], axis=1)
    cos64 = jnp.concatenate([c_i, c_i, c_i, c_i], axis=1)
    sin_lo = jnp.concatenate([-s_i, z_i, -s_i, z_i], axis=1)
    sin_hi = jnp.concatenate([z_i, s_i, z_i, s_i], axis=1)
    return cosf, sinf, cos64, sin_lo, sin_hi


def kernel(x, mem, positions, w_in, pool_w, pool_scale, w_o, ln1_g, ln1_b, w_mq, w_mk, w_mv, w_mo, ln2_g, ln2_b,
           w_group_router, b_group_router, w_expert_router, b_expert_router, w_gate, w_up, w_down, ln3_g, ln3_b):
    batch, seq, d = x.shape
    n_mem = mem.shape[1]
    n = batch * seq
    bn_attn = N_KV_HEADS * HEAD_DIM
    tables = _rope_tables(positions, n)
    xf = x.reshape(n, d)
    n_slots = 2 * n + N_EXPERTS * MOE_BLK
    nblk = n_slots // MOE_BLK

    for l in range(w_in.shape[0]):
        x_bf = xf.astype(BF16)
        kw_col = MIX_POOL + 7 * bn_attn
        w_kw = jnp.pad(w_in[l][:, kw_col:], ((0, 0), (0, LANES - (w_in.shape[2] - kw_col))))

        v_pool = _matmul([x_bf], [w_in], layer=l, name="inproj_pool", bm=1024, bn=DENSE_BN, n=MIX_POOL)
        h2 = _inproj_attn(x_bf, w_in, tables, layer=l, col_off_blocks=MIX_POOL // bn_attn, bm=1024, bn=bn_attn)
        k2, kw = _inproj_idx(x_bf, w_kw, tables, bm=1024)
        a_pool = _pool_mixer(v_pool, pool_w[l].astype(BF16), pool_scale[l], batch=batch, seq=seq, ts=512)
        a_attn = _dsa_attention(h2, k2, kw, batch=batch, seq=seq, bn=bn_attn)
        res = (xf, (1024, DENSE_BN), lambda i, j: (i, j))
        pre = _matmul([a_pool, a_attn], [w_o, w_o], layer=l, name="outproj", bm=1024, bn=DENSE_BN, n=d,
                      extras=(res,), epilogue=_residual_epilogue)
        x1, x1_bf = _layer_norm(pre, ln1_g[l], ln1_b[l], bm=256)

        mq_scale = (d // N_MEM_HEADS) ** -0.5

        def q_epilogue(acc, ex, outs):
            outs[0][...] = (acc * mq_scale).astype(outs[0].dtype)

        qm = _matmul([x1_bf], [w_mq], layer=l, name="mem_q", bm=1024, bn=DENSE_BN, n=d, out_dtype=BF16,
                     epilogue=q_epilogue)
        mem_bf = mem.reshape(batch * n_mem, d).astype(BF16)
        km = _matmul([mem_bf], [w_mk], layer=l, name="mem_k", bm=batch * n_mem, bn=DENSE_BN, n=d, out_dtype=BF16)
        vm = _matmul([mem_bf], [w_mv], layer=l, name="mem_v", bm=batch * n_mem, bn=DENSE_BN, n=d, out_dtype=BF16)
        om = _cross_attention(qm, km, vm, batch=batch, seq=seq, n_mem=n_mem, bm=512)
        res = (x1, (1024, DENSE_BN), lambda i, j: (i, j))
        pre = _matmul([om], [w_mo], layer=l, name="mem_o", bm=1024, bn=DENSE_BN, n=d, extras=(res,),
                      epilogue=_residual_epilogue)
        x2, x2_rows = _layer_norm(pre, ln2_g[l], ln2_b[l], bm=256, packed=True)

        w_r = jnp.pad(jnp.concatenate([w_group_router[l], w_expert_router[l]], axis=1),
                      ((0, 0), (0, LANES - N_GROUPS - N_EXPERTS)))
        b_r = jnp.pad(jnp.concatenate([b_group_router[l], b_expert_router[l]]),
                      (0, LANES - N_GROUPS - N_EXPERTS)).reshape(1, LANES)
        ids, gates = _router(x2, w_r, b_r, bm=512)
        ranks, totals = _slot_ranks(ids, bt=512)
        c0 = totals[0, :N_EXPERTS].astype(I32)
        c1 = totals[0, N_EXPERTS:].astype(I32)
        padded = ((c0 + c1 + MOE_BLK - 1) // MOE_BLK) * MOE_BLK
        pend = jnp.cumsum(padded)
        pstart = pend - padded
        table = jnp.concatenate([pstart, pstart + c0]).astype(F32).reshape(1, LANES)
        pos = _slot_positions(ids, ranks, table, bt=512)
        pos0, pos1 = pos[:, 0], pos[:, 1]
        nused = pend[-1] // MOE_BLK
        blk_i = jnp.minimum(jnp.arange(nblk, dtype=I32), nused - 1)
        blk_e = jnp.sum((pend[None, :] <= (blk_i * MOE_BLK)[:, None]).astype(I32), axis=1)
        blk_e = jnp.minimum(blk_e, N_EXPERTS - 1)
        later = blk_e[None, :] > blk_e[:, None]
        nxt_e = jnp.min(jnp.where(later, blk_e[None, :], N_EXPERTS), axis=1)
        nxt_e = jnp.where(nxt_e == N_EXPERTS, -1, nxt_e).astype(I32)
        new_e = jnp.concatenate([jnp.ones((1,), I32), (blk_e[1:] != blk_e[:-1]).astype(I32)])
        set_idx = (jnp.cumsum(new_e) - 1) % 2

        xs = _dispatch(x2_rows, pos0, pos1, n_slots, bt=256)
        ys = _expert_ffn(xs, blk_e, nxt_e, set_idx.astype(I32), nused.reshape(1).astype(I32),
                         w_gate[l], w_up[l], w_down[l])
        xf = _combine(x2, gates, ys, pos0, pos1, ln3_g[l], ln3_b[l], bt=256)
    return xf.reshape(batch, seq, d)
```

```python
import functools

import jax
import jax.numpy as jnp
from jax import lax
from jax.experimental import pallas as pl
from jax.experimental.pallas import tpu as pltpu

F32 = jnp.float32
BF16 = jnp.bfloat16
I32 = jnp.int32
U32 = jnp.uint32

MIX_POOL = 2048
N_POOL_GROUPS = 4
POOL_WINDOWS = (2, 4, 8, 16)
POOL_GW = MIX_POOL // N_POOL_GROUPS
HEAD_DIM = 128
N_Q_HEADS = 16
N_KV_HEADS = 4
Q_PER_KV = N_Q_HEADS // N_KV_HEADS
N_IDX_HEADS = 8
IDX_DIM = 64
TOPK_MAX = 256
ROPE_THETA = 10000.0
N_MEM_HEADS = 4
N_GROUPS = 8
EXPERTS_PER_GROUP = 8
N_EXPERTS = N_GROUPS * EXPERTS_PER_GROUP
LN_EPS = 1e-5
DEPTH = 1
ALPHA = (2.0 * DEPTH) ** 0.25

LANES = 128
QB = 128
LC = 512
SC = 256
DENSE_BN = 512
MOE_BLK = 256
W_GROUPS = 8
W_STAGE_SLOTS = 4
POOL_HALO = 16
NEG = -1e30
N_HALVINGS = 8
LOG2E = 1.4426950408889634
VMEM_LIMIT = 56 * 1024 * 1024


def _cparams(n_axes, vmem=VMEM_LIMIT):
    return pltpu.CompilerParams(dimension_semantics=("arbitrary",) * n_axes, vmem_limit_bytes=vmem)


def _dot(a, b):
    return jnp.dot(a, b, preferred_element_type=F32)


def _dot_nt(a, b):
    return lax.dot_general(a, b, (((1,), (1,)), ((), ())), preferred_element_type=F32)


def _dot_tn(a, b):
    return lax.dot_general(a, b, (((0,), (0,)), ((), ())), preferred_element_type=F32)


FOLD_ROWS = 64


def _fold_rows(x, reduce_fn):
    r, c = x.shape
    return reduce_fn(x.reshape(r // FOLD_ROWS, FOLD_ROWS, c), axis=0)


def _mm_kernel(*refs, n_lhs, n_extra, epilogue):
    lhs = refs[:n_lhs]
    ws = refs[n_lhs:2 * n_lhs]
    extras = refs[2 * n_lhs:2 * n_lhs + n_extra]
    outs = refs[2 * n_lhs + n_extra:-n_lhs]
    w_bf = refs[-n_lhs:]

    @pl.when(pl.program_id(1) == 0)
    def _():
        for l in range(n_lhs):
            w_bf[l][...] = ws[l][...].astype(BF16)

    acc = _dot(lhs[0][...], w_bf[0][...])
    for l in range(1, n_lhs):
        acc = acc + _dot(lhs[l][...], w_bf[l][...])
    epilogue(acc, extras, outs)


def _matmul(lhs_list, w_list, *, layer, name, bm, bn, n, w_col_off=0, extras=(), out_dtype=F32, epilogue=None):
    m = lhs_list[0].shape[0]
    grid = (n // bn, m // bm)
    in_specs = [pl.BlockSpec((bm, a.shape[1]), lambda j, i: (i, 0)) for a in lhs_list]
    in_specs += [pl.BlockSpec((None, a.shape[1], bn), (lambda j, i, l=l: (layer, l, j + w_col_off)))
                 for l, a in enumerate(lhs_list)]
    in_specs += [pl.BlockSpec(blk, (lambda j, i, f=f: f(i, j))) for (_, blk, f) in extras]
    if epilogue is None:
        def epilogue(acc, ex, outs):
            outs[0][...] = acc.astype(outs[0].dtype)
    kern = functools.partial(_mm_kernel, n_lhs=len(lhs_list), n_extra=len(extras), epilogue=epilogue)
    return pl.pallas_call(
        kern, grid=grid, in_specs=in_specs,
        out_specs=pl.BlockSpec((bm, bn), lambda j, i: (i, j)),
        out_shape=jax.ShapeDtypeStruct((m, n), out_dtype),
        scratch_shapes=[pltpu.VMEM((a.shape[1], bn), BF16) for a in lhs_list],
        compiler_params=_cparams(2), name=name,
    )(*lhs_list, *w_list, *[e[0] for e in extras])


def _residual_epilogue(acc, ex, outs):
    outs[0][...] = acc + ALPHA * ex[0][...]


def _rope128(a, cosf, sinf):
    return a * cosf + pltpu.roll(a, 64, 1) * sinf


def _rope64(a, cos64, sin_lo, sin_hi):
    return a * cos64 + pltpu.roll(a, 96, 1) * sin_lo + pltpu.roll(a, 32, 1) * sin_hi


def _inproj_kernel(x_ref, w_ref, cosf_ref, sinf_ref, cos64_ref, sinlo_ref, sinhi_ref,
                   h_ref, acc_ref, w_bf, *, q_scale):
    j = pl.program_id(0)

    @pl.when(pl.program_id(1) == 0)
    def _():
        w_bf[...] = w_ref[...].astype(BF16)

    acc_ref[...] = _dot(x_ref[...], w_bf[...])
    bn = acc_ref.shape[1]

    @pl.when(j < 5)
    def _():
        scale = jnp.where(j < 4, q_scale, 1.0).astype(F32)
        cosf = cosf_ref[...] * scale
        sinf = sinf_ref[...] * scale
        for c in range(bn // LANES):
            a = acc_ref[:, c * LANES:(c + 1) * LANES]
            h_ref[:, c * LANES:(c + 1) * LANES] = _rope128(a, cosf, sinf).astype(h_ref.dtype)

    @pl.when(j == 5)
    def _():
        h_ref[...] = acc_ref[...].astype(h_ref.dtype)

    @pl.when(j == 6)
    def _():
        for c in range(bn // LANES):
            a = acc_ref[:, c * LANES:(c + 1) * LANES]
            r = _rope64(a, cos64_ref[...], sinlo_ref[...], sinhi_ref[...])
            h_ref[:, c * LANES:(c + 1) * LANES] = r.astype(h_ref.dtype)


def _inproj_attn(x_bf, w, tables, *, layer, col_off_blocks, bm, bn):
    m, d = x_bf.shape
    n_tiles = 7
    q_scale = HEAD_DIM ** -0.5 * LOG2E
    tab_spec = pl.BlockSpec((bm, LANES), lambda j, i: (i, 0))
    return pl.pallas_call(
        functools.partial(_inproj_kernel, q_scale=q_scale), grid=(n_tiles, m // bm),
        in_specs=[pl.BlockSpec((bm, d), lambda j, i: (i, 0)),
                  pl.BlockSpec((None, d, bn), lambda j, i: (layer, 0, j + col_off_blocks))] + [tab_spec] * 5,
        out_specs=pl.BlockSpec((bm, bn), lambda j, i: (i, j)),
        out_shape=jax.ShapeDtypeStruct((m, n_tiles * bn), BF16),
        scratch_shapes=[pltpu.VMEM((bm, bn), F32), pltpu.VMEM((d, bn), BF16)],
        compiler_params=_cparams(2), name="inproj_attn",
    )(x_bf, w, *tables)


def _inproj_idx_kernel(x_ref, w_ref, cos64_ref, sinlo_ref, sinhi_ref, k2_ref, kw_ref, *, wi_scale):
    a = _dot(x_ref[...], w_ref[...].astype(BF16))
    r = _rope64(a, cos64_ref[...], sinlo_ref[...], sinhi_ref[...])
    lane = lax.broadcasted_iota(I32, a.shape, 1)
    kw_ref[...] = jnp.where(lane < IDX_DIM, r, a * wi_scale)
    k2_ref[...] = jnp.where(lane < IDX_DIM, r, pltpu.roll(r, 64, 1)).astype(k2_ref.dtype)


def _inproj_idx(x_bf, w_kw, tables, *, bm):
    m, d = x_bf.shape
    wi_scale = (N_IDX_HEADS ** -0.5) * (IDX_DIM ** -0.5)
    tab_spec = pl.BlockSpec((bm, LANES), lambda i: (i, 0))
    return pl.pallas_call(
        functools.partial(_inproj_idx_kernel, wi_scale=wi_scale), grid=(m // bm,),
        in_specs=[pl.BlockSpec((bm, d), lambda i: (i, 0)), pl.BlockSpec((d, LANES), lambda i: (0, 0))]
        + [tab_spec] * 3,
        out_specs=[tab_spec, tab_spec],
        out_shape=[jax.ShapeDtypeStruct((m, LANES), BF16), jax.ShapeDtypeStruct((m, LANES), F32)],
        compiler_params=_cparams(1), name="inproj_idx",
    )(x_bf, w_kw, *tables[2:])


def _pool_kernel(v_ref, pw_ref, ps_ref, o_ref, ext_ref):
    s = pl.program_id(1)
    ts = v_ref.shape[0]

    @pl.when(s == 0)
    def _():
        ext_ref[0:POOL_HALO, :] = jnp.zeros((POOL_HALO, ext_ref.shape[1]), F32)

    ext_ref[POOL_HALO:, :] = v_ref[...]
    t = s * ts + lax.broadcasted_iota(I32, (ts, 1), 0)
    for g, w in enumerate(POOL_WINDOWS):
        cols = slice(g * POOL_GW, (g + 1) * POOL_GW)
        e = ext_ref[:, cols]
        step = 1
        while step < w:
            e = e + pltpu.roll(e, step, 0)
            step *= 2
        win = e[POOL_HALO:, :]
        cnt = jnp.minimum(t + 1, w).astype(F32)
        pooled = win / cnt - v_ref[:, cols]
        mixed = _dot(pooled.astype(BF16), pw_ref[g])
        o_ref[:, cols] = (mixed * ps_ref[:, cols]).astype(o_ref.dtype)
    ext_ref[0:POOL_HALO, :] = v_ref[ts - POOL_HALO:, :]


def _pool_mixer(v_pool, pool_w_bf, pool_scale, *, batch, seq, ts):
    v3 = v_pool.reshape(batch, seq, MIX_POOL)
    out = pl.pallas_call(
        _pool_kernel, grid=(batch, seq // ts),
        in_specs=[pl.BlockSpec((None, ts, MIX_POOL), lambda b, s: (b, s, 0)),
                  pl.BlockSpec((N_POOL_GROUPS, POOL_GW, POOL_GW), lambda b, s: (0, 0, 0)),
                  pl.BlockSpec((1, MIX_POOL), lambda b, s: (0, 0))],
        out_specs=pl.BlockSpec((None, ts, MIX_POOL), lambda b, s: (b, s, 0)),
        out_shape=jax.ShapeDtypeStruct((batch, seq, MIX_POOL), BF16),
        scratch_shapes=[pltpu.VMEM((POOL_HALO + ts, MIX_POOL), F32)],
        compiler_params=_cparams(2), name="pool_mixer",
    )(v3, pool_w_bf, pool_scale.reshape(1, MIX_POOL))
    return out.reshape(batch * seq, MIX_POOL)


def _dsa_kernel(q_ref, k_ref, v_ref, k2_ref, qi_ref, kw_ref, o_ref,
                key_sc, bias_sc, xi_sc, qt_sc, m_sc, l_sc, acc_sc, s_sc, *, n_sel, seq):
    blk = pl.program_id(1)
    nchunk = blk // (LC // QB) + 1
    q_pos = blk * QB + lax.broadcasted_iota(I32, (1, QB), 1)
    k_iota = lax.broadcasted_iota(I32, (LC, 1), 0)
    sub = lax.broadcasted_iota(I32, (LANES, QB), 0)
    eye = jnp.where(sub == lax.broadcasted_iota(I32, (LANES, QB), 1), 1.0, 0.0).astype(BF16)

    def transposed(x):
        return _dot_nt(eye, x)

    for jj in range(N_IDX_HEADS // 2):
        xt = transposed(qi_ref[:, jj * LANES:(jj + 1) * LANES])
        xi_sc[jj, :, 0:QB] = jnp.where(sub < IDX_DIM, xt, 0.0).astype(BF16)
        xi_sc[jj, :, QB:2 * QB] = jnp.where(sub >= IDX_DIM, xt, 0.0).astype(BF16)
    for g in range(N_KV_HEADS):
        for r in range(Q_PER_KV):
            h = g * Q_PER_KV + r
            qt_sc[g, 0:HEAD_DIM, r * QB:(r + 1) * QB] = transposed(
                q_ref[:, h * HEAD_DIM:(h + 1) * HEAD_DIM]).astype(BF16)
            qt_sc[g, HEAD_DIM:, r * QB:(r + 1) * QB] = eye
    kw_t = kw_ref[...].T
    wi = [kw_t[IDX_DIM + h:IDX_DIM + h + 1, :] for h in range(N_IDX_HEADS)]

    inf = jnp.inf
    part = (FOLD_ROWS, QB)

    def score_chunk(c, carry):
        s_min, s_max = carry
        off = pl.multiple_of(c * LC, LC)
        k2c = k2_ref[pl.ds(off, LC), :]
        sc = jnp.zeros((LC, QB), F32)
        for jj in range(N_IDX_HEADS // 2):
            rel = jnp.maximum(_dot(k2c, xi_sc[jj]), 0.0)
            sc = sc + rel[:, 0:QB] * wi[2 * jj] + rel[:, QB:2 * QB] * wi[2 * jj + 1]
        causal = off + k_iota <= q_pos
        key_sc[pl.ds(off, LC), :] = jnp.where(causal, sc, -inf)
        s_min = jnp.minimum(s_min, _fold_rows(jnp.where(causal, sc, inf), jnp.min))
        s_max = jnp.maximum(s_max, _fold_rows(jnp.where(causal, sc, -inf), jnp.max))
        return s_min, s_max

    s_min, s_max = lax.fori_loop(0, nchunk, score_chunk, (jnp.full(part, inf, F32), jnp.full(part, -inf, F32)))

    n_sel_chunk = (blk * QB + QB + SC - 1) // SC
    s_iota = lax.broadcasted_iota(I32, (SC, 1), 0)

    def count(pred_fn):
        def body(c, acc):
            off = pl.multiple_of(c * SC, SC)
            m = pred_fn(key_sc[pl.ds(off, SC), :], off)
            return acc + _fold_rows(m, jnp.sum)
        acc = lax.fori_loop(0, n_sel_chunk, body, jnp.zeros(part, F32))
        return jnp.sum(acc, axis=0, keepdims=True)

    has_thr = q_pos + 1 >= int(n_sel)
    lo0 = jnp.where(has_thr, jnp.min(s_min, axis=0, keepdims=True), -inf)
    hi0 = jnp.where(has_thr, jnp.max(s_max, axis=0, keepdims=True), -inf)

    def midpoint(lo, hi):
        mid = 0.5 * (lo + hi)
        return jnp.where(mid <= lo, hi, mid)

    def halve(i, st):
        lo, hi = st
        mid = midpoint(lo, hi)
        enough = count(lambda sc, off: jnp.where(sc >= mid, 1.0, 0.0)) >= n_sel
        return jnp.where(enough, mid, lo), jnp.where(enough, hi, mid)

    lo0, hi0 = lax.fori_loop(0, N_HALVINGS, halve, (lo0, hi0))

    def open_rows(lo, hi):
        return jnp.max(jnp.where(lo < hi, 1.0, 0.0)) > 0.0

    def bisect_cond(st):
        it, lo, hi = st
        return open_rows(lo, hi) & (it < seq)

    def bisect_body(st):
        it, lo, hi = st
        mid = midpoint(lo, hi)

        def body(c, acc):
            cnt, up, dn = acc
            off = pl.multiple_of(c * SC, SC)
            sc = key_sc[pl.ds(off, SC), :]
            ge = sc >= mid
            cnt = cnt + _fold_rows(jnp.where(ge, 1.0, 0.0), jnp.sum)
            up = jnp.minimum(up, _fold_rows(jnp.where(ge, sc, inf), jnp.min))
            dn = jnp.maximum(dn, _fold_rows(jnp.where(ge, -inf, sc), jnp.max))
            return cnt, up, dn

        cnt, up, dn = lax.fori_loop(
            0, n_sel_chunk, body, (jnp.zeros(part, F32), jnp.full(part, inf, F32), jnp.full(part, -inf, F32)))
        enough = jnp.sum(cnt, axis=0, keepdims=True) >= n_sel
        is_open = lo < hi
        new_lo = jnp.where(is_open & enough, jnp.min(up, axis=0, keepdims=True), lo)
        new_hi = jnp.where(is_open & jnp.logical_not(enough), jnp.max(dn, axis=0, keepdims=True), hi)
        return it + 1, new_lo, new_hi

    _, thr, _ = lax.while_loop(bisect_cond, bisect_body, (jnp.int32(0), lo0, hi0))
    n_gt = count(lambda kk, off: jnp.where(kk > thr, 1.0, 0.0))
    n_eq = count(lambda kk, off: jnp.where(kk == thr, 1.0, 0.0))
    need = n_sel - n_gt

    def tie_search():
        def tbody(i, p):
            cand = p | jnp.left_shift(jnp.int32(1), (seq.bit_length() - 2) - i)
            cnt = count(lambda kk, off: jnp.where(kk == thr, jnp.where(off + s_iota < cand, 1.0, 0.0), 0.0))
            return jnp.where(cnt < need, cand, p)
        return lax.fori_loop(0, seq.bit_length() - 1, tbody, jnp.zeros((1, QB), I32))

    ambiguous = jnp.max(jnp.where(has_thr & (n_eq > need), 1.0, 0.0)) > 0.0
    tie_hi = lax.cond(ambiguous, tie_search, lambda: jnp.full((1, QB), seq, I32))
    tie_hi = jnp.where(has_thr, tie_hi, -1)

    def bias_chunk(c, carry):
        off = pl.multiple_of(c * LC, LC)
        kk = key_sc[pl.ds(off, LC), :]
        tie_ok = jnp.where(off + k_iota <= tie_hi, 0.0, NEG)
        bias = jnp.where(kk > thr, 0.0, jnp.where(kk == thr, tie_ok, NEG))
        bias_sc[pl.ds(off, LC), :] = bias.astype(BF16)
        return carry

    lax.fori_loop(0, nchunk, bias_chunk, 0)

    m_sc[...] = jnp.full(m_sc.shape, NEG, F32)
    l_sc[...] = jnp.zeros(l_sc.shape, F32)
    acc_sc[...] = jnp.zeros(acc_sc.shape, F32)

    def attn_chunk(c, carry):
        off = pl.multiple_of(c * LC, LC)
        bias = bias_sc[pl.ds(off, LC), :]

        def logits(g):
            kc = k_ref[pl.ds(off, LC), g * HEAD_DIM:(g + 1) * HEAD_DIM]
            return _dot(jnp.concatenate([kc, bias], axis=1), qt_sc[g])

        m_news = []
        for g in range(N_KV_HEADS):
            s = logits(g)
            s_sc[g] = s
            m_news.append(jnp.maximum(m_sc[g], jnp.max(_fold_rows(s, jnp.max), axis=0, keepdims=True)))
        for g in range(N_KV_HEADS):
            vc = v_ref[pl.ds(off, LC), g * HEAD_DIM:(g + 1) * HEAD_DIM]
            m_old = m_sc[g]
            m_new = m_news[g]
            alpha = jnp.exp2(m_old - m_new)
            p = jnp.exp2(s_sc[g] - m_new)
            l_sc[g] = alpha * l_sc[g] + jnp.sum(_fold_rows(p, jnp.sum), axis=0, keepdims=True)
            acc_sc[g] = alpha * acc_sc[g] + _dot_tn(vc, p.astype(BF16))
            m_sc[g] = m_new
        return carry

    lax.fori_loop(0, nchunk, attn_chunk, 0)

    for g in range(N_KV_HEADS):
        o_t = acc_sc[g] / l_sc[g]
        for r in range(Q_PER_KV):
            h = g * Q_PER_KV + r
            o_ref[:, h * HEAD_DIM:(h + 1) * HEAD_DIM] = o_t[:, r * QB:(r + 1) * QB].T.astype(o_ref.dtype)


def _dsa_attention(h2, k2, kw, *, batch, seq, bn):
    nb = seq // QB
    n_sel = min(TOPK_MAX, seq // 4)
    q_cols = N_Q_HEADS * HEAD_DIM
    kv_cols = N_KV_HEADS * HEAD_DIM
    assert bn == kv_cols and q_cols == 4 * bn and seq % LC == 0 and LC >= n_sel
    kern = functools.partial(_dsa_kernel, n_sel=float(n_sel), seq=seq)
    rows = Q_PER_KV * QB
    return pl.pallas_call(
        kern, grid=(batch, nb),
        in_specs=[pl.BlockSpec((QB, q_cols), lambda b, i: (b * nb + i, 0)),
                  pl.BlockSpec((seq, kv_cols), lambda b, i: (b, 4)),
                  pl.BlockSpec((seq, kv_cols), lambda b, i: (b, 5)),
                  pl.BlockSpec((seq, LANES), lambda b, i: (b, 0)),
                  pl.BlockSpec((QB, bn), lambda b, i: (b * nb + i, 6)),
                  pl.BlockSpec((QB, LANES), lambda b, i: (b * nb + i, 0))],
        out_specs=pl.BlockSpec((QB, q_cols), lambda b, i: (b * nb + i, 0)),
        out_shape=jax.ShapeDtypeStruct((batch * seq, q_cols), BF16),
        scratch_shapes=[pltpu.VMEM((seq, QB), F32),
                        pltpu.VMEM((seq, QB), BF16),
                        pltpu.VMEM((N_IDX_HEADS // 2, LANES, 2 * QB), BF16),
                        pltpu.VMEM((N_KV_HEADS, HEAD_DIM + QB, rows), BF16),
                        pltpu.VMEM((N_KV_HEADS, 1, rows), F32),
                        pltpu.VMEM((N_KV_HEADS, 1, rows), F32),
                        pltpu.VMEM((N_KV_HEADS, HEAD_DIM, rows), F32),
                        pltpu.VMEM((N_KV_HEADS, LC, rows), F32)],
        compiler_params=_cparams(2), name="dsa_attention",
    )(h2, h2, h2, k2, h2, kw)


def _ln_rows(x, g, b):
    mu = jnp.mean(x, axis=-1, keepdims=True)
    xc = x - mu
    var = jnp.mean(xc * xc, axis=-1, keepdims=True)
    return xc * lax.rsqrt(var + LN_EPS) * g + b


def _pack_halves(y):
    half = y.shape[1] // 2
    return pltpu.pack_elementwise([y[:, :half], y[:, half:]], packed_dtype=BF16)


def _unpack_halves(p):
    lo = pltpu.unpack_elementwise(p, index=0, packed_dtype=BF16, unpacked_dtype=F32)
    hi = pltpu.unpack_elementwise(p, index=1, packed_dtype=BF16, unpacked_dtype=F32)
    return jnp.concatenate([lo, hi], axis=1)


def _ln_kernel(x_ref, g_ref, b_ref, o_ref, o2_ref, *, packed):
    y = _ln_rows(x_ref[...], g_ref[...], b_ref[...])
    o_ref[...] = y
    o2_ref[...] = _pack_halves(y) if packed else y.astype(o2_ref.dtype)


def _layer_norm(x, g, b, *, bm, packed=False):
    m, d = x.shape
    row = pl.BlockSpec((bm, d), lambda i: (i, 0))
    vec = pl.BlockSpec((1, d), lambda i: (0, 0))
    second = ((m, d // 2), U32) if packed else ((m, d), BF16)
    return pl.pallas_call(
        functools.partial(_ln_kernel, packed=packed), grid=(m // bm,), in_specs=[row, vec, vec],
        out_specs=[row, pl.BlockSpec((bm, second[0][1]), lambda i: (i, 0))],
        out_shape=[jax.ShapeDtypeStruct((m, d), F32), jax.ShapeDtypeStruct(*second)],
        compiler_params=_cparams(1), name="layer_norm",
    )(x, g.reshape(1, d), b.reshape(1, d))


def _xattn_kernel(q_ref, k_ref, v_ref, o_ref):
    dh = q_ref.shape[1] // N_MEM_HEADS
    for h in range(N_MEM_HEADS):
        cols = slice(h * dh, (h + 1) * dh)
        s = _dot_nt(q_ref[:, cols], k_ref[:, cols])
        p = jnp.exp(s - jnp.max(s, axis=1, keepdims=True))
        o = _dot(p.astype(BF16), v_ref[:, cols]) / jnp.sum(p, axis=1, keepdims=True)
        o_ref[:, cols] = o.astype(o_ref.dtype)


def _cross_attention(q, k, v, *, batch, seq, n_mem, bm):
    d = q.shape[1]
    nb = seq // bm
    return pl.pallas_call(
        _xattn_kernel, grid=(batch, nb),
        in_specs=[pl.BlockSpec((bm, d), lambda b, i: (b * nb + i, 0)),
                  pl.BlockSpec((n_mem, d), lambda b, i: (b, 0)),
                  pl.BlockSpec((n_mem, d), lambda b, i: (b, 0))],
        out_specs=pl.BlockSpec((bm, d), lambda b, i: (b * nb + i, 0)),
        out_shape=jax.ShapeDtypeStruct((batch * seq, d), BF16),
        compiler_params=_cparams(2), name="cross_attention",
    )(q, k, v)


def _router_kernel(x_ref, w_ref, b_ref, id_ref, gate_ref):
    logits = jnp.dot(x_ref[...], w_ref[...], preferred_element_type=F32,
                     precision=lax.Precision.HIGHEST) + b_ref[...]
    lane = lax.broadcasted_iota(I32, logits.shape, 1)
    lane_f = lane.astype(F32)
    big = float(LANES)
    is_g = lane < N_GROUPS
    gl = jnp.where(is_g, logits, -jnp.inf)
    g_max = jnp.max(gl, axis=1, keepdims=True)
    g_idx = jnp.min(jnp.where(gl == g_max, lane_f, big), axis=1, keepdims=True)
    g_gate = 1.0 / jnp.sum(jnp.where(is_g, jnp.exp(gl - g_max), 0.0), axis=1, keepdims=True)
    e_lo = N_GROUPS + g_idx * EXPERTS_PER_GROUP
    in_grp = (lane_f >= e_lo) & (lane_f < e_lo + EXPERTS_PER_GROUP)
    el = jnp.where(in_grp, logits, -jnp.inf)
    v1 = jnp.max(el, axis=1, keepdims=True)
    i1 = jnp.min(jnp.where(el == v1, lane_f, big), axis=1, keepdims=True)
    el2 = jnp.where(lane_f == i1, -jnp.inf, el)
    v2 = jnp.max(el2, axis=1, keepdims=True)
    i2 = jnp.min(jnp.where(el2 == v2, lane_f, big), axis=1, keepdims=True)
    z = jnp.exp(v2 - v1)
    w1 = g_gate / (1.0 + z)
    w2 = g_gate * z / (1.0 + z)
    ids = jnp.where(lane == 0, i1 - N_GROUPS, jnp.where(lane == 1, i2 - N_GROUPS, 0.0))
    id_ref[...] = ids.astype(I32)
    gate_ref[...] = jnp.where(lane == 0, w1, jnp.where(lane == 1, w2, 0.0))


def _router(x, w_r, b_r, *, bm):
    m, d = x.shape
    out = pl.BlockSpec((bm, LANES), lambda i: (i, 0))
    return pl.pallas_call(
        _router_kernel, grid=(m // bm,),
        in_specs=[pl.BlockSpec((bm, d), lambda i: (i, 0)),
                  pl.BlockSpec((d, LANES), lambda i: (0, 0)),
                  pl.BlockSpec((1, LANES), lambda i: (0, 0))],
        out_specs=[out, out],
        out_shape=[jax.ShapeDtypeStruct((m, LANES), I32), jax.ShapeDtypeStruct((m, LANES), F32)],
        compiler_params=_cparams(1), name="router",
    )(x, w_r, b_r)


def _rank_kernel(id_ref, rank_ref, cnt_ref, carry_ref):
    i = pl.program_id(0)
    bt = id_ref.shape[0]

    @pl.when(i == 0)
    def _():
        carry_ref[...] = jnp.zeros(carry_ref.shape, F32)

    ids = id_ref[...]
    lane = lax.broadcasted_iota(I32, (bt, LANES), 1)
    e0 = ids[:, 0:1]
    e1 = ids[:, 1:2] + N_EXPERTS
    hit0 = lane == e0
    hit1 = lane == e1
    onehot = jnp.where(hit0, 1.0, jnp.where(hit1, 1.0, 0.0))
    r_io = lax.broadcasted_iota(I32, (bt, bt), 0)
    c_io = lax.broadcasted_iota(I32, (bt, bt), 1)
    tri = jnp.where(c_io < r_io, 1.0, 0.0).astype(BF16)
    prefix = _dot(tri, onehot.astype(BF16)) + carry_ref[0:1, :]
    rank0 = jnp.sum(jnp.where(hit0, prefix, 0.0), axis=1, keepdims=True)
    rank1 = jnp.sum(jnp.where(hit1, prefix, 0.0), axis=1, keepdims=True)
    rank_ref[...] = jnp.where(lane == 0, rank0, jnp.where(lane == 1, rank1, 0.0))
    total = carry_ref[0:1, :] + jnp.sum(onehot, axis=0, keepdims=True)
    carry_ref[...] = jnp.broadcast_to(total, carry_ref.shape)
    cnt_ref[...] = jnp.broadcast_to(total, cnt_ref.shape)


def _slot_ranks(ids, *, bt):
    m = ids.shape[0]
    return pl.pallas_call(
        _rank_kernel, grid=(m // bt,),
        in_specs=[pl.BlockSpec((bt, LANES), lambda i: (i, 0))],
        out_specs=[pl.BlockSpec((bt, LANES), lambda i: (i, 0)),
                   pl.BlockSpec((8, LANES), lambda i: (0, 0))],
        out_shape=[jax.ShapeDtypeStruct((m, LANES), F32), jax.ShapeDtypeStruct((8, LANES), F32)],
        scratch_shapes=[pltpu.VMEM((8, LANES), F32)],
        compiler_params=_cparams(1), name="slot_ranks",
    )(ids)


def _pos_kernel(id_ref, rank_ref, tab_ref, pos_ref):
    ids = id_ref[...]
    ranks = rank_ref[...]
    lane = lax.broadcasted_iota(I32, ids.shape, 1)
    tab = tab_ref[...]
    base0 = jnp.sum(jnp.where(lane == ids[:, 0:1], tab, 0.0), axis=1, keepdims=True)
    base1 = jnp.sum(jnp.where(lane == ids[:, 1:2] + N_EXPERTS, tab, 0.0), axis=1, keepdims=True)
    pos = jnp.where(lane == 0, base0 + ranks[:, 0:1], jnp.where(lane == 1, base1 + ranks[:, 1:2], 0.0))
    pos_ref[...] = pos.astype(I32)


def _slot_positions(ids, ranks, table, *, bt):
    m = ids.shape[0]
    blk = pl.BlockSpec((bt, LANES), lambda i: (i, 0))
    return pl.pallas_call(
        _pos_kernel, grid=(m // bt,),
        in_specs=[blk, blk, pl.BlockSpec((1, LANES), lambda i: (0, 0))],
        out_specs=blk,
        out_shape=jax.ShapeDtypeStruct((m, LANES), I32),
        compiler_params=_cparams(1), name="slot_positions",
    )(ids, ranks, table)


def _row_copy(src_ref, src_row, dst_ref, dst_row, sem):
    return pltpu.make_async_copy(src_ref.at[pl.ds(src_row, 1)], dst_ref.at[pl.ds(dst_row, 1)], sem)


def _dispatch_kernel(pos0_ref, pos1_ref, x_ref, xs_in_ref, xs_ref, sem):
    del xs_in_ref
    i = pl.program_id(0)
    bt = x_ref.shape[0]

    def issue(r, carry):
        t = i * bt + r
        _row_copy(x_ref, r, xs_ref, pos0_ref[t], sem.at[0]).start()
        _row_copy(x_ref, r, xs_ref, pos1_ref[t], sem.at[1]).start()
        return carry

    lax.fori_loop(0, bt, issue, 0)
    for k in range(2):
        pltpu.make_async_copy(x_ref, xs_ref.at[pl.ds(0, bt)], sem.at[k]).wait()


def _dispatch(x, pos0, pos1, n_slots, *, bt):
    m, d = x.shape
    xs0 = jnp.zeros((n_slots, d), x.dtype)
    return pl.pallas_call(
        _dispatch_kernel,
        grid_spec=pltpu.PrefetchScalarGridSpec(
            num_scalar_prefetch=2, grid=(m // bt,),
            in_specs=[pl.BlockSpec((bt, d), lambda i, p0, p1: (i, 0)),
                      pl.BlockSpec(memory_space=pl.ANY)],
            out_specs=pl.BlockSpec(memory_space=pl.ANY),
            scratch_shapes=[pltpu.SemaphoreType.DMA((2,))]),
        out_shape=jax.ShapeDtypeStruct((n_slots, d), x.dtype),
        input_output_aliases={3: 0},
        compiler_params=_cparams(1), name="dispatch",
    )(pos0, pos1, x, xs0)


def _ffn_kernel(blk_e_ref, nxt_e_ref, set_ref, nused_ref, x_ref, wg_hbm, wu_hbm, wd_hbm, y_ref,
                wg_bf, wu_bf, wd_bf, st_g, st_u, st_d, sem):
    i = pl.program_id(0)
    d, ff = wg_bf.shape[1], wg_bf.shape[2]
    rg, rd = d // W_GROUPS, ff // W_GROUPS

    n_slot = st_g.shape[0]

    def group_copies(e, g):
        slot = g % n_slot
        return (pltpu.make_async_copy(wg_hbm.at[e, pl.ds(g * rg, rg), :], st_g.at[slot], sem.at[0, slot]),
                pltpu.make_async_copy(wu_hbm.at[e, pl.ds(g * rg, rg), :], st_u.at[slot], sem.at[1, slot]),
                pltpu.make_async_copy(wd_hbm.at[e, pl.ds(g * rd, rd), :], st_d.at[slot], sem.at[2, slot]))

    def start(e, g):
        for c in group_copies(e, g):
            c.start()

    def finish(e, g, s):
        for c in group_copies(e, g):
            c.wait()
        slot = g % n_slot
        wg_bf[s, g * rg:(g + 1) * rg, :] = st_g[slot].astype(BF16)
        wu_bf[s, g * rg:(g + 1) * rg, :] = st_u[slot].astype(BF16)
        wd_bf[s, g * rd:(g + 1) * rd, :] = st_d[slot].astype(BF16)

    def load_expert(e, s, between=()):
        for g in range(n_slot):
            start(e, g)
        for g in range(W_GROUPS):
            if g < len(between):
                between[g]()
            finish(e, g, s)
            if g + n_slot < W_GROUPS:
                start(e, g + n_slot)

    e = blk_e_ref[i]
    s = set_ref[i]
    nxt = nxt_e_ref[i]
    first_of_expert = jnp.logical_or(i == 0, blk_e_ref[jnp.maximum(i - 1, 0)] != e)
    active = i < nused_ref[0]
    prefetch = jnp.logical_and(jnp.logical_and(active, first_of_expert), nxt >= 0)

    @pl.when(i == 0)
    def _():
        load_expert(e, s)

    def ffn_steps():
        vals = {}

        def gate_step():
            vals["x"] = _unpack_halves(x_ref[...]).astype(BF16)
            vals["gate"] = _dot(vals["x"], wg_bf[s])

        def up_step():
            gate = vals["gate"]
            vals["h"] = (gate * (1.0 / (1.0 + jnp.exp(-gate))) * _dot(vals["x"], wu_bf[s])).astype(BF16)

        def down_step():
            y_ref[...] = _pack_halves(_dot(vals["h"], wd_bf[s]))

        return gate_step, up_step, down_step

    @pl.when(prefetch)
    def _():
        load_expert(nxt, 1 - s, between=ffn_steps())

    @pl.when(jnp.logical_and(active, jnp.logical_not(prefetch)))
    def _():
        for step in ffn_steps():
            step()


def _expert_ffn(xs, blk_e, nxt_e, set_idx, nused, wg, wu, wd):
    n_slots, dp = xs.shape
    d, ff = wg.shape[1], wg.shape[2]
    nblk = n_slots // MOE_BLK
    assert dp * 2 == d and d % W_GROUPS == 0 and ff % W_GROUPS == 0 and W_GROUPS >= W_STAGE_SLOTS > 3

    def row_map(i, be, nx, si, nu):
        return (jnp.minimum(i, nu[0] - 1), 0)

    hbm = pl.BlockSpec(memory_space=pl.ANY)
    return pl.pallas_call(
        _ffn_kernel,
        grid_spec=pltpu.PrefetchScalarGridSpec(
            num_scalar_prefetch=4, grid=(nblk,),
            in_specs=[pl.BlockSpec((MOE_BLK, dp), row_map), hbm, hbm, hbm],
            out_specs=pl.BlockSpec((MOE_BLK, dp), row_map),
            scratch_shapes=[pltpu.VMEM((2, d, ff), BF16), pltpu.VMEM((2, d, ff), BF16), pltpu.VMEM((2, ff, d), BF16),
                            pltpu.VMEM((W_STAGE_SLOTS, d // W_GROUPS, ff), F32),
                            pltpu.VMEM((W_STAGE_SLOTS, d // W_GROUPS, ff), F32),
                            pltpu.VMEM((W_STAGE_SLOTS, ff // W_GROUPS, d), F32),
                            pltpu.SemaphoreType.DMA((3, W_STAGE_SLOTS))]),
        out_shape=jax.ShapeDtypeStruct((n_slots, dp), xs.dtype),
        input_output_aliases={4: 0},
        compiler_params=_cparams(1), name="expert_ffn",
    )(blk_e, nxt_e, set_idx, nused, xs, wg, wu, wd)


def _combine_kernel(pos0_ref, pos1_ref, x_ref, gate_ref, g_ref, b_ref, ys_ref, o_ref, y0_ref, y1_ref, sem):
    i = pl.program_id(0)
    bt = x_ref.shape[0]
    slot = i % 2

    def issue(step, to_slot):
        def body(r, carry):
            t = step * bt + r
            _row_copy(ys_ref, pos0_ref[t], y0_ref.at[to_slot], r, sem.at[0, to_slot]).start()
            _row_copy(ys_ref, pos1_ref[t], y1_ref.at[to_slot], r, sem.at[1, to_slot]).start()
            return carry
        lax.fori_loop(0, bt, body, 0)

    @pl.when(i == 0)
    def _():
        issue(0, 0)

    @pl.when(i + 1 < pl.num_programs(0))
    def _():
        issue(i + 1, 1 - slot)

    pltpu.make_async_copy(ys_ref.at[pl.ds(0, bt)], y0_ref.at[slot], sem.at[0, slot]).wait()
    pltpu.make_async_copy(ys_ref.at[pl.ds(0, bt)], y1_ref.at[slot], sem.at[1, slot]).wait()
    gates = gate_ref[...]
    f = _unpack_halves(y0_ref[slot]) * gates[:, 0:1] + _unpack_halves(y1_ref[slot]) * gates[:, 1:2]
    o_ref[...] = _ln_rows(ALPHA * x_ref[...] + f, g_ref[...], b_ref[...])


def _combine(x, gates, ys, pos0, pos1, g, b, *, bt):
    m, d = x.shape
    dp = ys.shape[1]
    row = pl.BlockSpec((bt, d), lambda i, p0, p1: (i, 0))
    vec = pl.BlockSpec((1, d), lambda i, p0, p1: (0, 0))
    return pl.pallas_call(
        _combine_kernel,
        grid_spec=pltpu.PrefetchScalarGridSpec(
            num_scalar_prefetch=2, grid=(m // bt,),
            in_specs=[row, pl.BlockSpec((bt, LANES), lambda i, p0, p1: (i, 0)), vec, vec,
                      pl.BlockSpec(memory_space=pl.ANY)],
            out_specs=row,
            scratch_shapes=[pltpu.VMEM((2, bt, dp), ys.dtype), pltpu.VMEM((2, bt, dp), ys.dtype),
                            pltpu.SemaphoreType.DMA((2, 2))]),
        out_shape=jax.ShapeDtypeStruct((m, d), F32),
        compiler_params=_cparams(1), name="combine_ln",
    )(pos0, pos1, x, gates, g.reshape(1, d), b.reshape(1, d), ys)


def _rope_tables(positions, n_rows):
    pos = positions.astype(F32).reshape(n_rows, 1)

    def cs(dim):
        inv = 1.0 / (ROPE_THETA ** (jnp.arange(0, dim, 2, dtype=F32) / dim))
        ang = pos * inv[None, :]
        return jnp.cos(ang), jnp.sin(ang)

    c_h, s_h = cs(HEAD_DIM)
    c_i, s_i = cs(IDX_DIM)
    z_i = jnp.zeros_like(s_i)
    cosf = jnp.concatenate([c_h, c_h], axis=1)
    sinf = jnp.concatenate([-s_h, s_h], axis=1)
    cos64 = jnp.concatenate([c_i, c_i, c_i, c_i], axis=1)
    sin_lo = jnp.concatenate([-s_i, z_i, -s_i, z_i], axis=1)
    sin_hi = jnp.concatenate([z_i, s_i, z_i, s_i], axis=1)
    return cosf, sinf, cos64, sin_lo, sin_hi


def kernel(x, mem, positions, w_in, pool_w, pool_scale, w_o, ln1_g, ln1_b, w_mq, w_mk, w_mv, w_mo, ln2_g, ln2_b,
           w_group_router, b_group_router, w_expert_router, b_expert_router, w_gate, w_up, w_down, ln3_g, ln3_b):
    batch, seq, d = x.shape
    n_mem = mem.shape[1]
    n = batch * seq
    bn_attn = N_KV_HEADS * HEAD_DIM
    tables = _rope_tables(positions, n)
    xf = x.reshape(n, d)
    n_slots = 2 * n + N_EXPERTS * MOE_BLK
    nblk = n_slots // MOE_BLK

    for l in range(w_in.shape[0]):
        x_bf = xf.astype(BF16)
        kw_col = MIX_POOL + 7 * bn_attn
        w_kw = jnp.pad(w_in[l][:, kw_col:], ((0, 0), (0, LANES - (w_in.shape[2] - kw_col))))

        v_pool = _matmul([x_bf], [w_in], layer=l, name="inproj_pool", bm=1024, bn=DENSE_BN, n=MIX_POOL)
        h2 = _inproj_attn(x_bf, w_in, tables, layer=l, col_off_blocks=MIX_POOL // bn_attn, bm=1024, bn=bn_attn)
        k2, kw = _inproj_idx(x_bf, w_kw, tables, bm=1024)
        a_pool = _pool_mixer(v_pool, pool_w[l].astype(BF16), pool_scale[l], batch=batch, seq=seq, ts=512)
        a_attn = _dsa_attention(h2, k2, kw, batch=batch, seq=seq, bn=bn_attn)
        res = (xf, (1024, DENSE_BN), lambda i, j: (i, j))
        pre = _matmul([a_pool, a_attn], [w_o, w_o], layer=l, name="outproj", bm=1024, bn=DENSE_BN, n=d,
                      extras=(res,), epilogue=_residual_epilogue)
        x1, x1_bf = _layer_norm(pre, ln1_g[l], ln1_b[l], bm=256)

        mq_scale = (d // N_MEM_HEADS) ** -0.5

        def q_epilogue(acc, ex, outs):
            outs[0][...] = (acc * mq_scale).astype(outs[0].dtype)

        qm = _matmul([x1_bf], [w_mq], layer=l, name="mem_q", bm=1024, bn=DENSE_BN, n=d, out_dtype=BF16,
                     epilogue=q_epilogue)
        mem_bf = mem.reshape(batch * n_mem, d).astype(BF16)
        km = _matmul([mem_bf], [w_mk], layer=l, name="mem_k", bm=batch * n_mem, bn=DENSE_BN, n=d, out_dtype=BF16)
        vm = _matmul([mem_bf], [w_mv], layer=l, name="mem_v", bm=batch * n_mem, bn=DENSE_BN, n=d, out_dtype=BF16)
        om = _cross_attention(qm, km, vm, batch=batch, seq=seq, n_mem=n_mem, bm=512)
        res = (x1, (1024, DENSE_BN), lambda i, j: (i, j))
        pre = _matmul([om], [w_mo], layer=l, name="mem_o", bm=1024, bn=DENSE_BN, n=d, extras=(res,),
                      epilogue=_residual_epilogue)
        x2, x2_rows = _layer_norm(pre, ln2_g[l], ln2_b[l], bm=256, packed=True)

        w_r = jnp.pad(jnp.concatenate([w_group_router[l], w_expert_router[l]], axis=1),
                      ((0, 0), (0, LANES - N_GROUPS - N_EXPERTS)))
        b_r = jnp.pad(jnp.concatenate([b_group_router[l], b_expert_router[l]]),
                      (0, LANES - N_GROUPS - N_EXPERTS)).reshape(1, LANES)
        ids, gates = _router(x2, w_r, b_r, bm=512)
        ranks, totals = _slot_ranks(ids, bt=512)
        c0 = totals[0, :N_EXPERTS].astype(I32)
        c1 = totals[0, N_EXPERTS:].astype(I32)
        padded = ((c0 + c1 + MOE_BLK - 1) // MOE_BLK) * MOE_BLK
        pend = jnp.cumsum(padded)
        pstart = pend - padded
        table = jnp.concatenate([pstart, pstart + c0]).astype(F32).reshape(1, LANES)
        pos = _slot_positions(ids, ranks, table, bt=512)
        pos0, pos1 = pos[:, 0], pos[:, 1]
        nused = pend[-1] // MOE_BLK
        blk_i = jnp.minimum(jnp.arange(nblk, dtype=I32), nused - 1)
        blk_e = jnp.sum((pend[None, :] <= (blk_i * MOE_BLK)[:, None]).astype(I32), axis=1)
        blk_e = jnp.minimum(blk_e, N_EXPERTS - 1)
        later = blk_e[None, :] > blk_e[:, None]
        nxt_e = jnp.min(jnp.where(later, blk_e[None, :], N_EXPERTS), axis=1)
        nxt_e = jnp.where(nxt_e == N_EXPERTS, -1, nxt_e).astype(I32)
        new_e = jnp.concatenate([jnp.ones((1,), I32), (blk_e[1:] != blk_e[:-1]).astype(I32)])
        set_idx = (jnp.cumsum(new_e) - 1) % 2

        xs = _dispatch(x2_rows, pos0, pos1, n_slots, bt=256)
        ys = _expert_ffn(xs, blk_e, nxt_e, set_idx.astype(I32), nused.reshape(1).astype(I32),
                         w_gate[l], w_up[l], w_down[l])
        xf = _combine(x2, gates, ys, pos0, pos1, ln3_g[l], ln3_b[l], bt=256)
    return xf.reshape(batch, seq, d)
```

```python
import functools

import jax
import jax.numpy as jnp
from jax import lax
from jax.experimental import pallas as pl
from jax.experimental.pallas import tpu as pltpu

F32 = jnp.float32
BF16 = jnp.bfloat16
I32 = jnp.int32
U32 = jnp.uint32

MIX_POOL = 2048
N_POOL_GROUPS = 4
POOL_WINDOWS = (2, 4, 8, 16)
POOL_GW = MIX_POOL // N_POOL_GROUPS
HEAD_DIM = 128
N_Q_HEADS = 16
N_KV_HEADS = 4
Q_PER_KV = N_Q_HEADS // N_KV_HEADS
N_IDX_HEADS = 8
IDX_DIM = 64
TOPK_MAX = 256
ROPE_THETA = 10000.0
N_MEM_HEADS = 4
N_GROUPS = 8
EXPERTS_PER_GROUP = 8
N_EXPERTS = N_GROUPS * EXPERTS_PER_GROUP
LN_EPS = 1e-5
DEPTH = 1
ALPHA = (2.0 * DEPTH) ** 0.25

LANES = 128
QB = 128
LC = 512
SC = 256
DENSE_BN = 512
MOE_BLK = 256
W_GROUPS = 8
W_STAGE_SLOTS = 4
POOL_HALO = 16
NEG = -1e30
N_HALVINGS = 12
LOG2E = 1.4426950408889634
VMEM_LIMIT = 56 * 1024 * 1024


def _cparams(n_axes, vmem=VMEM_LIMIT):
    return pltpu.CompilerParams(dimension_semantics=("arbitrary",) * n_axes, vmem_limit_bytes=vmem)


def _dot(a, b):
    return jnp.dot(a, b, preferred_element_type=F32)


def _dot_nt(a, b):
    return lax.dot_general(a, b, (((1,), (1,)), ((), ())), preferred_element_type=F32)


def _dot_tn(a, b):
    return lax.dot_general(a, b, (((0,), (0,)), ((), ())), preferred_element_type=F32)


FOLD_ROWS = 32


def _fold_rows(x, reduce_fn):
    r, c = x.shape
    return reduce_fn(x.reshape(r // FOLD_ROWS, FOLD_ROWS, c), axis=0)


def _mm_kernel(*refs, n_lhs, n_extra, epilogue):
    lhs = refs[:n_lhs]
    ws = refs[n_lhs:2 * n_lhs]
    extras = refs[2 * n_lhs:2 * n_lhs + n_extra]
    outs = refs[2 * n_lhs + n_extra:-n_lhs]
    w_bf = refs[-n_lhs:]

    @pl.when(pl.program_id(1) == 0)
    def _():
        for l in range(n_lhs):
            w_bf[l][...] = ws[l][...].astype(BF16)

    acc = _dot(lhs[0][...], w_bf[0][...])
    for l in range(1, n_lhs):
        acc = acc + _dot(lhs[l][...], w_bf[l][...])
    epilogue(acc, extras, outs)


def _matmul(lhs_list, w_list, *, layer, name, bm, bn, n, w_col_off=0, extras=(), out_dtype=F32, epilogue=None):
    m = lhs_list[0].shape[0]
    grid = (n // bn, m // bm)
    in_specs = [pl.BlockSpec((bm, a.shape[1]), lambda j, i: (i, 0)) for a in lhs_list]
    in_specs += [pl.BlockSpec((None, a.shape[1], bn), (lambda j, i, l=l: (layer, l, j + w_col_off)))
                 for l, a in enumerate(lhs_list)]
    in_specs += [pl.BlockSpec(blk, (lambda j, i, f=f: f(i, j))) for (_, blk, f) in extras]
    if epilogue is None:
        def epilogue(acc, ex, outs):
            outs[0][...] = acc.astype(outs[0].dtype)
    kern = functools.partial(_mm_kernel, n_lhs=len(lhs_list), n_extra=len(extras), epilogue=epilogue)
    return pl.pallas_call(
        kern, grid=grid, in_specs=in_specs,
        out_specs=pl.BlockSpec((bm, bn), lambda j, i: (i, j)),
        out_shape=jax.ShapeDtypeStruct((m, n), out_dtype),
        scratch_shapes=[pltpu.VMEM((a.shape[1], bn), BF16) for a in lhs_list],
        compiler_params=_cparams(2), name=name,
    )(*lhs_list, *w_list, *[e[0] for e in extras])


def _residual_epilogue(acc, ex, outs):
    outs[0][...] = acc + ALPHA * ex[0][...]


def _rope128(a, cosf, sinf):
    return a * cosf + pltpu.roll(a, 64, 1) * sinf


def _rope64(a, cos64, sin_lo, sin_hi):
    return a * cos64 + pltpu.roll(a, 96, 1) * sin_lo + pltpu.roll(a, 32, 1) * sin_hi


def _inproj_kernel(x_ref, w_ref, cosf_ref, sinf_ref, cos64_ref, sinlo_ref, sinhi_ref,
                   h_ref, acc_ref, w_bf, *, q_scale):
    j = pl.program_id(0)

    @pl.when(pl.program_id(1) == 0)
    def _():
        w_bf[...] = w_ref[...].astype(BF16)

    acc_ref[...] = _dot(x_ref[...], w_bf[...])
    bn = acc_ref.shape[1]

    @pl.when(j < 5)
    def _():
        scale = jnp.where(j < 4, q_scale, 1.0).astype(F32)
        cosf = cosf_ref[...] * scale
        sinf = sinf_ref[...] * scale
        for c in range(bn // LANES):
            a = acc_ref[:, c * LANES:(c + 1) * LANES]
            h_ref[:, c * LANES:(c + 1) * LANES] = _rope128(a, cosf, sinf).astype(h_ref.dtype)

    @pl.when(j == 5)
    def _():
        h_ref[...] = acc_ref[...].astype(h_ref.dtype)

    @pl.when(j == 6)
    def _():
        for c in range(bn // LANES):
            a = acc_ref[:, c * LANES:(c + 1) * LANES]
            r = _rope64(a, cos64_ref[...], sinlo_ref[...], sinhi_ref[...])
            h_ref[:, c * LANES:(c + 1) * LANES] = r.astype(h_ref.dtype)


def _inproj_attn(x_bf, w, tables, *, layer, col_off_blocks, bm, bn):
    m, d = x_bf.shape
    n_tiles = 7
    q_scale = HEAD_DIM ** -0.5 * LOG2E
    tab_spec = pl.BlockSpec((bm, LANES), lambda j, i: (i, 0))
    return pl.pallas_call(
        functools.partial(_inproj_kernel, q_scale=q_scale), grid=(n_tiles, m // bm),
        in_specs=[pl.BlockSpec((bm, d), lambda j, i: (i, 0)),
                  pl.BlockSpec((None, d, bn), lambda j, i: (layer, 0, j + col_off_blocks))] + [tab_spec] * 5,
        out_specs=pl.BlockSpec((bm, bn), lambda j, i: (i, j)),
        out_shape=jax.ShapeDtypeStruct((m, n_tiles * bn), BF16),
        scratch_shapes=[pltpu.VMEM((bm, bn), F32), pltpu.VMEM((d, bn), BF16)],
        compiler_params=_cparams(2), name="inproj_attn",
    )(x_bf, w, *tables)


def _inproj_idx_kernel(x_ref, w_ref, cos64_ref, sinlo_ref, sinhi_ref, k2_ref, kw_ref, *, wi_scale):
    a = _dot(x_ref[...], w_ref[...].astype(BF16))
    r = _rope64(a, cos64_ref[...], sinlo_ref[...], sinhi_ref[...])
    lane = lax.broadcasted_iota(I32, a.shape, 1)
    kw_ref[...] = jnp.where(lane < IDX_DIM, r, a * wi_scale)
    k2_ref[...] = jnp.where(lane < IDX_DIM, r, pltpu.roll(r, 64, 1)).astype(k2_ref.dtype)


def _inproj_idx(x_bf, w_kw, tables, *, bm):
    m, d = x_bf.shape
    wi_scale = (N_IDX_HEADS ** -0.5) * (IDX_DIM ** -0.5)
    tab_spec = pl.BlockSpec((bm, LANES), lambda i: (i, 0))
    return pl.pallas_call(
        functools.partial(_inproj_idx_kernel, wi_scale=wi_scale), grid=(m // bm,),
        in_specs=[pl.BlockSpec((bm, d), lambda i: (i, 0)), pl.BlockSpec((d, LANES), lambda i: (0, 0))]
        + [tab_spec] * 3,
        out_specs=[tab_spec, tab_spec],
        out_shape=[jax.ShapeDtypeStruct((m, LANES), BF16), jax.ShapeDtypeStruct((m, LANES), F32)],
        compiler_params=_cparams(1), name="inproj_idx",
    )(x_bf, w_kw, *tables[2:])


def _pool_kernel(v_ref, pw_ref, ps_ref, o_ref, ext_ref):
    s = pl.program_id(1)
    ts = v_ref.shape[0]

    @pl.when(s == 0)
    def _():
        ext_ref[0:POOL_HALO, :] = jnp.zeros((POOL_HALO, ext_ref.shape[1]), F32)

    ext_ref[POOL_HALO:, :] = v_ref[...]
    t = s * ts + lax.broadcasted_iota(I32, (ts, 1), 0)
    for g, w in enumerate(POOL_WINDOWS):
        cols = slice(g * POOL_GW, (g + 1) * POOL_GW)
        e = ext_ref[:, cols]
        step = 1
        while step < w:
            e = e + pltpu.roll(e, step, 0)
            step *= 2
        win = e[POOL_HALO:, :]
        cnt = jnp.minimum(t + 1, w).astype(F32)
        pooled = win / cnt - v_ref[:, cols]
        mixed = _dot(pooled.astype(BF16), pw_ref[g])
        o_ref[:, cols] = (mixed * ps_ref[:, cols]).astype(o_ref.dtype)
    ext_ref[0:POOL_HALO, :] = v_ref[ts - POOL_HALO:, :]


def _pool_mixer(v_pool, pool_w_bf, pool_scale, *, batch, seq, ts):
    v3 = v_pool.reshape(batch, seq, MIX_POOL)
    out = pl.pallas_call(
        _pool_kernel, grid=(batch, seq // ts),
        in_specs=[pl.BlockSpec((None, ts, MIX_POOL), lambda b, s: (b, s, 0)),
                  pl.BlockSpec((N_POOL_GROUPS, POOL_GW, POOL_GW), lambda b, s: (0, 0, 0)),
                  pl.BlockSpec((1, MIX_POOL), lambda b, s: (0, 0))],
        out_specs=pl.BlockSpec((None, ts, MIX_POOL), lambda b, s: (b, s, 0)),
        out_shape=jax.ShapeDtypeStruct((batch, seq, MIX_POOL), BF16),
        scratch_shapes=[pltpu.VMEM((POOL_HALO + ts, MIX_POOL), F32)],
        compiler_params=_cparams(2), name="pool_mixer",
    )(v3, pool_w_bf, pool_scale.reshape(1, MIX_POOL))
    return out.reshape(batch * seq, MIX_POOL)


def _dsa_kernel(q_ref, k_ref, v_ref, k2_ref, qi_ref, kw_ref, o_ref,
                key_sc, bias_sc, xi_sc, qt_sc, m_sc, l_sc, acc_sc, s_sc, *, n_sel, seq):
    blk = pl.program_id(1)
    nchunk = blk // (LC // QB) + 1
    q_pos = blk * QB + lax.broadcasted_iota(I32, (1, QB), 1)
    k_iota = lax.broadcasted_iota(I32, (LC, 1), 0)
    sub = lax.broadcasted_iota(I32, (LANES, QB), 0)
    eye = jnp.where(sub == lax.broadcasted_iota(I32, (LANES, QB), 1), 1.0, 0.0).astype(BF16)

    def transposed(x):
        return _dot_nt(eye, x)

    for jj in range(N_IDX_HEADS // 2):
        xt = transposed(qi_ref[:, jj * LANES:(jj + 1) * LANES])
        xi_sc[jj, :, 0:QB] = jnp.where(sub < IDX_DIM, xt, 0.0).astype(BF16)
        xi_sc[jj, :, QB:2 * QB] = jnp.where(sub >= IDX_DIM, xt, 0.0).astype(BF16)
    for g in range(N_KV_HEADS):
        for r in range(Q_PER_KV):
            h = g * Q_PER_KV + r
            qt_sc[g, 0:HEAD_DIM, r * QB:(r + 1) * QB] = transposed(
                q_ref[:, h * HEAD_DIM:(h + 1) * HEAD_DIM]).astype(BF16)
            qt_sc[g, HEAD_DIM:, r * QB:(r + 1) * QB] = eye
    kw_t = kw_ref[...].T
    wi = [kw_t[IDX_DIM + h:IDX_DIM + h + 1, :] for h in range(N_IDX_HEADS)]

    inf = jnp.inf
    part = (FOLD_ROWS, QB)

    def score_chunk(c, carry):
        s_min, s_max = carry
        off = pl.multiple_of(c * LC, LC)
        k2c = k2_ref[pl.ds(off, LC), :]
        sc = jnp.zeros((LC, QB), F32)
        for jj in range(N_IDX_HEADS // 2):
            rel = jnp.maximum(_dot(k2c, xi_sc[jj]), 0.0)
            sc = sc + rel[:, 0:QB] * wi[2 * jj] + rel[:, QB:2 * QB] * wi[2 * jj + 1]
        causal = off + k_iota <= q_pos
        key_sc[pl.ds(off, LC), :] = jnp.where(causal, sc, -inf)
        s_min = jnp.minimum(s_min, _fold_rows(jnp.where(causal, sc, inf), jnp.min))
        s_max = jnp.maximum(s_max, _fold_rows(jnp.where(causal, sc, -inf), jnp.max))
        return s_min, s_max

    s_min, s_max = lax.fori_loop(0, nchunk, score_chunk, (jnp.full(part, inf, F32), jnp.full(part, -inf, F32)))

    n_sel_chunk = (blk * QB + QB + SC - 1) // SC
    s_iota = lax.broadcasted_iota(I32, (SC, 1), 0)

    def count(pred_fn):
        def body(c, acc):
            off = pl.multiple_of(c * SC, SC)
            m = pred_fn(key_sc[pl.ds(off, SC), :], off)
            return acc + _fold_rows(m, jnp.sum)
        acc = lax.fori_loop(0, n_sel_chunk, body, jnp.zeros(part, F32))
        return jnp.sum(acc, axis=0, keepdims=True)

    has_thr = q_pos + 1 >= int(n_sel)
    lo0 = jnp.where(has_thr, jnp.min(s_min, axis=0, keepdims=True), -inf)
    hi0 = jnp.where(has_thr, jnp.max(s_max, axis=0, keepdims=True), -inf)

    def midpoint(lo, hi):
        mid = 0.5 * (lo + hi)
        return jnp.where(mid <= lo, hi, mid)

    def halve(i, st):
        lo, hi = st
        mid = midpoint(lo, hi)
        enough = count(lambda sc, off: jnp.where(sc >= mid, 1.0, 0.0)) >= n_sel
        return jnp.where(enough, mid, lo), jnp.where(enough, hi, mid)

    lo0, hi0 = lax.fori_loop(0, N_HALVINGS, halve, (lo0, hi0))

    def open_rows(lo, hi):
        return jnp.max(jnp.where(lo < hi, 1.0, 0.0)) > 0.0

    def bisect_cond(st):
        it, lo, hi = st
        return open_rows(lo, hi) & (it < seq)

    def bisect_body(st):
        it, lo, hi = st
        mid = midpoint(lo, hi)

        def body(c, acc):
            cnt, up, dn = acc
            off = pl.multiple_of(c * SC, SC)
            sc = key_sc[pl.ds(off, SC), :]
            ge = sc >= mid
            cnt = cnt + _fold_rows(jnp.where(ge, 1.0, 0.0), jnp.sum)
            up = jnp.minimum(up, _fold_rows(jnp.where(ge, sc, inf), jnp.min))
            dn = jnp.maximum(dn, _fold_rows(jnp.where(ge, -inf, sc), jnp.max))
            return cnt, up, dn

        cnt, up, dn = lax.fori_loop(
            0, n_sel_chunk, body, (jnp.zeros(part, F32), jnp.full(part, inf, F32), jnp.full(part, -inf, F32)))
        enough = jnp.sum(cnt, axis=0, keepdims=True) >= n_sel
        is_open = lo < hi
        new_lo = jnp.where(is_open & enough, jnp.min(up, axis=0, keepdims=True), lo)
        new_hi = jnp.where(is_open & jnp.logical_not(enough), jnp.max(dn, axis=0, keepdims=True), hi)
        return it + 1, new_lo, new_hi

    _, thr, _ = lax.while_loop(bisect_cond, bisect_body, (jnp.int32(0), lo0, hi0))
    n_gt = count(lambda kk, off: jnp.where(kk > thr, 1.0, 0.0))
    n_eq = count(lambda kk, off: jnp.where(kk == thr, 1.0, 0.0))
    need = n_sel - n_gt

    def tie_search():
        def tbody(i, p):
            cand = p | jnp.left_shift(jnp.int32(1), (seq.bit_length() - 2) - i)
            cnt = count(lambda kk, off: jnp.where(kk == thr, jnp.where(off + s_iota < cand, 1.0, 0.0), 0.0))
            return jnp.where(cnt < need, cand, p)
        return lax.fori_loop(0, seq.bit_length() - 1, tbody, jnp.zeros((1, QB), I32))

    ambiguous = jnp.max(jnp.where(has_thr & (n_eq > need), 1.0, 0.0)) > 0.0
    tie_hi = lax.cond(ambiguous, tie_search, lambda: jnp.full((1, QB), seq, I32))
    tie_hi = jnp.where(has_thr, tie_hi, -1)

    def bias_chunk(c, carry):
        off = pl.multiple_of(c * LC, LC)
        kk = key_sc[pl.ds(off, LC), :]
        tie_ok = jnp.where(off + k_iota <= tie_hi, 0.0, NEG)
        bias = jnp.where(kk > thr, 0.0, jnp.where(kk == thr, tie_ok, NEG))
        bias_sc[pl.ds(off, LC), :] = bias.astype(BF16)
        return carry

    lax.fori_loop(0, nchunk, bias_chunk, 0)

    m_sc[...] = jnp.full(m_sc.shape, NEG, F32)
    l_sc[...] = jnp.zeros(l_sc.shape, F32)
    acc_sc[...] = jnp.zeros(acc_sc.shape, F32)

    def attn_chunk(c, carry):
        off = pl.multiple_of(c * LC, LC)
        bias = bias_sc[pl.ds(off, LC), :]

        def logits(g):
            kc = k_ref[pl.ds(off, LC), g * HEAD_DIM:(g + 1) * HEAD_DIM]
            return _dot(jnp.concatenate([kc, bias], axis=1), qt_sc[g])

        m_news = []
        for g in range(N_KV_HEADS):
            s = logits(g)
            s_sc[g] = s
            m_news.append(jnp.maximum(m_sc[g], jnp.max(_fold_rows(s, jnp.max), axis=0, keepdims=True)))
        for g in range(N_KV_HEADS):
            vc = v_ref[pl.ds(off, LC), g * HEAD_DIM:(g + 1) * HEAD_DIM]
            m_old = m_sc[g]
            m_new = m_news[g]
            alpha = jnp.exp2(m_old - m_new)
            p = jnp.exp2(s_sc[g] - m_new)
            l_sc[g] = alpha * l_sc[g] + jnp.sum(_fold_rows(p, jnp.sum), axis=0, keepdims=True)
            acc_sc[g] = alpha * acc_sc[g] + _dot_tn(vc, p.astype(BF16))
            m_sc[g] = m_new
        return carry

    lax.fori_loop(0, nchunk, attn_chunk, 0)

    for g in range(N_KV_HEADS):
        o_t = acc_sc[g] / l_sc[g]
        for r in range(Q_PER_KV):
            h = g * Q_PER_KV + r
            o_ref[:, h * HEAD_DIM:(h + 1) * HEAD_DIM] = o_t[:, r * QB:(r + 1) * QB].T.astype(o_ref.dtype)


def _dsa_attention(h2, k2, kw, *, batch, seq, bn):
    nb = seq // QB
    n_sel = min(TOPK_MAX, seq // 4)
    q_cols = N_Q_HEADS * HEAD_DIM
    kv_cols = N_KV_HEADS * HEAD_DIM
    assert bn == kv_cols and q_cols == 4 * bn and seq % LC == 0 and LC >= n_sel
    kern = functools.partial(_dsa_kernel, n_sel=float(n_sel), seq=seq)
    rows = Q_PER_KV * QB
    return pl.pallas_call(
        kern, grid=(batch, nb),
        in_specs=[pl.BlockSpec((QB, q_cols), lambda b, i: (b * nb + i, 0)),
                  pl.BlockSpec((seq, kv_cols), lambda b, i: (b, 4)),
                  pl.BlockSpec((seq, kv_cols), lambda b, i: (b, 5)),
                  pl.BlockSpec((seq, LANES), lambda b, i: (b, 0)),
                  pl.BlockSpec((QB, bn), lambda b, i: (b * nb + i, 6)),
                  pl.BlockSpec((QB, LANES), lambda b, i: (b * nb + i, 0))],
        out_specs=pl.BlockSpec((QB, q_cols), lambda b, i: (b * nb + i, 0)),
        out_shape=jax.ShapeDtypeStruct((batch * seq, q_cols), BF16),
        scratch_shapes=[pltpu.VMEM((seq, QB), F32),
                        pltpu.VMEM((seq, QB), BF16),
                        pltpu.VMEM((N_IDX_HEADS // 2, LANES, 2 * QB), BF16),
                        pltpu.VMEM((N_KV_HEADS, HEAD_DIM + QB, rows), BF16),
                        pltpu.VMEM((N_KV_HEADS, 1, rows), F32),
                        pltpu.VMEM((N_KV_HEADS, 1, rows), F32),
                        pltpu.VMEM((N_KV_HEADS, HEAD_DIM, rows), F32),
                        pltpu.VMEM((N_KV_HEADS, LC, rows), F32)],
        compiler_params=_cparams(2), name="dsa_attention",
    )(h2, h2, h2, k2, h2, kw)


def _ln_rows(x, g, b):
    mu = jnp.mean(x, axis=-1, keepdims=True)
    xc = x - mu
    var = jnp.mean(xc * xc, axis=-1, keepdims=True)
    return xc * lax.rsqrt(var + LN_EPS) * g + b


def _pack_halves(y):
    half = y.shape[1] // 2
    return pltpu.pack_elementwise([y[:, :half], y[:, half:]], packed_dtype=BF16)


def _unpack_halves(p):
    lo = pltpu.unpack_elementwise(p, index=0, packed_dtype=BF16, unpacked_dtype=F32)
    hi = pltpu.unpack_elementwise(p, index=1, packed_dtype=BF16, unpacked_dtype=F32)
    return jnp.concatenate([lo, hi], axis=1)


def _ln_kernel(x_ref, g_ref, b_ref, o_ref, o2_ref, *, packed):
    y = _ln_rows(x_ref[...], g_ref[...], b_ref[...])
    o_ref[...] = y
    o2_ref[...] = _pack_halves(y) if packed else y.astype(o2_ref.dtype)


def _layer_norm(x, g, b, *, bm, packed=False):
    m, d = x.shape
    row = pl.BlockSpec((bm, d), lambda i: (i, 0))
    vec = pl.BlockSpec((1, d), lambda i: (0, 0))
    second = ((m, d // 2), U32) if packed else ((m, d), BF16)
    return pl.pallas_call(
        functools.partial(_ln_kernel, packed=packed), grid=(m // bm,), in_specs=[row, vec, vec],
        out_specs=[row, pl.BlockSpec((bm, second[0][1]), lambda i: (i, 0))],
        out_shape=[jax.ShapeDtypeStruct((m, d), F32), jax.ShapeDtypeStruct(*second)],
        compiler_params=_cparams(1), name="layer_norm",
    )(x, g.reshape(1, d), b.reshape(1, d))


def _xattn_kernel(q_ref, k_ref, v_ref, o_ref):
    dh = q_ref.shape[1] // N_MEM_HEADS
    for h in range(N_MEM_HEADS):
        cols = slice(h * dh, (h + 1) * dh)
        s = _dot_nt(q_ref[:, cols], k_ref[:, cols])
        p = jnp.exp(s - jnp.max(s, axis=1, keepdims=True))
        o = _dot(p.astype(BF16), v_ref[:, cols]) / jnp.sum(p, axis=1, keepdims=True)
        o_ref[:, cols] = o.astype(o_ref.dtype)


def _cross_attention(q, k, v, *, batch, seq, n_mem, bm):
    d = q.shape[1]
    nb = seq // bm
    return pl.pallas_call(
        _xattn_kernel, grid=(batch, nb),
        in_specs=[pl.BlockSpec((bm, d), lambda b, i: (b * nb + i, 0)),
                  pl.BlockSpec((n_mem, d), lambda b, i: (b, 0)),
                  pl.BlockSpec((n_mem, d), lambda b, i: (b, 0))],
        out_specs=pl.BlockSpec((bm, d), lambda b, i: (b * nb + i, 0)),
        out_shape=jax.ShapeDtypeStruct((batch * seq, d), BF16),
        compiler_params=_cparams(2), name="cross_attention",
    )(q, k, v)


def _router_kernel(x_ref, w_ref, b_ref, id_ref, gate_ref):
    logits = jnp.dot(x_ref[...], w_ref[...], preferred_element_type=F32,
                     precision=lax.Precision.HIGHEST) + b_ref[...]
    lane = lax.broadcasted_iota(I32, logits.shape, 1)
    lane_f = lane.astype(F32)
    big = float(LANES)
    is_g = lane < N_GROUPS
    gl = jnp.where(is_g, logits, -jnp.inf)
    g_max = jnp.max(gl, axis=1, keepdims=True)
    g_idx = jnp.min(jnp.where(gl == g_max, lane_f, big), axis=1, keepdims=True)
    g_gate = 1.0 / jnp.sum(jnp.where(is_g, jnp.exp(gl - g_max), 0.0), axis=1, keepdims=True)
    e_lo = N_GROUPS + g_idx * EXPERTS_PER_GROUP
    in_grp = (lane_f >= e_lo) & (lane_f < e_lo + EXPERTS_PER_GROUP)
    el = jnp.where(in_grp, logits, -jnp.inf)
    v1 = jnp.max(el, axis=1, keepdims=True)
    i1 = jnp.min(jnp.where(el == v1, lane_f, big), axis=1, keepdims=True)
    el2 = jnp.where(lane_f == i1, -jnp.inf, el)
    v2 = jnp.max(el2, axis=1, keepdims=True)
    i2 = jnp.min(jnp.where(el2 == v2, lane_f, big), axis=1, keepdims=True)
    z = jnp.exp(v2 - v1)
    w1 = g_gate / (1.0 + z)
    w2 = g_gate * z / (1.0 + z)
    ids = jnp.where(lane == 0, i1 - N_GROUPS, jnp.where(lane == 1, i2 - N_GROUPS, 0.0))
    id_ref[...] = ids.astype(I32)
    gate_ref[...] = jnp.where(lane == 0, w1, jnp.where(lane == 1, w2, 0.0))


def _router(x, w_r, b_r, *, bm):
    m, d = x.shape
    out = pl.BlockSpec((bm, LANES), lambda i: (i, 0))
    return pl.pallas_call(
        _router_kernel, grid=(m // bm,),
        in_specs=[pl.BlockSpec((bm, d), lambda i: (i, 0)),
                  pl.BlockSpec((d, LANES), lambda i: (0, 0)),
                  pl.BlockSpec((1, LANES), lambda i: (0, 0))],
        out_specs=[out, out],
        out_shape=[jax.ShapeDtypeStruct((m, LANES), I32), jax.ShapeDtypeStruct((m, LANES), F32)],
        compiler_params=_cparams(1), name="router",
    )(x, w_r, b_r)


def _rank_kernel(id_ref, rank_ref, cnt_ref, carry_ref):
    i = pl.program_id(0)
    bt = id_ref.shape[0]

    @pl.when(i == 0)
    def _():
        carry_ref[...] = jnp.zeros(carry_ref.shape, F32)

    ids = id_ref[...]
    lane = lax.broadcasted_iota(I32, (bt, LANES), 1)
    e0 = ids[:, 0:1]
    e1 = ids[:, 1:2] + N_EXPERTS
    hit0 = lane == e0
    hit1 = lane == e1
    onehot = jnp.where(hit0, 1.0, jnp.where(hit1, 1.0, 0.0))
    r_io = lax.broadcasted_iota(I32, (bt, bt), 0)
    c_io = lax.broadcasted_iota(I32, (bt, bt), 1)
    tri = jnp.where(c_io < r_io, 1.0, 0.0).astype(BF16)
    prefix = _dot(tri, onehot.astype(BF16)) + carry_ref[0:1, :]
    rank0 = jnp.sum(jnp.where(hit0, prefix, 0.0), axis=1, keepdims=True)
    rank1 = jnp.sum(jnp.where(hit1, prefix, 0.0), axis=1, keepdims=True)
    rank_ref[...] = jnp.where(lane == 0, rank0, jnp.where(lane == 1, rank1, 0.0))
    total = carry_ref[0:1, :] + jnp.sum(onehot, axis=0, keepdims=True)
    carry_ref[...] = jnp.broadcast_to(total, carry_ref.shape)
    cnt_ref[...] = jnp.broadcast_to(total, cnt_ref.shape)


def _slot_ranks(ids, *, bt):
    m = ids.shape[0]
    return pl.pallas_call(
        _rank_kernel, grid=(m // bt,),
        in_specs=[pl.BlockSpec((bt, LANES), lambda i: (i, 0))],
        out_specs=[pl.BlockSpec((bt, LANES), lambda i: (i, 0)),
                   pl.BlockSpec((8, LANES), lambda i: (0, 0))],
        out_shape=[jax.ShapeDtypeStruct((m, LANES), F32), jax.ShapeDtypeStruct((8, LANES), F32)],
        scratch_shapes=[pltpu.VMEM((8, LANES), F32)],
        compiler_params=_cparams(1), name="slot_ranks",
    )(ids)


def _pos_kernel(id_ref, rank_ref, tab_ref, pos_ref):
    ids = id_ref[...]
    ranks = rank_ref[...]
    lane = lax.broadcasted_iota(I32, ids.shape, 1)
    tab = tab_ref[...]
    base0 = jnp.sum(jnp.where(lane == ids[:, 0:1], tab, 0.0), axis=1, keepdims=True)
    base1 = jnp.sum(jnp.where(lane == ids[:, 1:2] + N_EXPERTS, tab, 0.0), axis=1, keepdims=True)
    pos = jnp.where(lane == 0, base0 + ranks[:, 0:1], jnp.where(lane == 1, base1 + ranks[:, 1:2], 0.0))
    pos_ref[...] = pos.astype(I32)


def _slot_positions(ids, ranks, table, *, bt):
    m = ids.shape[0]
    blk = pl.BlockSpec((bt, LANES), lambda i: (i, 0))
    return pl.pallas_call(
        _pos_kernel, grid=(m // bt,),
        in_specs=[blk, blk, pl.BlockSpec((1, LANES), lambda i: (0, 0))],
        out_specs=blk,
        out_shape=jax.ShapeDtypeStruct((m, LANES), I32),
        compiler_params=_cparams(1), name="slot_positions",
    )(ids, ranks, table)


def _row_copy(src_ref, src_row, dst_ref, dst_row, sem):
    return pltpu.make_async_copy(src_ref.at[pl.ds(src_row, 1)], dst_ref.at[pl.ds(dst_row, 1)], sem)


def _dispatch_kernel(pos0_ref, pos1_ref, x_ref, xs_in_ref, xs_ref, sem):
    del xs_in_ref
    i = pl.program_id(0)
    bt = x_ref.shape[0]

    def issue(r, carry):
        t = i * bt + r
        _row_copy(x_ref, r, xs_ref, pos0_ref[t], sem.at[0]).start()
        _row_copy(x_ref, r, xs_ref, pos1_ref[t], sem.at[1]).start()
        return carry

    lax.fori_loop(0, bt, issue, 0)
    for k in range(2):
        pltpu.make_async_copy(x_ref, xs_ref.at[pl.ds(0, bt)], sem.at[k]).wait()


def _dispatch(x, pos0, pos1, n_slots, *, bt):
    m, d = x.shape
    xs0 = jnp.zeros((n_slots, d), x.dtype)
    return pl.pallas_call(
        _dispatch_kernel,
        grid_spec=pltpu.PrefetchScalarGridSpec(
            num_scalar_prefetch=2, grid=(m // bt,),
            in_specs=[pl.BlockSpec((bt, d), lambda i, p0, p1: (i, 0)),
                      pl.BlockSpec(memory_space=pl.ANY)],
            out_specs=pl.BlockSpec(memory_space=pl.ANY),
            scratch_shapes=[pltpu.SemaphoreType.DMA((2,))]),
        out_shape=jax.ShapeDtypeStruct((n_slots, d), x.dtype),
        input_output_aliases={3: 0},
        compiler_params=_cparams(1), name="dispatch",
    )(pos0, pos1, x, xs0)


def _ffn_kernel(blk_e_ref, nxt_e_ref, set_ref, nused_ref, x_ref, wg_hbm, wu_hbm, wd_hbm, y_ref,
                wg_bf, wu_bf, wd_bf, st_g, st_u, st_d, sem):
    i = pl.program_id(0)
    d, ff = wg_bf.shape[1], wg_bf.shape[2]
    rg, rd = d // W_GROUPS, ff // W_GROUPS

    n_slot = st_g.shape[0]

    def group_copies(e, g):
        slot = g % n_slot
        return (pltpu.make_async_copy(wg_hbm.at[e, pl.ds(g * rg, rg), :], st_g.at[slot], sem.at[0, slot]),
                pltpu.make_async_copy(wu_hbm.at[e, pl.ds(g * rg, rg), :], st_u.at[slot], sem.at[1, slot]),
                pltpu.make_async_copy(wd_hbm.at[e, pl.ds(g * rd, rd), :], st_d.at[slot], sem.at[2, slot]))

    def start(e, g):
        for c in group_copies(e, g):
            c.start()

    def finish(e, g, s):
        for c in group_copies(e, g):
            c.wait()
        slot = g % n_slot
        wg_bf[s, g * rg:(g + 1) * rg, :] = st_g[slot].astype(BF16)
        wu_bf[s, g * rg:(g + 1) * rg, :] = st_u[slot].astype(BF16)
        wd_bf[s, g * rd:(g + 1) * rd, :] = st_d[slot].astype(BF16)

    def load_expert(e, s, between=()):
        for g in range(n_slot):
            start(e, g)
        for g in range(W_GROUPS):
            if g < len(between):
                between[g]()
            finish(e, g, s)
            if g + n_slot < W_GROUPS:
                start(e, g + n_slot)

    e = blk_e_ref[i]
    s = set_ref[i]
    nxt = nxt_e_ref[i]
    first_of_expert = jnp.logical_or(i == 0, blk_e_ref[jnp.maximum(i - 1, 0)] != e)
    active = i < nused_ref[0]
    prefetch = jnp.logical_and(jnp.logical_and(active, first_of_expert), nxt >= 0)

    @pl.when(i == 0)
    def _():
        load_expert(e, s)

    def ffn_steps():
        vals = {}

        def gate_step():
            vals["x"] = _unpack_halves(x_ref[...]).astype(BF16)
            vals["gate"] = _dot(vals["x"], wg_bf[s])

        def up_step():
            gate = vals["gate"]
            vals["h"] = (gate * (1.0 / (1.0 + jnp.exp(-gate))) * _dot(vals["x"], wu_bf[s])).astype(BF16)

        def down_step():
            y_ref[...] = _pack_halves(_dot(vals["h"], wd_bf[s]))

        return gate_step, up_step, down_step

    @pl.when(prefetch)
    def _():
        load_expert(nxt, 1 - s, between=ffn_steps())

    @pl.when(jnp.logical_and(active, jnp.logical_not(prefetch)))
    def _():
        for step in ffn_steps():
            step()


def _expert_ffn(xs, blk_e, nxt_e, set_idx, nused, wg, wu, wd):
    n_slots, dp = xs.shape
    d, ff = wg.shape[1], wg.shape[2]
    nblk = n_slots // MOE_BLK
    assert dp * 2 == d and d % W_GROUPS == 0 and ff % W_GROUPS == 0 and W_GROUPS >= W_STAGE_SLOTS > 3

    def row_map(i, be, nx, si, nu):
        return (jnp.minimum(i, nu[0] - 1), 0)

    hbm = pl.BlockSpec(memory_space=pl.ANY)
    return pl.pallas_call(
        _ffn_kernel,
        grid_spec=pltpu.PrefetchScalarGridSpec(
            num_scalar_prefetch=4, grid=(nblk,),
            in_specs=[pl.BlockSpec((MOE_BLK, dp), row_map), hbm, hbm, hbm],
            out_specs=pl.BlockSpec((MOE_BLK, dp), row_map),
            scratch_shapes=[pltpu.VMEM((2, d, ff), BF16), pltpu.VMEM((2, d, ff), BF16), pltpu.VMEM((2, ff, d), BF16),
                            pltpu.VMEM((W_STAGE_SLOTS, d // W_GROUPS, ff), F32),
                            pltpu.VMEM((W_STAGE_SLOTS, d // W_GROUPS, ff), F32),
                            pltpu.VMEM((W_STAGE_SLOTS, ff // W_GROUPS, d), F32),
                            pltpu.SemaphoreType.DMA((3, W_STAGE_SLOTS))]),
        out_shape=jax.ShapeDtypeStruct((n_slots, dp), xs.dtype),
        input_output_aliases={4: 0},
        compiler_params=_cparams(1), name="expert_ffn",
    )(blk_e, nxt_e, set_idx, nused, xs, wg, wu, wd)


def _combine_kernel(pos0_ref, pos1_ref, x_ref, gate_ref, g_ref, b_ref, ys_ref, o_ref, y0_ref, y1_ref, sem):
    i = pl.program_id(0)
    bt = x_ref.shape[0]
    slot = i % 2

    def issue(step, to_slot):
        def body(r, carry):
            t = step * bt + r
            _row_copy(ys_ref, pos0_ref[t], y0_ref.at[to_slot], r, sem.at[0, to_slot]).start()
            _row_copy(ys_ref, pos1_ref[t], y1_ref.at[to_slot], r, sem.at[1, to_slot]).start()
            return carry
        lax.fori_loop(0, bt, body, 0)

    @pl.when(i == 0)
    def _():
        issue(0, 0)

    @pl.when(i + 1 < pl.num_programs(0))
    def _():
        issue(i + 1, 1 - slot)

    pltpu.make_async_copy(ys_ref.at[pl.ds(0, bt)], y0_ref.at[slot], sem.at[0, slot]).wait()
    pltpu.make_async_copy(ys_ref.at[pl.ds(0, bt)], y1_ref.at[slot], sem.at[1, slot]).wait()
    gates = gate_ref[...]
    f = _unpack_halves(y0_ref[slot]) * gates[:, 0:1] + _unpack_halves(y1_ref[slot]) * gates[:, 1:2]
    o_ref[...] = _ln_rows(ALPHA * x_ref[...] + f, g_ref[...], b_ref[...])


def _combine(x, gates, ys, pos0, pos1, g, b, *, bt):
    m, d = x.shape
    dp = ys.shape[1]
    row = pl.BlockSpec((bt, d), lambda i, p0, p1: (i, 0))
    vec = pl.BlockSpec((1, d), lambda i, p0, p1: (0, 0))
    return pl.pallas_call(
        _combine_kernel,
        grid_spec=pltpu.PrefetchScalarGridSpec(
            num_scalar_prefetch=2, grid=(m // bt,),
            in_specs=[row, pl.BlockSpec((bt, LANES), lambda i, p0, p1: (i, 0)), vec, vec,
                      pl.BlockSpec(memory_space=pl.ANY)],
            out_specs=row,
            scratch_shapes=[pltpu.VMEM((2, bt, dp), ys.dtype), pltpu.VMEM((2, bt, dp), ys.dtype),
                            pltpu.SemaphoreType.DMA((2, 2))]),
        out_shape=jax.ShapeDtypeStruct((m, d), F32),
        compiler_params=_cparams(1), name="combine_ln",
    )(pos0, pos1, x, gates, g.reshape(1, d), b.reshape(1, d), ys)


def _rope_tables(positions, n_rows):
    pos = positions.astype(F32).reshape(n_rows, 1)

    def cs(dim):
        inv = 1.0 / (ROPE_THETA ** (jnp.arange(0, dim, 2, dtype=F32) / dim))
        ang = pos * inv[None, :]
        return jnp.cos(ang), jnp.sin(ang)

    c_h, s_h = cs(HEAD_DIM)
    c_i, s_i = cs(IDX_DIM)
    z_i = jnp.zeros_like(s_i)
    cosf = jnp.concatenate([c_h, c_h], axis=1)
    sinf = jnp.concatenate([-s_h, s_h], axis=1)
    cos64 = jnp.concatenate([c_i, c_i, c_i, c_i], axis=1)
    sin_lo = jnp.concatenate([-s_i, z_i, -s_i, z_i], axis=1)
    sin_hi = jnp.concatenate([z_i, s_i, z_i, s_i], axis=1)
    return cosf, sinf, cos64, sin_lo, sin_hi


def kernel(x, mem, positions, w_in, pool_w, pool_scale, w_o, ln1_g, ln1_b, w_mq, w_mk, w_mv, w_mo, ln2_g, ln2_b,
           w_group_router, b_group_router, w_expert_router, b_expert_router, w_gate, w_up, w_down, ln3_g, ln3_b):
    batch, seq, d = x.shape
    n_mem = mem.shape[1]
    n = batch * seq
    bn_attn = N_KV_HEADS * HEAD_DIM
    tables = _rope_tables(positions, n)
    xf = x.reshape(n, d)
    n_slots = 2 * n + N_EXPERTS * MOE_BLK
    nblk = n_slots // MOE_BLK

    for l in range(w_in.shape[0]):
        x_bf = xf.astype(BF16)
        kw_col = MIX_POOL + 7 * bn_attn
        w_kw = jnp.pad(w_in[l][:, kw_col:], ((0, 0), (0, LANES - (w_in.shape[2] - kw_col))))

        v_pool = _matmul([x_bf], [w_in], layer=l, name="inproj_pool", bm=1024, bn=DENSE_BN, n=MIX_POOL)
        h2 = _inproj_attn(x_bf, w_in, tables, layer=l, col_off_blocks=MIX_POOL // bn_attn, bm=1024, bn=bn_attn)
        k2, kw = _inproj_idx(x_bf, w_kw, tables, bm=1024)
        a_pool = _pool_mixer(v_pool, pool_w[l].astype(BF16), pool_scale[l], batch=batch, seq=seq, ts=512)
        a_attn = _dsa_attention(h2, k2, kw, batch=batch, seq=seq, bn=bn_attn)
        res = (xf, (1024, DENSE_BN), lambda i, j: (i, j))
        pre = _matmul([a_pool, a_attn], [w_o, w_o], layer=l, name="outproj", bm=1024, bn=DENSE_BN, n=d,
                      extras=(res,), epilogue=_residual_epilogue)
        x1, x1_bf = _layer_norm(pre, ln1_g[l], ln1_b[l], bm=256)

        mq_scale = (d // N_MEM_HEADS) ** -0.5

        def q_epilogue(acc, ex, outs):
            outs[0][...] = (acc * mq_scale).astype(outs[0].dtype)

        qm = _matmul([x1_bf], [w_mq], layer=l, name="mem_q", bm=1024, bn=DENSE_BN, n=d, out_dtype=BF16,
                     epilogue=q_epilogue)
        mem_bf = mem.reshape(batch * n_mem, d).astype(BF16)
        km = _matmul([mem_bf], [w_mk], layer=l, name="mem_k", bm=batch * n_mem, bn=DENSE_BN, n=d, out_dtype=BF16)
        vm = _matmul([mem_bf], [w_mv], layer=l, name="mem_v", bm=batch * n_mem, bn=DENSE_BN, n=d, out_dtype=BF16)
        om = _cross_attention(qm, km, vm, batch=batch, seq=seq, n_mem=n_mem, bm=512)
        res = (x1, (1024, DENSE_BN), lambda i, j: (i, j))
        pre = _matmul([om], [w_mo], layer=l, name="mem_o", bm=1024, bn=DENSE_BN, n=d, extras=(res,),
                      epilogue=_residual_epilogue)
        x2, x2_rows = _layer_norm(pre, ln2_g[l], ln2_b[l], bm=256, packed=True)

        w_r = jnp.pad(jnp.concatenate([w_group_router[l], w_expert_router[l]], axis=1),
                      ((0, 0), (0, LANES - N_GROUPS - N_EXPERTS)))
        b_r = jnp.pad(jnp.concatenate([b_group_router[l], b_expert_router[l]]),
                      (0, LANES - N_GROUPS - N_EXPERTS)).reshape(1, LANES)
        ids, gates = _router(x2, w_r, b_r, bm=512)
        ranks, totals = _slot_ranks(ids, bt=512)
        c0 = totals[0, :N_EXPERTS].astype(I32)
        c1 = totals[0, N_EXPERTS:].astype(I32)
        padded = ((c0 + c1 + MOE_BLK - 1) // MOE_BLK) * MOE_BLK
        pend = jnp.cumsum(padded)
        pstart = pend - padded
        table = jnp.concatenate([pstart, pstart + c0]).astype(F32).reshape(1, LANES)
        pos = _slot_positions(ids, ranks, table, bt=512)
        pos0, pos1 = pos[:, 0], pos[:, 1]
        nused = pend[-1] // MOE_BLK
        blk_i = jnp.minimum(jnp.arange(nblk, dtype=I32), nused - 1)
        blk_e = jnp.sum((pend[None, :] <= (blk_i * MOE_BLK)[:, None]).astype(I32), axis=1)
        blk_e = jnp.minimum(blk_e, N_EXPERTS - 1)
        later = blk_e[None, :] > blk_e[:, None]
        nxt_e = jnp.min(jnp.where(later, blk_e[None, :], N_EXPERTS), axis=1)
        nxt_e = jnp.where(nxt_e == N_EXPERTS, -1, nxt_e).astype(I32)
        new_e = jnp.concatenate([jnp.ones((1,), I32), (blk_e[1:] != blk_e[:-1]).astype(I32)])
        set_idx = (jnp.cumsum(new_e) - 1) % 2

        xs = _dispatch(x2_rows, pos0, pos1, n_slots, bt=256)
        ys = _expert_ffn(xs, blk_e, nxt_e, set_idx.astype(I32), nused.reshape(1).astype(I32),
                         w_gate[l], w_up[l], w_down[l])
        xf = _combine(x2, gates, ys, pos0, pos1, ln3_g[l], ln3_b[l], bt=256)
    return xf.reshape(batch, seq, d)
```

```python
import functools

import jax
import jax.numpy as jnp
from jax import lax
from jax.experimental import pallas as pl
from jax.experimental.pallas import tpu as pltpu

F32 = jnp.float32
BF16 = jnp.bfloat16
I32 = jnp.int32
U32 = jnp.uint32

MIX_POOL = 2048
N_POOL_GROUPS = 4
POOL_WINDOWS = (2, 4, 8, 16)
POOL_GW = MIX_POOL // N_POOL_GROUPS
HEAD_DIM = 128
N_Q_HEADS = 16
N_KV_HEADS = 4
Q_PER_KV = N_Q_HEADS // N_KV_HEADS
N_IDX_HEADS = 8
IDX_DIM = 64
TOPK_MAX = 256
ROPE_THETA = 10000.0
N_MEM_HEADS = 4
N_GROUPS = 8
EXPERTS_PER_GROUP = 8
N_EXPERTS = N_GROUPS * EXPERTS_PER_GROUP
LN_EPS = 1e-5
DEPTH = 1
ALPHA = (2.0 * DEPTH) ** 0.25

LANES = 128
QB = 128
LC = 512
SC = 256
DENSE_BN = 512
MOE_BLK = 256
W_GROUPS = 8
W_STAGE_SLOTS = 4
POOL_HALO = 16
NEG = -1e30
N_HALVINGS = 12
LOG2E = 1.4426950408889634
VMEM_LIMIT = 56 * 1024 * 1024


def _cparams(n_axes, vmem=VMEM_LIMIT):
    return pltpu.CompilerParams(dimension_semantics=("arbitrary",) * n_axes, vmem_limit_bytes=vmem)


def _dot(a, b):
    return jnp.dot(a, b, preferred_element_type=F32)


def _dot_nt(a, b):
    return lax.dot_general(a, b, (((1,), (1,)), ((), ())), preferred_element_type=F32)


def _dot_tn(a, b):
    return lax.dot_general(a, b, (((0,), (0,)), ((), ())), preferred_element_type=F32)


FOLD_ROWS = 32


def _fold_rows(x, reduce_fn):
    r, c = x.shape
    return reduce_fn(x.reshape(r // FOLD_ROWS, FOLD_ROWS, c), axis=0)


def _mm_kernel(*refs, n_lhs, n_extra, epilogue):
    lhs = refs[:n_lhs]
    ws = refs[n_lhs:2 * n_lhs]
    extras = refs[2 * n_lhs:2 * n_lhs + n_extra]
    outs = refs[2 * n_lhs + n_extra:-n_lhs]
    w_bf = refs[-n_lhs:]

    @pl.when(pl.program_id(1) == 0)
    def _():
        for l in range(n_lhs):
            w_bf[l][...] = ws[l][...].astype(BF16)

    acc = _dot(lhs[0][...], w_bf[0][...])
    for l in range(1, n_lhs):
        acc = acc + _dot(lhs[l][...], w_bf[l][...])
    epilogue(acc, extras, outs)


def _matmul(lhs_list, w_list, *, layer, name, bm, bn, n, w_col_off=0, extras=(), out_dtype=F32, epilogue=None):
    m = lhs_list[0].shape[0]
    grid = (n // bn, m // bm)
    in_specs = [pl.BlockSpec((bm, a.shape[1]), lambda j, i: (i, 0)) for a in lhs_list]
    in_specs += [pl.BlockSpec((None, a.shape[1], bn), (lambda j, i, l=l: (layer, l, j + w_col_off)))
                 for l, a in enumerate(lhs_list)]
    in_specs += [pl.BlockSpec(blk, (lambda j, i, f=f: f(i, j))) for (_, blk, f) in extras]
    if epilogue is None:
        def epilogue(acc, ex, outs):
            outs[0][...] = acc.astype(outs[0].dtype)
    kern = functools.partial(_mm_kernel, n_lhs=len(lhs_list), n_extra=len(extras), epilogue=epilogue)
    return pl.pallas_call(
        kern, grid=grid, in_specs=in_specs,
        out_specs=pl.BlockSpec((bm, bn), lambda j, i: (i, j)),
        out_shape=jax.ShapeDtypeStruct((m, n), out_dtype),
        scratch_shapes=[pltpu.VMEM((a.shape[1], bn), BF16) for a in lhs_list],
        compiler_params=_cparams(2), name=name,
    )(*lhs_list, *w_list, *[e[0] for e in extras])


def _residual_epilogue(acc, ex, outs):
    outs[0][...] = acc + ALPHA * ex[0][...]


def _rope128(a, cosf, sinf):
    return a * cosf + pltpu.roll(a, 64, 1) * sinf


def _rope64(a, cos64, sin_lo, sin_hi):
    return a * cos64 + pltpu.roll(a, 96, 1) * sin_lo + pltpu.roll(a, 32, 1) * sin_hi


def _inproj_kernel(x_ref, w_ref, cosf_ref, sinf_ref, cos64_ref, sinlo_ref, sinhi_ref,
                   h_ref, acc_ref, w_bf, *, q_scale):
    j = pl.program_id(0)

    @pl.when(pl.program_id(1) == 0)
    def _():
        w_bf[...] = w_ref[...].astype(BF16)

    acc_ref[...] = _dot(x_ref[...], w_bf[...])
    bn = acc_ref.shape[1]

    @pl.when(j < 5)
    def _():
        scale = jnp.where(j < 4, q_scale, 1.0).astype(F32)
        cosf = cosf_ref[...] * scale
        sinf = sinf_ref[...] * scale
        for c in range(bn // LANES):
            a = acc_ref[:, c * LANES:(c + 1) * LANES]
            h_ref[:, c * LANES:(c + 1) * LANES] = _rope128(a, cosf, sinf).astype(h_ref.dtype)

    @pl.when(j == 5)
    def _():
        h_ref[...] = acc_ref[...].astype(h_ref.dtype)

    @pl.when(j == 6)
    def _():
        for c in range(bn // LANES):
            a = acc_ref[:, c * LANES:(c + 1) * LANES]
            r = _rope64(a, cos64_ref[...], sinlo_ref[...], sinhi_ref[...])
            h_ref[:, c * LANES:(c + 1) * LANES] = r.astype(h_ref.dtype)


def _inproj_attn(x_bf, w, tables, *, layer, col_off_blocks, bm, bn):
    m, d = x_bf.shape
    n_tiles = 7
    q_scale = HEAD_DIM ** -0.5 * LOG2E
    tab_spec = pl.BlockSpec((bm, LANES), lambda j, i: (i, 0))
    return pl.pallas_call(
        functools.partial(_inproj_kernel, q_scale=q_scale), grid=(n_tiles, m // bm),
        in_specs=[pl.BlockSpec((bm, d), lambda j, i: (i, 0)),
                  pl.BlockSpec((None, d, bn), lambda j, i: (layer, 0, j + col_off_blocks))] + [tab_spec] * 5,
        out_specs=pl.BlockSpec((bm, bn), lambda j, i: (i, j)),
        out_shape=jax.ShapeDtypeStruct((m, n_tiles * bn), BF16),
        scratch_shapes=[pltpu.VMEM((bm, bn), F32), pltpu.VMEM((d, bn), BF16)],
        compiler_params=_cparams(2), name="inproj_attn",
    )(x_bf, w, *tables)


def _inproj_idx_kernel(x_ref, w_ref, cos64_ref, sinlo_ref, sinhi_ref, k2_ref, kw_ref, *, wi_scale):
    a = _dot(x_ref[...], w_ref[...].astype(BF16))
    r = _rope64(a, cos64_ref[...], sinlo_ref[...], sinhi_ref[...])
    lane = lax.broadcasted_iota(I32, a.shape, 1)
    kw_ref[...] = jnp.where(lane < IDX_DIM, r, a * wi_scale)
    k2_ref[...] = jnp.where(lane < IDX_DIM, r, pltpu.roll(r, 64, 1)).astype(k2_ref.dtype)


def _inproj_idx(x_bf, w_kw, tables, *, bm):
    m, d = x_bf.shape
    wi_scale = (N_IDX_HEADS ** -0.5) * (IDX_DIM ** -0.5)
    tab_spec = pl.BlockSpec((bm, LANES), lambda i: (i, 0))
    return pl.pallas_call(
        functools.partial(_inproj_idx_kernel, wi_scale=wi_scale), grid=(m // bm,),
        in_specs=[pl.BlockSpec((bm, d), lambda i: (i, 0)), pl.BlockSpec((d, LANES), lambda i: (0, 0))]
        + [tab_spec] * 3,
        out_specs=[tab_spec, tab_spec],
        out_shape=[jax.ShapeDtypeStruct((m, LANES), BF16), jax.ShapeDtypeStruct((m, LANES), F32)],
        compiler_params=_cparams(1), name="inproj_idx",
    )(x_bf, w_kw, *tables[2:])


def _pool_kernel(v_ref, pw_ref, ps_ref, o_ref, ext_ref):
    s = pl.program_id(1)
    ts = v_ref.shape[0]

    @pl.when(s == 0)
    def _():
        ext_ref[0:POOL_HALO, :] = jnp.zeros((POOL_HALO, ext_ref.shape[1]), F32)

    ext_ref[POOL_HALO:, :] = v_ref[...]
    t = s * ts + lax.broadcasted_iota(I32, (ts, 1), 0)
    for g, w in enumerate(POOL_WINDOWS):
        cols = slice(g * POOL_GW, (g + 1) * POOL_GW)
        e = ext_ref[:, cols]
        step = 1
        while step < w:
            e = e + pltpu.roll(e, step, 0)
            step *= 2
        win = e[POOL_HALO:, :]
        cnt = jnp.minimum(t + 1, w).astype(F32)
        pooled = win / cnt - v_ref[:, cols]
        mixed = _dot(pooled.astype(BF16), pw_ref[g])
        o_ref[:, cols] = (mixed * ps_ref[:, cols]).astype(o_ref.dtype)
    ext_ref[0:POOL_HALO, :] = v_ref[ts - POOL_HALO:, :]


def _pool_mixer(v_pool, pool_w_bf, pool_scale, *, batch, seq, ts):
    v3 = v_pool.reshape(batch, seq, MIX_POOL)
    out = pl.pallas_call(
        _pool_kernel, grid=(batch, seq // ts),
        in_specs=[pl.BlockSpec((None, ts, MIX_POOL), lambda b, s: (b, s, 0)),
                  pl.BlockSpec((N_POOL_GROUPS, POOL_GW, POOL_GW), lambda b, s: (0, 0, 0)),
                  pl.BlockSpec((1, MIX_POOL), lambda b, s: (0, 0))],
        out_specs=pl.BlockSpec((None, ts, MIX_POOL), lambda b, s: (b, s, 0)),
        out_shape=jax.ShapeDtypeStruct((batch, seq, MIX_POOL), BF16),
        scratch_shapes=[pltpu.VMEM((POOL_HALO + ts, MIX_POOL), F32)],
        compiler_params=_cparams(2), name="pool_mixer",
    )(v3, pool_w_bf, pool_scale.reshape(1, MIX_POOL))
    return out.reshape(batch * seq, MIX_POOL)


def _dsa_kernel(q_ref, k_ref, v_ref, k2_ref, qi_ref, kw_ref, o_ref,
                key_sc, bias_sc, xi_sc, qt_sc, m_sc, l_sc, acc_sc, s_sc, cmax_sc, s2_sc, cmax2_sc, *, n_sel, seq):
    blk = pl.program_id(1)
    nchunk = blk // (LC // QB) + 1
    q_pos = blk * QB + lax.broadcasted_iota(I32, (1, QB), 1)
    k_iota = lax.broadcasted_iota(I32, (LC, 1), 0)
    sub = lax.broadcasted_iota(I32, (LANES, QB), 0)
    eye = jnp.where(sub == lax.broadcasted_iota(I32, (LANES, QB), 1), 1.0, 0.0).astype(BF16)

    def transposed(x):
        return _dot_nt(eye, x)

    for jj in range(N_IDX_HEADS // 2):
        xt = transposed(qi_ref[:, jj * LANES:(jj + 1) * LANES])
        xi_sc[jj, :, 0:QB] = jnp.where(sub < IDX_DIM, xt, 0.0).astype(BF16)
        xi_sc[jj, :, QB:2 * QB] = jnp.where(sub >= IDX_DIM, xt, 0.0).astype(BF16)
    for g in range(N_KV_HEADS):
        for r in range(Q_PER_KV):
            h = g * Q_PER_KV + r
            qt_sc[g, 0:HEAD_DIM, r * QB:(r + 1) * QB] = transposed(
                q_ref[:, h * HEAD_DIM:(h + 1) * HEAD_DIM]).astype(BF16)
            qt_sc[g, HEAD_DIM:, r * QB:(r + 1) * QB] = eye
    kw_t = kw_ref[...].T
    wi = [kw_t[IDX_DIM + h:IDX_DIM + h + 1, :] for h in range(N_IDX_HEADS)]

    inf = jnp.inf
    part = (FOLD_ROWS, QB)

    def score_chunk(c, carry):
        s_min, s_max = carry
        off = pl.multiple_of(c * LC, LC)
        k2c = k2_ref[pl.ds(off, LC), :]
        sc = jnp.zeros((LC, QB), F32)
        for jj in range(N_IDX_HEADS // 2):
            rel = jnp.maximum(_dot(k2c, xi_sc[jj]), 0.0)
            sc = sc + rel[:, 0:QB] * wi[2 * jj] + rel[:, QB:2 * QB] * wi[2 * jj + 1]
        causal = off + k_iota <= q_pos
        key_sc[pl.ds(off, LC), :] = jnp.where(causal, sc, -inf)
        s_min = jnp.minimum(s_min, _fold_rows(jnp.where(causal, sc, inf), jnp.min))
        s_max = jnp.maximum(s_max, _fold_rows(jnp.where(causal, sc, -inf), jnp.max))
        return s_min, s_max

    s_min, s_max = lax.fori_loop(0, nchunk, score_chunk, (jnp.full(part, inf, F32), jnp.full(part, -inf, F32)))

    n_sel_chunk = (blk * QB + QB + SC - 1) // SC
    s_iota = lax.broadcasted_iota(I32, (SC, 1), 0)

    def count(pred_fn):
        def body(c, acc):
            off = pl.multiple_of(c * SC, SC)
            m = pred_fn(key_sc[pl.ds(off, SC), :], off)
            return acc + _fold_rows(m, jnp.sum)
        acc = lax.fori_loop(0, n_sel_chunk, body, jnp.zeros(part, F32))
        return jnp.sum(acc, axis=0, keepdims=True)

    has_thr = q_pos + 1 >= int(n_sel)
    lo0 = jnp.where(has_thr, jnp.min(s_min, axis=0, keepdims=True), -inf)
    hi0 = jnp.where(has_thr, jnp.max(s_max, axis=0, keepdims=True), -inf)

    def midpoint(lo, hi):
        mid = 0.5 * (lo + hi)
        return jnp.where(mid <= lo, hi, mid)

    def halve(i, st):
        lo, hi = st
        mid = midpoint(lo, hi)
        enough = count(lambda sc, off: jnp.where(sc >= mid, 1.0, 0.0)) >= n_sel
        return jnp.where(enough, mid, lo), jnp.where(enough, hi, mid)

    lo0, hi0 = lax.fori_loop(0, N_HALVINGS, halve, (lo0, hi0))

    def open_rows(lo, hi):
        return jnp.max(jnp.where(lo < hi, 1.0, 0.0)) > 0.0

    def bisect_cond(st):
        it, lo, hi = st
        return open_rows(lo, hi) & (it < seq)

    def bisect_body(st):
        it, lo, hi = st
        mid = midpoint(lo, hi)

        def body(c, acc):
            cnt, up, dn = acc
            off = pl.multiple_of(c * SC, SC)
            sc = key_sc[pl.ds(off, SC), :]
            ge = sc >= mid
            cnt = cnt + _fold_rows(jnp.where(ge, 1.0, 0.0), jnp.sum)
            up = jnp.minimum(up, _fold_rows(jnp.where(ge, sc, inf), jnp.min))
            dn = jnp.maximum(dn, _fold_rows(jnp.where(ge, -inf, sc), jnp.max))
            return cnt, up, dn

        cnt, up, dn = lax.fori_loop(
            0, n_sel_chunk, body, (jnp.zeros(part, F32), jnp.full(part, inf, F32), jnp.full(part, -inf, F32)))
        enough = jnp.sum(cnt, axis=0, keepdims=True) >= n_sel
        is_open = lo < hi
        new_lo = jnp.where(is_open & enough, jnp.min(up, axis=0, keepdims=True), lo)
        new_hi = jnp.where(is_open & jnp.logical_not(enough), jnp.max(dn, axis=0, keepdims=True), hi)
        return it + 1, new_lo, new_hi

    _, thr, _ = lax.while_loop(bisect_cond, bisect_body, (jnp.int32(0), lo0, hi0))
    n_gt = count(lambda kk, off: jnp.where(kk > thr, 1.0, 0.0))
    n_eq = count(lambda kk, off: jnp.where(kk == thr, 1.0, 0.0))
    need = n_sel - n_gt

    def tie_search():
        def tbody(i, p):
            cand = p | jnp.left_shift(jnp.int32(1), (seq.bit_length() - 2) - i)
            cnt = count(lambda kk, off: jnp.where(kk == thr, jnp.where(off + s_iota < cand, 1.0, 0.0), 0.0))
            return jnp.where(cnt < need, cand, p)
        return lax.fori_loop(0, seq.bit_length() - 1, tbody, jnp.zeros((1, QB), I32))

    ambiguous = jnp.max(jnp.where(has_thr & (n_eq > need), 1.0, 0.0)) > 0.0
    tie_hi = lax.cond(ambiguous, tie_search, lambda: jnp.full((1, QB), seq, I32))
    tie_hi = jnp.where(has_thr, tie_hi, -1)

    def bias_chunk(c, carry):
        off = pl.multiple_of(c * LC, LC)
        kk = key_sc[pl.ds(off, LC), :]
        tie_ok = jnp.where(off + k_iota <= tie_hi, 0.0, NEG)
        bias = jnp.where(kk > thr, 0.0, jnp.where(kk == thr, tie_ok, NEG))
        bias_sc[pl.ds(off, LC), :] = bias.astype(BF16)
        return carry

    lax.fori_loop(0, nchunk, bias_chunk, 0)

    m_sc[...] = jnp.full(m_sc.shape, NEG, F32)
    l_sc[...] = jnp.zeros(l_sc.shape, F32)
    acc_sc[...] = jnp.zeros(acc_sc.shape, F32)

    bufs = ((s_sc, cmax_sc), (s2_sc, cmax2_sc))

    def logits(c, g, buf):
        off = pl.multiple_of(c * LC, LC)
        kc = k_ref[pl.ds(off, LC), g * HEAD_DIM:(g + 1) * HEAD_DIM]
        s = _dot(jnp.concatenate([kc, bias_sc[pl.ds(off, LC), :]], axis=1), qt_sc[g])
        buf[0][g] = s
        buf[1][g] = jnp.max(_fold_rows(s, jnp.max), axis=0, keepdims=True)

    def accumulate(c, g, buf):
        off = pl.multiple_of(c * LC, LC)
        vc = v_ref[pl.ds(off, LC), g * HEAD_DIM:(g + 1) * HEAD_DIM]
        m_old = m_sc[g]
        m_new = jnp.maximum(m_old, buf[1][g])
        alpha = jnp.exp2(m_old - m_new)
        p = jnp.exp2(buf[0][g] - m_new)
        l_sc[g] = alpha * l_sc[g] + jnp.sum(_fold_rows(p, jnp.sum), axis=0, keepdims=True)
        acc_sc[g] = alpha * acc_sc[g] + _dot_tn(vc, p.astype(BF16))
        m_sc[g] = m_new

    def step(c, cur, nxt):
        for g in range(N_KV_HEADS):
            accumulate(c, g, cur)
            logits(c + 1, g, nxt)

    def last(c, cur):
        for g in range(N_KV_HEADS):
            accumulate(c, g, cur)

    for g in range(N_KV_HEADS):
        logits(0, g, bufs[0])
    n_pairs = (nchunk - 1) // 2

    def attn_pair(p, carry):
        step(2 * p, bufs[0], bufs[1])
        step(2 * p + 1, bufs[1], bufs[0])
        return carry

    lax.fori_loop(0, n_pairs, attn_pair, 0)
    c_tail = 2 * n_pairs

    @pl.when(nchunk - c_tail == 2)
    def _():
        step(c_tail, bufs[0], bufs[1])
        last(c_tail + 1, bufs[1])

    @pl.when(nchunk - c_tail == 1)
    def _():
        last(c_tail, bufs[0])

    for g in range(N_KV_HEADS):
        o_t = acc_sc[g] / l_sc[g]
        for r in range(Q_PER_KV):
            h = g * Q_PER_KV + r
            o_ref[:, h * HEAD_DIM:(h + 1) * HEAD_DIM] = o_t[:, r * QB:(r + 1) * QB].T.astype(o_ref.dtype)


def _dsa_attention(h2, k2, kw, *, batch, seq, bn):
    nb = seq // QB
    n_sel = min(TOPK_MAX, seq // 4)
    q_cols = N_Q_HEADS * HEAD_DIM
    kv_cols = N_KV_HEADS * HEAD_DIM
    assert bn == kv_cols and q_cols == 4 * bn and seq % LC == 0 and LC >= n_sel
    kern = functools.partial(_dsa_kernel, n_sel=float(n_sel), seq=seq)
    rows = Q_PER_KV * QB
    return pl.pallas_call(
        kern, grid=(batch, nb),
        in_specs=[pl.BlockSpec((QB, q_cols), lambda b, i: (b * nb + i, 0)),
                  pl.BlockSpec((seq, kv_cols), lambda b, i: (b, 4)),
                  pl.BlockSpec((seq, kv_cols), lambda b, i: (b, 5)),
                  pl.BlockSpec((seq, LANES), lambda b, i: (b, 0)),
                  pl.BlockSpec((QB, bn), lambda b, i: (b * nb + i, 6)),
                  pl.BlockSpec((QB, LANES), lambda b, i: (b * nb + i, 0))],
        out_specs=pl.BlockSpec((QB, q_cols), lambda b, i: (b * nb + i, 0)),
        out_shape=jax.ShapeDtypeStruct((batch * seq, q_cols), BF16),
        scratch_shapes=[pltpu.VMEM((seq, QB), F32),
                        pltpu.VMEM((seq, QB), BF16),
                        pltpu.VMEM((N_IDX_HEADS // 2, LANES, 2 * QB), BF16),
                        pltpu.VMEM((N_KV_HEADS, HEAD_DIM + QB, rows), BF16),
                        pltpu.VMEM((N_KV_HEADS, 1, rows), F32),
                        pltpu.VMEM((N_KV_HEADS, 1, rows), F32),
                        pltpu.VMEM((N_KV_HEADS, HEAD_DIM, rows), F32),
                        pltpu.VMEM((N_KV_HEADS, LC, rows), F32), pltpu.VMEM((N_KV_HEADS, 1, rows), F32),
                        pltpu.VMEM((N_KV_HEADS, LC, rows), F32), pltpu.VMEM((N_KV_HEADS, 1, rows), F32)],
        compiler_params=_cparams(2), name="dsa_attention",
    )(h2, h2, h2, k2, h2, kw)


def _ln_rows(x, g, b):
    mu = jnp.mean(x, axis=-1, keepdims=True)
    xc = x - mu
    var = jnp.mean(xc * xc, axis=-1, keepdims=True)
    return xc * lax.rsqrt(var + LN_EPS) * g + b


def _pack_halves(y):
    half = y.shape[1] // 2
    return pltpu.pack_elementwise([y[:, :half], y[:, half:]], packed_dtype=BF16)


def _unpack_halves(p):
    lo = pltpu.unpack_elementwise(p, index=0, packed_dtype=BF16, unpacked_dtype=F32)
    hi = pltpu.unpack_elementwise(p, index=1, packed_dtype=BF16, unpacked_dtype=F32)
    return jnp.concatenate([lo, hi], axis=1)


def _ln_kernel(x_ref, g_ref, b_ref, o_ref, o2_ref, *, packed):
    y = _ln_rows(x_ref[...], g_ref[...], b_ref[...])
    o_ref[...] = y
    o2_ref[...] = _pack_halves(y) if packed else y.astype(o2_ref.dtype)


def _layer_norm(x, g, b, *, bm, packed=False):
    m, d = x.shape
    row = pl.BlockSpec((bm, d), lambda i: (i, 0))
    vec = pl.BlockSpec((1, d), lambda i: (0, 0))
    second = ((m, d // 2), U32) if packed else ((m, d), BF16)
    return pl.pallas_call(
        functools.partial(_ln_kernel, packed=packed), grid=(m // bm,), in_specs=[row, vec, vec],
        out_specs=[row, pl.BlockSpec((bm, second[0][1]), lambda i: (i, 0))],
        out_shape=[jax.ShapeDtypeStruct((m, d), F32), jax.ShapeDtypeStruct(*second)],
        compiler_params=_cparams(1), name="layer_norm",
    )(x, g.reshape(1, d), b.reshape(1, d))


def _xattn_kernel(q_ref, k_ref, v_ref, o_ref):
    dh = q_ref.shape[1] // N_MEM_HEADS
    for h in range(N_MEM_HEADS):
        cols = slice(h * dh, (h + 1) * dh)
        s = _dot_nt(q_ref[:, cols], k_ref[:, cols])
        p = jnp.exp(s - jnp.max(s, axis=1, keepdims=True))
        o = _dot(p.astype(BF16), v_ref[:, cols]) / jnp.sum(p, axis=1, keepdims=True)
        o_ref[:, cols] = o.astype(o_ref.dtype)


def _cross_attention(q, k, v, *, batch, seq, n_mem, bm):
    d = q.shape[1]
    nb = seq // bm
    return pl.pallas_call(
        _xattn_kernel, grid=(batch, nb),
        in_specs=[pl.BlockSpec((bm, d), lambda b, i: (b * nb + i, 0)),
                  pl.BlockSpec((n_mem, d), lambda b, i: (b, 0)),
                  pl.BlockSpec((n_mem, d), lambda b, i: (b, 0))],
        out_specs=pl.BlockSpec((bm, d), lambda b, i: (b * nb + i, 0)),
        out_shape=jax.ShapeDtypeStruct((batch * seq, d), BF16),
        compiler_params=_cparams(2), name="cross_attention",
    )(q, k, v)


def _router_kernel(x_ref, w_ref, b_ref, id_ref, gate_ref):
    logits = jnp.dot(x_ref[...], w_ref[...], preferred_element_type=F32,
                     precision=lax.Precision.HIGHEST) + b_ref[...]
    lane = lax.broadcasted_iota(I32, logits.shape, 1)
    lane_f = lane.astype(F32)
    big = float(LANES)
    is_g = lane < N_GROUPS
    gl = jnp.where(is_g, logits, -jnp.inf)
    g_max = jnp.max(gl, axis=1, keepdims=True)
    g_idx = jnp.min(jnp.where(gl == g_max, lane_f, big), axis=1, keepdims=True)
    g_gate = 1.0 / jnp.sum(jnp.where(is_g, jnp.exp(gl - g_max), 0.0), axis=1, keepdims=True)
    e_lo = N_GROUPS + g_idx * EXPERTS_PER_GROUP
    in_grp = (lane_f >= e_lo) & (lane_f < e_lo + EXPERTS_PER_GROUP)
    el = jnp.where(in_grp, logits, -jnp.inf)
    v1 = jnp.max(el, axis=1, keepdims=True)
    i1 = jnp.min(jnp.where(el == v1, lane_f, big), axis=1, keepdims=True)
    el2 = jnp.where(lane_f == i1, -jnp.inf, el)
    v2 = jnp.max(el2, axis=1, keepdims=True)
    i2 = jnp.min(jnp.where(el2 == v2, lane_f, big), axis=1, keepdims=True)
    z = jnp.exp(v2 - v1)
    w1 = g_gate / (1.0 + z)
    w2 = g_gate * z / (1.0 + z)
    ids = jnp.where(lane == 0, i1 - N_GROUPS, jnp.where(lane == 1, i2 - N_GROUPS, 0.0))
    id_ref[...] = ids.astype(I32)
    gate_ref[...] = jnp.where(lane == 0, w1, jnp.where(lane == 1, w2, 0.0))


def _router(x, w_r, b_r, *, bm):
    m, d = x.shape
    out = pl.BlockSpec((bm, LANES), lambda i: (i, 0))
    return pl.pallas_call(
        _router_kernel, grid=(m // bm,),
        in_specs=[pl.BlockSpec((bm, d), lambda i: (i, 0)),
                  pl.BlockSpec((d, LANES), lambda i: (0, 0)),
                  pl.BlockSpec((1, LANES), lambda i: (0, 0))],
        out_specs=[out, out],
        out_shape=[jax.ShapeDtypeStruct((m, LANES), I32), jax.ShapeDtypeStruct((m, LANES), F32)],
        compiler_params=_cparams(1), name="router",
    )(x, w_r, b_r)


def _rank_kernel(id_ref, rank_ref, cnt_ref, carry_ref):
    i = pl.program_id(0)
    bt = id_ref.shape[0]

    @pl.when(i == 0)
    def _():
        carry_ref[...] = jnp.zeros(carry_ref.shape, F32)

    ids = id_ref[...]
    lane = lax.broadcasted_iota(I32, (bt, LANES), 1)
    e0 = ids[:, 0:1]
    e1 = ids[:, 1:2] + N_EXPERTS
    hit0 = lane == e0
    hit1 = lane == e1
    onehot = jnp.where(hit0, 1.0, jnp.where(hit1, 1.0, 0.0))
    r_io = lax.broadcasted_iota(I32, (bt, bt), 0)
    c_io = lax.broadcasted_iota(I32, (bt, bt), 1)
    tri = jnp.where(c_io < r_io, 1.0, 0.0).astype(BF16)
    prefix = _dot(tri, onehot.astype(BF16)) + carry_ref[0:1, :]
    rank0 = jnp.sum(jnp.where(hit0, prefix, 0.0), axis=1, keepdims=True)
    rank1 = jnp.sum(jnp.where(hit1, prefix, 0.0), axis=1, keepdims=True)
    rank_ref[...] = jnp.where(lane == 0, rank0, jnp.where(lane == 1, rank1, 0.0))
    total = carry_ref[0:1, :] + jnp.sum(onehot, axis=0, keepdims=True)
    carry_ref[...] = jnp.broadcast_to(total, carry_ref.shape)
    cnt_ref[...] = jnp.broadcast_to(total, cnt_ref.shape)


def _slot_ranks(ids, *, bt):
    m = ids.shape[0]
    return pl.pallas_call(
        _rank_kernel, grid=(m // bt,),
        in_specs=[pl.BlockSpec((bt, LANES), lambda i: (i, 0))],
        out_specs=[pl.BlockSpec((bt, LANES), lambda i: (i, 0)),
                   pl.BlockSpec((8, LANES), lambda i: (0, 0))],
        out_shape=[jax.ShapeDtypeStruct((m, LANES), F32), jax.ShapeDtypeStruct((8, LANES), F32)],
        scratch_shapes=[pltpu.VMEM((8, LANES), F32)],
        compiler_params=_cparams(1), name="slot_ranks",
    )(ids)


def _pos_kernel(id_ref, rank_ref, tab_ref, pos_ref):
    ids = id_ref[...]
    ranks = rank_ref[...]
    lane = lax.broadcasted_iota(I32, ids.shape, 1)
    tab = tab_ref[...]
    base0 = jnp.sum(jnp.where(lane == ids[:, 0:1], tab, 0.0), axis=1, keepdims=True)
    base1 = jnp.sum(jnp.where(lane == ids[:, 1:2] + N_EXPERTS, tab, 0.0), axis=1, keepdims=True)
    pos = jnp.where(lane == 0, base0 + ranks[:, 0:1], jnp.where(lane == 1, base1 + ranks[:, 1:2], 0.0))
    pos_ref[...] = pos.astype(I32)


def _slot_positions(ids, ranks, table, *, bt):
    m = ids.shape[0]
    blk = pl.BlockSpec((bt, LANES), lambda i: (i, 0))
    return pl.pallas_call(
        _pos_kernel, grid=(m // bt,),
        in_specs=[blk, blk, pl.BlockSpec((1, LANES), lambda i: (0, 0))],
        out_specs=blk,
        out_shape=jax.ShapeDtypeStruct((m, LANES), I32),
        compiler_params=_cparams(1), name="slot_positions",
    )(ids, ranks, table)


def _row_copy(src_ref, src_row, dst_ref, dst_row, sem):
    return pltpu.make_async_copy(src_ref.at[pl.ds(src_row, 1)], dst_ref.at[pl.ds(dst_row, 1)], sem)


def _dispatch_kernel(pos0_ref, pos1_ref, x_ref, xs_in_ref, xs_ref, sem):
    del xs_in_ref
    i = pl.program_id(0)
    bt = x_ref.shape[0]

    def issue(r, carry):
        t = i * bt + r
        _row_copy(x_ref, r, xs_ref, pos0_ref[t], sem.at[0]).start()
        _row_copy(x_ref, r, xs_ref, pos1_ref[t], sem.at[1]).start()
        return carry

    lax.fori_loop(0, bt, issue, 0)
    for k in range(2):
        pltpu.make_async_copy(x_ref, xs_ref.at[pl.ds(0, bt)], sem.at[k]).wait()


def _dispatch(x, pos0, pos1, n_slots, *, bt):
    m, d = x.shape
    xs0 = jnp.zeros((n_slots, d), x.dtype)
    return pl.pallas_call(
        _dispatch_kernel,
        grid_spec=pltpu.PrefetchScalarGridSpec(
            num_scalar_prefetch=2, grid=(m // bt,),
            in_specs=[pl.BlockSpec((bt, d), lambda i, p0, p1: (i, 0)),
                      pl.BlockSpec(memory_space=pl.ANY)],
            out_specs=pl.BlockSpec(memory_space=pl.ANY),
            scratch_shapes=[pltpu.SemaphoreType.DMA((2,))]),
        out_shape=jax.ShapeDtypeStruct((n_slots, d), x.dtype),
        input_output_aliases={3: 0},
        compiler_params=_cparams(1), name="dispatch",
    )(pos0, pos1, x, xs0)


def _ffn_kernel(blk_e_ref, nxt_e_ref, set_ref, nused_ref, x_ref, wg_hbm, wu_hbm, wd_hbm, y_ref,
                wg_bf, wu_bf, wd_bf, st_g, st_u, st_d, sem):
    i = pl.program_id(0)
    d, ff = wg_bf.shape[1], wg_bf.shape[2]
    rg, rd = d // W_GROUPS, ff // W_GROUPS

    n_slot = st_g.shape[0]

    def group_copies(e, g):
        slot = g % n_slot
        return (pltpu.make_async_copy(wg_hbm.at[e, pl.ds(g * rg, rg), :], st_g.at[slot], sem.at[0, slot]),
                pltpu.make_async_copy(wu_hbm.at[e, pl.ds(g * rg, rg), :], st_u.at[slot], sem.at[1, slot]),
                pltpu.make_async_copy(wd_hbm.at[e, pl.ds(g * rd, rd), :], st_d.at[slot], sem.at[2, slot]))

    def start(e, g):
        for c in group_copies(e, g):
            c.start()

    def finish(e, g, s):
        for c in group_copies(e, g):
            c.wait()
        slot = g % n_slot
        wg_bf[s, g * rg:(g + 1) * rg, :] = st_g[slot].astype(BF16)
        wu_bf[s, g * rg:(g + 1) * rg, :] = st_u[slot].astype(BF16)
        wd_bf[s, g * rd:(g + 1) * rd, :] = st_d[slot].astype(BF16)

    def load_expert(e, s, between=()):
        for g in range(n_slot):
            start(e, g)
        for g in range(W_GROUPS):
            if g < len(between):
                between[g]()
            finish(e, g, s)
            if g + n_slot < W_GROUPS:
                start(e, g + n_slot)

    e = blk_e_ref[i]
    s = set_ref[i]
    nxt = nxt_e_ref[i]
    first_of_expert = jnp.logical_or(i == 0, blk_e_ref[jnp.maximum(i - 1, 0)] != e)
    active = i < nused_ref[0]
    prefetch = jnp.logical_and(jnp.logical_and(active, first_of_expert), nxt >= 0)

    @pl.when(i == 0)
    def _():
        load_expert(e, s)

    def ffn_steps():
        vals = {}

        def gate_step():
            vals["x"] = _unpack_halves(x_ref[...]).astype(BF16)
            vals["gate"] = _dot(vals["x"], wg_bf[s])

        def up_step():
            gate = vals["gate"]
            vals["h"] = (gate * (1.0 / (1.0 + jnp.exp(-gate))) * _dot(vals["x"], wu_bf[s])).astype(BF16)

        def down_step():
            y_ref[...] = _pack_halves(_dot(vals["h"], wd_bf[s]))

        return gate_step, up_step, down_step

    @pl.when(prefetch)
    def _():
        load_expert(nxt, 1 - s, between=ffn_steps())

    @pl.when(jnp.logical_and(active, jnp.logical_not(prefetch)))
    def _():
        for step in ffn_steps():
            step()


def _expert_ffn(xs, blk_e, nxt_e, set_idx, nused, wg, wu, wd):
    n_slots, dp = xs.shape
    d, ff = wg.shape[1], wg.shape[2]
    nblk = n_slots // MOE_BLK
    assert dp * 2 == d and d % W_GROUPS == 0 and ff % W_GROUPS == 0 and W_GROUPS >= W_STAGE_SLOTS > 3

    def row_map(i, be, nx, si, nu):
        return (jnp.minimum(i, nu[0] - 1), 0)

    hbm = pl.BlockSpec(memory_space=pl.ANY)
    return pl.pallas_call(
        _ffn_kernel,
        grid_spec=pltpu.PrefetchScalarGridSpec(
            num_scalar_prefetch=4, grid=(nblk,),
            in_specs=[pl.BlockSpec((MOE_BLK, dp), row_map), hbm, hbm, hbm],
            out_specs=pl.BlockSpec((MOE_BLK, dp), row_map),
            scratch_shapes=[pltpu.VMEM((2, d, ff), BF16), pltpu.VMEM((2, d, ff), BF16), pltpu.VMEM((2, ff, d), BF16),
                            pltpu.VMEM((W_STAGE_SLOTS, d // W_GROUPS, ff), F32),
                            pltpu.VMEM((W_STAGE_SLOTS, d // W_GROUPS, ff), F32),
                            pltpu.VMEM((W_STAGE_SLOTS, ff // W_GROUPS, d), F32),
                            pltpu.SemaphoreType.DMA((3, W_STAGE_SLOTS))]),
        out_shape=jax.ShapeDtypeStruct((n_slots, dp), xs.dtype),
        input_output_aliases={4: 0},
        compiler_params=_cparams(1), name="expert_ffn",
    )(blk_e, nxt_e, set_idx, nused, xs, wg, wu, wd)


def _combine_kernel(pos0_ref, pos1_ref, x_ref, gate_ref, g_ref, b_ref, ys_ref, o_ref, y0_ref, y1_ref, sem):
    i = pl.program_id(0)
    bt = x_ref.shape[0]
    slot = i % 2

    def issue(step, to_slot):
        def body(r, carry):
            t = step * bt + r
            _row_copy(ys_ref, pos0_ref[t], y0_ref.at[to_slot], r, sem.at[0, to_slot]).start()
            _row_copy(ys_ref, pos1_ref[t], y1_ref.at[to_slot], r, sem.at[1, to_slot]).start()
            return carry
        lax.fori_loop(0, bt, body, 0)

    @pl.when(i == 0)
    def _():
        issue(0, 0)

    @pl.when(i + 1 < pl.num_programs(0))
    def _():
        issue(i + 1, 1 - slot)

    pltpu.make_async_copy(ys_ref.at[pl.ds(0, bt)], y0_ref.at[slot], sem.at[0, slot]).wait()
    pltpu.make_async_copy(ys_ref.at[pl.ds(0, bt)], y1_ref.at[slot], sem.at[1, slot]).wait()
    gates = gate_ref[...]
    f = _unpack_halves(y0_ref[slot]) * gates[:, 0:1] + _unpack_halves(y1_ref[slot]) * gates[:, 1:2]
    o_ref[...] = _ln_rows(ALPHA * x_ref[...] + f, g_ref[...], b_ref[...])


def _combine(x, gates, ys, pos0, pos1, g, b, *, bt):
    m, d = x.shape
    dp = ys.shape[1]
    row = pl.BlockSpec((bt, d), lambda i, p0, p1: (i, 0))
    vec = pl.BlockSpec((1, d), lambda i, p0, p1: (0, 0))
    return pl.pallas_call(
        _combine_kernel,
        grid_spec=pltpu.PrefetchScalarGridSpec(
            num_scalar_prefetch=2, grid=(m // bt,),
            in_specs=[row, pl.BlockSpec((bt, LANES), lambda i, p0, p1: (i, 0)), vec, vec,
                      pl.BlockSpec(memory_space=pl.ANY)],
            out_specs=row,
            scratch_shapes=[pltpu.VMEM((2, bt, dp), ys.dtype), pltpu.VMEM((2, bt, dp), ys.dtype),
                            pltpu.SemaphoreType.DMA((2, 2))]),
        out_shape=jax.ShapeDtypeStruct((m, d), F32),
        compiler_params=_cparams(1), name="combine_ln",
    )(pos0, pos1, x, gates, g.reshape(1, d), b.reshape(1, d), ys)


def _rope_tables(positions, n_rows):
    pos = positions.astype(F32).reshape(n_rows, 1)

    def cs(dim):
        inv = 1.0 / (ROPE_THETA ** (jnp.arange(0, dim, 2, dtype=F32) / dim))
        ang = pos * inv[None, :]
        return jnp.cos(ang), jnp.sin(ang)

    c_h, s_h = cs(HEAD_DIM)
    c_i, s_i = cs(IDX_DIM)
    z_i = jnp.zeros_like(s_i)
    cosf = jnp.concatenate([c_h, c_h], axis=1)
    sinf = jnp.concatenate([-s_h, s_h], axis=1)
    cos64 = jnp.concatenate([c_i, c_i, c_i, c_i], axis=1)
    sin_lo = jnp.concatenate([-s_i, z_i, -s_i, z_i], axis=1)
    sin_hi = jnp.concatenate([z_i, s_i, z_i, s_i], axis=1)
    return cosf, sinf, cos64, sin_lo, sin_hi


def kernel(x, mem, positions, w_in, pool_w, pool_scale, w_o, ln1_g, ln1_b, w_mq, w_mk, w_mv, w_mo, ln2_g, ln2_b,
           w_group_router, b_group_router, w_expert_router, b_expert_router, w_gate, w_up, w_down, ln3_g, ln3_b):
    batch, seq, d = x.shape
    n_mem = mem.shape[1]
    n = batch * seq
    bn_attn = N_KV_HEADS * HEAD_DIM
    tables = _rope_tables(positions, n)
    xf = x.reshape(n, d)
    n_slots = 2 * n + N_EXPERTS * MOE_BLK
    nblk = n_slots // MOE_BLK

    for l in range(w_in.shape[0]):
        x_bf = xf.astype(BF16)
        kw_col = MIX_POOL + 7 * bn_attn
        w_kw = jnp.pad(w_in[l][:, kw_col:], ((0, 0), (0, LANES - (w_in.shape[2] - kw_col))))

        v_pool = _matmul([x_bf], [w_in], layer=l, name="inproj_pool", bm=1024, bn=DENSE_BN, n=MIX_POOL)
        h2 = _inproj_attn(x_bf, w_in, tables, layer=l, col_off_blocks=MIX_POOL // bn_attn, bm=1024, bn=bn_attn)
        k2, kw = _inproj_idx(x_bf, w_kw, tables, bm=1024)
        a_pool = _pool_mixer(v_pool, pool_w[l].astype(BF16), pool_scale[l], batch=batch, seq=seq, ts=512)
        a_attn = _dsa_attention(h2, k2, kw, batch=batch, seq=seq, bn=bn_attn)
        res = (xf, (1024, DENSE_BN), lambda i, j: (i, j))
        pre = _matmul([a_pool, a_attn], [w_o, w_o], layer=l, name="outproj", bm=1024, bn=DENSE_BN, n=d,
                      extras=(res,), epilogue=_residual_epilogue)
        x1, x1_bf = _layer_norm(pre, ln1_g[l], ln1_b[l], bm=256)

        mq_scale = (d // N_MEM_HEADS) ** -0.5

        def q_epilogue(acc, ex, outs):
            outs[0][...] = (acc * mq_scale).astype(outs[0].dtype)

        qm = _matmul([x1_bf], [w_mq], layer=l, name="mem_q", bm=1024, bn=DENSE_BN, n=d, out_dtype=BF16,
                     epilogue=q_epilogue)
        mem_bf = mem.reshape(batch * n_mem, d).astype(BF16)
        km = _matmul([mem_bf], [w_mk], layer=l, name="mem_k", bm=batch * n_mem, bn=DENSE_BN, n=d, out_dtype=BF16)
        vm = _matmul([mem_bf], [w_mv], layer=l, name="mem_v", bm=batch * n_mem, bn=DENSE_BN, n=d, out_dtype=BF16)
        om = _cross_attention(qm, km, vm, batch=batch, seq=seq, n_mem=n_mem, bm=512)
        res = (x1, (1024, DENSE_BN), lambda i, j: (i, j))
        pre = _matmul([om], [w_mo], layer=l, name="mem_o", bm=1024, bn=DENSE_BN, n=d, extras=(res,),
                      epilogue=_residual_epilogue)
        x2, x2_rows = _layer_norm(pre, ln2_g[l], ln2_b[l], bm=256, packed=True)

        w_r = jnp.pad(jnp.concatenate([w_group_router[l], w_expert_router[l]], axis=1),
                      ((0, 0), (0, LANES - N_GROUPS - N_EXPERTS)))
        b_r = jnp.pad(jnp.concatenate([b_group_router[l], b_expert_router[l]]),
                      (0, LANES - N_GROUPS - N_EXPERTS)).reshape(1, LANES)
        ids, gates = _router(x2, w_r, b_r, bm=512)
        ranks, totals = _slot_ranks(ids, bt=512)
        c0 = totals[0, :N_EXPERTS].astype(I32)
        c1 = totals[0, N_EXPERTS:].astype(I32)
        padded = ((c0 + c1 + MOE_BLK - 1) // MOE_BLK) * MOE_BLK
        pend = jnp.cumsum(padded)
        pstart = pend - padded
        table = jnp.concatenate([pstart, pstart + c0]).astype(F32).reshape(1, LANES)
        pos = _slot_positions(ids, ranks, table, bt=512)
        pos0, pos1 = pos[:, 0], pos[:, 1]
        nused = pend[-1] // MOE_BLK
        blk_i = jnp.minimum(jnp.arange(nblk, dtype=I32), nused - 1)
        blk_e = jnp.sum((pend[None, :] <= (blk_i * MOE_BLK)[:, None]).astype(I32), axis=1)
        blk_e = jnp.minimum(blk_e, N_EXPERTS - 1)
        later = blk_e[None, :] > blk_e[:, None]
        nxt_e = jnp.min(jnp.where(later, blk_e[None, :], N_EXPERTS), axis=1)
        nxt_e = jnp.where(nxt_e == N_EXPERTS, -1, nxt_e).astype(I32)
        new_e = jnp.concatenate([jnp.ones((1,), I32), (blk_e[1:] != blk_e[:-1]).astype(I32)])
        set_idx = (jnp.cumsum(new_e) - 1) % 2

        xs = _dispatch(x2_rows, pos0, pos1, n_slots, bt=256)
        ys = _expert_ffn(xs, blk_e, nxt_e, set_idx.astype(I32), nused.reshape(1).astype(I32),
                         w_gate[l], w_up[l], w_down[l])
        xf = _combine(x2, gates, ys, pos0, pos1, ln3_g[l], ln3_b[l], bt=256)
    return xf.reshape(batch, seq, d)
```

```python
import functools

import jax
import jax.numpy as jnp
from jax import lax
from jax.experimental import pallas as pl
from jax.experimental.pallas import tpu as pltpu

F32 = jnp.float32
BF16 = jnp.bfloat16
I32 = jnp.int32
U32 = jnp.uint32

MIX_POOL = 2048
N_POOL_GROUPS = 4
POOL_WINDOWS = (2, 4, 8, 16)
POOL_GW = MIX_POOL // N_POOL_GROUPS
HEAD_DIM = 128
N_Q_HEADS = 16
N_KV_HEADS = 4
Q_PER_KV = N_Q_HEADS // N_KV_HEADS
N_IDX_HEADS = 8
IDX_DIM = 64
TOPK_MAX = 256
ROPE_THETA = 10000.0
N_MEM_HEADS = 4
N_GROUPS = 8
EXPERTS_PER_GROUP = 8
N_EXPERTS = N_GROUPS * EXPERTS_PER_GROUP
LN_EPS = 1e-5
DEPTH = 1
ALPHA = (2.0 * DEPTH) ** 0.25

LANES = 128
QB = 128
LC = 512
SC = 256
DENSE_BN = 512
MOE_BLK = 256
W_GROUPS = 8
W_STAGE_SLOTS = 4
POOL_HALO = 16
NEG = -1e30
N_HALVINGS = 12
LOG2E = 1.4426950408889634
VMEM_LIMIT = 56 * 1024 * 1024


def _cparams(n_axes, vmem=VMEM_LIMIT):
    return pltpu.CompilerParams(dimension_semantics=("arbitrary",) * n_axes, vmem_limit_bytes=vmem)


def _dot(a, b):
    return jnp.dot(a, b, preferred_element_type=F32)


def _dot_nt(a, b):
    return lax.dot_general(a, b, (((1,), (1,)), ((), ())), preferred_element_type=F32)


def _dot_tn(a, b):
    return lax.dot_general(a, b, (((0,), (0,)), ((), ())), preferred_element_type=F32)


FOLD_ROWS = 32


def _fold_rows(x, reduce_fn):
    r, c = x.shape
    return reduce_fn(x.reshape(r // FOLD_ROWS, FOLD_ROWS, c), axis=0)


def _mm_kernel(*refs, n_lhs, n_extra, epilogue):
    lhs = refs[:n_lhs]
    ws = refs[n_lhs:2 * n_lhs]
    extras = refs[2 * n_lhs:2 * n_lhs + n_extra]
    outs = refs[2 * n_lhs + n_extra:-n_lhs]
    w_bf = refs[-n_lhs:]

    @pl.when(pl.program_id(1) == 0)
    def _():
        for l in range(n_lhs):
            w_bf[l][...] = ws[l][...].astype(BF16)

    acc = _dot(lhs[0][...], w_bf[0][...])
    for l in range(1, n_lhs):
        acc = acc + _dot(lhs[l][...], w_bf[l][...])
    epilogue(acc, extras, outs)


def _matmul(lhs_list, w_list, *, layer, name, bm, bn, n, w_col_off=0, extras=(), out_dtype=F32, epilogue=None):
    m = lhs_list[0].shape[0]
    grid = (n // bn, m // bm)
    in_specs = [pl.BlockSpec((bm, a.shape[1]), lambda j, i: (i, 0)) for a in lhs_list]
    in_specs += [pl.BlockSpec((None, a.shape[1], bn), (lambda j, i, l=l: (layer, l, j + w_col_off)))
                 for l, a in enumerate(lhs_list)]
    in_specs += [pl.BlockSpec(blk, (lambda j, i, f=f: f(i, j))) for (_, blk, f) in extras]
    if epilogue is None:
        def epilogue(acc, ex, outs):
            outs[0][...] = acc.astype(outs[0].dtype)
    kern = functools.partial(_mm_kernel, n_lhs=len(lhs_list), n_extra=len(extras), epilogue=epilogue)
    return pl.pallas_call(
        kern, grid=grid, in_specs=in_specs,
        out_specs=pl.BlockSpec((bm, bn), lambda j, i: (i, j)),
        out_shape=jax.ShapeDtypeStruct((m, n), out_dtype),
        scratch_shapes=[pltpu.VMEM((a.shape[1], bn), BF16) for a in lhs_list],
        compiler_params=_cparams(2), name=name,
    )(*lhs_list, *w_list, *[e[0] for e in extras])


def _residual_epilogue(acc, ex, outs):
    outs[0][...] = acc + ALPHA * ex[0][...]


def _rope128(a, cosf, sinf):
    return a * cosf + pltpu.roll(a, 64, 1) * sinf


def _rope64(a, cos64, sin_lo, sin_hi):
    return a * cos64 + pltpu.roll(a, 96, 1) * sin_lo + pltpu.roll(a, 32, 1) * sin_hi


def _inproj_kernel(x_ref, w_ref, t0_ref, t1_ref, t2_ref, o_ref, w_bf, *, rope, scale):
    @pl.when(pl.program_id(1) == 0)
    def _():
        w_bf[...] = w_ref[...].astype(BF16)

    bm, bn = o_ref.shape
    half = bm // 2
    for h in range(2):
        rows = pl.ds(h * half, half)
        acc = _dot(x_ref[rows, :], w_bf[...])
        if rope == HEAD_DIM:
            t0, t1 = t0_ref[rows, :] * scale, t1_ref[rows, :] * scale
        elif rope == IDX_DIM:
            t0, t1, t2 = t0_ref[rows, :], t1_ref[rows, :], t2_ref[rows, :]
        for c in range(bn // LANES):
            a = acc[:, c * LANES:(c + 1) * LANES]
            if rope == HEAD_DIM:
                a = _rope128(a, t0, t1)
            elif rope == IDX_DIM:
                a = _rope64(a, t0, t1, t2)
            o_ref[rows, c * LANES:(c + 1) * LANES] = a.astype(o_ref.dtype)


def _inproj_rope(x_bf, w, tabs, *, layer, name, col_off_blocks, n, rope, scale=1.0, bm, bn):
    m, d = x_bf.shape
    tab_spec = pl.BlockSpec((bm, LANES), lambda j, i: (i, 0))
    return pl.pallas_call(
        functools.partial(_inproj_kernel, rope=rope, scale=scale), grid=(n // bn, m // bm),
        in_specs=[pl.BlockSpec((bm, d), lambda j, i: (i, 0)),
                  pl.BlockSpec((None, d, bn), lambda j, i: (layer, 0, j + col_off_blocks))] + [tab_spec] * 3,
        out_specs=pl.BlockSpec((bm, bn), lambda j, i: (i, j)),
        out_shape=jax.ShapeDtypeStruct((m, n), BF16),
        scratch_shapes=[pltpu.VMEM((d, bn), BF16)],
        compiler_params=_cparams(2), name=name,
    )(x_bf, w, *tabs)


def _inproj_idx_kernel(x_ref, w_ref, cos64_ref, sinlo_ref, sinhi_ref, k2_ref, kw_ref, *, wi_scale):
    a = _dot(x_ref[...], w_ref[...].astype(BF16))
    r = _rope64(a, cos64_ref[...], sinlo_ref[...], sinhi_ref[...])
    lane = lax.broadcasted_iota(I32, a.shape, 1)
    kw_ref[...] = jnp.where(lane < IDX_DIM, r, a * wi_scale)
    k2_ref[...] = jnp.where(lane < IDX_DIM, r, pltpu.roll(r, 64, 1)).astype(k2_ref.dtype)


def _inproj_idx(x_bf, w_kw, tables, *, bm):
    m, d = x_bf.shape
    wi_scale = (N_IDX_HEADS ** -0.5) * (IDX_DIM ** -0.5)
    tab_spec = pl.BlockSpec((bm, LANES), lambda i: (i, 0))
    return pl.pallas_call(
        functools.partial(_inproj_idx_kernel, wi_scale=wi_scale), grid=(m // bm,),
        in_specs=[pl.BlockSpec((bm, d), lambda i: (i, 0)), pl.BlockSpec((d, LANES), lambda i: (0, 0))]
        + [tab_spec] * 3,
        out_specs=[tab_spec, tab_spec],
        out_shape=[jax.ShapeDtypeStruct((m, LANES), BF16), jax.ShapeDtypeStruct((m, LANES), F32)],
        compiler_params=_cparams(1), name="inproj_idx",
    )(x_bf, w_kw, *tables[2:])


def _pool_kernel(v_ref, pw_ref, ps_ref, o_ref, ext_ref):
    s = pl.program_id(1)
    ts = v_ref.shape[0]

    @pl.when(s == 0)
    def _():
        ext_ref[0:POOL_HALO, :] = jnp.zeros((POOL_HALO, ext_ref.shape[1]), F32)

    ext_ref[POOL_HALO:, :] = v_ref[...]
    t = s * ts + lax.broadcasted_iota(I32, (ts, 1), 0)
    for g, w in enumerate(POOL_WINDOWS):
        cols = slice(g * POOL_GW, (g + 1) * POOL_GW)
        e = ext_ref[:, cols]
        step = 1
        while step < w:
            e = e + pltpu.roll(e, step, 0)
            step *= 2
        win = e[POOL_HALO:, :]
        cnt = jnp.minimum(t + 1, w).astype(F32)
        pooled = win / cnt - v_ref[:, cols]
        mixed = _dot(pooled.astype(BF16), pw_ref[g])
        o_ref[:, cols] = (mixed * ps_ref[:, cols]).astype(o_ref.dtype)
    ext_ref[0:POOL_HALO, :] = v_ref[ts - POOL_HALO:, :]


def _pool_mixer(v_pool, pool_w_bf, pool_scale, *, batch, seq, ts):
    v3 = v_pool.reshape(batch, seq, MIX_POOL)
    out = pl.pallas_call(
        _pool_kernel, grid=(batch, seq // ts),
        in_specs=[pl.BlockSpec((None, ts, MIX_POOL), lambda b, s: (b, s, 0)),
                  pl.BlockSpec((N_POOL_GROUPS, POOL_GW, POOL_GW), lambda b, s: (0, 0, 0)),
                  pl.BlockSpec((1, MIX_POOL), lambda b, s: (0, 0))],
        out_specs=pl.BlockSpec((None, ts, MIX_POOL), lambda b, s: (b, s, 0)),
        out_shape=jax.ShapeDtypeStruct((batch, seq, MIX_POOL), BF16),
        scratch_shapes=[pltpu.VMEM((POOL_HALO + ts, MIX_POOL), F32)],
        compiler_params=_cparams(2), name="pool_mixer",
    )(v3, pool_w_bf, pool_scale.reshape(1, MIX_POOL))
    return out.reshape(batch * seq, MIX_POOL)


def _dsa_kernel(q_ref, k_ref, v_ref, k2_ref, qi_ref, kw_ref, o_ref,
                key_sc, bias_sc, xi_sc, qt_sc, m_sc, l_sc, acc_sc, s_sc, cmax_sc, s2_sc, cmax2_sc, *, n_sel, seq):
    blk = pl.program_id(1)
    nchunk = blk // (LC // QB) + 1
    q_pos = blk * QB + lax.broadcasted_iota(I32, (1, QB), 1)
    k_iota = lax.broadcasted_iota(I32, (LC, 1), 0)
    sub = lax.broadcasted_iota(I32, (LANES, QB), 0)
    eye = jnp.where(sub == lax.broadcasted_iota(I32, (LANES, QB), 1), 1.0, 0.0).astype(BF16)

    def transposed(x):
        return _dot_nt(eye, x)

    for jj in range(N_IDX_HEADS // 2):
        xt = transposed(qi_ref[:, jj * LANES:(jj + 1) * LANES])
        xi_sc[jj, :, 0:QB] = jnp.where(sub < IDX_DIM, xt, 0.0).astype(BF16)
        xi_sc[jj, :, QB:2 * QB] = jnp.where(sub >= IDX_DIM, xt, 0.0).astype(BF16)
    for g in range(N_KV_HEADS):
        for r in range(Q_PER_KV):
            h = g * Q_PER_KV + r
            qt_sc[g, 0:HEAD_DIM, r * QB:(r + 1) * QB] = transposed(
                q_ref[:, h * HEAD_DIM:(h + 1) * HEAD_DIM]).astype(BF16)
            qt_sc[g, HEAD_DIM:, r * QB:(r + 1) * QB] = eye
    kw_t = kw_ref[...].T
    wi = [kw_t[IDX_DIM + h:IDX_DIM + h + 1, :] for h in range(N_IDX_HEADS)]

    inf = jnp.inf
    part = (FOLD_ROWS, QB)

    def score_chunk(c, carry):
        s_min, s_max = carry
        off = pl.multiple_of(c * LC, LC)
        k2c = k2_ref[pl.ds(off, LC), :]
        sc = jnp.zeros((LC, QB), F32)
        for jj in range(N_IDX_HEADS // 2):
            rel = jnp.maximum(_dot(k2c, xi_sc[jj]), 0.0)
            sc = sc + rel[:, 0:QB] * wi[2 * jj] + rel[:, QB:2 * QB] * wi[2 * jj + 1]
        causal = off + k_iota <= q_pos
        key_sc[pl.ds(off, LC), :] = jnp.where(causal, sc, -inf)
        s_min = jnp.minimum(s_min, _fold_rows(jnp.where(causal, sc, inf), jnp.min))
        s_max = jnp.maximum(s_max, _fold_rows(jnp.where(causal, sc, -inf), jnp.max))
        return s_min, s_max

    s_min, s_max = lax.fori_loop(0, nchunk, score_chunk, (jnp.full(part, inf, F32), jnp.full(part, -inf, F32)))

    n_sel_chunk = (blk * QB + QB + SC - 1) // SC
    s_iota = lax.broadcasted_iota(I32, (SC, 1), 0)

    def count(pred_fn):
        def body(c, acc):
            off = pl.multiple_of(c * SC, SC)
            m = pred_fn(key_sc[pl.ds(off, SC), :], off)
            return acc + _fold_rows(m, jnp.sum)
        acc = lax.fori_loop(0, n_sel_chunk, body, jnp.zeros(part, F32))
        return jnp.sum(acc, axis=0, keepdims=True)

    has_thr = q_pos + 1 >= int(n_sel)
    lo0 = jnp.where(has_thr, jnp.min(s_min, axis=0, keepdims=True), -inf)
    hi0 = jnp.where(has_thr, jnp.max(s_max, axis=0, keepdims=True), -inf)

    def midpoint(lo, hi):
        mid = 0.5 * (lo + hi)
        return jnp.where(mid <= lo, hi, mid)

    def halve(i, st):
        lo, hi = st
        mid = midpoint(lo, hi)
        enough = count(lambda sc, off: jnp.where(sc >= mid, 1.0, 0.0)) >= n_sel
        return jnp.where(enough, mid, lo), jnp.where(enough, hi, mid)

    lo0, hi0 = lax.fori_loop(0, N_HALVINGS, halve, (lo0, hi0))

    def open_rows(lo, hi):
        return jnp.max(jnp.where(lo < hi, 1.0, 0.0)) > 0.0

    def bisect_cond(st):
        it, lo, hi = st
        return open_rows(lo, hi) & (it < seq)

    def bisect_body(st):
        it, lo, hi = st
        mid = midpoint(lo, hi)

        def body(c, acc):
            cnt, up, dn = acc
            off = pl.multiple_of(c * SC, SC)
            sc = key_sc[pl.ds(off, SC), :]
            ge = sc >= mid
            cnt = cnt + _fold_rows(jnp.where(ge, 1.0, 0.0), jnp.sum)
            up = jnp.minimum(up, _fold_rows(jnp.where(ge, sc, inf), jnp.min))
            dn = jnp.maximum(dn, _fold_rows(jnp.where(ge, -inf, sc), jnp.max))
            return cnt, up, dn

        cnt, up, dn = lax.fori_loop(
            0, n_sel_chunk, body, (jnp.zeros(part, F32), jnp.full(part, inf, F32), jnp.full(part, -inf, F32)))
        enough = jnp.sum(cnt, axis=0, keepdims=True) >= n_sel
        is_open = lo < hi
        new_lo = jnp.where(is_open & enough, jnp.min(up, axis=0, keepdims=True), lo)
        new_hi = jnp.where(is_open & jnp.logical_not(enough), jnp.max(dn, axis=0, keepdims=True), hi)
        return it + 1, new_lo, new_hi

    _, thr, _ = lax.while_loop(bisect_cond, bisect_body, (jnp.int32(0), lo0, hi0))
    n_gt = count(lambda kk, off: jnp.where(kk > thr, 1.0, 0.0))
    n_eq = count(lambda kk, off: jnp.where(kk == thr, 1.0, 0.0))
    need = n_sel - n_gt

    def tie_search():
        def tbody(i, p):
            cand = p | jnp.left_shift(jnp.int32(1), (seq.bit_length() - 2) - i)
            cnt = count(lambda kk, off: jnp.where(kk == thr, jnp.where(off + s_iota < cand, 1.0, 0.0), 0.0))
            return jnp.where(cnt < need, cand, p)
        return lax.fori_loop(0, seq.bit_length() - 1, tbody, jnp.zeros((1, QB), I32))

    ambiguous = jnp.max(jnp.where(has_thr & (n_eq > need), 1.0, 0.0)) > 0.0
    tie_hi = lax.cond(ambiguous, tie_search, lambda: jnp.full((1, QB), seq, I32))
    tie_hi = jnp.where(has_thr, tie_hi, -1)

    def bias_chunk(c, carry):
        off = pl.multiple_of(c * LC, LC)
        kk = key_sc[pl.ds(off, LC), :]
        tie_ok = jnp.where(off + k_iota <= tie_hi, 0.0, NEG)
        bias = jnp.where(kk > thr, 0.0, jnp.where(kk == thr, tie_ok, NEG))
        bias_sc[pl.ds(off, LC), :] = bias.astype(BF16)
        return carry

    lax.fori_loop(0, nchunk, bias_chunk, 0)

    m_sc[...] = jnp.full(m_sc.shape, NEG, F32)
    l_sc[...] = jnp.zeros(l_sc.shape, F32)
    acc_sc[...] = jnp.zeros(acc_sc.shape, F32)

    bufs = ((s_sc, cmax_sc), (s2_sc, cmax2_sc))

    def logits(c, g, buf):
        off = pl.multiple_of(c * LC, LC)
        kc = k_ref[pl.ds(off, LC), g * HEAD_DIM:(g + 1) * HEAD_DIM]
        s = _dot(jnp.concatenate([kc, bias_sc[pl.ds(off, LC), :]], axis=1), qt_sc[g])
        buf[0][g] = s
        buf[1][g] = jnp.max(_fold_rows(s, jnp.max), axis=0, keepdims=True)

    def accumulate(c, g, buf):
        off = pl.multiple_of(c * LC, LC)
        vc = v_ref[pl.ds(off, LC), g * HEAD_DIM:(g + 1) * HEAD_DIM]
        m_old = m_sc[g]
        m_new = jnp.maximum(m_old, buf[1][g])
        alpha = jnp.exp2(m_old - m_new)
        p = jnp.exp2(buf[0][g] - m_new)
        l_sc[g] = alpha * l_sc[g] + jnp.sum(_fold_rows(p, jnp.sum), axis=0, keepdims=True)
        acc_sc[g] = alpha * acc_sc[g] + _dot_tn(vc, p.astype(BF16))
        m_sc[g] = m_new

    def step(c, cur, nxt):
        for g in range(N_KV_HEADS):
            accumulate(c, g, cur)
            logits(c + 1, g, nxt)

    def last(c, cur):
        for g in range(N_KV_HEADS):
            accumulate(c, g, cur)

    for g in range(N_KV_HEADS):
        logits(0, g, bufs[0])
    n_pairs = (nchunk - 1) // 2

    def attn_pair(p, carry):
        step(2 * p, bufs[0], bufs[1])
        step(2 * p + 1, bufs[1], bufs[0])
        return carry

    lax.fori_loop(0, n_pairs, attn_pair, 0)
    c_tail = 2 * n_pairs

    @pl.when(nchunk - c_tail == 2)
    def _():
        step(c_tail, bufs[0], bufs[1])
        last(c_tail + 1, bufs[1])

    @pl.when(nchunk - c_tail == 1)
    def _():
        last(c_tail, bufs[0])

    for g in range(N_KV_HEADS):
        o_t = acc_sc[g] / l_sc[g]
        for r in range(Q_PER_KV):
            h = g * Q_PER_KV + r
            o_ref[:, h * HEAD_DIM:(h + 1) * HEAD_DIM] = o_t[:, r * QB:(r + 1) * QB].T.astype(o_ref.dtype)


def _dsa_attention(q, k, v, qi, k2, kw, *, batch, seq):
    nb = seq // QB
    n_sel = min(TOPK_MAX, seq // 4)
    q_cols = N_Q_HEADS * HEAD_DIM
    kv_cols = N_KV_HEADS * HEAD_DIM
    qi_cols = N_IDX_HEADS * IDX_DIM
    assert seq % LC == 0 and seq % SC == 0 and LC >= n_sel
    kern = functools.partial(_dsa_kernel, n_sel=float(n_sel), seq=seq)
    rows = Q_PER_KV * QB
    return pl.pallas_call(
        kern, grid=(batch, nb),
        in_specs=[pl.BlockSpec((QB, q_cols), lambda b, i: (b * nb + i, 0)),
                  pl.BlockSpec((seq, kv_cols), lambda b, i: (b, 0)),
                  pl.BlockSpec((seq, kv_cols), lambda b, i: (b, 0)),
                  pl.BlockSpec((seq, LANES), lambda b, i: (b, 0)),
                  pl.BlockSpec((QB, qi_cols), lambda b, i: (b * nb + i, 0)),
                  pl.BlockSpec((QB, LANES), lambda b, i: (b * nb + i, 0))],
        out_specs=pl.BlockSpec((QB, q_cols), lambda b, i: (b * nb + i, 0)),
        out_shape=jax.ShapeDtypeStruct((batch * seq, q_cols), BF16),
        scratch_shapes=[pltpu.VMEM((seq, QB), F32),
                        pltpu.VMEM((seq, QB), BF16),
                        pltpu.VMEM((N_IDX_HEADS // 2, LANES, 2 * QB), BF16),
                        pltpu.VMEM((N_KV_HEADS, HEAD_DIM + QB, rows), BF16),
                        pltpu.VMEM((N_KV_HEADS, 1, rows), F32),
                        pltpu.VMEM((N_KV_HEADS, 1, rows), F32),
                        pltpu.VMEM((N_KV_HEADS, HEAD_DIM, rows), F32),
                        pltpu.VMEM((N_KV_HEADS, LC, rows), F32), pltpu.VMEM((N_KV_HEADS, 1, rows), F32),
                        pltpu.VMEM((N_KV_HEADS, LC, rows), F32), pltpu.VMEM((N_KV_HEADS, 1, rows), F32)],
        compiler_params=_cparams(2), name="dsa_attention",
    )(q, k, v, k2, qi, kw)


def _ln_rows(x, g, b):
    mu = jnp.mean(x, axis=-1, keepdims=True)
    xc = x - mu
    var = jnp.mean(xc * xc, axis=-1, keepdims=True)
    return xc * lax.rsqrt(var + LN_EPS) * g + b


def _pack_halves(y):
    half = y.shape[1] // 2
    return pltpu.pack_elementwise([y[:, :half], y[:, half:]], packed_dtype=BF16)


def _unpack_halves(p):
    lo = pltpu.unpack_elementwise(p, index=0, packed_dtype=BF16, unpacked_dtype=F32)
    hi = pltpu.unpack_elementwise(p, index=1, packed_dtype=BF16, unpacked_dtype=F32)
    return jnp.concatenate([lo, hi], axis=1)


def _ln_kernel(x_ref, g_ref, b_ref, o_ref, o2_ref, *, packed):
    y = _ln_rows(x_ref[...], g_ref[...], b_ref[...])
    o_ref[...] = y
    o2_ref[...] = _pack_halves(y) if packed else y.astype(o2_ref.dtype)


def _layer_norm(x, g, b, *, bm, packed=False):
    m, d = x.shape
    row = pl.BlockSpec((bm, d), lambda i: (i, 0))
    vec = pl.BlockSpec((1, d), lambda i: (0, 0))
    second = ((m, d // 2), U32) if packed else ((m, d), BF16)
    return pl.pallas_call(
        functools.partial(_ln_kernel, packed=packed), grid=(m // bm,), in_specs=[row, vec, vec],
        out_specs=[row, pl.BlockSpec((bm, second[0][1]), lambda i: (i, 0))],
        out_shape=[jax.ShapeDtypeStruct((m, d), F32), jax.ShapeDtypeStruct(*second)],
        compiler_params=_cparams(1), name="layer_norm",
    )(x, g.reshape(1, d), b.reshape(1, d))


def _xattn_kernel(q_ref, k_ref, v_ref, o_ref):
    dh = q_ref.shape[1] // N_MEM_HEADS
    for h in range(N_MEM_HEADS):
        cols = slice(h * dh, (h + 1) * dh)
        s = _dot_nt(q_ref[:, cols], k_ref[:, cols])
        p = jnp.exp(s - jnp.max(s, axis=1, keepdims=True))
        o = _dot(p.astype(BF16), v_ref[:, cols]) / jnp.sum(p, axis=1, keepdims=True)
        o_ref[:, cols] = o.astype(o_ref.dtype)


def _cross_attention(q, k, v, *, batch, seq, n_mem, bm):
    d = q.shape[1]
    nb = seq // bm
    return pl.pallas_call(
        _xattn_kernel, grid=(batch, nb),
        in_specs=[pl.BlockSpec((bm, d), lambda b, i: (b * nb + i, 0)),
                  pl.BlockSpec((n_mem, d), lambda b, i: (b, 0)),
                  pl.BlockSpec((n_mem, d), lambda b, i: (b, 0))],
        out_specs=pl.BlockSpec((bm, d), lambda b, i: (b * nb + i, 0)),
        out_shape=jax.ShapeDtypeStruct((batch * seq, d), BF16),
        compiler_params=_cparams(2), name="cross_attention",
    )(q, k, v)


def _router_kernel(x_ref, w_ref, b_ref, id_ref, gate_ref):
    logits = jnp.dot(x_ref[...], w_ref[...], preferred_element_type=F32,
                     precision=lax.Precision.HIGHEST) + b_ref[...]
    lane = lax.broadcasted_iota(I32, logits.shape, 1)
    lane_f = lane.astype(F32)
    big = float(LANES)
    is_g = lane < N_GROUPS
    gl = jnp.where(is_g, logits, -jnp.inf)
    g_max = jnp.max(gl, axis=1, keepdims=True)
    g_idx = jnp.min(jnp.where(gl == g_max, lane_f, big), axis=1, keepdims=True)
    g_gate = 1.0 / jnp.sum(jnp.where(is_g, jnp.exp(gl - g_max), 0.0), axis=1, keepdims=True)
    e_lo = N_GROUPS + g_idx * EXPERTS_PER_GROUP
    in_grp = (lane_f >= e_lo) & (lane_f < e_lo + EXPERTS_PER_GROUP)
    el = jnp.where(in_grp, logits, -jnp.inf)
    v1 = jnp.max(el, axis=1, keepdims=True)
    i1 = jnp.min(jnp.where(el == v1, lane_f, big), axis=1, keepdims=True)
    el2 = jnp.where(lane_f == i1, -jnp.inf, el)
    v2 = jnp.max(el2, axis=1, keepdims=True)
    i2 = jnp.min(jnp.where(el2 == v2, lane_f, big), axis=1, keepdims=True)
    z = jnp.exp(v2 - v1)
    w1 = g_gate / (1.0 + z)
    w2 = g_gate * z / (1.0 + z)
    ids = jnp.where(lane == 0, i1 - N_GROUPS, jnp.where(lane == 1, i2 - N_GROUPS, 0.0))
    id_ref[...] = ids.astype(I32)
    gate_ref[...] = jnp.where(lane == 0, w1, jnp.where(lane == 1, w2, 0.0))


def _router(x, w_r, b_r, *, bm):
    m, d = x.shape
    out = pl.BlockSpec((bm, LANES), lambda i: (i, 0))
    return pl.pallas_call(
        _router_kernel, grid=(m // bm,),
        in_specs=[pl.BlockSpec((bm, d), lambda i: (i, 0)),
                  pl.BlockSpec((d, LANES), lambda i: (0, 0)),
                  pl.BlockSpec((1, LANES), lambda i: (0, 0))],
        out_specs=[out, out],
        out_shape=[jax.ShapeDtypeStruct((m, LANES), I32), jax.ShapeDtypeStruct((m, LANES), F32)],
        compiler_params=_cparams(1), name="router",
    )(x, w_r, b_r)


def _rank_kernel(id_ref, rank_ref, cnt_ref, carry_ref):
    i = pl.program_id(0)
    bt = id_ref.shape[0]

    @pl.when(i == 0)
    def _():
        carry_ref[...] = jnp.zeros(carry_ref.shape, F32)

    ids = id_ref[...]
    lane = lax.broadcasted_iota(I32, (bt, LANES), 1)
    e0 = ids[:, 0:1]
    e1 = ids[:, 1:2] + N_EXPERTS
    hit0 = lane == e0
    hit1 = lane == e1
    onehot = jnp.where(hit0, 1.0, jnp.where(hit1, 1.0, 0.0))
    r_io = lax.broadcasted_iota(I32, (bt, bt), 0)
    c_io = lax.broadcasted_iota(I32, (bt, bt), 1)
    tri = jnp.where(c_io < r_io, 1.0, 0.0).astype(BF16)
    prefix = _dot(tri, onehot.astype(BF16)) + carry_ref[0:1, :]
    rank0 = jnp.sum(jnp.where(hit0, prefix, 0.0), axis=1, keepdims=True)
    rank1 = jnp.sum(jnp.where(hit1, prefix, 0.0), axis=1, keepdims=True)
    rank_ref[...] = jnp.where(lane == 0, rank0, jnp.where(lane == 1, rank1, 0.0))
    total = carry_ref[0:1, :] + jnp.sum(onehot, axis=0, keepdims=True)
    carry_ref[...] = jnp.broadcast_to(total, carry_ref.shape)
    cnt_ref[...] = jnp.broadcast_to(total, cnt_ref.shape)


def _slot_ranks(ids, *, bt):
    m = ids.shape[0]
    return pl.pallas_call(
        _rank_kernel, grid=(m // bt,),
        in_specs=[pl.BlockSpec((bt, LANES), lambda i: (i, 0))],
        out_specs=[pl.BlockSpec((bt, LANES), lambda i: (i, 0)),
                   pl.BlockSpec((8, LANES), lambda i: (0, 0))],
        out_shape=[jax.ShapeDtypeStruct((m, LANES), F32), jax.ShapeDtypeStruct((8, LANES), F32)],
        scratch_shapes=[pltpu.VMEM((8, LANES), F32)],
        compiler_params=_cparams(1), name="slot_ranks",
    )(ids)


def _pos_kernel(id_ref, rank_ref, tab_ref, pos_ref):
    ids = id_ref[...]
    ranks = rank_ref[...]
    lane = lax.broadcasted_iota(I32, ids.shape, 1)
    tab = tab_ref[...]
    base0 = jnp.sum(jnp.where(lane == ids[:, 0:1], tab, 0.0), axis=1, keepdims=True)
    base1 = jnp.sum(jnp.where(lane == ids[:, 1:2] + N_EXPERTS, tab, 0.0), axis=1, keepdims=True)
    pos = jnp.where(lane == 0, base0 + ranks[:, 0:1], jnp.where(lane == 1, base1 + ranks[:, 1:2], 0.0))
    pos_ref[...] = pos.astype(I32)


def _slot_positions(ids, ranks, table, *, bt):
    m = ids.shape[0]
    blk = pl.BlockSpec((bt, LANES), lambda i: (i, 0))
    return pl.pallas_call(
        _pos_kernel, grid=(m // bt,),
        in_specs=[blk, blk, pl.BlockSpec((1, LANES), lambda i: (0, 0))],
        out_specs=blk,
        out_shape=jax.ShapeDtypeStruct((m, LANES), I32),
        compiler_params=_cparams(1), name="slot_positions",
    )(ids, ranks, table)


def _row_copy(src_ref, src_row, dst_ref, dst_row, sem):
    return pltpu.make_async_copy(src_ref.at[pl.ds(src_row, 1)], dst_ref.at[pl.ds(dst_row, 1)], sem)


def _dispatch_kernel(pos0_ref, pos1_ref, x_ref, xs_in_ref, xs_ref, sem):
    del xs_in_ref
    i = pl.program_id(0)
    bt = x_ref.shape[0]

    def issue(r, carry):
        t = i * bt + r
        _row_copy(x_ref, r, xs_ref, pos0_ref[t], sem.at[0]).start()
        _row_copy(x_ref, r, xs_ref, pos1_ref[t], sem.at[1]).start()
        return carry

    lax.fori_loop(0, bt, issue, 0)
    for k in range(2):
        pltpu.make_async_copy(x_ref, xs_ref.at[pl.ds(0, bt)], sem.at[k]).wait()


def _dispatch(x, pos0, pos1, n_slots, *, bt):
    m, d = x.shape
    xs0 = jnp.zeros((n_slots, d), x.dtype)
    return pl.pallas_call(
        _dispatch_kernel,
        grid_spec=pltpu.PrefetchScalarGridSpec(
            num_scalar_prefetch=2, grid=(m // bt,),
            in_specs=[pl.BlockSpec((bt, d), lambda i, p0, p1: (i, 0)),
                      pl.BlockSpec(memory_space=pl.ANY)],
            out_specs=pl.BlockSpec(memory_space=pl.ANY),
            scratch_shapes=[pltpu.SemaphoreType.DMA((2,))]),
        out_shape=jax.ShapeDtypeStruct((n_slots, d), x.dtype),
        input_output_aliases={3: 0},
        compiler_params=_cparams(1), name="dispatch",
    )(pos0, pos1, x, xs0)


def _ffn_kernel(blk_e_ref, nxt_e_ref, set_ref, nused_ref, x_ref, wg_hbm, wu_hbm, wd_hbm, y_ref,
                wg_bf, wu_bf, wd_bf, st_g, st_u, st_d, sem):
    i = pl.program_id(0)
    d, ff = wg_bf.shape[1], wg_bf.shape[2]
    rg, rd = d // W_GROUPS, ff // W_GROUPS

    n_slot = st_g.shape[0]

    def group_copies(e, g):
        slot = g % n_slot
        return (pltpu.make_async_copy(wg_hbm.at[e, pl.ds(g * rg, rg), :], st_g.at[slot], sem.at[0, slot]),
                pltpu.make_async_copy(wu_hbm.at[e, pl.ds(g * rg, rg), :], st_u.at[slot], sem.at[1, slot]),
                pltpu.make_async_copy(wd_hbm.at[e, pl.ds(g * rd, rd), :], st_d.at[slot], sem.at[2, slot]))

    def start(e, g):
        for c in group_copies(e, g):
            c.start()

    def finish(e, g, s):
        for c in group_copies(e, g):
            c.wait()
        slot = g % n_slot
        wg_bf[s, g * rg:(g + 1) * rg, :] = st_g[slot].astype(BF16)
        wu_bf[s, g * rg:(g + 1) * rg, :] = st_u[slot].astype(BF16)
        wd_bf[s, g * rd:(g + 1) * rd, :] = st_d[slot].astype(BF16)

    def load_expert(e, s, between=()):
        for g in range(n_slot):
            start(e, g)
        for g in range(W_GROUPS):
            if g < len(between):
                between[g]()
            finish(e, g, s)
            if g + n_slot < W_GROUPS:
                start(e, g + n_slot)

    e = blk_e_ref[i]
    s = set_ref[i]
    nxt = nxt_e_ref[i]
    first_of_expert = jnp.logical_or(i == 0, blk_e_ref[jnp.maximum(i - 1, 0)] != e)
    active = i < nused_ref[0]
    prefetch = jnp.logical_and(jnp.logical_and(active, first_of_expert), nxt >= 0)

    @pl.when(i == 0)
    def _():
        load_expert(e, s)

    def ffn_steps():
        vals = {}

        def gate_step():
            vals["x"] = _unpack_halves(x_ref[...]).astype(BF16)
            vals["gate"] = _dot(vals["x"], wg_bf[s])

        def up_step():
            gate = vals["gate"]
            vals["h"] = (gate * (1.0 / (1.0 + jnp.exp(-gate))) * _dot(vals["x"], wu_bf[s])).astype(BF16)

        def down_step():
            y_ref[...] = _pack_halves(_dot(vals["h"], wd_bf[s]))

        return gate_step, up_step, down_step

    @pl.when(prefetch)
    def _():
        load_expert(nxt, 1 - s, between=ffn_steps())

    @pl.when(jnp.logical_and(active, jnp.logical_not(prefetch)))
    def _():
        for step in ffn_steps():
            step()


def _expert_ffn(xs, blk_e, nxt_e, set_idx, nused, wg, wu, wd):
    n_slots, dp = xs.shape
    d, ff = wg.shape[1], wg.shape[2]
    nblk = n_slots // MOE_BLK
    assert dp * 2 == d and d % W_GROUPS == 0 and ff % W_GROUPS == 0 and W_GROUPS >= W_STAGE_SLOTS > 3

    def row_map(i, be, nx, si, nu):
        return (jnp.minimum(i, nu[0] - 1), 0)

    hbm = pl.BlockSpec(memory_space=pl.ANY)
    return pl.pallas_call(
        _ffn_kernel,
        grid_spec=pltpu.PrefetchScalarGridSpec(
            num_scalar_prefetch=4, grid=(nblk,),
            in_specs=[pl.BlockSpec((MOE_BLK, dp), row_map), hbm, hbm, hbm],
            out_specs=pl.BlockSpec((MOE_BLK, dp), row_map),
            scratch_shapes=[pltpu.VMEM((2, d, ff), BF16), pltpu.VMEM((2, d, ff), BF16), pltpu.VMEM((2, ff, d), BF16),
                            pltpu.VMEM((W_STAGE_SLOTS, d // W_GROUPS, ff), F32),
                            pltpu.VMEM((W_STAGE_SLOTS, d // W_GROUPS, ff), F32),
                            pltpu.VMEM((W_STAGE_SLOTS, ff // W_GROUPS, d), F32),
                            pltpu.SemaphoreType.DMA((3, W_STAGE_SLOTS))]),
        out_shape=jax.ShapeDtypeStruct((n_slots, dp), xs.dtype),
        input_output_aliases={4: 0},
        compiler_params=_cparams(1), name="expert_ffn",
    )(blk_e, nxt_e, set_idx, nused, xs, wg, wu, wd)


def _combine_kernel(pos0_ref, pos1_ref, x_ref, gate_ref, g_ref, b_ref, ys_ref, o_ref, y0_ref, y1_ref, sem):
    i = pl.program_id(0)
    bt = x_ref.shape[0]
    slot = i % 2

    def issue(step, to_slot):
        def body(r, carry):
            t = step * bt + r
            _row_copy(ys_ref, pos0_ref[t], y0_ref.at[to_slot], r, sem.at[0, to_slot]).start()
            _row_copy(ys_ref, pos1_ref[t], y1_ref.at[to_slot], r, sem.at[1, to_slot]).start()
            return carry
        lax.fori_loop(0, bt, body, 0)

    @pl.when(i == 0)
    def _():
        issue(0, 0)

    @pl.when(i + 1 < pl.num_programs(0))
    def _():
        issue(i + 1, 1 - slot)

    pltpu.make_async_copy(ys_ref.at[pl.ds(0, bt)], y0_ref.at[slot], sem.at[0, slot]).wait()
    pltpu.make_async_copy(ys_ref.at[pl.ds(0, bt)], y1_ref.at[slot], sem.at[1, slot]).wait()
    gates = gate_ref[...]
    f = _unpack_halves(y0_ref[slot]) * gates[:, 0:1] + _unpack_halves(y1_ref[slot]) * gates[:, 1:2]
    o_ref[...] = _ln_rows(ALPHA * x_ref[...] + f, g_ref[...], b_ref[...])


def _combine(x, gates, ys, pos0, pos1, g, b, *, bt):
    m, d = x.shape
    dp = ys.shape[1]
    row = pl.BlockSpec((bt, d), lambda i, p0, p1: (i, 0))
    vec = pl.BlockSpec((1, d), lambda i, p0, p1: (0, 0))
    return pl.pallas_call(
        _combine_kernel,
        grid_spec=pltpu.PrefetchScalarGridSpec(
            num_scalar_prefetch=2, grid=(m // bt,),
            in_specs=[row, pl.BlockSpec((bt, LANES), lambda i, p0, p1: (i, 0)), vec, vec,
                      pl.BlockSpec(memory_space=pl.ANY)],
            out_specs=row,
            scratch_shapes=[pltpu.VMEM((2, bt, dp), ys.dtype), pltpu.VMEM((2, bt, dp), ys.dtype),
                            pltpu.SemaphoreType.DMA((2, 2))]),
        out_shape=jax.ShapeDtypeStruct((m, d), F32),
        compiler_params=_cparams(1), name="combine_ln",
    )(pos0, pos1, x, gates, g.reshape(1, d), b.reshape(1, d), ys)


def _rope_tables(positions, n_rows):
    pos = positions.astype(F32).reshape(n_rows, 1)

    def cs(dim):
        inv = 1.0 / (ROPE_THETA ** (jnp.arange(0, dim, 2, dtype=F32) / dim))
        ang = pos * inv[None, :]
        return jnp.cos(ang), jnp.sin(ang)

    c_h, s_h = cs(HEAD_DIM)
    c_i, s_i = cs(IDX_DIM)
    z_i = jnp.zeros_like(s_i)
    cosf = jnp.concatenate([c_h, c_h], axis=1)
    sinf = jnp.concatenate([-s_h, s_h], axis=1)
    cos64 = jnp.concatenate([c_i, c_i, c_i, c_i], axis=1)
    sin_lo = jnp.concatenate([-s_i, z_i, -s_i, z_i], axis=1)
    sin_hi = jnp.concatenate([z_i, s_i, z_i, s_i], axis=1)
    return cosf, sinf, cos64, sin_lo, sin_hi


def kernel(x, mem, positions, w_in, pool_w, pool_scale, w_o, ln1_g, ln1_b, w_mq, w_mk, w_mv, w_mo, ln2_g, ln2_b,
           w_group_router, b_group_router, w_expert_router, b_expert_router, w_gate, w_up, w_down, ln3_g, ln3_b):
    batch, seq, d = x.shape
    n_mem = mem.shape[1]
    n = batch * seq
    bn_attn = N_KV_HEADS * HEAD_DIM
    tables = _rope_tables(positions, n)
    xf = x.reshape(n, d)
    n_slots = 2 * n + N_EXPERTS * MOE_BLK
    nblk = n_slots // MOE_BLK

    for l in range(w_in.shape[0]):
        x_bf = xf.astype(BF16)
        kw_col = MIX_POOL + 7 * bn_attn
        w_kw = jnp.pad(w_in[l][:, kw_col:], ((0, 0), (0, LANES - (w_in.shape[2] - kw_col))))

        v_pool = _matmul([x_bf], [w_in], layer=l, name="inproj_pool", bm=1024, bn=DENSE_BN, n=MIX_POOL)
        blk0 = MIX_POOL // bn_attn
        q_cols = N_Q_HEADS * HEAD_DIM
        q_scale = HEAD_DIM ** -0.5 * LOG2E
        rope_h, rope_i = tables[:2] + tables[:1], tables[2:]
        proj = functools.partial(_inproj_rope, x_bf, w_in, layer=l, bm=1024, bn=bn_attn)
        q = proj(rope_h, name="inproj_q", col_off_blocks=blk0, n=q_cols, rope=HEAD_DIM, scale=q_scale)
        k = proj(rope_h, name="inproj_k", col_off_blocks=blk0 + 4, n=bn_attn, rope=HEAD_DIM)
        v = proj(rope_h, name="inproj_v", col_off_blocks=blk0 + 5, n=bn_attn, rope=None)
        qi = proj(rope_i, name="inproj_qi", col_off_blocks=blk0 + 6, n=bn_attn, rope=IDX_DIM)
        k2, kw = _inproj_idx(x_bf, w_kw, tables, bm=1024)
        a_pool = _pool_mixer(v_pool, pool_w[l].astype(BF16), pool_scale[l], batch=batch, seq=seq, ts=512)
        a_attn = _dsa_attention(q, k, v, qi, k2, kw, batch=batch, seq=seq)
        res = (xf, (1024, DENSE_BN), lambda i, j: (i, j))
        pre = _matmul([a_pool, a_attn], [w_o, w_o], layer=l, name="outproj", bm=1024, bn=DENSE_BN, n=d,
                      extras=(res,), epilogue=_residual_epilogue)
        x1, x1_bf = _layer_norm(pre, ln1_g[l], ln1_b[l], bm=256)

        mq_scale = (d // N_MEM_HEADS) ** -0.5

        def q_epilogue(acc, ex, outs):
            outs[0][...] = (acc * mq_scale).astype(outs[0].dtype)

        qm = _matmul([x1_bf], [w_mq], layer=l, name="mem_q", bm=1024, bn=DENSE_BN, n=d, out_dtype=BF16,
                     epilogue=q_epilogue)
        mem_bf = mem.reshape(batch * n_mem, d).astype(BF16)
        km = _matmul([mem_bf], [w_mk], layer=l, name="mem_k", bm=batch * n_mem, bn=DENSE_BN, n=d, out_dtype=BF16)
        vm = _matmul([mem_bf], [w_mv], layer=l, name="mem_v", bm=batch * n_mem, bn=DENSE_BN, n=d, out_dtype=BF16)
        om = _cross_attention(qm, km, vm, batch=batch, seq=seq, n_mem=n_mem, bm=512)
        res = (x1, (1024, DENSE_BN), lambda i, j: (i, j))
        pre = _matmul([om], [w_mo], layer=l, name="mem_o", bm=1024, bn=DENSE_BN, n=d, extras=(res,),
                      epilogue=_residual_epilogue)
        x2, x2_rows = _layer_norm(pre, ln2_g[l], ln2_b[l], bm=256, packed=True)

        w_r = jnp.pad(jnp.concatenate([w_group_router[l], w_expert_router[l]], axis=1),
                      ((0, 0), (0, LANES - N_GROUPS - N_EXPERTS)))
        b_r = jnp.pad(jnp.concatenate([b_group_router[l], b_expert_router[l]]),
                      (0, LANES - N_GROUPS - N_EXPERTS)).reshape(1, LANES)
        ids, gates = _router(x2, w_r, b_r, bm=512)
        ranks, totals = _slot_ranks(ids, bt=512)
        c0 = totals[0, :N_EXPERTS].astype(I32)
        c1 = totals[0, N_EXPERTS:].astype(I32)
        padded = ((c0 + c1 + MOE_BLK - 1) // MOE_BLK) * MOE_BLK
        pend = jnp.cumsum(padded)
        pstart = pend - padded
        table = jnp.concatenate([pstart, pstart + c0]).astype(F32).reshape(1, LANES)
        pos = _slot_positions(ids, ranks, table, bt=512)
        pos0, pos1 = pos[:, 0], pos[:, 1]
        nused = pend[-1] // MOE_BLK
        blk_i = jnp.minimum(jnp.arange(nblk, dtype=I32), nused - 1)
        blk_e = jnp.sum((pend[None, :] <= (blk_i * MOE_BLK)[:, None]).astype(I32), axis=1)
        blk_e = jnp.minimum(blk_e, N_EXPERTS - 1)
        later = blk_e[None, :] > blk_e[:, None]
        nxt_e = jnp.min(jnp.where(later, blk_e[None, :], N_EXPERTS), axis=1)
        nxt_e = jnp.where(nxt_e == N_EXPERTS, -1, nxt_e).astype(I32)
        new_e = jnp.concatenate([jnp.ones((1,), I32), (blk_e[1:] != blk_e[:-1]).astype(I32)])
        set_idx = (jnp.cumsum(new_e) - 1) % 2

        xs = _dispatch(x2_rows, pos0, pos1, n_slots, bt=256)
        ys = _expert_ffn(xs, blk_e, nxt_e, set_idx.astype(I32), nused.reshape(1).astype(I32),
                         w_gate[l], w_up[l], w_down[l])
        xf = _combine(x2, gates, ys, pos0, pos1, ln3_g[l], ln3_b[l], bt=256)
    return xf.reshape(batch, seq, d)
```

```python
import functools

import jax
import jax.numpy as jnp
from jax import lax
from jax.experimental import pallas as pl
from jax.experimental.pallas import tpu as pltpu

F32 = jnp.float32
BF16 = jnp.bfloat16
I32 = jnp.int32
U32 = jnp.uint32

MIX_POOL = 2048
N_POOL_GROUPS = 4
POOL_WINDOWS = (2, 4, 8, 16)
POOL_GW = MIX_POOL // N_POOL_GROUPS
HEAD_DIM = 128
N_Q_HEADS = 16
N_KV_HEADS = 4
Q_PER_KV = N_Q_HEADS // N_KV_HEADS
N_IDX_HEADS = 8
IDX_DIM = 64
TOPK_MAX = 256
ROPE_THETA = 10000.0
N_MEM_HEADS = 4
N_GROUPS = 8
EXPERTS_PER_GROUP = 8
N_EXPERTS = N_GROUPS * EXPERTS_PER_GROUP
LN_EPS = 1e-5
DEPTH = 1
ALPHA = (2.0 * DEPTH) ** 0.25

LANES = 128
QB = 128
LC = 512
SC = 256
DENSE_BN = 1024
MOE_BLK = 256
W_GROUPS = 8
W_STAGE_SLOTS = 4
POOL_HALO = 16
NEG = -1e30
N_HALVINGS = 12
LOG2E = 1.4426950408889634
VMEM_LIMIT = 56 * 1024 * 1024


def _cparams(n_axes, vmem=VMEM_LIMIT):
    return pltpu.CompilerParams(dimension_semantics=("arbitrary",) * n_axes, vmem_limit_bytes=vmem)


def _dot(a, b):
    return jnp.dot(a, b, preferred_element_type=F32)


def _dot_nt(a, b):
    return lax.dot_general(a, b, (((1,), (1,)), ((), ())), preferred_element_type=F32)


def _dot_tn(a, b):
    return lax.dot_general(a, b, (((0,), (0,)), ((), ())), preferred_element_type=F32)


FOLD_ROWS = 32


def _fold_rows(x, reduce_fn):
    r, c = x.shape
    return reduce_fn(x.reshape(r // FOLD_ROWS, FOLD_ROWS, c), axis=0)


def _mm_kernel(*refs, n_lhs, n_extra, epilogue):
    lhs = refs[:n_lhs]
    ws = refs[n_lhs:2 * n_lhs]
    extras = refs[2 * n_lhs:2 * n_lhs + n_extra]
    outs = refs[2 * n_lhs + n_extra:-n_lhs]
    w_bf = refs[-n_lhs:]

    @pl.when(pl.program_id(1) == 0)
    def _():
        for l in range(n_lhs):
            w_bf[l][...] = ws[l][...].astype(BF16)

    acc = _dot(lhs[0][...], w_bf[0][...])
    for l in range(1, n_lhs):
        acc = acc + _dot(lhs[l][...], w_bf[l][...])
    epilogue(acc, extras, outs)


def _matmul(lhs_list, w_list, *, layer, name, bm, bn, n, w_col_off=0, extras=(), out_dtype=F32, epilogue=None):
    m = lhs_list[0].shape[0]
    grid = (n // bn, m // bm)
    in_specs = [pl.BlockSpec((bm, a.shape[1]), lambda j, i: (i, 0)) for a in lhs_list]
    in_specs += [pl.BlockSpec((None, a.shape[1], bn), (lambda j, i, l=l: (layer, l, j + w_col_off)),
                              pipeline_mode=pl.Buffered(1))
                 for l, a in enumerate(lhs_list)]
    in_specs += [pl.BlockSpec(blk, (lambda j, i, f=f: f(i, j))) for (_, blk, f) in extras]
    if epilogue is None:
        def epilogue(acc, ex, outs):
            outs[0][...] = acc.astype(outs[0].dtype)
    kern = functools.partial(_mm_kernel, n_lhs=len(lhs_list), n_extra=len(extras), epilogue=epilogue)
    return pl.pallas_call(
        kern, grid=grid, in_specs=in_specs,
        out_specs=pl.BlockSpec((bm, bn), lambda j, i: (i, j)),
        out_shape=jax.ShapeDtypeStruct((m, n), out_dtype),
        scratch_shapes=[pltpu.VMEM((a.shape[1], bn), BF16) for a in lhs_list],
        compiler_params=_cparams(2), name=name,
    )(*lhs_list, *w_list, *[e[0] for e in extras])


def _residual_epilogue(acc, ex, outs):
    outs[0][...] = acc + ALPHA * ex[0][...]


def _rope128(a, cosf, sinf):
    return a * cosf + pltpu.roll(a, 64, 1) * sinf


def _rope64(a, cos64, sin_lo, sin_hi):
    return a * cos64 + pltpu.roll(a, 96, 1) * sin_lo + pltpu.roll(a, 32, 1) * sin_hi


def _inproj_kernel(x_ref, w_ref, t0_ref, t1_ref, t2_ref, o_ref, w_bf, *, rope, scale):
    @pl.when(pl.program_id(1) == 0)
    def _():
        w_bf[...] = w_ref[...].astype(BF16)

    bm, bn = o_ref.shape
    half = bm // 2
    for h in range(2):
        rows = pl.ds(h * half, half)
        acc = _dot(x_ref[rows, :], w_bf[...])
        if rope == HEAD_DIM:
            t0, t1 = t0_ref[rows, :] * scale, t1_ref[rows, :] * scale
        elif rope == IDX_DIM:
            t0, t1, t2 = t0_ref[rows, :], t1_ref[rows, :], t2_ref[rows, :]
        for c in range(bn // LANES):
            a = acc[:, c * LANES:(c + 1) * LANES]
            if rope == HEAD_DIM:
                a = _rope128(a, t0, t1)
            elif rope == IDX_DIM:
                a = _rope64(a, t0, t1, t2)
            o_ref[rows, c * LANES:(c + 1) * LANES] = a.astype(o_ref.dtype)


def _inproj_rope(x_bf, w, tabs, *, layer, name, col_off_blocks, n, rope, scale=1.0, bm, bn):
    m, d = x_bf.shape
    tab_spec = pl.BlockSpec((bm, LANES), lambda j, i: (i, 0))
    return pl.pallas_call(
        functools.partial(_inproj_kernel, rope=rope, scale=scale), grid=(n // bn, m // bm),
        in_specs=[pl.BlockSpec((bm, d), lambda j, i: (i, 0)),
                  pl.BlockSpec((None, d, bn), lambda j, i: (layer, 0, j + col_off_blocks))] + [tab_spec] * 3,
        out_specs=pl.BlockSpec((bm, bn), lambda j, i: (i, j)),
        out_shape=jax.ShapeDtypeStruct((m, n), BF16),
        scratch_shapes=[pltpu.VMEM((d, bn), BF16)],
        compiler_params=_cparams(2), name=name,
    )(x_bf, w, *tabs)


def _inproj_idx_kernel(x_ref, w_ref, cos64_ref, sinlo_ref, sinhi_ref, k2_ref, kw_ref, *, wi_scale):
    a = _dot(x_ref[...], w_ref[...].astype(BF16))
    r = _rope64(a, cos64_ref[...], sinlo_ref[...], sinhi_ref[...])
    lane = lax.broadcasted_iota(I32, a.shape, 1)
    kw_ref[...] = jnp.where(lane < IDX_DIM, r, a * wi_scale)
    k2_ref[...] = jnp.where(lane < IDX_DIM, r, pltpu.roll(r, 64, 1)).astype(k2_ref.dtype)


def _inproj_idx(x_bf, w_kw, tables, *, bm):
    m, d = x_bf.shape
    wi_scale = (N_IDX_HEADS ** -0.5) * (IDX_DIM ** -0.5)
    tab_spec = pl.BlockSpec((bm, LANES), lambda i: (i, 0))
    return pl.pallas_call(
        functools.partial(_inproj_idx_kernel, wi_scale=wi_scale), grid=(m // bm,),
        in_specs=[pl.BlockSpec((bm, d), lambda i: (i, 0)), pl.BlockSpec((d, LANES), lambda i: (0, 0))]
        + [tab_spec] * 3,
        out_specs=[tab_spec, tab_spec],
        out_shape=[jax.ShapeDtypeStruct((m, LANES), BF16), jax.ShapeDtypeStruct((m, LANES), F32)],
        compiler_params=_cparams(1), name="inproj_idx",
    )(x_bf, w_kw, *tables[2:])


def _pool_kernel(v_ref, pw_ref, ps_ref, o_ref, ext_ref):
    s = pl.program_id(1)
    ts = v_ref.shape[0]

    @pl.when(s == 0)
    def _():
        ext_ref[0:POOL_HALO, :] = jnp.zeros((POOL_HALO, ext_ref.shape[1]), F32)

    ext_ref[POOL_HALO:, :] = v_ref[...]
    t = s * ts + lax.broadcasted_iota(I32, (ts, 1), 0)
    for g, w in enumerate(POOL_WINDOWS):
        cols = slice(g * POOL_GW, (g + 1) * POOL_GW)
        e = ext_ref[:, cols]
        step = 1
        while step < w:
            e = e + pltpu.roll(e, step, 0)
            step *= 2
        win = e[POOL_HALO:, :]
        cnt = jnp.minimum(t + 1, w).astype(F32)
        pooled = win / cnt - v_ref[:, cols]
        mixed = _dot(pooled.astype(BF16), pw_ref[g])
        o_ref[:, cols] = (mixed * ps_ref[:, cols]).astype(o_ref.dtype)
    ext_ref[0:POOL_HALO, :] = v_ref[ts - POOL_HALO:, :]


def _pool_mixer(v_pool, pool_w_bf, pool_scale, *, batch, seq, ts):
    v3 = v_pool.reshape(batch, seq, MIX_POOL)
    out = pl.pallas_call(
        _pool_kernel, grid=(batch, seq // ts),
        in_specs=[pl.BlockSpec((None, ts, MIX_POOL), lambda b, s: (b, s, 0)),
                  pl.BlockSpec((N_POOL_GROUPS, POOL_GW, POOL_GW), lambda b, s: (0, 0, 0)),
                  pl.BlockSpec((1, MIX_POOL), lambda b, s: (0, 0))],
        out_specs=pl.BlockSpec((None, ts, MIX_POOL), lambda b, s: (b, s, 0)),
        out_shape=jax.ShapeDtypeStruct((batch, seq, MIX_POOL), BF16),
        scratch_shapes=[pltpu.VMEM((POOL_HALO + ts, MIX_POOL), F32)],
        compiler_params=_cparams(2), name="pool_mixer",
    )(v3, pool_w_bf, pool_scale.reshape(1, MIX_POOL))
    return out.reshape(batch * seq, MIX_POOL)


def _dsa_kernel(q_ref, k_ref, v_ref, k2_ref, qi_ref, kw_ref, o_ref,
                key_sc, bias_sc, xi_sc, qt_sc, m_sc, l_sc, acc_sc, s_sc, cmax_sc, s2_sc, cmax2_sc, *, n_sel, seq):
    blk = pl.program_id(1)
    nchunk = blk // (LC // QB) + 1
    q_pos = blk * QB + lax.broadcasted_iota(I32, (1, QB), 1)
    k_iota = lax.broadcasted_iota(I32, (LC, 1), 0)
    sub = lax.broadcasted_iota(I32, (LANES, QB), 0)
    eye = jnp.where(sub == lax.broadcasted_iota(I32, (LANES, QB), 1), 1.0, 0.0).astype(BF16)

    def transposed(x):
        return _dot_nt(eye, x)

    for jj in range(N_IDX_HEADS // 2):
        xt = transposed(qi_ref[:, jj * LANES:(jj + 1) * LANES])
        xi_sc[jj, :, 0:QB] = jnp.where(sub < IDX_DIM, xt, 0.0).astype(BF16)
        xi_sc[jj, :, QB:2 * QB] = jnp.where(sub >= IDX_DIM, xt, 0.0).astype(BF16)
    for g in range(N_KV_HEADS):
        for r in range(Q_PER_KV):
            h = g * Q_PER_KV + r
            qt_sc[g, 0:HEAD_DIM, r * QB:(r + 1) * QB] = transposed(
                q_ref[:, h * HEAD_DIM:(h + 1) * HEAD_DIM]).astype(BF16)
            qt_sc[g, HEAD_DIM:, r * QB:(r + 1) * QB] = eye
    kw_t = kw_ref[...].T
    wi = [kw_t[IDX_DIM + h:IDX_DIM + h + 1, :] for h in range(N_IDX_HEADS)]

    inf = jnp.inf
    part = (FOLD_ROWS, QB)

    def score_chunk(c, carry):
        s_min, s_max = carry
        off = pl.multiple_of(c * LC, LC)
        k2c = k2_ref[pl.ds(off, LC), :]
        sc = jnp.zeros((LC, QB), F32)
        for jj in range(N_IDX_HEADS // 2):
            rel = jnp.maximum(_dot(k2c, xi_sc[jj]), 0.0)
            sc = sc + rel[:, 0:QB] * wi[2 * jj] + rel[:, QB:2 * QB] * wi[2 * jj + 1]
        causal = off + k_iota <= q_pos
        key_sc[pl.ds(off, LC), :] = jnp.where(causal, sc, -inf)
        s_min = jnp.minimum(s_min, _fold_rows(jnp.where(causal, sc, inf), jnp.min))
        s_max = jnp.maximum(s_max, _fold_rows(jnp.where(causal, sc, -inf), jnp.max))
        return s_min, s_max

    s_min, s_max = lax.fori_loop(0, nchunk, score_chunk, (jnp.full(part, inf, F32), jnp.full(part, -inf, F32)))

    n_sel_chunk = (blk * QB + QB + SC - 1) // SC
    s_iota = lax.broadcasted_iota(I32, (SC, 1), 0)

    def count(pred_fn):
        def body(c, acc):
            off = pl.multiple_of(c * SC, SC)
            m = pred_fn(key_sc[pl.ds(off, SC), :], off)
            return acc + _fold_rows(m, jnp.sum)
        acc = lax.fori_loop(0, n_sel_chunk, body, jnp.zeros(part, F32))
        return jnp.sum(acc, axis=0, keepdims=True)

    has_thr = q_pos + 1 >= int(n_sel)
    lo0 = jnp.where(has_thr, jnp.min(s_min, axis=0, keepdims=True), -inf)
    hi0 = jnp.where(has_thr, jnp.max(s_max, axis=0, keepdims=True), -inf)

    def midpoint(lo, hi):
        mid = 0.5 * (lo + hi)
        return jnp.where(mid <= lo, hi, mid)

    def halve(i, st):
        lo, hi = st
        mid = midpoint(lo, hi)
        enough = count(lambda sc, off: jnp.where(sc >= mid, 1.0, 0.0)) >= n_sel
        return jnp.where(enough, mid, lo), jnp.where(enough, hi, mid)

    lo0, hi0 = lax.fori_loop(0, N_HALVINGS, halve, (lo0, hi0))

    def open_rows(lo, hi):
        return jnp.max(jnp.where(lo < hi, 1.0, 0.0)) > 0.0

    def bisect_cond(st):
        it, lo, hi = st
        return open_rows(lo, hi) & (it < seq)

    def bisect_body(st):
        it, lo, hi = st
        mid = midpoint(lo, hi)

        def body(c, acc):
            cnt, up, dn = acc
            off = pl.multiple_of(c * SC, SC)
            sc = key_sc[pl.ds(off, SC), :]
            ge = sc >= mid
            cnt = cnt + _fold_rows(jnp.where(ge, 1.0, 0.0), jnp.sum)
            up = jnp.minimum(up, _fold_rows(jnp.where(ge, sc, inf), jnp.min))
            dn = jnp.maximum(dn, _fold_rows(jnp.where(ge, -inf, sc), jnp.max))
            return cnt, up, dn

        cnt, up, dn = lax.fori_loop(
            0, n_sel_chunk, body, (jnp.zeros(part, F32), jnp.full(part, inf, F32), jnp.full(part, -inf, F32)))
        enough = jnp.sum(cnt, axis=0, keepdims=True) >= n_sel
        is_open = lo < hi
        new_lo = jnp.where(is_open & enough, jnp.min(up, axis=0, keepdims=True), lo)
        new_hi = jnp.where(is_open & jnp.logical_not(enough), jnp.max(dn, axis=0, keepdims=True), hi)
        return it + 1, new_lo, new_hi

    _, thr, _ = lax.while_loop(bisect_cond, bisect_body, (jnp.int32(0), lo0, hi0))
    n_gt = count(lambda kk, off: jnp.where(kk > thr, 1.0, 0.0))
    n_eq = count(lambda kk, off: jnp.where(kk == thr, 1.0, 0.0))
    need = n_sel - n_gt

    def tie_search():
        def tbody(i, p):
            cand = p | jnp.left_shift(jnp.int32(1), (seq.bit_length() - 2) - i)
            cnt = count(lambda kk, off: jnp.where(kk == thr, jnp.where(off + s_iota < cand, 1.0, 0.0), 0.0))
            return jnp.where(cnt < need, cand, p)
        return lax.fori_loop(0, seq.bit_length() - 1, tbody, jnp.zeros((1, QB), I32))

    ambiguous = jnp.max(jnp.where(has_thr & (n_eq > need), 1.0, 0.0)) > 0.0
    tie_hi = lax.cond(ambiguous, tie_search, lambda: jnp.full((1, QB), seq, I32))
    tie_hi = jnp.where(has_thr, tie_hi, -1)

    def bias_chunk(c, carry):
        off = pl.multiple_of(c * LC, LC)
        kk = key_sc[pl.ds(off, LC), :]
        tie_ok = jnp.where(off + k_iota <= tie_hi, 0.0, NEG)
        bias = jnp.where(kk > thr, 0.0, jnp.where(kk == thr, tie_ok, NEG))
        bias_sc[pl.ds(off, LC), :] = bias.astype(BF16)
        return carry

    lax.fori_loop(0, nchunk, bias_chunk, 0)

    m_sc[...] = jnp.full(m_sc.shape, NEG, F32)
    l_sc[...] = jnp.zeros(l_sc.shape, F32)
    acc_sc[...] = jnp.zeros(acc_sc.shape, F32)

    bufs = ((s_sc, cmax_sc), (s2_sc, cmax2_sc))

    def logits(c, g, buf):
        off = pl.multiple_of(c * LC, LC)
        kc = k_ref[pl.ds(off, LC), g * HEAD_DIM:(g + 1) * HEAD_DIM]
        s = _dot(jnp.concatenate([kc, bias_sc[pl.ds(off, LC), :]], axis=1), qt_sc[g])
        buf[0][g] = s
        buf[1][g] = jnp.max(_fold_rows(s, jnp.max), axis=0, keepdims=True)

    def accumulate(c, g, buf):
        off = pl.multiple_of(c * LC, LC)
        vc = v_ref[pl.ds(off, LC), g * HEAD_DIM:(g + 1) * HEAD_DIM]
        m_old = m_sc[g]
        m_new = jnp.maximum(m_old, buf[1][g])
        alpha = jnp.exp2(m_old - m_new)
        p = jnp.exp2(buf[0][g] - m_new)
        l_sc[g] = alpha * l_sc[g] + jnp.sum(_fold_rows(p, jnp.sum), axis=0, keepdims=True)
        acc_sc[g] = alpha * acc_sc[g] + _dot_tn(vc, p.astype(BF16))
        m_sc[g] = m_new

    def step(c, cur, nxt):
        for g in range(N_KV_HEADS):
            accumulate(c, g, cur)
            logits(c + 1, g, nxt)

    def last(c, cur):
        for g in range(N_KV_HEADS):
            accumulate(c, g, cur)

    for g in range(N_KV_HEADS):
        logits(0, g, bufs[0])
    n_pairs = (nchunk - 1) // 2

    def attn_pair(p, carry):
        step(2 * p, bufs[0], bufs[1])
        step(2 * p + 1, bufs[1], bufs[0])
        return carry

    lax.fori_loop(0, n_pairs, attn_pair, 0)
    c_tail = 2 * n_pairs

    @pl.when(nchunk - c_tail == 2)
    def _():
        step(c_tail, bufs[0], bufs[1])
        last(c_tail + 1, bufs[1])

    @pl.when(nchunk - c_tail == 1)
    def _():
        last(c_tail, bufs[0])

    for g in range(N_KV_HEADS):
        o_t = acc_sc[g] / l_sc[g]
        for r in range(Q_PER_KV):
            h = g * Q_PER_KV + r
            o_ref[:, h * HEAD_DIM:(h + 1) * HEAD_DIM] = o_t[:, r * QB:(r + 1) * QB].T.astype(o_ref.dtype)


def _dsa_attention(q, k, v, qi, k2, kw, *, batch, seq):
    nb = seq // QB
    n_sel = min(TOPK_MAX, seq // 4)
    q_cols = N_Q_HEADS * HEAD_DIM
    kv_cols = N_KV_HEADS * HEAD_DIM
    qi_cols = N_IDX_HEADS * IDX_DIM
    assert seq % LC == 0 and seq % SC == 0 and LC >= n_sel
    kern = functools.partial(_dsa_kernel, n_sel=float(n_sel), seq=seq)
    rows = Q_PER_KV * QB
    return pl.pallas_call(
        kern, grid=(batch, nb),
        in_specs=[pl.BlockSpec((QB, q_cols), lambda b, i: (b * nb + i, 0)),
                  pl.BlockSpec((seq, kv_cols), lambda b, i: (b, 0)),
                  pl.BlockSpec((seq, kv_cols), lambda b, i: (b, 0)),
                  pl.BlockSpec((seq, LANES), lambda b, i: (b, 0)),
                  pl.BlockSpec((QB, qi_cols), lambda b, i: (b * nb + i, 0)),
                  pl.BlockSpec((QB, LANES), lambda b, i: (b * nb + i, 0))],
        out_specs=pl.BlockSpec((QB, q_cols), lambda b, i: (b * nb + i, 0)),
        out_shape=jax.ShapeDtypeStruct((batch * seq, q_cols), BF16),
        scratch_shapes=[pltpu.VMEM((seq, QB), F32),
                        pltpu.VMEM((seq, QB), BF16),
                        pltpu.VMEM((N_IDX_HEADS // 2, LANES, 2 * QB), BF16),
                        pltpu.VMEM((N_KV_HEADS, HEAD_DIM + QB, rows), BF16),
                        pltpu.VMEM((N_KV_HEADS, 1, rows), F32),
                        pltpu.VMEM((N_KV_HEADS, 1, rows), F32),
                        pltpu.VMEM((N_KV_HEADS, HEAD_DIM, rows), F32),
                        pltpu.VMEM((N_KV_HEADS, LC, rows), F32), pltpu.VMEM((N_KV_HEADS, 1, rows), F32),
                        pltpu.VMEM((N_KV_HEADS, LC, rows), F32), pltpu.VMEM((N_KV_HEADS, 1, rows), F32)],
        compiler_params=_cparams(2), name="dsa_attention",
    )(q, k, v, k2, qi, kw)


def _ln_rows(x, g, b):
    mu = jnp.mean(x, axis=-1, keepdims=True)
    xc = x - mu
    var = jnp.mean(xc * xc, axis=-1, keepdims=True)
    return xc * lax.rsqrt(var + LN_EPS) * g + b


def _pack_halves(y):
    half = y.shape[1] // 2
    return pltpu.pack_elementwise([y[:, :half], y[:, half:]], packed_dtype=BF16)


def _unpack_halves(p):
    lo = pltpu.unpack_elementwise(p, index=0, packed_dtype=BF16, unpacked_dtype=F32)
    hi = pltpu.unpack_elementwise(p, index=1, packed_dtype=BF16, unpacked_dtype=F32)
    return jnp.concatenate([lo, hi], axis=1)


def _ln_kernel(x_ref, g_ref, b_ref, o_ref, o2_ref, *, packed):
    y = _ln_rows(x_ref[...], g_ref[...], b_ref[...])
    o_ref[...] = y
    o2_ref[...] = _pack_halves(y) if packed else y.astype(o2_ref.dtype)


def _layer_norm(x, g, b, *, bm, packed=False):
    m, d = x.shape
    row = pl.BlockSpec((bm, d), lambda i: (i, 0))
    vec = pl.BlockSpec((1, d), lambda i: (0, 0))
    second = ((m, d // 2), U32) if packed else ((m, d), BF16)
    return pl.pallas_call(
        functools.partial(_ln_kernel, packed=packed), grid=(m // bm,), in_specs=[row, vec, vec],
        out_specs=[row, pl.BlockSpec((bm, second[0][1]), lambda i: (i, 0))],
        out_shape=[jax.ShapeDtypeStruct((m, d), F32), jax.ShapeDtypeStruct(*second)],
        compiler_params=_cparams(1), name="layer_norm",
    )(x, g.reshape(1, d), b.reshape(1, d))


def _xattn_kernel(q_ref, k_ref, v_ref, o_ref):
    dh = q_ref.shape[1] // N_MEM_HEADS
    for h in range(N_MEM_HEADS):
        cols = slice(h * dh, (h + 1) * dh)
        s = _dot_nt(q_ref[:, cols], k_ref[:, cols])
        p = jnp.exp(s - jnp.max(s, axis=1, keepdims=True))
        o = _dot(p.astype(BF16), v_ref[:, cols]) / jnp.sum(p, axis=1, keepdims=True)
        o_ref[:, cols] = o.astype(o_ref.dtype)


def _cross_attention(q, k, v, *, batch, seq, n_mem, bm):
    d = q.shape[1]
    nb = seq // bm
    return pl.pallas_call(
        _xattn_kernel, grid=(batch, nb),
        in_specs=[pl.BlockSpec((bm, d), lambda b, i: (b * nb + i, 0)),
                  pl.BlockSpec((n_mem, d), lambda b, i: (b, 0)),
                  pl.BlockSpec((n_mem, d), lambda b, i: (b, 0))],
        out_specs=pl.BlockSpec((bm, d), lambda b, i: (b * nb + i, 0)),
        out_shape=jax.ShapeDtypeStruct((batch * seq, d), BF16),
        compiler_params=_cparams(2), name="cross_attention",
    )(q, k, v)


def _router_kernel(x_ref, w_ref, b_ref, id_ref, gate_ref):
    logits = jnp.dot(x_ref[...], w_ref[...], preferred_element_type=F32,
                     precision=lax.Precision.HIGHEST) + b_ref[...]
    lane = lax.broadcasted_iota(I32, logits.shape, 1)
    lane_f = lane.astype(F32)
    big = float(LANES)
    is_g = lane < N_GROUPS
    gl = jnp.where(is_g, logits, -jnp.inf)
    g_max = jnp.max(gl, axis=1, keepdims=True)
    g_idx = jnp.min(jnp.where(gl == g_max, lane_f, big), axis=1, keepdims=True)
    g_gate = 1.0 / jnp.sum(jnp.where(is_g, jnp.exp(gl - g_max), 0.0), axis=1, keepdims=True)
    e_lo = N_GROUPS + g_idx * EXPERTS_PER_GROUP
    in_grp = (lane_f >= e_lo) & (lane_f < e_lo + EXPERTS_PER_GROUP)
    el = jnp.where(in_grp, logits, -jnp.inf)
    v1 = jnp.max(el, axis=1, keepdims=True)
    i1 = jnp.min(jnp.where(el == v1, lane_f, big), axis=1, keepdims=True)
    el2 = jnp.where(lane_f == i1, -jnp.inf, el)
    v2 = jnp.max(el2, axis=1, keepdims=True)
    i2 = jnp.min(jnp.where(el2 == v2, lane_f, big), axis=1, keepdims=True)
    z = jnp.exp(v2 - v1)
    w1 = g_gate / (1.0 + z)
    w2 = g_gate * z / (1.0 + z)
    ids = jnp.where(lane == 0, i1 - N_GROUPS, jnp.where(lane == 1, i2 - N_GROUPS, 0.0))
    id_ref[...] = ids.astype(I32)
    gate_ref[...] = jnp.where(lane == 0, w1, jnp.where(lane == 1, w2, 0.0))


def _router(x, w_r, b_r, *, bm):
    m, d = x.shape
    out = pl.BlockSpec((bm, LANES), lambda i: (i, 0))
    return pl.pallas_call(
        _router_kernel, grid=(m // bm,),
        in_specs=[pl.BlockSpec((bm, d), lambda i: (i, 0)),
                  pl.BlockSpec((d, LANES), lambda i: (0, 0)),
                  pl.BlockSpec((1, LANES), lambda i: (0, 0))],
        out_specs=[out, out],
        out_shape=[jax.ShapeDtypeStruct((m, LANES), I32), jax.ShapeDtypeStruct((m, LANES), F32)],
        compiler_params=_cparams(1), name="router",
    )(x, w_r, b_r)


def _rank_kernel(id_ref, rank_ref, cnt_ref, carry_ref):
    i = pl.program_id(0)
    bt = id_ref.shape[0]

    @pl.when(i == 0)
    def _():
        carry_ref[...] = jnp.zeros(carry_ref.shape, F32)

    ids = id_ref[...]
    lane = lax.broadcasted_iota(I32, (bt, LANES), 1)
    e0 = ids[:, 0:1]
    e1 = ids[:, 1:2] + N_EXPERTS
    hit0 = lane == e0
    hit1 = lane == e1
    onehot = jnp.where(hit0, 1.0, jnp.where(hit1, 1.0, 0.0))
    r_io = lax.broadcasted_iota(I32, (bt, bt), 0)
    c_io = lax.broadcasted_iota(I32, (bt, bt), 1)
    tri = jnp.where(c_io < r_io, 1.0, 0.0).astype(BF16)
    prefix = _dot(tri, onehot.astype(BF16)) + carry_ref[0:1, :]
    rank0 = jnp.sum(jnp.where(hit0, prefix, 0.0), axis=1, keepdims=True)
    rank1 = jnp.sum(jnp.where(hit1, prefix, 0.0), axis=1, keepdims=True)
    rank_ref[...] = jnp.where(lane == 0, rank0, jnp.where(lane == 1, rank1, 0.0))
    total = carry_ref[0:1, :] + jnp.sum(onehot, axis=0, keepdims=True)
    carry_ref[...] = jnp.broadcast_to(total, carry_ref.shape)
    cnt_ref[...] = jnp.broadcast_to(total, cnt_ref.shape)


def _slot_ranks(ids, *, bt):
    m = ids.shape[0]
    return pl.pallas_call(
        _rank_kernel, grid=(m // bt,),
        in_specs=[pl.BlockSpec((bt, LANES), lambda i: (i, 0))],
        out_specs=[pl.BlockSpec((bt, LANES), lambda i: (i, 0)),
                   pl.BlockSpec((8, LANES), lambda i: (0, 0))],
        out_shape=[jax.ShapeDtypeStruct((m, LANES), F32), jax.ShapeDtypeStruct((8, LANES), F32)],
        scratch_shapes=[pltpu.VMEM((8, LANES), F32)],
        compiler_params=_cparams(1), name="slot_ranks",
    )(ids)


def _pos_kernel(id_ref, rank_ref, tab_ref, pos_ref):
    ids = id_ref[...]
    ranks = rank_ref[...]
    lane = lax.broadcasted_iota(I32, ids.shape, 1)
    tab = tab_ref[...]
    base0 = jnp.sum(jnp.where(lane == ids[:, 0:1], tab, 0.0), axis=1, keepdims=True)
    base1 = jnp.sum(jnp.where(lane == ids[:, 1:2] + N_EXPERTS, tab, 0.0), axis=1, keepdims=True)
    pos = jnp.where(lane == 0, base0 + ranks[:, 0:1], jnp.where(lane == 1, base1 + ranks[:, 1:2], 0.0))
    pos_ref[...] = pos.astype(I32)


def _slot_positions(ids, ranks, table, *, bt):
    m = ids.shape[0]
    blk = pl.BlockSpec((bt, LANES), lambda i: (i, 0))
    return pl.pallas_call(
        _pos_kernel, grid=(m // bt,),
        in_specs=[blk, blk, pl.BlockSpec((1, LANES), lambda i: (0, 0))],
        out_specs=blk,
        out_shape=jax.ShapeDtypeStruct((m, LANES), I32),
        compiler_params=_cparams(1), name="slot_positions",
    )(ids, ranks, table)


def _row_copy(src_ref, src_row, dst_ref, dst_row, sem):
    return pltpu.make_async_copy(src_ref.at[pl.ds(src_row, 1)], dst_ref.at[pl.ds(dst_row, 1)], sem)


def _dispatch_kernel(pos0_ref, pos1_ref, x_ref, xs_in_ref, xs_ref, sem):
    del xs_in_ref
    i = pl.program_id(0)
    bt = x_ref.shape[0]

    def issue(r, carry):
        t = i * bt + r
        _row_copy(x_ref, r, xs_ref, pos0_ref[t], sem.at[0]).start()
        _row_copy(x_ref, r, xs_ref, pos1_ref[t], sem.at[1]).start()
        return carry

    lax.fori_loop(0, bt, issue, 0)
    for k in range(2):
        pltpu.make_async_copy(x_ref, xs_ref.at[pl.ds(0, bt)], sem.at[k]).wait()


def _dispatch(x, pos0, pos1, n_slots, *, bt):
    m, d = x.shape
    xs0 = jnp.zeros((n_slots, d), x.dtype)
    return pl.pallas_call(
        _dispatch_kernel,
        grid_spec=pltpu.PrefetchScalarGridSpec(
            num_scalar_prefetch=2, grid=(m // bt,),
            in_specs=[pl.BlockSpec((bt, d), lambda i, p0, p1: (i, 0)),
                      pl.BlockSpec(memory_space=pl.ANY)],
            out_specs=pl.BlockSpec(memory_space=pl.ANY),
            scratch_shapes=[pltpu.SemaphoreType.DMA((2,))]),
        out_shape=jax.ShapeDtypeStruct((n_slots, d), x.dtype),
        input_output_aliases={3: 0},
        compiler_params=_cparams(1), name="dispatch",
    )(pos0, pos1, x, xs0)


def _ffn_kernel(blk_e_ref, nxt_e_ref, set_ref, nused_ref, x_ref, wg_hbm, wu_hbm, wd_hbm, y_ref,
                wg_bf, wu_bf, wd_bf, st_g, st_u, st_d, sem):
    i = pl.program_id(0)
    d, ff = wg_bf.shape[1], wg_bf.shape[2]
    rg, rd = d // W_GROUPS, ff // W_GROUPS

    n_slot = st_g.shape[0]

    def group_copies(e, g):
        slot = g % n_slot
        return (pltpu.make_async_copy(wg_hbm.at[e, pl.ds(g * rg, rg), :], st_g.at[slot], sem.at[0, slot]),
                pltpu.make_async_copy(wu_hbm.at[e, pl.ds(g * rg, rg), :], st_u.at[slot], sem.at[1, slot]),
                pltpu.make_async_copy(wd_hbm.at[e, pl.ds(g * rd, rd), :], st_d.at[slot], sem.at[2, slot]))

    def start(e, g):
        for c in group_copies(e, g):
            c.start()

    def finish(e, g, s):
        for c in group_copies(e, g):
            c.wait()
        slot = g % n_slot
        wg_bf[s, g * rg:(g + 1) * rg, :] = st_g[slot].astype(BF16)
        wu_bf[s, g * rg:(g + 1) * rg, :] = st_u[slot].astype(BF16)
        wd_bf[s, g * rd:(g + 1) * rd, :] = st_d[slot].astype(BF16)

    def load_expert(e, s, between=()):
        for g in range(n_slot):
            start(e, g)
        for g in range(W_GROUPS):
            if g < len(between):
                between[g]()
            finish(e, g, s)
            if g + n_slot < W_GROUPS:
                start(e, g + n_slot)

    e = blk_e_ref[i]
    s = set_ref[i]
    nxt = nxt_e_ref[i]
    first_of_expert = jnp.logical_or(i == 0, blk_e_ref[jnp.maximum(i - 1, 0)] != e)
    active = i < nused_ref[0]
    prefetch = jnp.logical_and(jnp.logical_and(active, first_of_expert), nxt >= 0)

    @pl.when(i == 0)
    def _():
        load_expert(e, s)

    def ffn_steps():
        vals = {}

        def gate_step():
            vals["x"] = _unpack_halves(x_ref[...]).astype(BF16)
            vals["gate"] = _dot(vals["x"], wg_bf[s])

        def up_step():
            gate = vals["gate"]
            vals["h"] = (gate * (1.0 / (1.0 + jnp.exp(-gate))) * _dot(vals["x"], wu_bf[s])).astype(BF16)

        def down_step():
            y_ref[...] = _pack_halves(_dot(vals["h"], wd_bf[s]))

        return gate_step, up_step, down_step

    @pl.when(prefetch)
    def _():
        load_expert(nxt, 1 - s, between=ffn_steps())

    @pl.when(jnp.logical_and(active, jnp.logical_not(prefetch)))
    def _():
        for step in ffn_steps():
            step()


def _expert_ffn(xs, blk_e, nxt_e, set_idx, nused, wg, wu, wd):
    n_slots, dp = xs.shape
    d, ff = wg.shape[1], wg.shape[2]
    nblk = n_slots // MOE_BLK
    assert dp * 2 == d and d % W_GROUPS == 0 and ff % W_GROUPS == 0 and W_GROUPS >= W_STAGE_SLOTS > 3

    def row_map(i, be, nx, si, nu):
        return (jnp.minimum(i, nu[0] - 1), 0)

    hbm = pl.BlockSpec(memory_space=pl.ANY)
    return pl.pallas_call(
        _ffn_kernel,
        grid_spec=pltpu.PrefetchScalarGridSpec(
            num_scalar_prefetch=4, grid=(nblk,),
            in_specs=[pl.BlockSpec((MOE_BLK, dp), row_map), hbm, hbm, hbm],
            out_specs=pl.BlockSpec((MOE_BLK, dp), row_map),
            scratch_shapes=[pltpu.VMEM((2, d, ff), BF16), pltpu.VMEM((2, d, ff), BF16), pltpu.VMEM((2, ff, d), BF16),
                            pltpu.VMEM((W_STAGE_SLOTS, d // W_GROUPS, ff), F32),
                            pltpu.VMEM((W_STAGE_SLOTS, d // W_GROUPS, ff), F32),
                            pltpu.VMEM((W_STAGE_SLOTS, ff // W_GROUPS, d), F32),
                            pltpu.SemaphoreType.DMA((3, W_STAGE_SLOTS))]),
        out_shape=jax.ShapeDtypeStruct((n_slots, dp), xs.dtype),
        input_output_aliases={4: 0},
        compiler_params=_cparams(1), name="expert_ffn",
    )(blk_e, nxt_e, set_idx, nused, xs, wg, wu, wd)


def _combine_kernel(pos0_ref, pos1_ref, x_ref, gate_ref, g_ref, b_ref, ys_ref, o_ref, y0_ref, y1_ref, sem):
    i = pl.program_id(0)
    bt = x_ref.shape[0]
    slot = i % 2

    def issue(step, to_slot):
        def body(r, carry):
            t = step * bt + r
            _row_copy(ys_ref, pos0_ref[t], y0_ref.at[to_slot], r, sem.at[0, to_slot]).start()
            _row_copy(ys_ref, pos1_ref[t], y1_ref.at[to_slot], r, sem.at[1, to_slot]).start()
            return carry
        lax.fori_loop(0, bt, body, 0)

    @pl.when(i == 0)
    def _():
        issue(0, 0)

    @pl.when(i + 1 < pl.num_programs(0))
    def _():
        issue(i + 1, 1 - slot)

    pltpu.make_async_copy(ys_ref.at[pl.ds(0, bt)], y0_ref.at[slot], sem.at[0, slot]).wait()
    pltpu.make_async_copy(ys_ref.at[pl.ds(0, bt)], y1_ref.at[slot], sem.at[1, slot]).wait()
    gates = gate_ref[...]
    f = _unpack_halves(y0_ref[slot]) * gates[:, 0:1] + _unpack_halves(y1_ref[slot]) * gates[:, 1:2]
    o_ref[...] = _ln_rows(ALPHA * x_ref[...] + f, g_ref[...], b_ref[...])


def _combine(x, gates, ys, pos0, pos1, g, b, *, bt):
    m, d = x.shape
    dp = ys.shape[1]
    row = pl.BlockSpec((bt, d), lambda i, p0, p1: (i, 0))
    vec = pl.BlockSpec((1, d), lambda i, p0, p1: (0, 0))
    return pl.pallas_call(
        _combine_kernel,
        grid_spec=pltpu.PrefetchScalarGridSpec(
            num_scalar_prefetch=2, grid=(m // bt,),
            in_specs=[row, pl.BlockSpec((bt, LANES), lambda i, p0, p1: (i, 0)), vec, vec,
                      pl.BlockSpec(memory_space=pl.ANY)],
            out_specs=row,
            scratch_shapes=[pltpu.VMEM((2, bt, dp), ys.dtype), pltpu.VMEM((2, bt, dp), ys.dtype),
                            pltpu.SemaphoreType.DMA((2, 2))]),
        out_shape=jax.ShapeDtypeStruct((m, d), F32),
        compiler_params=_cparams(1), name="combine_ln",
    )(pos0, pos1, x, gates, g.reshape(1, d), b.reshape(1, d), ys)


def _rope_tables(positions, n_rows):
    pos = positions.astype(F32).reshape(n_rows, 1)

    def cs(dim):
        inv = 1.0 / (ROPE_THETA ** (jnp.arange(0, dim, 2, dtype=F32) / dim))
        ang = pos * inv[None, :]
        return jnp.cos(ang), jnp.sin(ang)

    c_h, s_h = cs(HEAD_DIM)
    c_i, s_i = cs(IDX_DIM)
    z_i = jnp.zeros_like(s_i)
    cosf = jnp.concatenate([c_h, c_h], axis=1)
    sinf = jnp.concatenate([-s_h, s_h], axis=1)
    cos64 = jnp.concatenate([c_i, c_i, c_i, c_i], axis=1)
    sin_lo = jnp.concatenate([-s_i, z_i, -s_i, z_i], axis=1)
    sin_hi = jnp.concatenate([z_i, s_i, z_i, s_i], axis=1)
    return cosf, sinf, cos64, sin_lo, sin_hi


def kernel(x, mem, positions, w_in, pool_w, pool_scale, w_o, ln1_g, ln1_b, w_mq, w_mk, w_mv, w_mo, ln2_g, ln2_b,
           w_group_router, b_group_router, w_expert_router, b_expert_router, w_gate, w_up, w_down, ln3_g, ln3_b):
    batch, seq, d = x.shape
    n_mem = mem.shape[1]
    n = batch * seq
    bn_attn = N_KV_HEADS * HEAD_DIM
    tables = _rope_tables(positions, n)
    xf = x.reshape(n, d)
    n_slots = 2 * n + N_EXPERTS * MOE_BLK
    nblk = n_slots // MOE_BLK

    for l in range(w_in.shape[0]):
        x_bf = xf.astype(BF16)
        kw_col = MIX_POOL + 7 * bn_attn
        w_kw = jnp.pad(w_in[l][:, kw_col:], ((0, 0), (0, LANES - (w_in.shape[2] - kw_col))))

        v_pool = _matmul([x_bf], [w_in], layer=l, name="inproj_pool", bm=1024, bn=DENSE_BN, n=MIX_POOL)
        blk0 = MIX_POOL // bn_attn
        q_cols = N_Q_HEADS * HEAD_DIM
        q_scale = HEAD_DIM ** -0.5 * LOG2E
        rope_h, rope_i = tables[:2] + tables[:1], tables[2:]
        proj = functools.partial(_inproj_rope, x_bf, w_in, layer=l, bm=1024, bn=bn_attn)
        q = proj(rope_h, name="inproj_q", col_off_blocks=blk0, n=q_cols, rope=HEAD_DIM, scale=q_scale)
        k = proj(rope_h, name="inproj_k", col_off_blocks=blk0 + 4, n=bn_attn, rope=HEAD_DIM)
        v = proj(rope_h, name="inproj_v", col_off_blocks=blk0 + 5, n=bn_attn, rope=None)
        qi = proj(rope_i, name="inproj_qi", col_off_blocks=blk0 + 6, n=bn_attn, rope=IDX_DIM)
        k2, kw = _inproj_idx(x_bf, w_kw, tables, bm=1024)
        a_pool = _pool_mixer(v_pool, pool_w[l].astype(BF16), pool_scale[l], batch=batch, seq=seq, ts=512)
        a_attn = _dsa_attention(q, k, v, qi, k2, kw, batch=batch, seq=seq)
        res = (xf, (512, DENSE_BN), lambda i, j: (i, j))
        pre = _matmul([a_pool, a_attn], [w_o, w_o], layer=l, name="outproj", bm=512, bn=DENSE_BN, n=d,
                      extras=(res,), epilogue=_residual_epilogue)
        x1, x1_bf = _layer_norm(pre, ln1_g[l], ln1_b[l], bm=256)

        mq_scale = (d // N_MEM_HEADS) ** -0.5

        def q_epilogue(acc, ex, outs):
            outs[0][...] = (acc * mq_scale).astype(outs[0].dtype)

        qm = _matmul([x1_bf], [w_mq], layer=l, name="mem_q", bm=1024, bn=DENSE_BN, n=d, out_dtype=BF16,
                     epilogue=q_epilogue)
        mem_bf = mem.reshape(batch * n_mem, d).astype(BF16)
        km = _matmul([mem_bf], [w_mk], layer=l, name="mem_k", bm=batch * n_mem, bn=DENSE_BN, n=d, out_dtype=BF16)
        vm = _matmul([mem_bf], [w_mv], layer=l, name="mem_v", bm=batch * n_mem, bn=DENSE_BN, n=d, out_dtype=BF16)
        om = _cross_attention(qm, km, vm, batch=batch, seq=seq, n_mem=n_mem, bm=512)
        res = (x1, (512, DENSE_BN), lambda i, j: (i, j))
        pre = _matmul([om], [w_mo], layer=l, name="mem_o", bm=512, bn=DENSE_BN, n=d, extras=(res,),
                      epilogue=_residual_epilogue)
        x2, x2_rows = _layer_norm(pre, ln2_g[l], ln2_b[l], bm=256, packed=True)

        w_r = jnp.pad(jnp.concatenate([w_group_router[l], w_expert_router[l]], axis=1),
                      ((0, 0), (0, LANES - N_GROUPS - N_EXPERTS)))
        b_r = jnp.pad(jnp.concatenate([b_group_router[l], b_expert_router[l]]),
                      (0, LANES - N_GROUPS - N_EXPERTS)).reshape(1, LANES)
        ids, gates = _router(x2, w_r, b_r, bm=512)
        ranks, totals = _slot_ranks(ids, bt=512)
        c0 = totals[0, :N_EXPERTS].astype(I32)
        c1 = totals[0, N_EXPERTS:].astype(I32)
        padded = ((c0 + c1 + MOE_BLK - 1) // MOE_BLK) * MOE_BLK
        pend = jnp.cumsum(padded)
        pstart = pend - padded
        table = jnp.concatenate([pstart, pstart + c0]).astype(F32).reshape(1, LANES)
        pos = _slot_positions(ids, ranks, table, bt=512)
        pos0, pos1 = pos[:, 0], pos[:, 1]
        nused = pend[-1] // MOE_BLK
        blk_i = jnp.minimum(jnp.arange(nblk, dtype=I32), nused - 1)
        blk_e = jnp.sum((pend[None, :] <= (blk_i * MOE_BLK)[:, None]).astype(I32), axis=1)
        blk_e = jnp.minimum(blk_e, N_EXPERTS - 1)
        later = blk_e[None, :] > blk_e[:, None]
        nxt_e = jnp.min(jnp.where(later, blk_e[None, :], N_EXPERTS), axis=1)
        nxt_e = jnp.where(nxt_e == N_EXPERTS, -1, nxt_e).astype(I32)
        new_e = jnp.concatenate([jnp.ones((1,), I32), (blk_e[1:] != blk_e[:-1]).astype(I32)])
        set_idx = (jnp.cumsum(new_e) - 1) % 2

        xs = _dispatch(x2_rows, pos0, pos1, n_slots, bt=256)
        ys = _expert_ffn(xs, blk_e, nxt_e, set_idx.astype(I32), nused.reshape(1).astype(I32),
                         w_gate[l], w_up[l], w_down[l])
        xf = _combine(x2, gates, ys, pos0, pos1, ln3_g[l], ln3_b[l], bt=256)
    return xf.reshape(batch, seq, d)
```

```python
import functools

import jax
import jax.numpy as jnp
from jax import lax
from jax.experimental import pallas as pl
from jax.experimental.pallas import tpu as pltpu

F32 = jnp.float32
BF16 = jnp.bfloat16
I32 = jnp.int32
U32 = jnp.uint32

MIX_POOL = 2048
N_POOL_GROUPS = 4
POOL_WINDOWS = (2, 4, 8, 16)
POOL_GW = MIX_POOL // N_POOL_GROUPS
HEAD_DIM = 128
N_Q_HEADS = 16
N_KV_HEADS = 4
Q_PER_KV = N_Q_HEADS // N_KV_HEADS
N_IDX_HEADS = 8
IDX_DIM = 64
TOPK_MAX = 256
ROPE_THETA = 10000.0
N_MEM_HEADS = 4
N_GROUPS = 8
EXPERTS_PER_GROUP = 8
N_EXPERTS = N_GROUPS * EXPERTS_PER_GROUP
LN_EPS = 1e-5
DEPTH = 1
ALPHA = (2.0 * DEPTH) ** 0.25

LANES = 128
QB = 128
LC = 512
SC = 256
DENSE_BN = 512
MOE_BLK = 256
W_GROUPS = 8
W_STAGE_SLOTS = 4
POOL_HALO = 16
NEG = -1e30
N_HALVINGS = 12
LOG2E = 1.4426950408889634
VMEM_LIMIT = 56 * 1024 * 1024


def _cparams(n_axes, vmem=VMEM_LIMIT):
    return pltpu.CompilerParams(dimension_semantics=("arbitrary",) * n_axes, vmem_limit_bytes=vmem)


def _dot(a, b):
    return jnp.dot(a, b, preferred_element_type=F32)


def _dot_nt(a, b):
    return lax.dot_general(a, b, (((1,), (1,)), ((), ())), preferred_element_type=F32)


def _dot_tn(a, b):
    return lax.dot_general(a, b, (((0,), (0,)), ((), ())), preferred_element_type=F32)


FOLD_ROWS = 32


def _fold_rows(x, reduce_fn):
    r, c = x.shape
    return reduce_fn(x.reshape(r // FOLD_ROWS, FOLD_ROWS, c), axis=0)


def _mm_kernel(*refs, n_lhs, n_extra, epilogue):
    lhs = refs[:n_lhs]
    ws = refs[n_lhs:2 * n_lhs]
    extras = refs[2 * n_lhs:2 * n_lhs + n_extra]
    outs = refs[2 * n_lhs + n_extra:-n_lhs]
    w_bf = refs[-n_lhs:]

    @pl.when(pl.program_id(1) == 0)
    def _():
        for l in range(n_lhs):
            w_bf[l][...] = ws[l][...].astype(BF16)

    acc = _dot(lhs[0][...], w_bf[0][...])
    for l in range(1, n_lhs):
        acc = acc + _dot(lhs[l][...], w_bf[l][...])
    epilogue(acc, extras, outs)


def _matmul(lhs_list, w_list, *, layer, name, bm, bn, n, w_col_off=0, extras=(), out_dtype=F32, epilogue=None):
    m = lhs_list[0].shape[0]
    grid = (n // bn, m // bm)
    in_specs = [pl.BlockSpec((bm, a.shape[1]), lambda j, i: (i, 0)) for a in lhs_list]
    in_specs += [pl.BlockSpec((None, a.shape[1], bn), (lambda j, i, l=l: (layer, l, j + w_col_off)))
                 for l, a in enumerate(lhs_list)]
    in_specs += [pl.BlockSpec(blk, (lambda j, i, f=f: f(i, j))) for (_, blk, f) in extras]
    if epilogue is None:
        def epilogue(acc, ex, outs):
            outs[0][...] = acc.astype(outs[0].dtype)
    kern = functools.partial(_mm_kernel, n_lhs=len(lhs_list), n_extra=len(extras), epilogue=epilogue)
    return pl.pallas_call(
        kern, grid=grid, in_specs=in_specs,
        out_specs=pl.BlockSpec((bm, bn), lambda j, i: (i, j)),
        out_shape=jax.ShapeDtypeStruct((m, n), out_dtype),
        scratch_shapes=[pltpu.VMEM((a.shape[1], bn), BF16) for a in lhs_list],
        compiler_params=_cparams(2), name=name,
    )(*lhs_list, *w_list, *[e[0] for e in extras])


def _residual_epilogue(acc, ex, outs):
    outs[0][...] = acc + ALPHA * ex[0][...]


def _rope128(a, cosf, sinf):
    return a * cosf + pltpu.roll(a, 64, 1) * sinf


def _rope64(a, cos64, sin_lo, sin_hi):
    return a * cos64 + pltpu.roll(a, 96, 1) * sin_lo + pltpu.roll(a, 32, 1) * sin_hi


def _inproj_kernel(x_ref, w_ref, t0_ref, t1_ref, t2_ref, o_ref, w_bf, *, rope, scale):
    @pl.when(pl.program_id(1) == 0)
    def _():
        w_bf[...] = w_ref[...].astype(BF16)

    bm, bn = o_ref.shape
    half = bm // 2
    for h in range(2):
        rows = pl.ds(h * half, half)
        acc = _dot(x_ref[rows, :], w_bf[...])
        if rope == HEAD_DIM:
            t0, t1 = t0_ref[rows, :] * scale, t1_ref[rows, :] * scale
        elif rope == IDX_DIM:
            t0, t1, t2 = t0_ref[rows, :], t1_ref[rows, :], t2_ref[rows, :]
        for c in range(bn // LANES):
            a = acc[:, c * LANES:(c + 1) * LANES]
            if rope == HEAD_DIM:
                a = _rope128(a, t0, t1)
            elif rope == IDX_DIM:
                a = _rope64(a, t0, t1, t2)
            o_ref[rows, c * LANES:(c + 1) * LANES] = a.astype(o_ref.dtype)


def _inproj_rope(x_bf, w, tabs, *, layer, name, col_off_blocks, n, rope, scale=1.0, bm, bn):
    m, d = x_bf.shape
    tab_spec = pl.BlockSpec((bm, LANES), lambda j, i: (i, 0))
    return pl.pallas_call(
        functools.partial(_inproj_kernel, rope=rope, scale=scale), grid=(n // bn, m // bm),
        in_specs=[pl.BlockSpec((bm, d), lambda j, i: (i, 0)),
                  pl.BlockSpec((None, d, bn), lambda j, i: (layer, 0, j + col_off_blocks))] + [tab_spec] * 3,
        out_specs=pl.BlockSpec((bm, bn), lambda j, i: (i, j)),
        out_shape=jax.ShapeDtypeStruct((m, n), BF16),
        scratch_shapes=[pltpu.VMEM((d, bn), BF16)],
        compiler_params=_cparams(2), name=name,
    )(x_bf, w, *tabs)


def _inproj_idx_kernel(x_ref, w_ref, cos64_ref, sinlo_ref, sinhi_ref, k2_ref, kw_ref, *, wi_scale):
    a = _dot(x_ref[...], w_ref[...].astype(BF16))
    r = _rope64(a, cos64_ref[...], sinlo_ref[...], sinhi_ref[...])
    lane = lax.broadcasted_iota(I32, a.shape, 1)
    kw_ref[...] = jnp.where(lane < IDX_DIM, r, a * wi_scale)
    k2_ref[...] = jnp.where(lane < IDX_DIM, r, pltpu.roll(r, 64, 1)).astype(k2_ref.dtype)


def _inproj_idx(x_bf, w_kw, tables, *, bm):
    m, d = x_bf.shape
    wi_scale = (N_IDX_HEADS ** -0.5) * (IDX_DIM ** -0.5)
    tab_spec = pl.BlockSpec((bm, LANES), lambda i: (i, 0))
    return pl.pallas_call(
        functools.partial(_inproj_idx_kernel, wi_scale=wi_scale), grid=(m // bm,),
        in_specs=[pl.BlockSpec((bm, d), lambda i: (i, 0)), pl.BlockSpec((d, LANES), lambda i: (0, 0))]
        + [tab_spec] * 3,
        out_specs=[tab_spec, tab_spec],
        out_shape=[jax.ShapeDtypeStruct((m, LANES), BF16), jax.ShapeDtypeStruct((m, LANES), F32)],
        compiler_params=_cparams(1), name="inproj_idx",
    )(x_bf, w_kw, *tables[2:])


def _pool_kernel(v_ref, pw_ref, ps_ref, o_ref, ext_ref):
    s = pl.program_id(1)
    ts = v_ref.shape[0]

    @pl.when(s == 0)
    def _():
        ext_ref[0:POOL_HALO, :] = jnp.zeros((POOL_HALO, ext_ref.shape[1]), F32)

    ext_ref[POOL_HALO:, :] = v_ref[...]
    t = s * ts + lax.broadcasted_iota(I32, (ts, 1), 0)
    for g, w in enumerate(POOL_WINDOWS):
        cols = slice(g * POOL_GW, (g + 1) * POOL_GW)
        e = ext_ref[:, cols]
        step = 1
        while step < w:
            e = e + pltpu.roll(e, step, 0)
            step *= 2
        win = e[POOL_HALO:, :]
        cnt = jnp.minimum(t + 1, w).astype(F32)
        pooled = win / cnt - v_ref[:, cols]
        mixed = _dot(pooled.astype(BF16), pw_ref[g])
        o_ref[:, cols] = (mixed * ps_ref[:, cols]).astype(o_ref.dtype)
    ext_ref[0:POOL_HALO, :] = v_ref[ts - POOL_HALO:, :]


def _pool_mixer(v_pool, pool_w_bf, pool_scale, *, batch, seq, ts):
    v3 = v_pool.reshape(batch, seq, MIX_POOL)
    out = pl.pallas_call(
        _pool_kernel, grid=(batch, seq // ts),
        in_specs=[pl.BlockSpec((None, ts, MIX_POOL), lambda b, s: (b, s, 0)),
                  pl.BlockSpec((N_POOL_GROUPS, POOL_GW, POOL_GW), lambda b, s: (0, 0, 0)),
                  pl.BlockSpec((1, MIX_POOL), lambda b, s: (0, 0))],
        out_specs=pl.BlockSpec((None, ts, MIX_POOL), lambda b, s: (b, s, 0)),
        out_shape=jax.ShapeDtypeStruct((batch, seq, MIX_POOL), BF16),
        scratch_shapes=[pltpu.VMEM((POOL_HALO + ts, MIX_POOL), F32)],
        compiler_params=_cparams(2), name="pool_mixer",
    )(v3, pool_w_bf, pool_scale.reshape(1, MIX_POOL))
    return out.reshape(batch * seq, MIX_POOL)


def _dsa_kernel(q_ref, k_ref, v_ref, k2_ref, qi_ref, kw_ref, o_ref,
                key_sc, bias_sc, xi_sc, qt_sc, m_sc, l_sc, acc_sc, s_sc, cmax_sc, s2_sc, cmax2_sc, *, n_sel, seq):
    blk = pl.program_id(1)
    nchunk = blk // (LC // QB) + 1
    q_pos = blk * QB + lax.broadcasted_iota(I32, (1, QB), 1)
    k_iota = lax.broadcasted_iota(I32, (LC, 1), 0)
    sub = lax.broadcasted_iota(I32, (LANES, QB), 0)
    eye = jnp.where(sub == lax.broadcasted_iota(I32, (LANES, QB), 1), 1.0, 0.0).astype(BF16)

    def transposed(x):
        return _dot_nt(eye, x)

    for jj in range(N_IDX_HEADS // 2):
        xt = transposed(qi_ref[:, jj * LANES:(jj + 1) * LANES])
        xi_sc[jj, :, 0:QB] = jnp.where(sub < IDX_DIM, xt, 0.0).astype(BF16)
        xi_sc[jj, :, QB:2 * QB] = jnp.where(sub >= IDX_DIM, xt, 0.0).astype(BF16)
    for g in range(N_KV_HEADS):
        for r in range(Q_PER_KV):
            h = g * Q_PER_KV + r
            qt_sc[g, 0:HEAD_DIM, r * QB:(r + 1) * QB] = transposed(
                q_ref[:, h * HEAD_DIM:(h + 1) * HEAD_DIM]).astype(BF16)
            qt_sc[g, HEAD_DIM:, r * QB:(r + 1) * QB] = eye
    kw_t = kw_ref[...].T
    wi = [kw_t[IDX_DIM + h:IDX_DIM + h + 1, :] for h in range(N_IDX_HEADS)]

    inf = jnp.inf
    part = (FOLD_ROWS, QB)

    def score_chunk(c, carry):
        s_min, s_max = carry
        off = pl.multiple_of(c * LC, LC)
        k2c = k2_ref[pl.ds(off, LC), :]
        sc = jnp.zeros((LC, QB), F32)
        for jj in range(N_IDX_HEADS // 2):
            rel = jnp.maximum(_dot(k2c, xi_sc[jj]), 0.0)
            sc = sc + rel[:, 0:QB] * wi[2 * jj] + rel[:, QB:2 * QB] * wi[2 * jj + 1]
        causal = off + k_iota <= q_pos
        key_sc[pl.ds(off, LC), :] = jnp.where(causal, sc, -inf)
        s_min = jnp.minimum(s_min, _fold_rows(jnp.where(causal, sc, inf), jnp.min))
        s_max = jnp.maximum(s_max, _fold_rows(jnp.where(causal, sc, -inf), jnp.max))
        return s_min, s_max

    s_min, s_max = lax.fori_loop(0, nchunk, score_chunk, (jnp.full(part, inf, F32), jnp.full(part, -inf, F32)))

    n_sel_chunk = (blk * QB + QB + SC - 1) // SC
    s_iota = lax.broadcasted_iota(I32, (SC, 1), 0)

    def count(pred_fn):
        def body(c, acc):
            off = pl.multiple_of(c * SC, SC)
            m = pred_fn(key_sc[pl.ds(off, SC), :], off)
            return acc + _fold_rows(m, jnp.sum)
        acc = lax.fori_loop(0, n_sel_chunk, body, jnp.zeros(part, F32))
        return jnp.sum(acc, axis=0, keepdims=True)

    has_thr = q_pos + 1 >= int(n_sel)
    lo0 = jnp.where(has_thr, jnp.min(s_min, axis=0, keepdims=True), -inf)
    hi0 = jnp.where(has_thr, jnp.max(s_max, axis=0, keepdims=True), -inf)

    def midpoint(lo, hi):
        mid = 0.5 * (lo + hi)
        return jnp.where(mid <= lo, hi, mid)

    def halve(i, st):
        lo, hi = st
        mid = midpoint(lo, hi)
        enough = count(lambda sc, off: jnp.where(sc >= mid, 1.0, 0.0)) >= n_sel
        return jnp.where(enough, mid, lo), jnp.where(enough, hi, mid)

    lo0, hi0 = lax.fori_loop(0, N_HALVINGS, halve, (lo0, hi0))

    def open_rows(lo, hi):
        return jnp.max(jnp.where(lo < hi, 1.0, 0.0)) > 0.0

    def bisect_cond(st):
        it, lo, hi = st
        return open_rows(lo, hi) & (it < seq)

    def bisect_body(st):
        it, lo, hi = st
        mid = midpoint(lo, hi)

        def body(c, acc):
            cnt, up, dn = acc
            off = pl.multiple_of(c * SC, SC)
            sc = key_sc[pl.ds(off, SC), :]
            ge = sc >= mid
            cnt = cnt + _fold_rows(jnp.where(ge, 1.0, 0.0), jnp.sum)
            up = jnp.minimum(up, _fold_rows(jnp.where(ge, sc, inf), jnp.min))
            dn = jnp.maximum(dn, _fold_rows(jnp.where(ge, -inf, sc), jnp.max))
            return cnt, up, dn

        cnt, up, dn = lax.fori_loop(
            0, n_sel_chunk, body, (jnp.zeros(part, F32), jnp.full(part, inf, F32), jnp.full(part, -inf, F32)))
        enough = jnp.sum(cnt, axis=0, keepdims=True) >= n_sel
        is_open = lo < hi
        new_lo = jnp.where(is_open & enough, jnp.min(up, axis=0, keepdims=True), lo)
        new_hi = jnp.where(is_open & jnp.logical_not(enough), jnp.max(dn, axis=0, keepdims=True), hi)
        return it + 1, new_lo, new_hi

    _, thr, _ = lax.while_loop(bisect_cond, bisect_body, (jnp.int32(0), lo0, hi0))
    n_gt = count(lambda kk, off: jnp.where(kk > thr, 1.0, 0.0))
    n_eq = count(lambda kk, off: jnp.where(kk == thr, 1.0, 0.0))
    need = n_sel - n_gt

    def tie_search():
        def tbody(i, p):
            cand = p | jnp.left_shift(jnp.int32(1), (seq.bit_length() - 2) - i)
            cnt = count(lambda kk, off: jnp.where(kk == thr, jnp.where(off + s_iota < cand, 1.0, 0.0), 0.0))
            return jnp.where(cnt < need, cand, p)
        return lax.fori_loop(0, seq.bit_length() - 1, tbody, jnp.zeros((1, QB), I32))

    ambiguous = jnp.max(jnp.where(has_thr & (n_eq > need), 1.0, 0.0)) > 0.0
    tie_hi = lax.cond(ambiguous, tie_search, lambda: jnp.full((1, QB), seq, I32))
    tie_hi = jnp.where(has_thr, tie_hi, -1)

    def bias_chunk(c, carry):
        off = pl.multiple_of(c * LC, LC)
        kk = key_sc[pl.ds(off, LC), :]
        tie_ok = jnp.where(off + k_iota <= tie_hi, 0.0, NEG)
        bias = jnp.where(kk > thr, 0.0, jnp.where(kk == thr, tie_ok, NEG))
        bias_sc[pl.ds(off, LC), :] = bias.astype(BF16)
        return carry

    lax.fori_loop(0, nchunk, bias_chunk, 0)

    m_sc[...] = jnp.full(m_sc.shape, NEG, F32)
    l_sc[...] = jnp.zeros(l_sc.shape, F32)
    acc_sc[...] = jnp.zeros(acc_sc.shape, F32)

    bufs = ((s_sc, cmax_sc), (s2_sc, cmax2_sc))

    def logits(c, g, buf):
        off = pl.multiple_of(c * LC, LC)
        kc = k_ref[pl.ds(off, LC), g * HEAD_DIM:(g + 1) * HEAD_DIM]
        s = _dot(jnp.concatenate([kc, bias_sc[pl.ds(off, LC), :]], axis=1), qt_sc[g])
        buf[0][g] = s
        buf[1][g] = jnp.max(_fold_rows(s, jnp.max), axis=0, keepdims=True)

    def accumulate(c, g, buf):
        off = pl.multiple_of(c * LC, LC)
        vc = v_ref[pl.ds(off, LC), g * HEAD_DIM:(g + 1) * HEAD_DIM]
        m_old = m_sc[g]
        m_new = jnp.maximum(m_old, buf[1][g])
        alpha = jnp.exp2(m_old - m_new)
        p = jnp.exp2(buf[0][g] - m_new)
        l_sc[g] = alpha * l_sc[g] + jnp.sum(_fold_rows(p, jnp.sum), axis=0, keepdims=True)
        acc_sc[g] = alpha * acc_sc[g] + _dot_tn(vc, p.astype(BF16))
        m_sc[g] = m_new

    def step(c, cur, nxt):
        for g in range(N_KV_HEADS):
            accumulate(c, g, cur)
            logits(c + 1, g, nxt)

    def last(c, cur):
        for g in range(N_KV_HEADS):
            accumulate(c, g, cur)

    for g in range(N_KV_HEADS):
        logits(0, g, bufs[0])
    n_pairs = (nchunk - 1) // 2

    def attn_pair(p, carry):
        step(2 * p, bufs[0], bufs[1])
        step(2 * p + 1, bufs[1], bufs[0])
        return carry

    lax.fori_loop(0, n_pairs, attn_pair, 0)
    c_tail = 2 * n_pairs

    @pl.when(nchunk - c_tail == 2)
    def _():
        step(c_tail, bufs[0], bufs[1])
        last(c_tail + 1, bufs[1])

    @pl.when(nchunk - c_tail == 1)
    def _():
        last(c_tail, bufs[0])

    for g in range(N_KV_HEADS):
        o_t = acc_sc[g] / l_sc[g]
        for r in range(Q_PER_KV):
            h = g * Q_PER_KV + r
            o_ref[:, h * HEAD_DIM:(h + 1) * HEAD_DIM] = o_t[:, r * QB:(r + 1) * QB].T.astype(o_ref.dtype)


def _dsa_attention(q, k, v, qi, k2, kw, *, batch, seq):
    nb = seq // QB
    n_sel = min(TOPK_MAX, seq // 4)
    q_cols = N_Q_HEADS * HEAD_DIM
    kv_cols = N_KV_HEADS * HEAD_DIM
    qi_cols = N_IDX_HEADS * IDX_DIM
    assert seq % LC == 0 and seq % SC == 0 and LC >= n_sel
    kern = functools.partial(_dsa_kernel, n_sel=float(n_sel), seq=seq)
    rows = Q_PER_KV * QB
    return pl.pallas_call(
        kern, grid=(batch, nb),
        in_specs=[pl.BlockSpec((QB, q_cols), lambda b, i: (b * nb + i, 0)),
                  pl.BlockSpec((seq, kv_cols), lambda b, i: (b, 0)),
                  pl.BlockSpec((seq, kv_cols), lambda b, i: (b, 0)),
                  pl.BlockSpec((seq, LANES), lambda b, i: (b, 0)),
                  pl.BlockSpec((QB, qi_cols), lambda b, i: (b * nb + i, 0)),
                  pl.BlockSpec((QB, LANES), lambda b, i: (b * nb + i, 0))],
        out_specs=pl.BlockSpec((QB, q_cols), lambda b, i: (b * nb + i, 0)),
        out_shape=jax.ShapeDtypeStruct((batch * seq, q_cols), BF16),
        scratch_shapes=[pltpu.VMEM((seq, QB), F32),
                        pltpu.VMEM((seq, QB), BF16),
                        pltpu.VMEM((N_IDX_HEADS // 2, LANES, 2 * QB), BF16),
                        pltpu.VMEM((N_KV_HEADS, HEAD_DIM + QB, rows), BF16),
                        pltpu.VMEM((N_KV_HEADS, 1, rows), F32),
                        pltpu.VMEM((N_KV_HEADS, 1, rows), F32),
                        pltpu.VMEM((N_KV_HEADS, HEAD_DIM, rows), F32),
                        pltpu.VMEM((N_KV_HEADS, LC, rows), F32), pltpu.VMEM((N_KV_HEADS, 1, rows), F32),
                        pltpu.VMEM((N_KV_HEADS, LC, rows), F32), pltpu.VMEM((N_KV_HEADS, 1, rows), F32)],
        compiler_params=_cparams(2), name="dsa_attention",
    )(q, k, v, k2, qi, kw)


def _ln_rows(x, g, b):
    mu = jnp.mean(x, axis=-1, keepdims=True)
    xc = x - mu
    var = jnp.mean(xc * xc, axis=-1, keepdims=True)
    return xc * lax.rsqrt(var + LN_EPS) * g + b


def _pack_halves(y):
    half = y.shape[1] // 2
    return pltpu.pack_elementwise([y[:, :half], y[:, half:]], packed_dtype=BF16)


def _unpack_halves(p):
    lo = pltpu.unpack_elementwise(p, index=0, packed_dtype=BF16, unpacked_dtype=F32)
    hi = pltpu.unpack_elementwise(p, index=1, packed_dtype=BF16, unpacked_dtype=F32)
    return jnp.concatenate([lo, hi], axis=1)


def _ln_kernel(x_ref, g_ref, b_ref, o_ref, o2_ref, *, packed):
    y = _ln_rows(x_ref[...], g_ref[...], b_ref[...])
    o_ref[...] = y
    o2_ref[...] = _pack_halves(y) if packed else y.astype(o2_ref.dtype)


def _layer_norm(x, g, b, *, bm, packed=False):
    m, d = x.shape
    row = pl.BlockSpec((bm, d), lambda i: (i, 0))
    vec = pl.BlockSpec((1, d), lambda i: (0, 0))
    second = ((m, d // 2), U32) if packed else ((m, d), BF16)
    return pl.pallas_call(
        functools.partial(_ln_kernel, packed=packed), grid=(m // bm,), in_specs=[row, vec, vec],
        out_specs=[row, pl.BlockSpec((bm, second[0][1]), lambda i: (i, 0))],
        out_shape=[jax.ShapeDtypeStruct((m, d), F32), jax.ShapeDtypeStruct(*second)],
        compiler_params=_cparams(1), name="layer_norm",
    )(x, g.reshape(1, d), b.reshape(1, d))


def _xattn_kernel(q_ref, k_ref, v_ref, o_ref):
    dh = q_ref.shape[1] // N_MEM_HEADS
    for h in range(N_MEM_HEADS):
        cols = slice(h * dh, (h + 1) * dh)
        s = _dot_nt(q_ref[:, cols], k_ref[:, cols])
        p = jnp.exp(s - jnp.max(s, axis=1, keepdims=True))
        o = _dot(p.astype(BF16), v_ref[:, cols]) / jnp.sum(p, axis=1, keepdims=True)
        o_ref[:, cols] = o.astype(o_ref.dtype)


def _cross_attention(q, k, v, *, batch, seq, n_mem, bm):
    d = q.shape[1]
    nb = seq // bm
    return pl.pallas_call(
        _xattn_kernel, grid=(batch, nb),
        in_specs=[pl.BlockSpec((bm, d), lambda b, i: (b * nb + i, 0)),
                  pl.BlockSpec((n_mem, d), lambda b, i: (b, 0)),
                  pl.BlockSpec((n_mem, d), lambda b, i: (b, 0))],
        out_specs=pl.BlockSpec((bm, d), lambda b, i: (b * nb + i, 0)),
        out_shape=jax.ShapeDtypeStruct((batch * seq, d), BF16),
        compiler_params=_cparams(2), name="cross_attention",
    )(q, k, v)


def _split_bf16(a):
    hi = a.astype(BF16)
    return hi, (a - hi.astype(F32)).astype(BF16)


def _router_kernel(x_ref, w_ref, b_ref, id_ref, gate_ref):
    xh, xl = _split_bf16(x_ref[...])
    wh, wl = _split_bf16(w_ref[...])
    logits = _dot(xh, wh) + _dot(xl, wh) + _dot(xh, wl) + b_ref[...]
    lane = lax.broadcasted_iota(I32, logits.shape, 1)
    lane_f = lane.astype(F32)
    big = float(LANES)
    is_g = lane < N_GROUPS
    gl = jnp.where(is_g, logits, -jnp.inf)
    g_max = jnp.max(gl, axis=1, keepdims=True)
    g_idx = jnp.min(jnp.where(gl == g_max, lane_f, big), axis=1, keepdims=True)
    g_gate = 1.0 / jnp.sum(jnp.where(is_g, jnp.exp(gl - g_max), 0.0), axis=1, keepdims=True)
    e_lo = N_GROUPS + g_idx * EXPERTS_PER_GROUP
    in_grp = (lane_f >= e_lo) & (lane_f < e_lo + EXPERTS_PER_GROUP)
    el = jnp.where(in_grp, logits, -jnp.inf)
    v1 = jnp.max(el, axis=1, keepdims=True)
    i1 = jnp.min(jnp.where(el == v1, lane_f, big), axis=1, keepdims=True)
    el2 = jnp.where(lane_f == i1, -jnp.inf, el)
    v2 = jnp.max(el2, axis=1, keepdims=True)
    i2 = jnp.min(jnp.where(el2 == v2, lane_f, big), axis=1, keepdims=True)
    z = jnp.exp(v2 - v1)
    w1 = g_gate / (1.0 + z)
    w2 = g_gate * z / (1.0 + z)
    ids = jnp.where(lane == 0, i1 - N_GROUPS, jnp.where(lane == 1, i2 - N_GROUPS, 0.0))
    id_ref[...] = ids.astype(I32)
    gate_ref[...] = jnp.where(lane == 0, w1, jnp.where(lane == 1, w2, 0.0))


def _router(x, w_r, b_r, *, bm):
    m, d = x.shape
    out = pl.BlockSpec((bm, LANES), lambda i: (i, 0))
    return pl.pallas_call(
        _router_kernel, grid=(m // bm,),
        in_specs=[pl.BlockSpec((bm, d), lambda i: (i, 0)),
                  pl.BlockSpec((d, LANES), lambda i: (0, 0)),
                  pl.BlockSpec((1, LANES), lambda i: (0, 0))],
        out_specs=[out, out],
        out_shape=[jax.ShapeDtypeStruct((m, LANES), I32), jax.ShapeDtypeStruct((m, LANES), F32)],
        compiler_params=_cparams(1), name="router",
    )(x, w_r, b_r)


def _rank_kernel(id_ref, rank_ref, cnt_ref, carry_ref):
    i = pl.program_id(0)
    bt = id_ref.shape[0]

    @pl.when(i == 0)
    def _():
        carry_ref[...] = jnp.zeros(carry_ref.shape, F32)

    ids = id_ref[...]
    lane = lax.broadcasted_iota(I32, (bt, LANES), 1)
    e0 = ids[:, 0:1]
    e1 = ids[:, 1:2] + N_EXPERTS
    hit0 = lane == e0
    hit1 = lane == e1
    onehot = jnp.where(hit0, 1.0, jnp.where(hit1, 1.0, 0.0))
    r_io = lax.broadcasted_iota(I32, (bt, bt), 0)
    c_io = lax.broadcasted_iota(I32, (bt, bt), 1)
    tri = jnp.where(c_io < r_io, 1.0, 0.0).astype(BF16)
    prefix = _dot(tri, onehot.astype(BF16)) + carry_ref[0:1, :]
    rank0 = jnp.sum(jnp.where(hit0, prefix, 0.0), axis=1, keepdims=True)
    rank1 = jnp.sum(jnp.where(hit1, prefix, 0.0), axis=1, keepdims=True)
    rank_ref[...] = jnp.where(lane == 0, rank0, jnp.where(lane == 1, rank1, 0.0))
    total = carry_ref[0:1, :] + jnp.sum(onehot, axis=0, keepdims=True)
    carry_ref[...] = jnp.broadcast_to(total, carry_ref.shape)
    cnt_ref[...] = jnp.broadcast_to(total, cnt_ref.shape)


def _slot_ranks(ids, *, bt):
    m = ids.shape[0]
    return pl.pallas_call(
        _rank_kernel, grid=(m // bt,),
        in_specs=[pl.BlockSpec((bt, LANES), lambda i: (i, 0))],
        out_specs=[pl.BlockSpec((bt, LANES), lambda i: (i, 0)),
                   pl.BlockSpec((8, LANES), lambda i: (0, 0))],
        out_shape=[jax.ShapeDtypeStruct((m, LANES), F32), jax.ShapeDtypeStruct((8, LANES), F32)],
        scratch_shapes=[pltpu.VMEM((8, LANES), F32)],
        compiler_params=_cparams(1), name="slot_ranks",
    )(ids)


def _pos_kernel(id_ref, rank_ref, tab_ref, pos_ref):
    ids = id_ref[...]
    ranks = rank_ref[...]
    lane = lax.broadcasted_iota(I32, ids.shape, 1)
    tab = tab_ref[...]
    base0 = jnp.sum(jnp.where(lane == ids[:, 0:1], tab, 0.0), axis=1, keepdims=True)
    base1 = jnp.sum(jnp.where(lane == ids[:, 1:2] + N_EXPERTS, tab, 0.0), axis=1, keepdims=True)
    pos = jnp.where(lane == 0, base0 + ranks[:, 0:1], jnp.where(lane == 1, base1 + ranks[:, 1:2], 0.0))
    pos_ref[...] = pos.astype(I32)


def _slot_positions(ids, ranks, table, *, bt):
    m = ids.shape[0]
    blk = pl.BlockSpec((bt, LANES), lambda i: (i, 0))
    return pl.pallas_call(
        _pos_kernel, grid=(m // bt,),
        in_specs=[blk, blk, pl.BlockSpec((1, LANES), lambda i: (0, 0))],
        out_specs=blk,
        out_shape=jax.ShapeDtypeStruct((m, LANES), I32),
        compiler_params=_cparams(1), name="slot_positions",
    )(ids, ranks, table)


def _row_copy(src_ref, src_row, dst_ref, dst_row, sem):
    return pltpu.make_async_copy(src_ref.at[pl.ds(src_row, 1)], dst_ref.at[pl.ds(dst_row, 1)], sem)


def _dispatch_kernel(pos0_ref, pos1_ref, x_ref, xs_in_ref, xs_ref, sem):
    del xs_in_ref
    i = pl.program_id(0)
    bt = x_ref.shape[0]

    def issue(r, carry):
        t = i * bt + r
        _row_copy(x_ref, r, xs_ref, pos0_ref[t], sem.at[0]).start()
        _row_copy(x_ref, r, xs_ref, pos1_ref[t], sem.at[1]).start()
        return carry

    lax.fori_loop(0, bt, issue, 0)
    for k in range(2):
        pltpu.make_async_copy(x_ref, xs_ref.at[pl.ds(0, bt)], sem.at[k]).wait()


def _dispatch(x, pos0, pos1, n_slots, *, bt):
    m, d = x.shape
    xs0 = jnp.zeros((n_slots, d), x.dtype)
    return pl.pallas_call(
        _dispatch_kernel,
        grid_spec=pltpu.PrefetchScalarGridSpec(
            num_scalar_prefetch=2, grid=(m // bt,),
            in_specs=[pl.BlockSpec((bt, d), lambda i, p0, p1: (i, 0)),
                      pl.BlockSpec(memory_space=pl.ANY)],
            out_specs=pl.BlockSpec(memory_space=pl.ANY),
            scratch_shapes=[pltpu.SemaphoreType.DMA((2,))]),
        out_shape=jax.ShapeDtypeStruct((n_slots, d), x.dtype),
        input_output_aliases={3: 0},
        compiler_params=_cparams(1), name="dispatch",
    )(pos0, pos1, x, xs0)


def _ffn_kernel(blk_e_ref, nxt_e_ref, set_ref, nused_ref, x_ref, wg_hbm, wu_hbm, wd_hbm, y_ref,
                wg_bf, wu_bf, wd_bf, st_g, st_u, st_d, sem):
    i = pl.program_id(0)
    d, ff = wg_bf.shape[1], wg_bf.shape[2]
    rg, rd = d // W_GROUPS, ff // W_GROUPS

    n_slot = st_g.shape[0]

    def group_copies(e, g):
        slot = g % n_slot
        return (pltpu.make_async_copy(wg_hbm.at[e, pl.ds(g * rg, rg), :], st_g.at[slot], sem.at[0, slot]),
                pltpu.make_async_copy(wu_hbm.at[e, pl.ds(g * rg, rg), :], st_u.at[slot], sem.at[1, slot]),
                pltpu.make_async_copy(wd_hbm.at[e, pl.ds(g * rd, rd), :], st_d.at[slot], sem.at[2, slot]))

    def start(e, g):
        for c in group_copies(e, g):
            c.start()

    def finish(e, g, s):
        for c in group_copies(e, g):
            c.wait()
        slot = g % n_slot
        wg_bf[s, g * rg:(g + 1) * rg, :] = st_g[slot].astype(BF16)
        wu_bf[s, g * rg:(g + 1) * rg, :] = st_u[slot].astype(BF16)
        wd_bf[s, g * rd:(g + 1) * rd, :] = st_d[slot].astype(BF16)

    def load_expert(e, s, between=()):
        for g in range(n_slot):
            start(e, g)
        for g in range(W_GROUPS):
            if g < len(between):
                between[g]()
            finish(e, g, s)
            if g + n_slot < W_GROUPS:
                start(e, g + n_slot)

    e = blk_e_ref[i]
    s = set_ref[i]
    nxt = nxt_e_ref[i]
    first_of_expert = jnp.logical_or(i == 0, blk_e_ref[jnp.maximum(i - 1, 0)] != e)
    active = i < nused_ref[0]
    prefetch = jnp.logical_and(jnp.logical_and(active, first_of_expert), nxt >= 0)

    @pl.when(i == 0)
    def _():
        load_expert(e, s)

    def ffn_steps():
        vals = {}

        def gate_step():
            vals["x"] = _unpack_halves(x_ref[...]).astype(BF16)
            vals["gate"] = _dot(vals["x"], wg_bf[s])

        def up_step():
            gate = vals["gate"]
            vals["h"] = (gate * (1.0 / (1.0 + jnp.exp(-gate))) * _dot(vals["x"], wu_bf[s])).astype(BF16)

        def down_step():
            y_ref[...] = _pack_halves(_dot(vals["h"], wd_bf[s]))

        return gate_step, up_step, down_step

    @pl.when(prefetch)
    def _():
        load_expert(nxt, 1 - s, between=ffn_steps())

    @pl.when(jnp.logical_and(active, jnp.logical_not(prefetch)))
    def _():
        for step in ffn_steps():
            step()


def _expert_ffn(xs, blk_e, nxt_e, set_idx, nused, wg, wu, wd):
    n_slots, dp = xs.shape
    d, ff = wg.shape[1], wg.shape[2]
    nblk = n_slots // MOE_BLK
    assert dp * 2 == d and d % W_GROUPS == 0 and ff % W_GROUPS == 0 and W_GROUPS >= W_STAGE_SLOTS > 3

    def row_map(i, be, nx, si, nu):
        return (jnp.minimum(i, nu[0] - 1), 0)

    hbm = pl.BlockSpec(memory_space=pl.ANY)
    return pl.pallas_call(
        _ffn_kernel,
        grid_spec=pltpu.PrefetchScalarGridSpec(
            num_scalar_prefetch=4, grid=(nblk,),
            in_specs=[pl.BlockSpec((MOE_BLK, dp), row_map), hbm, hbm, hbm],
            out_specs=pl.BlockSpec((MOE_BLK, dp), row_map),
            scratch_shapes=[pltpu.VMEM((2, d, ff), BF16), pltpu.VMEM((2, d, ff), BF16), pltpu.VMEM((2, ff, d), BF16),
                            pltpu.VMEM((W_STAGE_SLOTS, d // W_GROUPS, ff), F32),
                            pltpu.VMEM((W_STAGE_SLOTS, d // W_GROUPS, ff), F32),
                            pltpu.VMEM((W_STAGE_SLOTS, ff // W_GROUPS, d), F32),
                            pltpu.SemaphoreType.DMA((3, W_STAGE_SLOTS))]),
        out_shape=jax.ShapeDtypeStruct((n_slots, dp), xs.dtype),
        input_output_aliases={4: 0},
        compiler_params=_cparams(1), name="expert_ffn",
    )(blk_e, nxt_e, set_idx, nused, xs, wg, wu, wd)


def _combine_kernel(pos0_ref, pos1_ref, x_ref, gate_ref, g_ref, b_ref, ys_ref, o_ref, y0_ref, y1_ref, sem):
    i = pl.program_id(0)
    bt = x_ref.shape[0]
    slot = i % 2

    def issue(step, to_slot):
        def body(r, carry):
            t = step * bt + r
            _row_copy(ys_ref, pos0_ref[t], y0_ref.at[to_slot], r, sem.at[0, to_slot]).start()
            _row_copy(ys_ref, pos1_ref[t], y1_ref.at[to_slot], r, sem.at[1, to_slot]).start()
            return carry
        lax.fori_loop(0, bt, body, 0)

    @pl.when(i == 0)
    def _():
        issue(0, 0)

    @pl.when(i + 1 < pl.num_programs(0))
    def _():
        issue(i + 1, 1 - slot)

    pltpu.make_async_copy(ys_ref.at[pl.ds(0, bt)], y0_ref.at[slot], sem.at[0, slot]).wait()
    pltpu.make_async_copy(ys_ref.at[pl.ds(0, bt)], y1_ref.at[slot], sem.at[1, slot]).wait()
    gates = gate_ref[...]
    f = _unpack_halves(y0_ref[slot]) * gates[:, 0:1] + _unpack_halves(y1_ref[slot]) * gates[:, 1:2]
    o_ref[...] = _ln_rows(ALPHA * x_ref[...] + f, g_ref[...], b_ref[...])


def _combine(x, gates, ys, pos0, pos1, g, b, *, bt):
    m, d = x.shape
    dp = ys.shape[1]
    row = pl.BlockSpec((bt, d), lambda i, p0, p1: (i, 0))
    vec = pl.BlockSpec((1, d), lambda i, p0, p1: (0, 0))
    return pl.pallas_call(
        _combine_kernel,
        grid_spec=pltpu.PrefetchScalarGridSpec(
            num_scalar_prefetch=2, grid=(m // bt,),
            in_specs=[row, pl.BlockSpec((bt, LANES), lambda i, p0, p1: (i, 0)), vec, vec,
                      pl.BlockSpec(memory_space=pl.ANY)],
            out_specs=row,
            scratch_shapes=[pltpu.VMEM((2, bt, dp), ys.dtype), pltpu.VMEM((2, bt, dp), ys.dtype),
                            pltpu.SemaphoreType.DMA((2, 2))]),
        out_shape=jax.ShapeDtypeStruct((m, d), F32),
        compiler_params=_cparams(1), name="combine_ln",
    )(pos0, pos1, x, gates, g.reshape(1, d), b.reshape(1, d), ys)


def _rope_tables(positions, n_rows):
    pos = positions.astype(F32).reshape(n_rows, 1)

    def cs(dim):
        inv = 1.0 / (ROPE_THETA ** (jnp.arange(0, dim, 2, dtype=F32) / dim))
        ang = pos * inv[None, :]
        return jnp.cos(ang), jnp.sin(ang)

    c_h, s_h = cs(HEAD_DIM)
    c_i, s_i = cs(IDX_DIM)
    z_i = jnp.zeros_like(s_i)
    cosf = jnp.concatenate([c_h, c_h], axis=1)
    sinf = jnp.concatenate([-s_h, s_h], axis=1)
    cos64 = jnp.concatenate([c_i, c_i, c_i, c_i], axis=1)
    sin_lo = jnp.concatenate([-s_i, z_i, -s_i, z_i], axis=1)
    sin_hi = jnp.concatenate([z_i, s_i, z_i, s_i], axis=1)
    return cosf, sinf, cos64, sin_lo, sin_hi


def kernel(x, mem, positions, w_in, pool_w, pool_scale, w_o, ln1_g, ln1_b, w_mq, w_mk, w_mv, w_mo, ln2_g, ln2_b,
           w_group_router, b_group_router, w_expert_router, b_expert_router, w_gate, w_up, w_down, ln3_g, ln3_b):
    batch, seq, d = x.shape
    n_mem = mem.shape[1]
    n = batch * seq
    bn_attn = N_KV_HEADS * HEAD_DIM
    tables = _rope_tables(positions, n)
    xf = x.reshape(n, d)
    n_slots = 2 * n + N_EXPERTS * MOE_BLK
    nblk = n_slots // MOE_BLK

    for l in range(w_in.shape[0]):
        x_bf = xf.astype(BF16)
        kw_col = MIX_POOL + 7 * bn_attn
        w_kw = jnp.pad(w_in[l][:, kw_col:], ((0, 0), (0, LANES - (w_in.shape[2] - kw_col))))

        v_pool = _matmul([x_bf], [w_in], layer=l, name="inproj_pool", bm=1024, bn=DENSE_BN, n=MIX_POOL)
        blk0 = MIX_POOL // bn_attn
        q_cols = N_Q_HEADS * HEAD_DIM
        q_scale = HEAD_DIM ** -0.5 * LOG2E
        rope_h, rope_i = tables[:2] + tables[:1], tables[2:]
        proj = functools.partial(_inproj_rope, x_bf, w_in, layer=l, bm=1024, bn=bn_attn)
        q = proj(rope_h, name="inproj_q", col_off_blocks=blk0, n=q_cols, rope=HEAD_DIM, scale=q_scale)
        k = proj(rope_h, name="inproj_k", col_off_blocks=blk0 + 4, n=bn_attn, rope=HEAD_DIM)
        v = proj(rope_h, name="inproj_v", col_off_blocks=blk0 + 5, n=bn_attn, rope=None)
        qi = proj(rope_i, name="inproj_qi", col_off_blocks=blk0 + 6, n=bn_attn, rope=IDX_DIM)
        k2, kw = _inproj_idx(x_bf, w_kw, tables, bm=1024)
        a_pool = _pool_mixer(v_pool, pool_w[l].astype(BF16), pool_scale[l], batch=batch, seq=seq, ts=512)
        a_attn = _dsa_attention(q, k, v, qi, k2, kw, batch=batch, seq=seq)
        res = (xf, (1024, DENSE_BN), lambda i, j: (i, j))
        pre = _matmul([a_pool, a_attn], [w_o, w_o], layer=l, name="outproj", bm=1024, bn=DENSE_BN, n=d,
                      extras=(res,), epilogue=_residual_epilogue)
        x1, x1_bf = _layer_norm(pre, ln1_g[l], ln1_b[l], bm=256)

        mq_scale = (d // N_MEM_HEADS) ** -0.5

        def q_epilogue(acc, ex, outs):
            outs[0][...] = (acc * mq_scale).astype(outs[0].dtype)

        qm = _matmul([x1_bf], [w_mq], layer=l, name="mem_q", bm=1024, bn=DENSE_BN, n=d, out_dtype=BF16,
                     epilogue=q_epilogue)
        mem_bf = mem.reshape(batch * n_mem, d).astype(BF16)
        km = _matmul([mem_bf], [w_mk], layer=l, name="mem_k", bm=batch * n_mem, bn=DENSE_BN, n=d, out_dtype=BF16)
        vm = _matmul([mem_bf], [w_mv], layer=l, name="mem_v", bm=batch * n_mem, bn=DENSE_BN, n=d, out_dtype=BF16)
        om = _cross_attention(qm, km, vm, batch=batch, seq=seq, n_mem=n_mem, bm=512)
        res = (x1, (1024, DENSE_BN), lambda i, j: (i, j))
        pre = _matmul([om], [w_mo], layer=l, name="mem_o", bm=1024, bn=DENSE_BN, n=d, extras=(res,),
                      epilogue=_residual_epilogue)
        x2, x2_rows = _layer_norm(pre, ln2_g[l], ln2_b[l], bm=256, packed=True)

        w_r = jnp.pad(jnp.concatenate([w_group_router[l], w_expert_router[l]], axis=1),
                      ((0, 0), (0, LANES - N_GROUPS - N_EXPERTS)))
        b_r = jnp.pad(jnp.concatenate([b_group_router[l], b_expert_router[l]]),
                      (0, LANES - N_GROUPS - N_EXPERTS)).reshape(1, LANES)
        ids, gates = _router(x2, w_r, b_r, bm=512)
        ranks, totals = _slot_ranks(ids, bt=512)
        c0 = totals[0, :N_EXPERTS].astype(I32)
        c1 = totals[0, N_EXPERTS:].astype(I32)
        padded = ((c0 + c1 + MOE_BLK - 1) // MOE_BLK) * MOE_BLK
        pend = jnp.cumsum(padded)
        pstart = pend - padded
        table = jnp.concatenate([pstart, pstart + c0]).astype(F32).reshape(1, LANES)
        pos = _slot_positions(ids, ranks, table, bt=512)
        pos0, pos1 = pos[:, 0], pos[:, 1]
        nused = pend[-1] // MOE_BLK
        blk_i = jnp.minimum(jnp.arange(nblk, dtype=I32), nused - 1)
        blk_e = jnp.sum((pend[None, :] <= (blk_i * MOE_BLK)[:, None]).astype(I32), axis=1)
        blk_e = jnp.minimum(blk_e, N_EXPERTS - 1)
        later = blk_e[None, :] > blk_e[:, None]
        nxt_e = jnp.min(jnp.where(later, blk_e[None, :], N_EXPERTS), axis=1)
        nxt_e = jnp.where(nxt_e == N_EXPERTS, -1, nxt_e).astype(I32)
        new_e = jnp.concatenate([jnp.ones((1,), I32), (blk_e[1:] != blk_e[:-1]).astype(I32)])
        set_idx = (jnp.cumsum(new_e) - 1) % 2

        xs = _dispatch(x2_rows, pos0, pos1, n_slots, bt=256)
        ys = _expert_ffn(xs, blk_e, nxt_e, set_idx.astype(I32), nused.reshape(1).astype(I32),
                         w_gate[l], w_up[l], w_down[l])
        xf = _combine(x2, gates, ys, pos0, pos1, ln3_g[l], ln3_b[l], bt=256)
    return xf.reshape(batch, seq, d)
```

```python
import functools

import jax
import jax.numpy as jnp
from jax import lax
from jax.experimental import pallas as pl
from jax.experimental.pallas import tpu as pltpu

F32 = jnp.float32
BF16 = jnp.bfloat16
I32 = jnp.int32
U32 = jnp.uint32

MIX_POOL = 2048
N_POOL_GROUPS = 4
POOL_WINDOWS = (2, 4, 8, 16)
POOL_GW = MIX_POOL // N_POOL_GROUPS
HEAD_DIM = 128
N_Q_HEADS = 16
N_KV_HEADS = 4
Q_PER_KV = N_Q_HEADS // N_KV_HEADS
N_IDX_HEADS = 8
IDX_DIM = 64
TOPK_MAX = 256
ROPE_THETA = 10000.0
N_MEM_HEADS = 4
N_GROUPS = 8
EXPERTS_PER_GROUP = 8
N_EXPERTS = N_GROUPS * EXPERTS_PER_GROUP
LN_EPS = 1e-5
DEPTH = 1
ALPHA = (2.0 * DEPTH) ** 0.25

LANES = 128
QB = 128
LC = 512
SC = 256
DENSE_BN = 512
MOE_BLK = 256
W_GROUPS = 8
W_STAGE_SLOTS = 6
POOL_HALO = 16
NEG = -1e30
N_HALVINGS = 12
LOG2E = 1.4426950408889634
VMEM_LIMIT = 56 * 1024 * 1024


def _cparams(n_axes, vmem=VMEM_LIMIT):
    return pltpu.CompilerParams(dimension_semantics=("arbitrary",) * n_axes, vmem_limit_bytes=vmem)


def _dot(a, b):
    return jnp.dot(a, b, preferred_element_type=F32)


def _dot_nt(a, b):
    return lax.dot_general(a, b, (((1,), (1,)), ((), ())), preferred_element_type=F32)


def _dot_tn(a, b):
    return lax.dot_general(a, b, (((0,), (0,)), ((), ())), preferred_element_type=F32)


FOLD_ROWS = 32


def _fold_rows(x, reduce_fn):
    r, c = x.shape
    return reduce_fn(x.reshape(r // FOLD_ROWS, FOLD_ROWS, c), axis=0)


def _mm_kernel(*refs, n_lhs, n_extra, epilogue):
    lhs = refs[:n_lhs]
    ws = refs[n_lhs:2 * n_lhs]
    extras = refs[2 * n_lhs:2 * n_lhs + n_extra]
    outs = refs[2 * n_lhs + n_extra:-n_lhs]
    w_bf = refs[-n_lhs:]

    @pl.when(pl.program_id(1) == 0)
    def _():
        for l in range(n_lhs):
            w_bf[l][...] = ws[l][...].astype(BF16)

    acc = _dot(lhs[0][...], w_bf[0][...])
    for l in range(1, n_lhs):
        acc = acc + _dot(lhs[l][...], w_bf[l][...])
    epilogue(acc, extras, outs)


def _matmul(lhs_list, w_list, *, layer, name, bm, bn, n, w_col_off=0, extras=(), out_dtype=F32, epilogue=None):
    m = lhs_list[0].shape[0]
    grid = (n // bn, m // bm)
    in_specs = [pl.BlockSpec((bm, a.shape[1]), lambda j, i: (i, 0)) for a in lhs_list]
    in_specs += [pl.BlockSpec((None, a.shape[1], bn), (lambda j, i, l=l: (layer, l, j + w_col_off)))
                 for l, a in enumerate(lhs_list)]
    in_specs += [pl.BlockSpec(blk, (lambda j, i, f=f: f(i, j))) for (_, blk, f) in extras]
    if epilogue is None:
        def epilogue(acc, ex, outs):
            outs[0][...] = acc.astype(outs[0].dtype)
    kern = functools.partial(_mm_kernel, n_lhs=len(lhs_list), n_extra=len(extras), epilogue=epilogue)
    return pl.pallas_call(
        kern, grid=grid, in_specs=in_specs,
        out_specs=pl.BlockSpec((bm, bn), lambda j, i: (i, j)),
        out_shape=jax.ShapeDtypeStruct((m, n), out_dtype),
        scratch_shapes=[pltpu.VMEM((a.shape[1], bn), BF16) for a in lhs_list],
        compiler_params=_cparams(2), name=name,
    )(*lhs_list, *w_list, *[e[0] for e in extras])


def _residual_epilogue(acc, ex, outs):
    outs[0][...] = acc + ALPHA * ex[0][...]


def _rope128(a, cosf, sinf):
    return a * cosf + pltpu.roll(a, 64, 1) * sinf


def _rope64(a, cos64, sin_lo, sin_hi):
    return a * cos64 + pltpu.roll(a, 96, 1) * sin_lo + pltpu.roll(a, 32, 1) * sin_hi


def _inproj_kernel(x_ref, w_ref, t0_ref, t1_ref, t2_ref, o_ref, w_bf, *, rope, scale):
    @pl.when(pl.program_id(1) == 0)
    def _():
        w_bf[...] = w_ref[...].astype(BF16)

    bm, bn = o_ref.shape
    half = bm // 2
    for h in range(2):
        rows = pl.ds(h * half, half)
        acc = _dot(x_ref[rows, :], w_bf[...])
        if rope == HEAD_DIM:
            t0, t1 = t0_ref[rows, :] * scale, t1_ref[rows, :] * scale
        elif rope == IDX_DIM:
            t0, t1, t2 = t0_ref[rows, :], t1_ref[rows, :], t2_ref[rows, :]
        for c in range(bn // LANES):
            a = acc[:, c * LANES:(c + 1) * LANES]
            if rope == HEAD_DIM:
                a = _rope128(a, t0, t1)
            elif rope == IDX_DIM:
                a = _rope64(a, t0, t1, t2)
            o_ref[rows, c * LANES:(c + 1) * LANES] = a.astype(o_ref.dtype)


def _inproj_rope(x_bf, w, tabs, *, layer, name, col_off_blocks, n, rope, scale=1.0, bm, bn):
    m, d = x_bf.shape
    tab_spec = pl.BlockSpec((bm, LANES), lambda j, i: (i, 0))
    return pl.pallas_call(
        functools.partial(_inproj_kernel, rope=rope, scale=scale), grid=(n // bn, m // bm),
        in_specs=[pl.BlockSpec((bm, d), lambda j, i: (i, 0)),
                  pl.BlockSpec((None, d, bn), lambda j, i: (layer, 0, j + col_off_blocks))] + [tab_spec] * 3,
        out_specs=pl.BlockSpec((bm, bn), lambda j, i: (i, j)),
        out_shape=jax.ShapeDtypeStruct((m, n), BF16),
        scratch_shapes=[pltpu.VMEM((d, bn), BF16)],
        compiler_params=_cparams(2), name=name,
    )(x_bf, w, *tabs)


def _inproj_idx_kernel(x_ref, w_ref, cos64_ref, sinlo_ref, sinhi_ref, k2_ref, kw_ref, *, wi_scale):
    a = _dot(x_ref[...], w_ref[...].astype(BF16))
    r = _rope64(a, cos64_ref[...], sinlo_ref[...], sinhi_ref[...])
    lane = lax.broadcasted_iota(I32, a.shape, 1)
    kw_ref[...] = jnp.where(lane < IDX_DIM, r, a * wi_scale)
    k2_ref[...] = jnp.where(lane < IDX_DIM, r, pltpu.roll(r, 64, 1)).astype(k2_ref.dtype)


def _inproj_idx(x_bf, w_kw, tables, *, bm):
    m, d = x_bf.shape
    wi_scale = (N_IDX_HEADS ** -0.5) * (IDX_DIM ** -0.5)
    tab_spec = pl.BlockSpec((bm, LANES), lambda i: (i, 0))
    return pl.pallas_call(
        functools.partial(_inproj_idx_kernel, wi_scale=wi_scale), grid=(m // bm,),
        in_specs=[pl.BlockSpec((bm, d), lambda i: (i, 0)), pl.BlockSpec((d, LANES), lambda i: (0, 0))]
        + [tab_spec] * 3,
        out_specs=[tab_spec, tab_spec],
        out_shape=[jax.ShapeDtypeStruct((m, LANES), BF16), jax.ShapeDtypeStruct((m, LANES), F32)],
        compiler_params=_cparams(1), name="inproj_idx",
    )(x_bf, w_kw, *tables[2:])


def _pool_kernel(v_ref, pw_ref, ps_ref, o_ref, ext_ref):
    s = pl.program_id(1)
    ts = v_ref.shape[0]

    @pl.when(s == 0)
    def _():
        ext_ref[0:POOL_HALO, :] = jnp.zeros((POOL_HALO, ext_ref.shape[1]), F32)

    ext_ref[POOL_HALO:, :] = v_ref[...]
    t = s * ts + lax.broadcasted_iota(I32, (ts, 1), 0)
    for g, w in enumerate(POOL_WINDOWS):
        cols = slice(g * POOL_GW, (g + 1) * POOL_GW)
        e = ext_ref[:, cols]
        step = 1
        while step < w:
            e = e + pltpu.roll(e, step, 0)
            step *= 2
        win = e[POOL_HALO:, :]
        cnt = jnp.minimum(t + 1, w).astype(F32)
        pooled = win / cnt - v_ref[:, cols]
        mixed = _dot(pooled.astype(BF16), pw_ref[g])
        o_ref[:, cols] = (mixed * ps_ref[:, cols]).astype(o_ref.dtype)
    ext_ref[0:POOL_HALO, :] = v_ref[ts - POOL_HALO:, :]


def _pool_mixer(v_pool, pool_w_bf, pool_scale, *, batch, seq, ts):
    v3 = v_pool.reshape(batch, seq, MIX_POOL)
    out = pl.pallas_call(
        _pool_kernel, grid=(batch, seq // ts),
        in_specs=[pl.BlockSpec((None, ts, MIX_POOL), lambda b, s: (b, s, 0)),
                  pl.BlockSpec((N_POOL_GROUPS, POOL_GW, POOL_GW), lambda b, s: (0, 0, 0)),
                  pl.BlockSpec((1, MIX_POOL), lambda b, s: (0, 0))],
        out_specs=pl.BlockSpec((None, ts, MIX_POOL), lambda b, s: (b, s, 0)),
        out_shape=jax.ShapeDtypeStruct((batch, seq, MIX_POOL), BF16),
        scratch_shapes=[pltpu.VMEM((POOL_HALO + ts, MIX_POOL), F32)],
        compiler_params=_cparams(2), name="pool_mixer",
    )(v3, pool_w_bf, pool_scale.reshape(1, MIX_POOL))
    return out.reshape(batch * seq, MIX_POOL)


def _dsa_kernel(q_ref, k_ref, v_ref, k2_ref, qi_ref, kw_ref, o_ref,
                key_sc, bias_sc, xi_sc, qt_sc, m_sc, l_sc, acc_sc, s_sc, cmax_sc, s2_sc, cmax2_sc, *, n_sel, seq):
    blk = pl.program_id(1)
    nchunk = blk // (LC // QB) + 1
    q_pos = blk * QB + lax.broadcasted_iota(I32, (1, QB), 1)
    k_iota = lax.broadcasted_iota(I32, (LC, 1), 0)
    sub = lax.broadcasted_iota(I32, (LANES, QB), 0)
    eye = jnp.where(sub == lax.broadcasted_iota(I32, (LANES, QB), 1), 1.0, 0.0).astype(BF16)

    def transposed(x):
        return _dot_nt(eye, x)

    for jj in range(N_IDX_HEADS // 2):
        xt = transposed(qi_ref[:, jj * LANES:(jj + 1) * LANES])
        xi_sc[jj, :, 0:QB] = jnp.where(sub < IDX_DIM, xt, 0.0).astype(BF16)
        xi_sc[jj, :, QB:2 * QB] = jnp.where(sub >= IDX_DIM, xt, 0.0).astype(BF16)
    for g in range(N_KV_HEADS):
        for r in range(Q_PER_KV):
            h = g * Q_PER_KV + r
            qt_sc[g, 0:HEAD_DIM, r * QB:(r + 1) * QB] = transposed(
                q_ref[:, h * HEAD_DIM:(h + 1) * HEAD_DIM]).astype(BF16)
            qt_sc[g, HEAD_DIM:, r * QB:(r + 1) * QB] = eye
    kw_t = kw_ref[...].T
    wi = [kw_t[IDX_DIM + h:IDX_DIM + h + 1, :] for h in range(N_IDX_HEADS)]

    inf = jnp.inf
    part = (FOLD_ROWS, QB)

    def score_chunk(c, carry):
        s_min, s_max = carry
        off = pl.multiple_of(c * LC, LC)
        k2c = k2_ref[pl.ds(off, LC), :]
        sc = jnp.zeros((LC, QB), F32)
        for jj in range(N_IDX_HEADS // 2):
            rel = jnp.maximum(_dot(k2c, xi_sc[jj]), 0.0)
            sc = sc + rel[:, 0:QB] * wi[2 * jj] + rel[:, QB:2 * QB] * wi[2 * jj + 1]
        causal = off + k_iota <= q_pos
        key_sc[pl.ds(off, LC), :] = jnp.where(causal, sc, -inf)
        s_min = jnp.minimum(s_min, _fold_rows(jnp.where(causal, sc, inf), jnp.min))
        s_max = jnp.maximum(s_max, _fold_rows(jnp.where(causal, sc, -inf), jnp.max))
        return s_min, s_max

    s_min, s_max = lax.fori_loop(0, nchunk, score_chunk, (jnp.full(part, inf, F32), jnp.full(part, -inf, F32)))

    n_sel_chunk = (blk * QB + QB + SC - 1) // SC
    s_iota = lax.broadcasted_iota(I32, (SC, 1), 0)

    def count(pred_fn):
        def body(c, acc):
            off = pl.multiple_of(c * SC, SC)
            m = pred_fn(key_sc[pl.ds(off, SC), :], off)
            return acc + _fold_rows(m, jnp.sum)
        acc = lax.fori_loop(0, n_sel_chunk, body, jnp.zeros(part, F32))
        return jnp.sum(acc, axis=0, keepdims=True)

    has_thr = q_pos + 1 >= int(n_sel)
    lo0 = jnp.where(has_thr, jnp.min(s_min, axis=0, keepdims=True), -inf)
    hi0 = jnp.where(has_thr, jnp.max(s_max, axis=0, keepdims=True), -inf)

    def midpoint(lo, hi):
        mid = 0.5 * (lo + hi)
        return jnp.where(mid <= lo, hi, mid)

    def halve(i, st):
        lo, hi = st
        mid = midpoint(lo, hi)
        enough = count(lambda sc, off: jnp.where(sc >= mid, 1.0, 0.0)) >= n_sel
        return jnp.where(enough, mid, lo), jnp.where(enough, hi, mid)

    lo0, hi0 = lax.fori_loop(0, N_HALVINGS, halve, (lo0, hi0))

    def open_rows(lo, hi):
        return jnp.max(jnp.where(lo < hi, 1.0, 0.0)) > 0.0

    def bisect_cond(st):
        it, lo, hi = st
        return open_rows(lo, hi) & (it < seq)

    def bisect_body(st):
        it, lo, hi = st
        mid = midpoint(lo, hi)

        def body(c, acc):
            cnt, up, dn = acc
            off = pl.multiple_of(c * SC, SC)
            sc = key_sc[pl.ds(off, SC), :]
            ge = sc >= mid
            cnt = cnt + _fold_rows(jnp.where(ge, 1.0, 0.0), jnp.sum)
            up = jnp.minimum(up, _fold_rows(jnp.where(ge, sc, inf), jnp.min))
            dn = jnp.maximum(dn, _fold_rows(jnp.where(ge, -inf, sc), jnp.max))
            return cnt, up, dn

        cnt, up, dn = lax.fori_loop(
            0, n_sel_chunk, body, (jnp.zeros(part, F32), jnp.full(part, inf, F32), jnp.full(part, -inf, F32)))
        enough = jnp.sum(cnt, axis=0, keepdims=True) >= n_sel
        is_open = lo < hi
        new_lo = jnp.where(is_open & enough, jnp.min(up, axis=0, keepdims=True), lo)
        new_hi = jnp.where(is_open & jnp.logical_not(enough), jnp.max(dn, axis=0, keepdims=True), hi)
        return it + 1, new_lo, new_hi

    _, thr, _ = lax.while_loop(bisect_cond, bisect_body, (jnp.int32(0), lo0, hi0))
    n_gt = count(lambda kk, off: jnp.where(kk > thr, 1.0, 0.0))
    n_eq = count(lambda kk, off: jnp.where(kk == thr, 1.0, 0.0))
    need = n_sel - n_gt

    def tie_search():
        def tbody(i, p):
            cand = p | jnp.left_shift(jnp.int32(1), (seq.bit_length() - 2) - i)
            cnt = count(lambda kk, off: jnp.where(kk == thr, jnp.where(off + s_iota < cand, 1.0, 0.0), 0.0))
            return jnp.where(cnt < need, cand, p)
        return lax.fori_loop(0, seq.bit_length() - 1, tbody, jnp.zeros((1, QB), I32))

    ambiguous = jnp.max(jnp.where(has_thr & (n_eq > need), 1.0, 0.0)) > 0.0
    tie_hi = lax.cond(ambiguous, tie_search, lambda: jnp.full((1, QB), seq, I32))
    tie_hi = jnp.where(has_thr, tie_hi, -1)

    def bias_chunk(c, carry):
        off = pl.multiple_of(c * LC, LC)
        kk = key_sc[pl.ds(off, LC), :]
        tie_ok = jnp.where(off + k_iota <= tie_hi, 0.0, NEG)
        bias = jnp.where(kk > thr, 0.0, jnp.where(kk == thr, tie_ok, NEG))
        bias_sc[pl.ds(off, LC), :] = bias.astype(BF16)
        return carry

    lax.fori_loop(0, nchunk, bias_chunk, 0)

    m_sc[...] = jnp.full(m_sc.shape, NEG, F32)
    l_sc[...] = jnp.zeros(l_sc.shape, F32)
    acc_sc[...] = jnp.zeros(acc_sc.shape, F32)

    bufs = ((s_sc, cmax_sc), (s2_sc, cmax2_sc))

    def logits(c, g, buf):
        off = pl.multiple_of(c * LC, LC)
        kc = k_ref[pl.ds(off, LC), g * HEAD_DIM:(g + 1) * HEAD_DIM]
        s = _dot(jnp.concatenate([kc, bias_sc[pl.ds(off, LC), :]], axis=1), qt_sc[g])
        buf[0][g] = s
        buf[1][g] = jnp.max(_fold_rows(s, jnp.max), axis=0, keepdims=True)

    def accumulate(c, g, buf):
        off = pl.multiple_of(c * LC, LC)
        vc = v_ref[pl.ds(off, LC), g * HEAD_DIM:(g + 1) * HEAD_DIM]
        m_old = m_sc[g]
        m_new = jnp.maximum(m_old, buf[1][g])
        alpha = jnp.exp2(m_old - m_new)
        p = jnp.exp2(buf[0][g] - m_new)
        l_sc[g] = alpha * l_sc[g] + jnp.sum(_fold_rows(p, jnp.sum), axis=0, keepdims=True)
        acc_sc[g] = alpha * acc_sc[g] + _dot_tn(vc, p.astype(BF16))
        m_sc[g] = m_new

    def step(c, cur, nxt):
        for g in range(N_KV_HEADS):
            accumulate(c, g, cur)
            logits(c + 1, g, nxt)

    def last(c, cur):
        for g in range(N_KV_HEADS):
            accumulate(c, g, cur)

    for g in range(N_KV_HEADS):
        logits(0, g, bufs[0])
    n_pairs = (nchunk - 1) // 2

    def attn_pair(p, carry):
        step(2 * p, bufs[0], bufs[1])
        step(2 * p + 1, bufs[1], bufs[0])
        return carry

    lax.fori_loop(0, n_pairs, attn_pair, 0)
    c_tail = 2 * n_pairs

    @pl.when(nchunk - c_tail == 2)
    def _():
        step(c_tail, bufs[0], bufs[1])
        last(c_tail + 1, bufs[1])

    @pl.when(nchunk - c_tail == 1)
    def _():
        last(c_tail, bufs[0])

    for g in range(N_KV_HEADS):
        o_t = acc_sc[g] / l_sc[g]
        for r in range(Q_PER_KV):
            h = g * Q_PER_KV + r
            o_ref[:, h * HEAD_DIM:(h + 1) * HEAD_DIM] = o_t[:, r * QB:(r + 1) * QB].T.astype(o_ref.dtype)


def _dsa_attention(q, k, v, qi, k2, kw, *, batch, seq):
    nb = seq // QB
    n_sel = min(TOPK_MAX, seq // 4)
    q_cols = N_Q_HEADS * HEAD_DIM
    kv_cols = N_KV_HEADS * HEAD_DIM
    qi_cols = N_IDX_HEADS * IDX_DIM
    assert seq % LC == 0 and seq % SC == 0 and LC >= n_sel
    kern = functools.partial(_dsa_kernel, n_sel=float(n_sel), seq=seq)
    rows = Q_PER_KV * QB
    return pl.pallas_call(
        kern, grid=(batch, nb),
        in_specs=[pl.BlockSpec((QB, q_cols), lambda b, i: (b * nb + i, 0)),
                  pl.BlockSpec((seq, kv_cols), lambda b, i: (b, 0)),
                  pl.BlockSpec((seq, kv_cols), lambda b, i: (b, 0)),
                  pl.BlockSpec((seq, LANES), lambda b, i: (b, 0)),
                  pl.BlockSpec((QB, qi_cols), lambda b, i: (b * nb + i, 0)),
                  pl.BlockSpec((QB, LANES), lambda b, i: (b * nb + i, 0))],
        out_specs=pl.BlockSpec((QB, q_cols), lambda b, i: (b * nb + i, 0)),
        out_shape=jax.ShapeDtypeStruct((batch * seq, q_cols), BF16),
        scratch_shapes=[pltpu.VMEM((seq, QB), F32),
                        pltpu.VMEM((seq, QB), BF16),
                        pltpu.VMEM((N_IDX_HEADS // 2, LANES, 2 * QB), BF16),
                        pltpu.VMEM((N_KV_HEADS, HEAD_DIM + QB, rows), BF16),
                        pltpu.VMEM((N_KV_HEADS, 1, rows), F32),
                        pltpu.VMEM((N_KV_HEADS, 1, rows), F32),
                        pltpu.VMEM((N_KV_HEADS, HEAD_DIM, rows), F32),
                        pltpu.VMEM((N_KV_HEADS, LC, rows), F32), pltpu.VMEM((N_KV_HEADS, 1, rows), F32),
                        pltpu.VMEM((N_KV_HEADS, LC, rows), F32), pltpu.VMEM((N_KV_HEADS, 1, rows), F32)],
        compiler_params=_cparams(2), name="dsa_attention",
    )(q, k, v, k2, qi, kw)


def _ln_rows(x, g, b):
    mu = jnp.mean(x, axis=-1, keepdims=True)
    xc = x - mu
    var = jnp.mean(xc * xc, axis=-1, keepdims=True)
    return xc * lax.rsqrt(var + LN_EPS) * g + b


def _pack_halves(y):
    half = y.shape[1] // 2
    return pltpu.pack_elementwise([y[:, :half], y[:, half:]], packed_dtype=BF16)


def _unpack_halves(p):
    lo = pltpu.unpack_elementwise(p, index=0, packed_dtype=BF16, unpacked_dtype=F32)
    hi = pltpu.unpack_elementwise(p, index=1, packed_dtype=BF16, unpacked_dtype=F32)
    return jnp.concatenate([lo, hi], axis=1)


def _ln_kernel(x_ref, g_ref, b_ref, o_ref, o2_ref, *, packed):
    y = _ln_rows(x_ref[...], g_ref[...], b_ref[...])
    o_ref[...] = y
    o2_ref[...] = _pack_halves(y) if packed else y.astype(o2_ref.dtype)


def _layer_norm(x, g, b, *, bm, packed=False):
    m, d = x.shape
    row = pl.BlockSpec((bm, d), lambda i: (i, 0))
    vec = pl.BlockSpec((1, d), lambda i: (0, 0))
    second = ((m, d // 2), U32) if packed else ((m, d), BF16)
    return pl.pallas_call(
        functools.partial(_ln_kernel, packed=packed), grid=(m // bm,), in_specs=[row, vec, vec],
        out_specs=[row, pl.BlockSpec((bm, second[0][1]), lambda i: (i, 0))],
        out_shape=[jax.ShapeDtypeStruct((m, d), F32), jax.ShapeDtypeStruct(*second)],
        compiler_params=_cparams(1), name="layer_norm",
    )(x, g.reshape(1, d), b.reshape(1, d))


def _xattn_kernel(q_ref, k_ref, v_ref, o_ref):
    dh = q_ref.shape[1] // N_MEM_HEADS
    for h in range(N_MEM_HEADS):
        cols = slice(h * dh, (h + 1) * dh)
        s = _dot_nt(q_ref[:, cols], k_ref[:, cols])
        p = jnp.exp(s - jnp.max(s, axis=1, keepdims=True))
        o = _dot(p.astype(BF16), v_ref[:, cols]) / jnp.sum(p, axis=1, keepdims=True)
        o_ref[:, cols] = o.astype(o_ref.dtype)


def _cross_attention(q, k, v, *, batch, seq, n_mem, bm):
    d = q.shape[1]
    nb = seq // bm
    return pl.pallas_call(
        _xattn_kernel, grid=(batch, nb),
        in_specs=[pl.BlockSpec((bm, d), lambda b, i: (b * nb + i, 0)),
                  pl.BlockSpec((n_mem, d), lambda b, i: (b, 0)),
                  pl.BlockSpec((n_mem, d), lambda b, i: (b, 0))],
        out_specs=pl.BlockSpec((bm, d), lambda b, i: (b * nb + i, 0)),
        out_shape=jax.ShapeDtypeStruct((batch * seq, d), BF16),
        compiler_params=_cparams(2), name="cross_attention",
    )(q, k, v)


def _split_bf16(a):
    hi = a.astype(BF16)
    return hi, (a - hi.astype(F32)).astype(BF16)


def _router_kernel(x_ref, w_ref, b_ref, id_ref, gate_ref):
    xh, xl = _split_bf16(x_ref[...])
    wh, wl = _split_bf16(w_ref[...])
    logits = _dot(xh, wh) + _dot(xl, wh) + _dot(xh, wl) + b_ref[...]
    lane = lax.broadcasted_iota(I32, logits.shape, 1)
    lane_f = lane.astype(F32)
    big = float(LANES)
    is_g = lane < N_GROUPS
    gl = jnp.where(is_g, logits, -jnp.inf)
    g_max = jnp.max(gl, axis=1, keepdims=True)
    g_idx = jnp.min(jnp.where(gl == g_max, lane_f, big), axis=1, keepdims=True)
    g_gate = 1.0 / jnp.sum(jnp.where(is_g, jnp.exp(gl - g_max), 0.0), axis=1, keepdims=True)
    e_lo = N_GROUPS + g_idx * EXPERTS_PER_GROUP
    in_grp = (lane_f >= e_lo) & (lane_f < e_lo + EXPERTS_PER_GROUP)
    el = jnp.where(in_grp, logits, -jnp.inf)
    v1 = jnp.max(el, axis=1, keepdims=True)
    i1 = jnp.min(jnp.where(el == v1, lane_f, big), axis=1, keepdims=True)
    el2 = jnp.where(lane_f == i1, -jnp.inf, el)
    v2 = jnp.max(el2, axis=1, keepdims=True)
    i2 = jnp.min(jnp.where(el2 == v2, lane_f, big), axis=1, keepdims=True)
    z = jnp.exp(v2 - v1)
    w1 = g_gate / (1.0 + z)
    w2 = g_gate * z / (1.0 + z)
    ids = jnp.where(lane == 0, i1 - N_GROUPS, jnp.where(lane == 1, i2 - N_GROUPS, 0.0))
    id_ref[...] = ids.astype(I32)
    gate_ref[...] = jnp.where(lane == 0, w1, jnp.where(lane == 1, w2, 0.0))


def _router(x, w_r, b_r, *, bm):
    m, d = x.shape
    out = pl.BlockSpec((bm, LANES), lambda i: (i, 0))
    return pl.pallas_call(
        _router_kernel, grid=(m // bm,),
        in_specs=[pl.BlockSpec((bm, d), lambda i: (i, 0)),
                  pl.BlockSpec((d, LANES), lambda i: (0, 0)),
                  pl.BlockSpec((1, LANES), lambda i: (0, 0))],
        out_specs=[out, out],
        out_shape=[jax.ShapeDtypeStruct((m, LANES), I32), jax.ShapeDtypeStruct((m, LANES), F32)],
        compiler_params=_cparams(1), name="router",
    )(x, w_r, b_r)


def _rank_kernel(id_ref, rank_ref, cnt_ref, carry_ref):
    i = pl.program_id(0)
    bt = id_ref.shape[0]

    @pl.when(i == 0)
    def _():
        carry_ref[...] = jnp.zeros(carry_ref.shape, F32)

    ids = id_ref[...]
    lane = lax.broadcasted_iota(I32, (bt, LANES), 1)
    e0 = ids[:, 0:1]
    e1 = ids[:, 1:2] + N_EXPERTS
    hit0 = lane == e0
    hit1 = lane == e1
    onehot = jnp.where(hit0, 1.0, jnp.where(hit1, 1.0, 0.0))
    r_io = lax.broadcasted_iota(I32, (bt, bt), 0)
    c_io = lax.broadcasted_iota(I32, (bt, bt), 1)
    tri = jnp.where(c_io < r_io, 1.0, 0.0).astype(BF16)
    prefix = _dot(tri, onehot.astype(BF16)) + carry_ref[0:1, :]
    rank0 = jnp.sum(jnp.where(hit0, prefix, 0.0), axis=1, keepdims=True)
    rank1 = jnp.sum(jnp.where(hit1, prefix, 0.0), axis=1, keepdims=True)
    rank_ref[...] = jnp.where(lane == 0, rank0, jnp.where(lane == 1, rank1, 0.0))
    total = carry_ref[0:1, :] + jnp.sum(onehot, axis=0, keepdims=True)
    carry_ref[...] = jnp.broadcast_to(total, carry_ref.shape)
    cnt_ref[...] = jnp.broadcast_to(total, cnt_ref.shape)


def _slot_ranks(ids, *, bt):
    m = ids.shape[0]
    return pl.pallas_call(
        _rank_kernel, grid=(m // bt,),
        in_specs=[pl.BlockSpec((bt, LANES), lambda i: (i, 0))],
        out_specs=[pl.BlockSpec((bt, LANES), lambda i: (i, 0)),
                   pl.BlockSpec((8, LANES), lambda i: (0, 0))],
        out_shape=[jax.ShapeDtypeStruct((m, LANES), F32), jax.ShapeDtypeStruct((8, LANES), F32)],
        scratch_shapes=[pltpu.VMEM((8, LANES), F32)],
        compiler_params=_cparams(1), name="slot_ranks",
    )(ids)


def _pos_kernel(id_ref, rank_ref, tab_ref, pos_ref):
    ids = id_ref[...]
    ranks = rank_ref[...]
    lane = lax.broadcasted_iota(I32, ids.shape, 1)
    tab = tab_ref[...]
    base0 = jnp.sum(jnp.where(lane == ids[:, 0:1], tab, 0.0), axis=1, keepdims=True)
    base1 = jnp.sum(jnp.where(lane == ids[:, 1:2] + N_EXPERTS, tab, 0.0), axis=1, keepdims=True)
    pos = jnp.where(lane == 0, base0 + ranks[:, 0:1], jnp.where(lane == 1, base1 + ranks[:, 1:2], 0.0))
    pos_ref[...] = pos.astype(I32)


def _slot_positions(ids, ranks, table, *, bt):
    m = ids.shape[0]
    blk = pl.BlockSpec((bt, LANES), lambda i: (i, 0))
    return pl.pallas_call(
        _pos_kernel, grid=(m // bt,),
        in_specs=[blk, blk, pl.BlockSpec((1, LANES), lambda i: (0, 0))],
        out_specs=blk,
        out_shape=jax.ShapeDtypeStruct((m, LANES), I32),
        compiler_params=_cparams(1), name="slot_positions",
    )(ids, ranks, table)


def _row_copy(src_ref, src_row, dst_ref, dst_row, sem):
    return pltpu.make_async_copy(src_ref.at[pl.ds(src_row, 1)], dst_ref.at[pl.ds(dst_row, 1)], sem)


def _dispatch_kernel(pos0_ref, pos1_ref, x_ref, xs_in_ref, xs_ref, sem):
    del xs_in_ref
    i = pl.program_id(0)
    bt = x_ref.shape[0]

    def issue(r, carry):
        t = i * bt + r
        _row_copy(x_ref, r, xs_ref, pos0_ref[t], sem.at[0]).start()
        _row_copy(x_ref, r, xs_ref, pos1_ref[t], sem.at[1]).start()
        return carry

    lax.fori_loop(0, bt, issue, 0)
    for k in range(2):
        pltpu.make_async_copy(x_ref, xs_ref.at[pl.ds(0, bt)], sem.at[k]).wait()


def _dispatch(x, pos0, pos1, n_slots, *, bt):
    m, d = x.shape
    xs0 = jnp.zeros((n_slots, d), x.dtype)
    return pl.pallas_call(
        _dispatch_kernel,
        grid_spec=pltpu.PrefetchScalarGridSpec(
            num_scalar_prefetch=2, grid=(m // bt,),
            in_specs=[pl.BlockSpec((bt, d), lambda i, p0, p1: (i, 0)),
                      pl.BlockSpec(memory_space=pl.ANY)],
            out_specs=pl.BlockSpec(memory_space=pl.ANY),
            scratch_shapes=[pltpu.SemaphoreType.DMA((2,))]),
        out_shape=jax.ShapeDtypeStruct((n_slots, d), x.dtype),
        input_output_aliases={3: 0},
        compiler_params=_cparams(1), name="dispatch",
    )(pos0, pos1, x, xs0)


def _ffn_kernel(blk_e_ref, nxt_e_ref, set_ref, nused_ref, x_ref, wg_hbm, wu_hbm, wd_hbm, y_ref,
                wg_bf, wu_bf, wd_bf, st_g, st_u, st_d, sem):
    i = pl.program_id(0)
    d, ff = wg_bf.shape[1], wg_bf.shape[2]
    rg, rd = d // W_GROUPS, ff // W_GROUPS

    n_slot = st_g.shape[0]

    def group_copies(e, g):
        slot = g % n_slot
        return (pltpu.make_async_copy(wg_hbm.at[e, pl.ds(g * rg, rg), :], st_g.at[slot], sem.at[0, slot]),
                pltpu.make_async_copy(wu_hbm.at[e, pl.ds(g * rg, rg), :], st_u.at[slot], sem.at[1, slot]),
                pltpu.make_async_copy(wd_hbm.at[e, pl.ds(g * rd, rd), :], st_d.at[slot], sem.at[2, slot]))

    def start(e, g):
        for c in group_copies(e, g):
            c.start()

    def finish(e, g, s):
        for c in group_copies(e, g):
            c.wait()
        slot = g % n_slot
        wg_bf[s, g * rg:(g + 1) * rg, :] = st_g[slot].astype(BF16)
        wu_bf[s, g * rg:(g + 1) * rg, :] = st_u[slot].astype(BF16)
        wd_bf[s, g * rd:(g + 1) * rd, :] = st_d[slot].astype(BF16)

    def load_expert(e, s, between=()):
        for g in range(n_slot):
            start(e, g)
        for g in range(W_GROUPS):
            if g < len(between):
                between[g]()
            finish(e, g, s)
            if g + n_slot < W_GROUPS:
                start(e, g + n_slot)

    e = blk_e_ref[i]
    s = set_ref[i]
    nxt = nxt_e_ref[i]
    first_of_expert = jnp.logical_or(i == 0, blk_e_ref[jnp.maximum(i - 1, 0)] != e)
    active = i < nused_ref[0]
    prefetch = jnp.logical_and(jnp.logical_and(active, first_of_expert), nxt >= 0)

    @pl.when(i == 0)
    def _():
        load_expert(e, s)

    def ffn_steps():
        vals = {}

        def gate_step():
            vals["x"] = _unpack_halves(x_ref[...]).astype(BF16)
            vals["gate"] = _dot(vals["x"], wg_bf[s])

        def up_step():
            gate = vals["gate"]
            vals["h"] = (gate * (1.0 / (1.0 + jnp.exp(-gate))) * _dot(vals["x"], wu_bf[s])).astype(BF16)

        def down_step():
            y_ref[...] = _pack_halves(_dot(vals["h"], wd_bf[s]))

        return gate_step, up_step, down_step

    @pl.when(prefetch)
    def _():
        load_expert(nxt, 1 - s, between=ffn_steps())

    @pl.when(jnp.logical_and(active, jnp.logical_not(prefetch)))
    def _():
        for step in ffn_steps():
            step()


def _expert_ffn(xs, blk_e, nxt_e, set_idx, nused, wg, wu, wd):
    n_slots, dp = xs.shape
    d, ff = wg.shape[1], wg.shape[2]
    nblk = n_slots // MOE_BLK
    assert dp * 2 == d and d % W_GROUPS == 0 and ff % W_GROUPS == 0 and W_GROUPS >= W_STAGE_SLOTS > 3

    def row_map(i, be, nx, si, nu):
        return (jnp.minimum(i, nu[0] - 1), 0)

    hbm = pl.BlockSpec(memory_space=pl.ANY)
    return pl.pallas_call(
        _ffn_kernel,
        grid_spec=pltpu.PrefetchScalarGridSpec(
            num_scalar_prefetch=4, grid=(nblk,),
            in_specs=[pl.BlockSpec((MOE_BLK, dp), row_map), hbm, hbm, hbm],
            out_specs=pl.BlockSpec((MOE_BLK, dp), row_map),
            scratch_shapes=[pltpu.VMEM((2, d, ff), BF16), pltpu.VMEM((2, d, ff), BF16), pltpu.VMEM((2, ff, d), BF16),
                            pltpu.VMEM((W_STAGE_SLOTS, d // W_GROUPS, ff), F32),
                            pltpu.VMEM((W_STAGE_SLOTS, d // W_GROUPS, ff), F32),
                            pltpu.VMEM((W_STAGE_SLOTS, ff // W_GROUPS, d), F32),
                            pltpu.SemaphoreType.DMA((3, W_STAGE_SLOTS))]),
        out_shape=jax.ShapeDtypeStruct((n_slots, dp), xs.dtype),
        input_output_aliases={4: 0},
        compiler_params=_cparams(1), name="expert_ffn",
    )(blk_e, nxt_e, set_idx, nused, xs, wg, wu, wd)


def _combine_kernel(pos0_ref, pos1_ref, x_ref, gate_ref, g_ref, b_ref, ys_ref, o_ref, y0_ref, y1_ref, sem):
    i = pl.program_id(0)
    bt = x_ref.shape[0]
    slot = i % 2

    def issue(step, to_slot):
        def body(r, carry):
            t = step * bt + r
            _row_copy(ys_ref, pos0_ref[t], y0_ref.at[to_slot], r, sem.at[0, to_slot]).start()
            _row_copy(ys_ref, pos1_ref[t], y1_ref.at[to_slot], r, sem.at[1, to_slot]).start()
            return carry
        lax.fori_loop(0, bt, body, 0)

    @pl.when(i == 0)
    def _():
        issue(0, 0)

    @pl.when(i + 1 < pl.num_programs(0))
    def _():
        issue(i + 1, 1 - slot)

    pltpu.make_async_copy(ys_ref.at[pl.ds(0, bt)], y0_ref.at[slot], sem.at[0, slot]).wait()
    pltpu.make_async_copy(ys_ref.at[pl.ds(0, bt)], y1_ref.at[slot], sem.at[1, slot]).wait()
    gates = gate_ref[...]
    f = _unpack_halves(y0_ref[slot]) * gates[:, 0:1] + _unpack_halves(y1_ref[slot]) * gates[:, 1:2]
    o_ref[...] = _ln_rows(ALPHA * x_ref[...] + f, g_ref[...], b_ref[...])


def _combine(x, gates, ys, pos0, pos1, g, b, *, bt):
    m, d = x.shape
    dp = ys.shape[1]
    row = pl.BlockSpec((bt, d), lambda i, p0, p1: (i, 0))
    vec = pl.BlockSpec((1, d), lambda i, p0, p1: (0, 0))
    return pl.pallas_call(
        _combine_kernel,
        grid_spec=pltpu.PrefetchScalarGridSpec(
            num_scalar_prefetch=2, grid=(m // bt,),
            in_specs=[row, pl.BlockSpec((bt, LANES), lambda i, p0, p1: (i, 0)), vec, vec,
                      pl.BlockSpec(memory_space=pl.ANY)],
            out_specs=row,
            scratch_shapes=[pltpu.VMEM((2, bt, dp), ys.dtype), pltpu.VMEM((2, bt, dp), ys.dtype),
                            pltpu.SemaphoreType.DMA((2, 2))]),
        out_shape=jax.ShapeDtypeStruct((m, d), F32),
        compiler_params=_cparams(1), name="combine_ln",
    )(pos0, pos1, x, gates, g.reshape(1, d), b.reshape(1, d), ys)


def _rope_tables(positions, n_rows):
    pos = positions.astype(F32).reshape(n_rows, 1)

    def cs(dim):
        inv = 1.0 / (ROPE_THETA ** (jnp.arange(0, dim, 2, dtype=F32) / dim))
        ang = pos * inv[None, :]
        return jnp.cos(ang), jnp.sin(ang)

    c_h, s_h = cs(HEAD_DIM)
    c_i, s_i = cs(IDX_DIM)
    z_i = jnp.zeros_like(s_i)
    cosf = jnp.concatenate([c_h, c_h], axis=1)
    sinf = jnp.concatenate([-s_h, s_h], axis=1)
    cos64 = jnp.concatenate([c_i, c_i, c_i, c_i], axis=1)
    sin_lo = jnp.concatenate([-s_i, z_i, -s_i, z_i], axis=1)
    sin_hi = jnp.concatenate([z_i, s_i, z_i, s_i], axis=1)
    return cosf, sinf, cos64, sin_lo, sin_hi


def kernel(x, mem, positions, w_in, pool_w, pool_scale, w_o, ln1_g, ln1_b, w_mq, w_mk, w_mv, w_mo, ln2_g, ln2_b,
           w_group_router, b_group_router, w_expert_router, b_expert_router, w_gate, w_up, w_down, ln3_g, ln3_b):
    batch, seq, d = x.shape
    n_mem = mem.shape[1]
    n = batch * seq
    bn_attn = N_KV_HEADS * HEAD_DIM
    tables = _rope_tables(positions, n)
    xf = x.reshape(n, d)
    n_slots = 2 * n + N_EXPERTS * MOE_BLK
    nblk = n_slots // MOE_BLK

    for l in range(w_in.shape[0]):
        x_bf = xf.astype(BF16)
        kw_col = MIX_POOL + 7 * bn_attn
        w_kw = jnp.pad(w_in[l][:, kw_col:], ((0, 0), (0, LANES - (w_in.shape[2] - kw_col))))

        v_pool = _matmul([x_bf], [w_in], layer=l, name="inproj_pool", bm=1024, bn=DENSE_BN, n=MIX_POOL)
        blk0 = MIX_POOL // bn_attn
        q_cols = N_Q_HEADS * HEAD_DIM
        q_scale = HEAD_DIM ** -0.5 * LOG2E
        rope_h, rope_i = tables[:2] + tables[:1], tables[2:]
        proj = functools.partial(_inproj_rope, x_bf, w_in, layer=l, bm=1024, bn=bn_attn)
        q = proj(rope_h, name="inproj_q", col_off_blocks=blk0, n=q_cols, rope=HEAD_DIM, scale=q_scale)
        k = proj(rope_h, name="inproj_k", col_off_blocks=blk0 + 4, n=bn_attn, rope=HEAD_DIM)
        v = proj(rope_h, name="inproj_v", col_off_blocks=blk0 + 5, n=bn_attn, rope=None)
        qi = proj(rope_i, name="inproj_qi", col_off_blocks=blk0 + 6, n=bn_attn, rope=IDX_DIM)
        k2, kw = _inproj_idx(x_bf, w_kw, tables, bm=1024)
        a_pool = _pool_mixer(v_pool, pool_w[l].astype(BF16), pool_scale[l], batch=batch, seq=seq, ts=512)
        a_attn = _dsa_attention(q, k, v, qi, k2, kw, batch=batch, seq=seq)
        res = (xf, (1024, DENSE_BN), lambda i, j: (i, j))
        pre = _matmul([a_pool, a_attn], [w_o, w_o], layer=l, name="outproj", bm=1024, bn=DENSE_BN, n=d,
                      extras=(res,), epilogue=_residual_epilogue)
        x1, x1_bf = _layer_norm(pre, ln1_g[l], ln1_b[l], bm=256)

        mq_scale = (d // N_MEM_HEADS) ** -0.5

        def q_epilogue(acc, ex, outs):
            outs[0][...] = (acc * mq_scale).astype(outs[0].dtype)

        qm = _matmul([x1_bf], [w_mq], layer=l, name="mem_q", bm=1024, bn=DENSE_BN, n=d, out_dtype=BF16,
                     epilogue=q_epilogue)
        mem_bf = mem.reshape(batch * n_mem, d).astype(BF16)
        km = _matmul([mem_bf], [w_mk], layer=l, name="mem_k", bm=batch * n_mem, bn=DENSE_BN, n=d, out_dtype=BF16)
        vm = _matmul([mem_bf], [w_mv], layer=l, name="mem_v", bm=batch * n_mem, bn=DENSE_BN, n=d, out_dtype=BF16)
        om = _cross_attention(qm, km, vm, batch=batch, seq=seq, n_mem=n_mem, bm=512)
        res = (x1, (1024, DENSE_BN), lambda i, j: (i, j))
        pre = _matmul([om], [w_mo], layer=l, name="mem_o", bm=1024, bn=DENSE_BN, n=d, extras=(res,),
                      epilogue=_residual_epilogue)
        x2, x2_rows = _layer_norm(pre, ln2_g[l], ln2_b[l], bm=256, packed=True)

        w_r = jnp.pad(jnp.concatenate([w_group_router[l], w_expert_router[l]], axis=1),
                      ((0, 0), (0, LANES - N_GROUPS - N_EXPERTS)))
        b_r = jnp.pad(jnp.concatenate([b_group_router[l], b_expert_router[l]]),
                      (0, LANES - N_GROUPS - N_EXPERTS)).reshape(1, LANES)
        ids, gates = _router(x2, w_r, b_r, bm=512)
        ranks, totals = _slot_ranks(ids, bt=512)
        c0 = totals[0, :N_EXPERTS].astype(I32)
        c1 = totals[0, N_EXPERTS:].astype(I32)
        padded = ((c0 + c1 + MOE_BLK - 1) // MOE_BLK) * MOE_BLK
        pend = jnp.cumsum(padded)
        pstart = pend - padded
        table = jnp.concatenate([pstart, pstart + c0]).astype(F32).reshape(1, LANES)
        pos = _slot_positions(ids, ranks, table, bt=512)
        pos0, pos1 = pos[:, 0], pos[:, 1]
        nused = pend[-1] // MOE_BLK
        blk_i = jnp.minimum(jnp.arange(nblk, dtype=I32), nused - 1)
        blk_e = jnp.sum((pend[None, :] <= (blk_i * MOE_BLK)[:, None]).astype(I32), axis=1)
        blk_e = jnp.minimum(blk_e, N_EXPERTS - 1)
        later = blk_e[None, :] > blk_e[:, None]
        nxt_e = jnp.min(jnp.where(later, blk_e[None, :], N_EXPERTS), axis=1)
        nxt_e = jnp.where(nxt_e == N_EXPERTS, -1, nxt_e).astype(I32)
        new_e = jnp.concatenate([jnp.ones((1,), I32), (blk_e[1:] != blk_e[:-1]).astype(I32)])
        set_idx = (jnp.cumsum(new_e) - 1) % 2

        xs = _dispatch(x2_rows, pos0, pos1, n_slots, bt=512)
        ys = _expert_ffn(xs, blk_e, nxt_e, set_idx.astype(I32), nused.reshape(1).astype(I32),
                         w_gate[l], w_up[l], w_down[l])
        xf = _combine(x2, gates, ys, pos0, pos1, ln3_g[l], ln3_b[l], bt=256)
    return xf.reshape(batch, seq, d)
```
